```python
import jax, jax.numpy as jnp
from jax import lax
import numpy as np

D_MODEL = 1024
BATCH = 32
SEQ = 2048
DEPTH = 1

HEAD_DIM = 64
ROPE_THETA = 10000.0
EPS = 1e-6
NEG_INF = -1e30
FORCE_SCORE = 1e9
ATTN_BLOCK = 128
NSA_HEADS = 8
NSA_KV_HEADS = 2
NSA_GROUP = NSA_HEADS // NSA_KV_HEADS
CMP_BLOCK = 32
CMP_STRIDE = 16
CMP_HIDDEN = 256
SEL_BLOCK = 64
N_SEL = 8
N_LOCAL_SEL = 2
WINDOW = 512
QUERY_CHUNK = 32
DIL_PAIRS = ((128, 1), (512, 4), (2048, 16))
DIL_HEADS_PER_GROUP = 2
DIL_HEADS = DIL_HEADS_PER_GROUP * len(DIL_PAIRS)
A_Q = NSA_HEADS * HEAD_DIM
A_KV = NSA_KV_HEADS * HEAD_DIM
DIL_W = DIL_HEADS * HEAD_DIM
IN_COLS = A_Q + 6 * A_KV + 3 * NSA_HEADS + 3 * DIL_W + 2 * D_MODEL
N_GROUPS = 4
EXPERTS_PER_GROUP = 8
N_EXPERTS = N_GROUPS * EXPERTS_PER_GROUP
TOP_K_INNER = 2
EXPERT_FF = 512
MOE_BLOCK = 256

kernel_name = "nsa_dilated_hmoe_hybrid_block"


def rms_norm(x, g):
    xf = x.astype(jnp.float32)
    y = xf * lax.rsqrt(jnp.mean(xf * xf, axis=-1, keepdims=True) + EPS)
    return (y * g.astype(jnp.float32)).astype(x.dtype)


def rope_tables(pos):
    inv = ROPE_THETA ** (-jnp.arange(0, HEAD_DIM, 2, dtype=jnp.float32) / HEAD_DIM)
    ang = pos.astype(jnp.float32)[..., None] * inv
    return jnp.cos(ang)[:, None], jnp.sin(ang)[:, None]


def apply_rope(x, cos, sin):
    x1, x2 = jnp.split(x.astype(jnp.float32), 2, axis=-1)
    return jnp.concatenate([x1 * cos - x2 * sin, x1 * sin + x2 * cos], axis=-1).astype(x.dtype)


def to_heads(t, n):
    b, s, _ = t.shape
    return t.reshape(b, s, n, HEAD_DIM).transpose(0, 2, 1, 3)


def banded_attention(q, k, v, window):
    b, hk, g, L, dh = q.shape
    blk = ATTN_BLOCK
    Lp = -(-L // blk) * blk
    nback = -(-window // blk)
    pad_end = Lp - L
    nkeys = (nback + 1) * blk
    qp = jnp.pad(q, ((0, 0), (0, 0), (0, 0), (0, pad_end), (0, 0)))
    kp = jnp.pad(k, ((0, 0), (0, 0), (nback * blk, pad_end), (0, 0)))
    vp = jnp.pad(v, ((0, 0), (0, 0), (nback * blk, pad_end), (0, 0)))
    scale = dh ** -0.5
    qi = jnp.arange(blk)
    kj = jnp.arange(nkeys) - nback * blk
    dist = qi[:, None] - kj[None, :]

    def one_block(bi):
        start = bi * blk
        qb = lax.dynamic_slice_in_dim(qp, start, blk, axis=3)
        kb = lax.dynamic_slice_in_dim(kp, start, nkeys, axis=2)
        vb = lax.dynamic_slice_in_dim(vp, start, nkeys, axis=2)
        s = jnp.einsum('bkgqd,bknd->bkgqn', qb, kb, preferred_element_type=jnp.float32) * scale
        valid = (dist >= 0) & (dist <= window) & ((start + kj)[None, :] >= 0)
        s = jnp.where(valid, s, -jnp.inf)
        lse = jax.nn.logsumexp(s, axis=-1)
        p = jnp.exp(s - lse[..., None])
        o = jnp.einsum('bkgqn,bknd->bkgqd', p.astype(vb.dtype), vb)
        return o, lse

    o, lse = lax.map(one_block, jnp.arange(Lp // blk))
    o = jnp.moveaxis(o, 0, 3).reshape(b, hk, g, Lp, dh)[:, :, :, :L]
    lse = jnp.moveaxis(lse, 0, 3).reshape(b, hk, g, Lp)[..., :L]
    return o, lse


def compress_blocks(kv, pe, w1, w2):
    s = kv.shape[2]
    n_c = (s - CMP_BLOCK) // CMP_STRIDE + 1
    idx = jnp.arange(n_c)[:, None] * CMP_STRIDE + jnp.arange(CMP_BLOCK)[None, :]
    blocks = kv[:, :, idx] + pe
    flat = blocks.reshape(*blocks.shape[:3], CMP_BLOCK * HEAD_DIM)
    return jax.nn.silu(flat @ w1) @ w2


def nsa_compressed_and_selected(q, kc, vc, k_slc, v_slc):
    b, hk, g, s_len, dh = q.shape
    n_c = kc.shape[2]
    n_s = s_len // SEL_BLOCK
    n_sel = min(N_SEL, n_s)
    scale = dh ** -0.5
    cmp_start = jnp.arange(n_c) * CMP_STRIDE
    cmp_end = cmp_start + CMP_BLOCK - 1
    sel_start = jnp.arange(n_s) * SEL_BLOCK
    overlap = jnp.clip(jnp.minimum(cmp_start[:, None] + CMP_BLOCK, sel_start[None, :] + SEL_BLOCK)
                       - jnp.maximum(cmp_start[:, None], sel_start[None, :]), 0, None).astype(jnp.float32) / CMP_BLOCK
    k_blocks = k_slc.reshape(b, hk, n_s, SEL_BLOCK, dh)
    v_blocks = v_slc.reshape(b, hk, n_s, SEL_BLOCK, dh)
    b_idx = jnp.arange(b)[:, None, None, None]
    h_idx = jnp.arange(hk)[None, :, None, None]
    blk_ids = jnp.arange(n_s)
    tok_in_blk = jnp.arange(SEL_BLOCK)

    def chunk(ci):
        start = ci * QUERY_CHUNK
        t = start + jnp.arange(QUERY_CHUNK)
        qc = lax.dynamic_slice_in_dim(q, start, QUERY_CHUNK, axis=3)
        s = jnp.einsum('bkgqd,bknd->bkgqn', qc, kc, preferred_element_type=jnp.float32) * scale
        valid = cmp_end[None, :] <= t[:, None]
        p = jnp.where(valid, jax.nn.softmax(jnp.where(valid, s, NEG_INF), axis=-1), 0.0)
        o_cmp = jnp.einsum('bkgqn,bknd->bkgqd', p.astype(vc.dtype), vc)
        imp = jnp.einsum('bkgqn,ns->bkqs', p, overlap)
        rel = (t // SEL_BLOCK)[:, None] - blk_ids[None, :]
        forced = (blk_ids[None, :] == 0) | ((rel >= 0) & (rel < N_LOCAL_SEL))
        score = jnp.where(rel < 0, NEG_INF, jnp.where(forced, FORCE_SCORE, imp))
        _, idx = lax.top_k(score, n_sel)
        kg = k_blocks[b_idx, h_idx, idx]
        vg = v_blocks[b_idx, h_idx, idx]
        tok = idx[..., None] * SEL_BLOCK + tok_in_blk
        ok = tok <= t[None, None, :, None, None]
        s2 = jnp.einsum('bkgqd,bkqnld->bkgqnl', qc, kg, preferred_element_type=jnp.float32) * scale
        p2 = jax.nn.softmax(jnp.where(ok[:, :, None], s2, -jnp.inf), axis=(-2, -1))
        o_slc = jnp.einsum('bkgqnl,bkqnld->bkgqd', p2.astype(vg.dtype), vg)
        return o_cmp, o_slc

    o_cmp, o_slc = lax.map(chunk, jnp.arange(s_len // QUERY_CHUNK))
    o_cmp = jnp.moveaxis(o_cmp, 0, 3).reshape(b, hk, g, s_len, dh)
    o_slc = jnp.moveaxis(o_slc, 0, 3).reshape(b, hk, g, s_len, dh)
    return o_cmp, o_slc


def dilated_attention(q, k, v):
    b, _, s_len, dh = q.shape
    hpg = DIL_HEADS_PER_GROUP
    outs, lses = [], []
    for gi, (w, d) in enumerate(DIL_PAIRS):
        sl = slice(gi * hpg, (gi + 1) * hpg)
        L = s_len // d

        def strided(t):
            return t[:, sl].reshape(b, hpg, L, d, dh).transpose(0, 1, 3, 2, 4).reshape(b, hpg * d, L, dh)

        o, lse = banded_attention(strided(q)[:, :, None], strided(k), strided(v), w // d)
        outs.append(o[:, :, 0].reshape(b, hpg, d, L, dh).transpose(0, 1, 3, 2, 4).reshape(b, hpg, s_len, dh))
        lses.append(lse[:, :, 0].reshape(b, hpg, d, L).transpose(0, 1, 3, 2).reshape(b, hpg, s_len))
    alpha = jax.nn.softmax(jnp.stack(lses), axis=0)
    o = jnp.stack(outs) * alpha[..., None].astype(outs[0].dtype)
    return o.transpose(1, 3, 0, 2, 4).reshape(b, s_len, DIL_W)


def hybrid_mixer(h, positions, w_in, nsa_q_norm, nsa_k_norm, cmp_pe_k, cmp_w1_k, cmp_w2_k,
                 cmp_pe_v, cmp_w1_v, cmp_w2_v, dil_q_norm, dil_k_norm, w_up_a, w_up_b, w_out):
    b, s_len, _ = h.shape
    proj = h @ w_in
    c1 = A_Q
    c2 = c1 + 6 * A_KV
    c3 = c2 + 3 * NSA_HEADS
    c4 = c3 + 3 * DIL_W
    q_a, kv_a, g_a, qkv_b, g_m = jnp.split(proj, [c1, c2, c3, c4], axis=-1)
    cos, sin = rope_tables(positions)

    q = apply_rope(rms_norm(to_heads(q_a, NSA_HEADS), nsa_q_norm), cos, sin)
    q = q.reshape(b, NSA_KV_HEADS, NSA_GROUP, s_len, HEAD_DIM)
    k_cmp, v_cmp, k_slc, v_slc, k_win, v_win = [to_heads(t, NSA_KV_HEADS) for t in jnp.split(kv_a, 6, axis=-1)]
    n_c = (s_len - CMP_BLOCK) // CMP_STRIDE + 1
    cmp_pos = positions[:, jnp.arange(n_c) * CMP_STRIDE + CMP_BLOCK - 1]
    ccos, csin = rope_tables(cmp_pos)
    kc = apply_rope(rms_norm(compress_blocks(k_cmp, cmp_pe_k, cmp_w1_k, cmp_w2_k), nsa_k_norm), ccos, csin)
    vc = compress_blocks(v_cmp, cmp_pe_v, cmp_w1_v, cmp_w2_v)
    k_slc = apply_rope(rms_norm(k_slc, nsa_k_norm), cos, sin)
    k_win = apply_rope(rms_norm(k_win, nsa_k_norm), cos, sin)
    o_cmp, o_slc = nsa_compressed_and_selected(q, kc, vc, k_slc, v_slc)
    o_win, _ = banded_attention(q, k_win, v_win, WINDOW)
    gates = jax.nn.sigmoid(g_a).reshape(b, s_len, 3, NSA_KV_HEADS, NSA_GROUP).transpose(2, 0, 3, 4, 1)[..., None]
    o_a = gates[0] * o_cmp + gates[1] * o_slc + gates[2] * o_win
    o_a = o_a.transpose(0, 3, 1, 2, 4).reshape(b, s_len, A_Q)

    q_b, k_b, v_b = [to_heads(t, DIL_HEADS) for t in jnp.split(qkv_b, 3, axis=-1)]
    q_b = apply_rope(rms_norm(q_b, dil_q_norm), cos, sin)
    k_b = apply_rope(rms_norm(k_b, dil_k_norm), cos, sin)
    o_b = dilated_attention(q_b, k_b, v_b)

    gm_a, gm_b = jnp.split(jax.nn.sigmoid(g_m), 2, axis=-1)
    y = gm_a * (o_a @ w_up_a) + gm_b * (o_b @ w_up_b)
    return y @ w_out


def hierarchical_moe(h, w_group, b_group, w_router, b_router, w_gate, w_up, w_down):
    b, s_len, d = h.shape
    T = b * s_len
    xt = h.reshape(T, d)
    g_logits = jnp.matmul(xt, w_group, preferred_element_type=jnp.float32) + b_group
    g_w, g_idx = lax.top_k(jax.nn.softmax(g_logits, axis=-1), 1)
    e_logits = jnp.einsum('td,gde->tge', xt, w_router, preferred_element_type=jnp.float32) + b_router
    e_logits = jnp.take_along_axis(e_logits, g_idx[:, :, None], axis=1)[:, 0]
    e_val, e_idx = lax.top_k(e_logits, TOP_K_INNER)
    weights = g_w * jax.nn.softmax(e_val, axis=-1)
    expert = g_idx * EXPERTS_PER_GROUP + e_idx

    A = T * TOP_K_INNER
    flat_e = expert.reshape(A)
    flat_tok = jnp.arange(A, dtype=jnp.int32) // TOP_K_INNER
    order = jnp.argsort(flat_e)
    sorted_e = flat_e[order]
    counts = jnp.bincount(flat_e, length=N_EXPERTS)
    starts = jnp.cumsum(counts) - counts
    padded = (counts + MOE_BLOCK - 1) // MOE_BLOCK * MOE_BLOCK
    pad_ends = jnp.cumsum(padded)
    pad_starts = pad_ends - padded
    dest_sorted = pad_starts[sorted_e] + jnp.arange(A, dtype=jnp.int32) - starts[sorted_e]
    P = A + N_EXPERTS * MOE_BLOCK
    n_blk = P // MOE_BLOCK
    slot_tok = jnp.full((P,), T, jnp.int32).at[dest_sorted].set(flat_tok[order])
    blk_expert = jnp.minimum(jnp.searchsorted(pad_ends, jnp.arange(n_blk) * MOE_BLOCK, side='right'), N_EXPERTS - 1)
    x_pad = jnp.concatenate([xt, jnp.zeros((1, d), xt.dtype)], axis=0)

    def run_block(args):
        tok, e = args
        xb = x_pad[tok]
        hid = jax.nn.silu(xb @ w_gate[e]) * (xb @ w_up[e])
        return hid @ w_down[e]

    y_slots = lax.map(run_block, (slot_tok.reshape(n_blk, MOE_BLOCK), blk_expert)).reshape(P, d)
    dest = jnp.zeros((A,), jnp.int32).at[order].set(dest_sorted)
    y = jnp.einsum('tkd,tk->td', y_slots[dest].reshape(T, TOP_K_INNER, d), weights.astype(xt.dtype))
    return y.reshape(b, s_len, d)


def setup_inputs(seed: int = 0) -> dict:
    key = jax.random.key(seed)
    ks = jax.random.split(key, 32)
    nrm = lambda k, shape, s: jax.random.normal(k, shape, jnp.float32) * s
    L, D = DEPTH, D_MODEL
    return {
        "x": nrm(ks[0], (BATCH, SEQ, D), 1.0),
        "c": nrm(ks[1], (BATCH, D), 1.0),
        "positions": (jnp.arange(SEQ, dtype=jnp.int32)[None, :]
                      + jax.random.randint(ks[2], (BATCH, 1), 0, 1024, dtype=jnp.int32)),
        "w_ada": nrm(ks[3], (L, D, 6 * D), 0.5 * D ** -0.5),
        "b_ada": nrm(ks[4], (L, 6 * D), 0.02),
        "norm1_g": 1.0 + nrm(ks[5], (L, D), 0.02),
        "norm2_g": 1.0 + nrm(ks[6], (L, D), 0.02),
        "w_in": nrm(ks[7], (L, D, IN_COLS), D ** -0.5),
        "nsa_q_norm": 1.0 + nrm(ks[8], (L, HEAD_DIM), 0.02),
        "nsa_k_norm": 1.0 + nrm(ks[9], (L, HEAD_DIM), 0.02),
        "cmp_pe_k": nrm(ks[10], (L, CMP_BLOCK, HEAD_DIM), 0.02),
        "cmp_w1_k": nrm(ks[11], (L, CMP_BLOCK * HEAD_DIM, CMP_HIDDEN), (CMP_BLOCK * HEAD_DIM) ** -0.5),
        "cmp_w2_k": nrm(ks[12], (L, CMP_HIDDEN, HEAD_DIM), CMP_HIDDEN ** -0.5),
        "cmp_pe_v": nrm(ks[13], (L, CMP_BLOCK, HEAD_DIM), 0.02),
        "cmp_w1_v": nrm(ks[14], (L, CMP_BLOCK * HEAD_DIM, CMP_HIDDEN), (CMP_BLOCK * HEAD_DIM) ** -0.5),
        "cmp_w2_v": nrm(ks[15], (L, CMP_HIDDEN, HEAD_DIM), CMP_HIDDEN ** -0.5),
        "dil_q_norm": 1.0 + nrm(ks[16], (L, HEAD_DIM), 0.02),
        "dil_k_norm": 1.0 + nrm(ks[17], (L, HEAD_DIM), 0.02),
        "w_up_a": nrm(ks[18], (L, A_Q, D), A_Q ** -0.5),
        "w_up_b": nrm(ks[19], (L, DIL_W, D), DIL_W ** -0.5),
        "w_out": nrm(ks[20], (L, D, D), D ** -0.5),
        "w_group": nrm(ks[21], (L, D, N_GROUPS), D ** -0.5),
        "b_group": nrm(ks[22], (L, N_GROUPS), 0.01),
        "w_router": nrm(ks[23], (L, N_GROUPS, D, EXPERTS_PER_GROUP), D ** -0.5),
        "b_router": nrm(ks[24], (L, N_GROUPS, EXPERTS_PER_GROUP), 0.01),
        "w_e_gate": nrm(ks[25], (L, N_EXPERTS, D, EXPERT_FF), D ** -0.5),
        "w_e_up": nrm(ks[26], (L, N_EXPERTS, D, EXPERT_FF), D ** -0.5),
        "w_e_down": nrm(ks[27], (L, N_EXPERTS, EXPERT_FF, D), EXPERT_FF ** -0.5),
    }


def reference(x, c, positions, w_ada, b_ada, norm1_g, norm2_g, w_in, nsa_q_norm, nsa_k_norm,
              cmp_pe_k, cmp_w1_k, cmp_w2_k, cmp_pe_v, cmp_w1_v, cmp_w2_v, dil_q_norm, dil_k_norm,
              w_up_a, w_up_b, w_out, w_group, b_group, w_router, b_router, w_e_gate, w_e_up, w_e_down):
    c_act = jax.nn.silu(c)
    for l in range(DEPTH):
        mod = (c_act @ w_ada[l] + b_ada[l])[:, None, :]
        sh1, sc1, gt1, sh2, sc2, gt2 = jnp.split(mod, 6, axis=-1)
        h = rms_norm(x, norm1_g[l]) * (1.0 + sc1) + sh1
        x = x + gt1 * hybrid_mixer(h, positions, w_in[l], nsa_q_norm[l], nsa_k_norm[l],
                                   cmp_pe_k[l], cmp_w1_k[l], cmp_w2_k[l], cmp_pe_v[l], cmp_w1_v[l], cmp_w2_v[l],
                                   dil_q_norm[l], dil_k_norm[l], w_up_a[l], w_up_b[l], w_out[l])
        h = rms_norm(x, norm2_g[l]) * (1.0 + sc2) + sh2
        x = x + gt2 * hierarchical_moe(h, w_group[l], b_group[l], w_router[l], b_router[l],
                                       w_e_gate[l], w_e_up[l], w_e_down[l])
    return x
```

```python
import functools

import jax
import jax.numpy as jnp
import numpy as np
from jax import lax
from jax.experimental import pallas as pl
from jax.experimental.pallas import tpu as pltpu

F32 = jnp.float32
BF16 = jnp.bfloat16
I32 = jnp.int32

D_MODEL = 1024
HEAD_DIM = 64
LANES = 128
ROPE_THETA = 10000.0
EPS = 1e-6
NEG_INF = -1e30
FORCE_SCORE = 1e9
MASKED = -1e30
PICKED = -3e38

NSA_HEADS = 8
NSA_KV_HEADS = 2
NSA_GROUP = 4
CMP_BLOCK = 32
CMP_STRIDE = 16
CMP_HIDDEN = 256
SEL_BLOCK = 64
N_SEL = 8
N_LOCAL_SEL = 2
WINDOW = 512
DIL_PAIRS = ((128, 1), (512, 4), (2048, 16))
DIL_GROUPS = 3
A_Q = NSA_HEADS * HEAD_DIM
A_KV = NSA_KV_HEADS * HEAD_DIM
DIL_W = 2 * DIL_GROUPS * HEAD_DIM
N_GROUPS = 4
EXPERTS_PER_GROUP = 8
N_EXPERTS = 32
TOP_K = 2
EXPERT_FF = 512
MOE_BLOCK = 256

VMEM_LIMIT = 56 * 1024 * 1024

T_QA = 0
T_KVC = 4
T_KVA = 6
T_DIL = 10
T_GA = 19
T_GM = 20
N_TILES = 36

PROJ_TM = 256
NSA_TQ = 128
NSA_TK = 512
DIL_T = 128
MERGE_TM = 256
RANK_TM = 512
COMB_TM = 256


def _dot(a, b):
    return jnp.dot(a, b, preferred_element_type=F32)


def _dot_nt(a, b):
    return lax.dot_general(a, b, (((1,), (1,)), ((), ())), preferred_element_type=F32)


def _split(a):
    hi = a.astype(BF16)
    lo = (a - hi.astype(F32)).astype(BF16)
    return hi, lo


def _params(*sem):
    return pltpu.CompilerParams(dimension_semantics=sem, vmem_limit_bytes=VMEM_LIMIT)


def _mod_kernel(c_ref, whi_ref, wlo_ref, b_ref, o_ref):
    c = c_ref[...]
    ca = c * jax.nn.sigmoid(c)
    hi, lo = _split(ca)
    whi = whi_ref[...]
    o_ref[...] = _dot(hi, whi) + _dot(lo, whi) + _dot(hi, wlo_ref[...]) + b_ref[...]


def _mod_call(c, w_ada, b_ada):
    b, d = c.shape
    n = w_ada.shape[1]
    whi, wlo = _split(w_ada)
    tn = 1024
    return pl.pallas_call(
        _mod_kernel,
        grid=(n // tn,),
        in_specs=[
            pl.BlockSpec((b, d), lambda j: (0, 0)),
            pl.BlockSpec((d, tn), lambda j: (0, j)),
            pl.BlockSpec((d, tn), lambda j: (0, j)),
            pl.BlockSpec((1, tn), lambda j: (0, j)),
        ],
        out_specs=pl.BlockSpec((b, tn), lambda j: (0, j)),
        out_shape=jax.ShapeDtypeStruct((b, n), F32),
        compiler_params=_params("parallel"),
        name="mod",
    )(c, whi, wlo, b_ada.reshape(1, n))


def _norm_rope(a, gain, cos, sin_signed, lo, first):
    sq = a * a
    s0 = jnp.sum(jnp.where(lo, sq, 0.0), axis=-1, keepdims=True)
    s1 = jnp.sum(jnp.where(lo, 0.0, sq), axis=-1, keepdims=True)
    r = jnp.where(lo, lax.rsqrt(s0 * (1.0 / HEAD_DIM) + EPS), lax.rsqrt(s1 * (1.0 / HEAD_DIM) + EPS))
    y = a * r * gain
    rot = jnp.where(first, pltpu.roll(y, 96, 1), pltpu.roll(y, 32, 1))
    return y * cos + rot * sin_signed


def _proj_kernel(x_ref, mod_ref, g1_ref, w_ref, gains_ref, cos_ref, sin_ref,
                 qext_ref, kvc_ref, kva_ref, dil_ref, ga_ref, gm_ref):
    tm = x_ref.shape[1]
    x = x_ref[0]
    ms = jnp.mean(x * x, axis=-1, keepdims=True)
    y = x * lax.rsqrt(ms + EPS) * g1_ref[...]
    sh1 = mod_ref[0, 0:1, :]
    sc1 = mod_ref[0, 1:2, :]
    h = (y * (1.0 + sc1) + sh1).astype(BF16)
    cos = cos_ref[0]
    sin = sin_ref[0]
    lane = lax.broadcasted_iota(I32, (tm, LANES), 1)
    lo = lane < HEAD_DIM
    first = (lane & (HEAD_DIM - 1)) < (HEAD_DIM // 2)
    nr = functools.partial(_norm_rope, cos=cos, sin_signed=sin, lo=lo, first=first)

    for c in range(N_TILES // 2):
        acc = _dot(h, w_ref[:, c * 2 * LANES:(c + 1) * 2 * LANES])
        for half in range(2):
            t = 2 * c + half
            a = acc[:, half * LANES:(half + 1) * LANES]
            if t < T_KVC:
                yq = nr(a, gains_ref[0:1, :])
                rq = pltpu.roll(yq, HEAD_DIM, 1)
                if t // 2 == 0:
                    e0 = jnp.where(lo, yq, 0.0)
                    e1 = jnp.where(lo, rq, 0.0)
                else:
                    e0 = jnp.where(lo, 0.0, rq)
                    e1 = jnp.where(lo, 0.0, yq)
                qext_ref[0, 2 * t] = e0.astype(BF16)
                qext_ref[0, 2 * t + 1] = e1.astype(BF16)
            elif t < T_KVA:
                kvc_ref[0, :, (t - T_KVC) * LANES:(t - T_KVC + 1) * LANES] = a
            elif t < T_DIL:
                j = t - T_KVA
                v = nr(a, gains_ref[1:2, :]) if j % 2 == 0 else a
                kva_ref[0, :, j * LANES:(j + 1) * LANES] = v.astype(BF16)
            elif t < T_GA:
                j = t - T_DIL
                if j < 3:
                    v = nr(a, gains_ref[2:3, :])
                elif j < 6:
                    v = nr(a, gains_ref[3:4, :])
                else:
                    v = a
                dil_ref[0, :, j * LANES:(j + 1) * LANES] = v.astype(BF16)
            elif t < T_GM:
                ga_ref[0] = jax.nn.sigmoid(a)
            else:
                j = t - T_GM
                gm_ref[0, :, j * LANES:(j + 1) * LANES] = jax.nn.sigmoid(a).astype(BF16)


def _proj_call(x, mod6, g1, w_perm, gains, cos_t, sin_t):
    b, s, d = x.shape
    tm = PROJ_TM
    n = N_TILES * LANES
    row = lambda bi, i: (bi, i, 0)
    return pl.pallas_call(
        _proj_kernel,
        grid=(b, s // tm),
        in_specs=[
            pl.BlockSpec((1, tm, d), row),
            pl.BlockSpec((1, 6, d), lambda bi, i: (bi, 0, 0)),
            pl.BlockSpec((1, d), lambda bi, i: (0, 0)),
            pl.BlockSpec((d, n), lambda bi, i: (0, 0)),
            pl.BlockSpec((4, LANES), lambda bi, i: (0, 0)),
            pl.BlockSpec((1, tm, LANES), row),
            pl.BlockSpec((1, tm, LANES), row),
        ],
        out_specs=[
            pl.BlockSpec((1, NSA_HEADS, tm, LANES), lambda bi, i: (bi, 0, i, 0)),
            pl.BlockSpec((1, tm, 2 * LANES), row),
            pl.BlockSpec((1, tm, 4 * LANES), row),
            pl.BlockSpec((1, tm, 9 * LANES), row),
            pl.BlockSpec((1, tm, LANES), row),
            pl.BlockSpec((1, tm, 2 * d), row),
        ],
        out_shape=[
            jax.ShapeDtypeStruct((b, NSA_HEADS, s, LANES), BF16),
            jax.ShapeDtypeStruct((b, s, 2 * LANES), F32),
            jax.ShapeDtypeStruct((b, s, 4 * LANES), BF16),
            jax.ShapeDtypeStruct((b, s, 9 * LANES), BF16),
            jax.ShapeDtypeStruct((b, s, LANES), F32),
            jax.ShapeDtypeStruct((b, s, 2 * d), BF16),
        ],
        compiler_params=_params("parallel", "parallel"),
        name="proj",
    )(x, mod6, g1, w_perm, gains, cos_t, sin_t)


def _cmp_kernel(c_ref, pe_ref, w1_ref, w2_ref, gain_ref, cos_ref, sin_ref, kc_ref, vc_ref):
    nb = c_ref.shape[2]
    half = c_ref.shape[3]
    lane = lax.broadcasted_iota(I32, (nb, LANES), 1)
    lo = lane < HEAD_DIM
    first = (lane & (HEAD_DIM - 1)) < (HEAD_DIM // 2)
    for kind in range(2):
        out = jnp.zeros((nb, LANES), F32)
        for hd in range(2):
            c = c_ref[0, kind * 2 + hd]
            top = _dot((c + pe_ref[kind, 0:1, :]).astype(BF16), w1_ref[kind, 0:half, :])
            bot = _dot((c + pe_ref[kind, 1:2, :]).astype(BF16), w1_ref[kind, half:2 * half, :])
            hid = top + pltpu.roll(bot, nb - 1, 0)
            hid = hid * jax.nn.sigmoid(hid)
            out = out + _dot(hid.astype(BF16), w2_ref[kind, hd])
        if kind == 0:
            out = _norm_rope(out, gain_ref[...], cos_ref[0], sin_ref[0], lo, first)
            kc_ref[0] = out.astype(BF16)
        else:
            vc_ref[0] = out.astype(BF16)


def _cmp_call(chunks, pe, w1, w2ext, gain_k, ccos, csin):
    b, _, nb, half = chunks.shape
    return pl.pallas_call(
        _cmp_kernel,
        grid=(b,),
        in_specs=[
            pl.BlockSpec((1, 4, nb, half), lambda bi: (bi, 0, 0, 0)),
            pl.BlockSpec((2, 2, half), lambda bi: (0, 0, 0)),
            pl.BlockSpec((2, 2 * half, CMP_HIDDEN), lambda bi: (0, 0, 0)),
            pl.BlockSpec((2, 2, CMP_HIDDEN, LANES), lambda bi: (0, 0, 0, 0)),
            pl.BlockSpec((1, LANES), lambda bi: (0, 0)),
            pl.BlockSpec((1, nb, LANES), lambda bi: (bi, 0, 0)),
            pl.BlockSpec((1, nb, LANES), lambda bi: (bi, 0, 0)),
        ],
        out_specs=[
            pl.BlockSpec((1, nb, LANES), lambda bi: (bi, 0, 0)),
            pl.BlockSpec((1, nb, LANES), lambda bi: (bi, 0, 0)),
        ],
        out_shape=[
            jax.ShapeDtypeStruct((b, nb, LANES), BF16),
            jax.ShapeDtypeStruct((b, nb, LANES), BF16),
        ],
        compiler_params=_params("parallel"),
        name="cmp",
    )(chunks, pe, w1, w2ext, gain_k, ccos, csin)


def _nsa_kernel(q_ref, kc_ref, vc_ref, ksl_ref, vsl_ref, kw_ref, vw_ref, ga_ref, ov_ref, e_ref,
                o_ref, m_scr, l_scr, acc_scr):
    tq = NSA_TQ
    tk = NSA_TK
    rows = NSA_GROUP * tq
    kh = pl.program_id(1)
    t0 = pl.program_id(2) * tq
    q4 = q_ref[0].reshape(rows, LANES)
    tpos = t0 + lax.broadcasted_iota(I32, (tq, 1), 0)
    tpos4 = t0 + (lax.broadcasted_iota(I32, (rows, 1), 0) & (tq - 1))
    lane = lax.broadcasted_iota(I32, (1, LANES), 1)

    s = _dot_nt(q4, kc_ref[0])
    valid = (lane * CMP_STRIDE + (CMP_BLOCK - 1)) <= tpos4
    sm = jnp.where(valid, s, NEG_INF)
    e = jnp.exp(sm - jnp.max(sm, axis=-1, keepdims=True))
    p = jnp.where(valid, e / jnp.sum(e, axis=-1, keepdims=True), 0.0)
    o_cmp = _dot(p.astype(BF16), vc_ref[0])

    psum = p[0:tq] + p[tq:2 * tq] + p[2 * tq:3 * tq] + p[3 * tq:4 * tq]
    p_hi, p_lo = _split(psum)
    ov = ov_ref[...]
    imp = _dot(p_hi, ov) + _dot(p_lo, ov)
    rel = (tpos >> 6) - lane
    forced = (lane == 0) | ((rel >= 0) & (rel < N_LOCAL_SEL))
    score = jnp.where(rel < 0, NEG_INF, jnp.where(forced, FORCE_SCORE, imp))
    score = jnp.where(lane < (ov_ref.shape[1] // 4), score, PICKED)
    sel = jnp.zeros((tq, LANES), F32)
    for _ in range(N_SEL):
        best = jnp.max(score, axis=-1, keepdims=True)
        idx = jnp.min(jnp.where(score == best, lane, LANES), axis=-1, keepdims=True)
        pick = lane == idx
        sel = jnp.where(pick, 1.0, sel)
        score = jnp.where(pick, PICKED, score)
    selb = sel.astype(BF16)

    m_scr[...] = jnp.full(m_scr.shape, MASKED, F32)
    l_scr[...] = jnp.zeros(l_scr.shape, F32)
    acc_scr[...] = jnp.zeros(acc_scr.shape, F32)
    kiota = lax.broadcasted_iota(I32, (1, tk), 1)

    def body(c, carry):
        ks = pl.multiple_of(c * tk, tk)
        kb = ksl_ref[0, pl.ds(ks, tk), :]
        vb = vsl_ref[0, pl.ds(ks, tk), :]
        sc = _dot_nt(q4, kb)
        se = _dot(selb, e_ref[c])
        ok = (se > 0.5) & ((ks + kiota) <= tpos)
        bias = jnp.where(ok, 0.0, MASKED)
        sc = (sc.reshape(NSA_GROUP, tq, tk) + bias[None]).reshape(rows, tk)
        m_old = m_scr[...]
        m_new = jnp.maximum(m_old, jnp.max(sc, axis=-1, keepdims=True))
        alpha = jnp.exp(m_old - m_new)
        pe = jnp.exp(sc - m_new)
        l_scr[...] = alpha * l_scr[...] + jnp.sum(pe, axis=-1, keepdims=True)
        acc_scr[...] = alpha * acc_scr[...] + _dot(pe.astype(BF16), vb)
        m_scr[...] = m_new
        return carry

    lax.fori_loop(0, (t0 + tq - 1) // tk + 1, body, 0)
    o_slc = acc_scr[...] / l_scr[...]

    nwin = WINDOW + tq
    ws = pl.multiple_of(jnp.maximum(t0 - WINDOW, 0), tq)
    kw = kw_ref[0, pl.ds(ws, nwin), :]
    vw = vw_ref[0, pl.ds(ws, nwin), :]
    sw = _dot_nt(q4, kw)
    dist = tpos4 - (ws + lax.broadcasted_iota(I32, (1, nwin), 1))
    sw = jnp.where((dist >= 0) & (dist <= WINDOW), sw, MASKED)
    pw = jnp.exp(sw - jnp.max(sw, axis=-1, keepdims=True))
    o_win = _dot(pw.astype(BF16), vw) / jnp.sum(pw, axis=-1, keepdims=True)

    ga = ga_ref[0]
    lo = lax.broadcasted_iota(I32, (tq, LANES), 1) < HEAD_DIM
    is0 = kh == 0
    heads = []
    for g in range(NSA_GROUP):
        r = slice(g * tq, (g + 1) * tq)
        og = jnp.zeros((tq, LANES), F32)
        for gi, ob in enumerate((o_cmp, o_slc, o_win)):
            c0 = gi * NSA_HEADS + g
            c1 = c0 + NSA_GROUP
            gate = jnp.where(is0, ga[:, c0:c0 + 1], ga[:, c1:c1 + 1])
            og = og + gate * ob[r]
        heads.append(og)
    for pair in range(2):
        a, bb = heads[2 * pair], heads[2 * pair + 1]
        ra = pltpu.roll(a, HEAD_DIM, 1)
        rb = pltpu.roll(bb, HEAD_DIM, 1)
        tile = jnp.where(lo, jnp.where(is0, a, ra), jnp.where(is0, rb, bb))
        o_ref[0, :, pair * LANES:(pair + 1) * LANES] = tile.astype(BF16)


def _nsa_call(qext, kc, vc, kva, ga, ov, e3):
    b, _, s, _ = qext.shape
    tq = NSA_TQ
    nb = kc.shape[1]
    full = lambda j: pl.BlockSpec((1, s, LANES), lambda bi, k, i: (bi, 0, j))
    return pl.pallas_call(
        _nsa_kernel,
        grid=(b, NSA_KV_HEADS, s // tq),
        in_specs=[
            pl.BlockSpec((1, NSA_GROUP, tq, LANES), lambda bi, k, i: (bi, k, i, 0)),
            pl.BlockSpec((1, nb, LANES), lambda bi, k, i: (bi, 0, 0)),
            pl.BlockSpec((1, nb, LANES), lambda bi, k, i: (bi, 0, 0)),
            full(0), full(1), full(2), full(3),
            pl.BlockSpec((1, tq, LANES), lambda bi, k, i: (bi, i, 0)),
            pl.BlockSpec(ov.shape, lambda bi, k, i: (0, 0)),
            pl.BlockSpec(e3.shape, lambda bi, k, i: (0, 0, 0)),
        ],
        out_specs=pl.BlockSpec((1, tq, 2 * LANES), lambda bi, k, i: (bi, i, k)),
        out_shape=jax.ShapeDtypeStruct((b, s, A_Q), BF16),
        scratch_shapes=[
            pltpu.VMEM((NSA_GROUP * tq, 1), F32),
            pltpu.VMEM((NSA_GROUP * tq, 1), F32),
            pltpu.VMEM((NSA_GROUP * tq, LANES), F32),
        ],
        compiler_params=_params("parallel", "parallel", "parallel"),
        name="nsa",
    )(qext, kc, vc, kva, kva, kva, kva, ga, ov, e3)


def _dil_kernel(q_ref, k_ref, v_ref, o_ref, lse_ref):
    t = DIL_T
    g = pl.program_id(1)
    i = pl.program_id(2)
    seg_tiles = jnp.where(g == 0, 16, jnp.where(g == 1, 4, 1))
    has_prev = (i & (seg_tiles - 1)) != 0
    q = q_ref[0, 0]
    lo = lax.broadcasted_iota(I32, (t, LANES), 1) < HEAD_DIM
    zero = jnp.zeros_like(q)
    q2 = jnp.concatenate([jnp.where(lo, q, zero), jnp.where(lo, zero, q)], axis=0)
    ps = pl.multiple_of(jnp.maximum(i - 1, 0) * t, t)
    cs = pl.multiple_of(i * t, t)
    kp = k_ref[0, 0, pl.ds(ps, t), :]
    vp = v_ref[0, 0, pl.ds(ps, t), :]
    kc = k_ref[0, 0, pl.ds(cs, t), :]
    vc = v_ref[0, 0, pl.ds(cs, t), :]
    qi = lax.broadcasted_iota(I32, (2 * t, t), 0) & (t - 1)
    kj = lax.broadcasted_iota(I32, (2 * t, t), 1)
    sp = jnp.where((kj >= qi) & has_prev, _dot_nt(q2, kp), MASKED)
    sc = jnp.where(qi >= kj, _dot_nt(q2, kc), MASKED)
    m = jnp.maximum(jnp.max(sp, axis=-1, keepdims=True), jnp.max(sc, axis=-1, keepdims=True))
    pp = jnp.exp(sp - m)
    pc = jnp.exp(sc - m)
    l = jnp.sum(pp, axis=-1, keepdims=True) + jnp.sum(pc, axis=-1, keepdims=True)
    o = (_dot(pp.astype(BF16), vp) + _dot(pc.astype(BF16), vc)) / l
    lse = m + jnp.log(l)
    o_ref[0, 0] = jnp.where(lo, o[0:t], o[t:2 * t]).astype(BF16)
    lse_ref[0, 0] = jnp.where(lo, lse[0:t], lse[t:2 * t])


def _dil_call(dil_in):
    b, _, s, _ = dil_in.shape
    t = DIL_T
    return pl.pallas_call(
        _dil_kernel,
        grid=(b, DIL_GROUPS, s // t),
        in_specs=[
            pl.BlockSpec((1, 1, t, LANES), lambda bi, g, i: (bi, g, i, 0)),
            pl.BlockSpec((1, 1, s, LANES), lambda bi, g, i: (bi, DIL_GROUPS + g, 0, 0)),
            pl.BlockSpec((1, 1, s, LANES), lambda bi, g, i: (bi, 2 * DIL_GROUPS + g, 0, 0)),
        ],
        out_specs=[
            pl.BlockSpec((1, 1, t, LANES), lambda bi, g, i: (bi, g, i, 0)),
            pl.BlockSpec((1, 1, t, LANES), lambda bi, g, i: (bi, g, i, 0)),
        ],
        out_shape=[
            jax.ShapeDtypeStruct((b, DIL_GROUPS, s, LANES), BF16),
            jax.ShapeDtypeStruct((b, DIL_GROUPS, s, LANES), F32),
        ],
        compiler_params=_params("parallel", "parallel", "parallel"),
        name="dil",
    )(dil_in, dil_in, dil_in)


def _merge_kernel(x_ref, oa_ref, od_ref, lse_ref, gm_ref, mod_ref, wa_ref, wb_ref, wo_ref, g2_ref,
                  rhi_ref, rlo_ref, rb_ref, x1_ref, h2_ref, route_ref):
    tm = x_ref.shape[0]
    d = x_ref.shape[1]
    l0, l1, l2 = lse_ref[0], lse_ref[1], lse_ref[2]
    mx = jnp.maximum(jnp.maximum(l0, l1), l2)
    e0, e1, e2 = jnp.exp(l0 - mx), jnp.exp(l1 - mx), jnp.exp(l2 - mx)
    den = e0 + e1 + e2
    ob = jnp.concatenate([
        (od_ref[0].astype(F32) * (e0 / den)).astype(BF16),
        (od_ref[1].astype(F32) * (e1 / den)).astype(BF16),
        (od_ref[2].astype(F32) * (e2 / den)).astype(BF16)], axis=1)
    ya = _dot(oa_ref[...], wa_ref[...])
    yb = _dot(ob, wb_ref[...])
    y = gm_ref[:, 0:d].astype(F32) * ya + gm_ref[:, d:2 * d].astype(F32) * yb
    z = _dot(y.astype(BF16), wo_ref[...])
    x1 = x_ref[...] + mod_ref[0, 2:3, :] * z
    x1_ref[...] = x1

    ms = jnp.mean(x1 * x1, axis=-1, keepdims=True)
    h2 = x1 * lax.rsqrt(ms + EPS) * g2_ref[...]
    h2 = h2 * (1.0 + mod_ref[0, 4:5, :]) + mod_ref[0, 3:4, :]
    h2_ref[...] = h2

    hi, lo = _split(h2)
    rhi = rhi_ref[...]
    logits = _dot(hi, rhi) + _dot(lo, rhi) + _dot(hi, rlo_ref[...]) + rb_ref[...]
    lane = lax.broadcasted_iota(I32, (tm, LANES), 1)
    gl = jnp.where(lane < N_GROUPS, logits, NEG_INF)
    gmax = jnp.max(gl, axis=-1, keepdims=True)
    g_w = 1.0 / jnp.sum(jnp.exp(gl - gmax), axis=-1, keepdims=True)
    g_idx = jnp.min(jnp.where(gl == gmax, lane, LANES), axis=-1, keepdims=True)
    e_lo = N_GROUPS + EXPERTS_PER_GROUP * g_idx
    el = jnp.where((lane >= e_lo) & (lane < e_lo + EXPERTS_PER_GROUP), logits, NEG_INF)
    m1 = jnp.max(el, axis=-1, keepdims=True)
    i1 = jnp.min(jnp.where(el == m1, lane, LANES), axis=-1, keepdims=True)
    el2 = jnp.where(lane == i1, PICKED, el)
    m2 = jnp.max(el2, axis=-1, keepdims=True)
    i2 = jnp.min(jnp.where(el2 == m2, lane, LANES), axis=-1, keepdims=True)
    ex = jnp.exp(m2 - m1)
    w1 = g_w * (1.0 / (1.0 + ex))
    w2 = g_w * (ex / (1.0 + ex))
    route = jnp.where(lane == 0, (i1 - N_GROUPS).astype(F32),
                      jnp.where(lane == 1, (i2 - N_GROUPS).astype(F32),
                                jnp.where(lane == 2, w1, jnp.where(lane == 3, w2, 0.0))))
    route_ref[...] = route


def _merge_call(x2, oa, od, lse, gm, mod6, wa, wb, wo, g2, rhi, rlo, rb, s):
    t, d = x2.shape
    tm = MERGE_TM
    per_b = s // tm
    row = lambda i: (i, 0)
    const = lambda i: (0, 0)
    return pl.pallas_call(
        _merge_kernel,
        grid=(t // tm,),
        in_specs=[
            pl.BlockSpec((tm, d), row),
            pl.BlockSpec((tm, A_Q), row),
            pl.BlockSpec((DIL_GROUPS, tm, LANES), lambda i: (0, i, 0)),
            pl.BlockSpec((DIL_GROUPS, tm, LANES), lambda i: (0, i, 0)),
            pl.BlockSpec((tm, 2 * d), row),
            pl.BlockSpec((1, 6, d), lambda i: (i // per_b, 0, 0)),
            pl.BlockSpec(wa.shape, const),
            pl.BlockSpec(wb.shape, const),
            pl.BlockSpec(wo.shape, const),
            pl.BlockSpec((1, d), const),
            pl.BlockSpec(rhi.shape, const),
            pl.BlockSpec(rlo.shape, const),
            pl.BlockSpec((1, LANES), const),
        ],
        out_specs=[
            pl.BlockSpec((tm, d), row),
            pl.BlockSpec((tm, d), row),
            pl.BlockSpec((tm, LANES), row),
        ],
        out_shape=[
            jax.ShapeDtypeStruct((t, d), F32),
            jax.ShapeDtypeStruct((t, d), F32),
            jax.ShapeDtypeStruct((t, LANES), F32),
        ],
        compiler_params=_params("parallel"),
        name="merge",
    )(x2, oa, od, lse, gm, mod6, wa, wb, wo, g2, rhi, rlo, rb)


def _rank_kernel(route_ref, rank_ref, count_ref, carry_scr):
    tm = route_ref.shape[0]

    @pl.when(pl.program_id(0) == 0)
    def _():
        carry_scr[...] = jnp.zeros(carry_scr.shape, F32)

    route = route_ref[...]
    lane = lax.broadcasted_iota(I32, (tm, LANES), 1)
    e1 = route[:, 0:1].astype(I32)
    e2 = route[:, 1:2].astype(I32)
    hit1 = lane == e1
    hit2 = lane == e2
    cnt = jnp.where(hit1 | hit2, 1.0, 0.0)
    r = lax.broadcasted_iota(I32, (tm, tm), 0)
    c = lax.broadcasted_iota(I32, (tm, tm), 1)
    below = jnp.where(c < r, 1.0, 0.0).astype(BF16)
    before = _dot(below, cnt.astype(BF16)) + carry_scr[...]
    r1 = jnp.sum(jnp.where(hit1, before, 0.0), axis=-1, keepdims=True)
    r2 = jnp.sum(jnp.where(hit2, before, 0.0), axis=-1, keepdims=True)
    rank_ref[...] = jnp.where(lane == 0, r1, jnp.where(lane == 1, r2, 0.0))
    carry_scr[...] = carry_scr[...] + jnp.sum(cnt, axis=0, keepdims=True)
    count_ref[...] = carry_scr[...]


def _rank_call(route):
    t = route.shape[0]
    tm = RANK_TM
    return pl.pallas_call(
        _rank_kernel,
        grid=(t // tm,),
        in_specs=[pl.BlockSpec((tm, LANES), lambda i: (i, 0))],
        out_specs=[
            pl.BlockSpec((tm, LANES), lambda i: (i, 0)),
            pl.BlockSpec((1, LANES), lambda i: (0, 0)),
        ],
        out_shape=[
            jax.ShapeDtypeStruct((t, LANES), F32),
            jax.ShapeDtypeStruct((1, LANES), F32),
        ],
        scratch_shapes=[pltpu.VMEM((1, LANES), F32)],
        compiler_params=_params("arbitrary"),
        name="rank",
    )(route)


def _row_gather(idx_hbm, src_hbm, idx_smem, buf, isem, dsem, n_rows):
    i = pl.program_id(0)
    n = pl.num_programs(0)
    slot = i % 2
    nxt = 1 - slot

    def idx_copy(step, sl):
        return pltpu.make_async_copy(idx_hbm.at[step], idx_smem.at[sl], isem.at[sl])

    def row_copy(row, r, sl):
        return pltpu.make_async_copy(src_hbm.at[pl.ds(row, 1)], buf.at[sl, pl.ds(r, 1)], dsem.at[sl])

    def issue_rows(sl):
        def one(r, carry):
            row_copy(idx_smem[sl, r], r, sl).start()
            return carry
        lax.fori_loop(0, n_rows, one, 0)

    def wait_rows(sl):
        def one(r, carry):
            row_copy(0, r, sl).wait()
            return carry
        lax.fori_loop(0, n_rows, one, 0)

    @pl.when(i == 0)
    def _():
        idx_copy(0, 0).start()
        idx_copy(0, 0).wait()
        issue_rows(0)

        @pl.when(n > 1)
        def _():
            idx_copy(1, 1).start()

    @pl.when(i + 1 < n)
    def _():
        idx_copy(i + 1, nxt).wait()
        issue_rows(nxt)

    @pl.when(i + 2 < n)
    def _():
        idx_copy(i + 2, slot).start()

    wait_rows(slot)
    return slot


def _expert_kernel(be_ref, tok_hbm, h_hbm, wg_ref, wu_ref, wd_ref, y_ref, idx_smem, xbuf, isem, dsem):
    del be_ref
    slot = _row_gather(tok_hbm, h_hbm, idx_smem, xbuf, isem, dsem, MOE_BLOCK)
    xb = xbuf[slot].astype(BF16)
    gate = _dot(xb, wg_ref[0])
    up = _dot(xb, wu_ref[0])
    hid = (gate * jax.nn.sigmoid(gate) * up).astype(BF16)
    y_ref[...] = _dot(hid, wd_ref[0])


def _expert_call(blk_expert, slot_tok, h2, wg, wu, wd):
    n_blk = slot_tok.shape[0]
    d = h2.shape[1]
    grid_spec = pltpu.PrefetchScalarGridSpec(
        num_scalar_prefetch=1,
        grid=(n_blk,),
        in_specs=[
            pl.BlockSpec(memory_space=pl.ANY),
            pl.BlockSpec(memory_space=pl.ANY),
            pl.BlockSpec((1, d, EXPERT_FF), lambda i, be: (be[i], 0, 0)),
            pl.BlockSpec((1, d, EXPERT_FF), lambda i, be: (be[i], 0, 0)),
            pl.BlockSpec((1, EXPERT_FF, d), lambda i, be: (be[i], 0, 0)),
        ],
        out_specs=pl.BlockSpec((MOE_BLOCK, d), lambda i, be: (i, 0)),
        scratch_shapes=[
            pltpu.SMEM((2, MOE_BLOCK), I32),
            pltpu.VMEM((2, MOE_BLOCK, d), F32),
            pltpu.SemaphoreType.DMA((2,)),
            pltpu.SemaphoreType.DMA((2,)),
        ],
    )
    return pl.pallas_call(
        _expert_kernel,
        grid_spec=grid_spec,
        out_shape=jax.ShapeDtypeStruct((n_blk * MOE_BLOCK, d), F32),
        compiler_params=_params("arbitrary"),
        name="experts",
    )(blk_expert, slot_tok, h2, wg, wu, wd)


def _combine_kernel(dest_hbm, y_hbm, x1_ref, route_ref, mod_ref, o_ref, idx_smem, ybuf, isem, dsem):
    tm = x1_ref.shape[0]
    slot = _row_gather(dest_hbm, y_hbm, idx_smem, ybuf, isem, dsem, TOP_K * tm)
    route = route_ref[...]
    y = route[:, 2:3] * ybuf[slot, 0:tm, :] + route[:, 3:4] * ybuf[slot, tm:2 * tm, :]
    o_ref[...] = x1_ref[...] + mod_ref[0, 5:6, :] * y


def _combine_call(dest_tiles, y_slots, x1, route, mod6, s):
    t, d = x1.shape
    tm = COMB_TM
    per_b = s // tm
    return pl.pallas_call(
        _combine_kernel,
        grid=(t // tm,),
        in_specs=[
            pl.BlockSpec(memory_space=pl.ANY),
            pl.BlockSpec(memory_space=pl.ANY),
            pl.BlockSpec((tm, d), lambda i: (i, 0)),
            pl.BlockSpec((tm, LANES), lambda i: (i, 0)),
            pl.BlockSpec((1, 6, d), lambda i: (i // per_b, 0, 0)),
        ],
        out_specs=pl.BlockSpec((tm, d), lambda i: (i, 0)),
        out_shape=jax.ShapeDtypeStruct((t, d), F32),
        scratch_shapes=[
            pltpu.SMEM((2, TOP_K * tm), I32),
            pltpu.VMEM((2, TOP_K * tm, d), F32),
            pltpu.SemaphoreType.DMA((2,)),
            pltpu.SemaphoreType.DMA((2,)),
        ],
        compiler_params=_params("arbitrary"),
        name="combine",
    )(dest_tiles, y_slots, x1, route, mod6)


def _rope_tiles(pos):
    inv = ROPE_THETA ** (-jnp.arange(0, HEAD_DIM, 2, dtype=F32) / HEAD_DIM)
    ang = pos.astype(F32)[..., None] * inv
    cos, sin = jnp.cos(ang), jnp.sin(ang)
    return (jnp.concatenate([cos, cos, cos, cos], axis=-1),
            jnp.concatenate([-sin, sin, -sin, sin], axis=-1))


def _selection_constants(s):
    n_c = s // CMP_STRIDE
    n_s = s // SEL_BLOCK
    cs = np.arange(n_c) * CMP_STRIDE
    ss = np.arange(n_s) * SEL_BLOCK
    ov = np.clip(np.minimum(cs[:, None] + CMP_BLOCK, ss[None, :] + SEL_BLOCK)
                 - np.maximum(cs[:, None], ss[None, :]), 0, None).astype(np.float32) / CMP_BLOCK
    ov_pad = np.zeros((n_c, LANES), np.float32)
    ov_pad[:, :n_s] = ov
    key_blk = np.arange(s) // SEL_BLOCK
    e = (np.arange(LANES)[:, None] == key_blk[None, :]).astype(np.float32)
    e3 = e.reshape(LANES, s // NSA_TK, NSA_TK).transpose(1, 0, 2)
    return jnp.asarray(ov_pad, BF16), jnp.asarray(e3, BF16)


def _to_residue_major(t, dil):
    b, s, w = t.shape
    return t.reshape(b, s // dil, dil, w).transpose(0, 2, 1, 3).reshape(b, s, w)


def _from_residue_major(t, dil):
    b, s, w = t.shape
    return t.reshape(b, dil, s // dil, w).transpose(0, 2, 1, 3).reshape(b, s, w)


def _mixer_and_router(x, c, positions, w_ada, b_ada, norm1_g, norm2_g, w_in, nsa_q_norm, nsa_k_norm,
                      cmp_pe_k, cmp_w1_k, cmp_w2_k, cmp_pe_v, cmp_w1_v, cmp_w2_v, dil_q_norm,
                      dil_k_norm, w_up_a, w_up_b, w_out, w_group, b_group, w_router, b_router):
    b, s, d = x.shape
    scale = HEAD_DIM ** -0.5
    mod6 = _mod_call(c, w_ada, b_ada).reshape(b, 6, d)

    c1 = A_Q
    c2 = c1 + 6 * A_KV
    c3 = c2 + 3 * NSA_HEADS
    c4 = c3 + 3 * DIL_W
    w_perm = jnp.concatenate([
        w_in[:, :c2], w_in[:, c3:c4],
        jnp.pad(w_in[:, c2:c3], ((0, 0), (0, LANES - 3 * NSA_HEADS))), w_in[:, c4:]], axis=1).astype(BF16)
    two = lambda g: jnp.concatenate([g, g]).astype(F32)
    gains = jnp.stack([two(nsa_q_norm) * scale, two(nsa_k_norm), two(dil_q_norm) * scale, two(dil_k_norm)])
    cos_t, sin_t = _rope_tiles(positions)
    qext, kvc, kva, dil, ga, gm = _proj_call(x, mod6, norm1_g.reshape(1, d), w_perm, gains, cos_t, sin_t)

    n_chunk = s // CMP_STRIDE
    half = CMP_STRIDE * HEAD_DIM
    chunks = kvc.reshape(b, s, 4, HEAD_DIM).transpose(0, 2, 1, 3).reshape(b, 4, n_chunk, half)
    pe = jnp.stack([cmp_pe_k.reshape(2, half), cmp_pe_v.reshape(2, half)])
    w1 = jnp.stack([cmp_w1_k, cmp_w1_v]).astype(BF16)
    zeros = jnp.zeros((CMP_HIDDEN, HEAD_DIM), F32)
    ext = lambda w: jnp.stack([jnp.concatenate([w, zeros], 1), jnp.concatenate([zeros, w], 1)])
    w2ext = jnp.stack([ext(cmp_w2_k), ext(cmp_w2_v)]).astype(BF16)
    cmp_pos = jnp.pad(positions[:, CMP_BLOCK - 1::CMP_STRIDE], ((0, 0), (0, 1)))
    ccos, csin = _rope_tiles(cmp_pos)
    kc, vc = _cmp_call(chunks, pe, w1, w2ext, two(nsa_k_norm).reshape(1, LANES), ccos, csin)

    ov, e3 = _selection_constants(s)
    o_a = _nsa_call(qext, kc, vc, kva, ga, ov, e3)

    parts = []
    for kind in range(3):
        for gi, (_, dl) in enumerate(DIL_PAIRS):
            col = (kind * DIL_GROUPS + gi) * LANES
            parts.append(_to_residue_major(dil[:, :, col:col + LANES], dl))
    o_dil, lse_dil = _dil_call(jnp.stack(parts, axis=1))
    od = jnp.stack([_from_residue_major(o_dil[:, gi], dl) for gi, (_, dl) in enumerate(DIL_PAIRS)])
    lse = jnp.stack([_from_residue_major(lse_dil[:, gi], dl) for gi, (_, dl) in enumerate(DIL_PAIRS)])

    t = b * s
    w_r = jnp.concatenate([w_group, w_router.transpose(1, 0, 2).reshape(d, N_EXPERTS)], axis=1)
    w_r = jnp.pad(w_r, ((0, 0), (0, LANES - w_r.shape[1])))
    rhi, rlo = _split(w_r)
    rb = jnp.pad(jnp.concatenate([b_group, b_router.reshape(-1)]), (0, LANES - N_GROUPS - N_EXPERTS))
    return _merge_call(
        x.reshape(t, d), o_a.reshape(t, A_Q), od.reshape(DIL_GROUPS, t, LANES),
        lse.reshape(DIL_GROUPS, t, LANES), gm.reshape(t, 2 * d), mod6,
        w_up_a.astype(BF16), w_up_b.astype(BF16), w_out.astype(BF16), norm2_g.reshape(1, d),
        rhi, rlo, rb.reshape(1, LANES).astype(F32), s), mod6


def _moe(x1, h2, route, mod6, w_e_gate, w_e_up, w_e_down, s):
    t, d = x1.shape
    rank, counts = _rank_call(route)
    expert = route[:, 0:TOP_K].astype(I32)
    counts = counts[0, :N_EXPERTS].astype(I32)
    padded = (counts + MOE_BLOCK - 1) // MOE_BLOCK * MOE_BLOCK
    pad_ends = jnp.cumsum(padded)
    pad_starts = pad_ends - padded
    dest = pad_starts[expert] + rank[:, 0:TOP_K].astype(I32)
    n_slots = t * TOP_K + N_EXPERTS * MOE_BLOCK
    n_blk = n_slots // MOE_BLOCK
    tok = jnp.broadcast_to(jnp.arange(t, dtype=I32)[:, None], (t, TOP_K))
    slot_tok = jnp.zeros((n_slots,), I32).at[dest.reshape(-1)].set(tok.reshape(-1))
    blk_expert = jnp.minimum(
        jnp.searchsorted(pad_ends, jnp.arange(n_blk, dtype=I32) * MOE_BLOCK, side='right'),
        N_EXPERTS - 1).astype(I32)
    y_slots = _expert_call(blk_expert, slot_tok.reshape(n_blk, MOE_BLOCK), h2,
                           w_e_gate.astype(BF16), w_e_up.astype(BF16), w_e_down.astype(BF16))
    tm = COMB_TM
    dest_tiles = dest.reshape(t // tm, tm, TOP_K).transpose(0, 2, 1).reshape(t // tm, TOP_K * tm)
    return _combine_call(dest_tiles, y_slots, x1, route, mod6, s)


def kernel(x, c, positions, w_ada, b_ada, norm1_g, norm2_g, w_in, nsa_q_norm, nsa_k_norm, cmp_pe_k,
           cmp_w1_k, cmp_w2_k, cmp_pe_v, cmp_w1_v, cmp_w2_v, dil_q_norm, dil_k_norm, w_up_a, w_up_b,
           w_out, w_group, b_group, w_router, b_router, w_e_gate, w_e_up, w_e_down):
    b, s, d = x.shape
    assert w_ada.shape[0] == 1 and d == D_MODEL and s % NSA_TK == 0
    (x1, h2, route), mod6 = _mixer_and_router(
        x, c, positions, w_ada[0], b_ada[0], norm1_g[0], norm2_g[0], w_in[0], nsa_q_norm[0],
        nsa_k_norm[0], cmp_pe_k[0], cmp_w1_k[0], cmp_w2_k[0], cmp_pe_v[0], cmp_w1_v[0], cmp_w2_v[0],
        dil_q_norm[0], dil_k_norm[0], w_up_a[0], w_up_b[0], w_out[0], w_group[0], b_group[0],
        w_router[0], b_router[0])
    out = _moe(x1, h2, route, mod6, w_e_gate[0], w_e_up[0], w_e_down[0], s)
    return out.reshape(b, s, d)
```

```python
import functools

import jax
import jax.numpy as jnp
import numpy as np
from jax import lax
from jax.experimental import pallas as pl
from jax.experimental.pallas import tpu as pltpu

F32 = jnp.float32
BF16 = jnp.bfloat16
I32 = jnp.int32

D_MODEL = 1024
HEAD_DIM = 64
LANES = 128
ROW_TILE = (D_MODEL // LANES, LANES)
ROPE_THETA = 10000.0
EPS = 1e-6
NEG_INF = -1e30
FORCE_SCORE = 1e9
MASKED = -1e30
PICKED = -3e38

NSA_HEADS = 8
NSA_KV_HEADS = 2
NSA_GROUP = 4
CMP_BLOCK = 32
CMP_STRIDE = 16
CMP_HIDDEN = 256
SEL_BLOCK = 64
N_SEL = 8
N_LOCAL_SEL = 2
WINDOW = 512
DIL_PAIRS = ((128, 1), (512, 4), (2048, 16))
DIL_GROUPS = 3
A_Q = NSA_HEADS * HEAD_DIM
A_KV = NSA_KV_HEADS * HEAD_DIM
DIL_W = 2 * DIL_GROUPS * HEAD_DIM
N_GROUPS = 4
EXPERTS_PER_GROUP = 8
N_EXPERTS = 32
TOP_K = 2
EXPERT_FF = 512
MOE_BLOCK = 256

VMEM_LIMIT = 56 * 1024 * 1024

T_QA = 0
T_KVC = 4
T_KVA = 6
T_DIL = 10
T_GA = 19
T_GM = 20
N_TILES = 36

PROJ_TM = 256
NSA_TQ = 128
NSA_TK = 512
DIL_T = 128
MERGE_TM = 256
RANK_TM = 512
MOE_TM = 256
DMA_UNROLL = 8


def _dot(a, b):
    return jnp.dot(a, b, preferred_element_type=F32)


def _dot_nt(a, b):
    return lax.dot_general(a, b, (((1,), (1,)), ((), ())), preferred_element_type=F32)


def _dot_tn(a, b):
    return lax.dot_general(a, b, (((0,), (0,)), ((), ())), preferred_element_type=F32)


def _split(a):
    hi = a.astype(BF16)
    lo = (a - hi.astype(F32)).astype(BF16)
    return hi, lo


def _params(*sem):
    return pltpu.CompilerParams(dimension_semantics=sem, vmem_limit_bytes=VMEM_LIMIT)


def _mod_kernel(c_ref, whi_ref, wlo_ref, b_ref, o_ref):
    c = c_ref[...]
    ca = c * jax.nn.sigmoid(c)
    hi, lo = _split(ca)
    whi = whi_ref[...]
    o_ref[...] = _dot(hi, whi) + _dot(lo, whi) + _dot(hi, wlo_ref[...]) + b_ref[...]


def _mod_call(c, w_ada, b_ada):
    b, d = c.shape
    n = w_ada.shape[1]
    whi, wlo = _split(w_ada)
    tn = 1024
    return pl.pallas_call(
        _mod_kernel,
        grid=(n // tn,),
        in_specs=[
            pl.BlockSpec((b, d), lambda j: (0, 0)),
            pl.BlockSpec((d, tn), lambda j: (0, j)),
            pl.BlockSpec((d, tn), lambda j: (0, j)),
            pl.BlockSpec((1, tn), lambda j: (0, j)),
        ],
        out_specs=pl.BlockSpec((b, tn), lambda j: (0, j)),
        out_shape=jax.ShapeDtypeStruct((b, n), F32),
        compiler_params=_params("parallel"),
        name="mod",
    )(c, whi, wlo, b_ada.reshape(1, n))


def _norm_rope(a, gain, cos, sin_signed, lo, first):
    sq = a * a
    s0 = jnp.sum(jnp.where(lo, sq, 0.0), axis=-1, keepdims=True)
    s1 = jnp.sum(jnp.where(lo, 0.0, sq), axis=-1, keepdims=True)
    r = jnp.where(lo, lax.rsqrt(s0 * (1.0 / HEAD_DIM) + EPS), lax.rsqrt(s1 * (1.0 / HEAD_DIM) + EPS))
    y = a * r * gain
    rot = jnp.where(first, pltpu.roll(y, 96, 1), pltpu.roll(y, 32, 1))
    return y * cos + rot * sin_signed


def _proj_kernel(x_ref, mod_ref, g1_ref, w_ref, gains_ref, cos_ref, sin_ref,
                 qext_ref, kvc_ref, kva_ref, dil_ref, ga_ref, gm_ref):
    tm = x_ref.shape[1]
    x = x_ref[0]
    ms = jnp.mean(x * x, axis=-1, keepdims=True)
    y = x * lax.rsqrt(ms + EPS) * g1_ref[...]
    sh1 = mod_ref[0, 0:1, :]
    sc1 = mod_ref[0, 1:2, :]
    h = (y * (1.0 + sc1) + sh1).astype(BF16)
    cos = cos_ref[0]
    sin = sin_ref[0]
    lane = lax.broadcasted_iota(I32, (tm, LANES), 1)
    lo = lane < HEAD_DIM
    first = (lane & (HEAD_DIM - 1)) < (HEAD_DIM // 2)
    nr = functools.partial(_norm_rope, cos=cos, sin_signed=sin, lo=lo, first=first)

    for c in range(N_TILES // 2):
        acc = _dot(h, w_ref[:, c * 2 * LANES:(c + 1) * 2 * LANES])
        for half in range(2):
            t = 2 * c + half
            a = acc[:, half * LANES:(half + 1) * LANES]
            if t < T_KVC:
                yq = nr(a, gains_ref[0:1, :])
                rq = pltpu.roll(yq, HEAD_DIM, 1)
                if t // 2 == 0:
                    e0 = jnp.where(lo, yq, 0.0)
                    e1 = jnp.where(lo, rq, 0.0)
                else:
                    e0 = jnp.where(lo, 0.0, rq)
                    e1 = jnp.where(lo, 0.0, yq)
                qext_ref[0, 2 * t] = e0.astype(BF16)
                qext_ref[0, 2 * t + 1] = e1.astype(BF16)
            elif t < T_KVA:
                kvc_ref[0, :, (t - T_KVC) * LANES:(t - T_KVC + 1) * LANES] = a
            elif t < T_DIL:
                j = t - T_KVA
                v = nr(a, gains_ref[1:2, :]) if j % 2 == 0 else a
                kva_ref[0, :, j * LANES:(j + 1) * LANES] = v.astype(BF16)
            elif t < T_GA:
                j = t - T_DIL
                if j < 3:
                    v = nr(a, gains_ref[2:3, :])
                elif j < 6:
                    v = nr(a, gains_ref[3:4, :])
                else:
                    v = a
                dil_ref[0, :, j * LANES:(j + 1) * LANES] = v.astype(BF16)
            elif t < T_GM:
                ga_ref[0] = jax.nn.sigmoid(a)
            else:
                j = t - T_GM
                gm_ref[0, :, j * LANES:(j + 1) * LANES] = jax.nn.sigmoid(a).astype(BF16)


def _proj_call(x, mod6, g1, w_perm, gains, cos_t, sin_t):
    b, s, d = x.shape
    tm = PROJ_TM
    n = N_TILES * LANES
    row = lambda bi, i: (bi, i, 0)
    return pl.pallas_call(
        _proj_kernel,
        grid=(b, s // tm),
        in_specs=[
            pl.BlockSpec((1, tm, d), row),
            pl.BlockSpec((1, 6, d), lambda bi, i: (bi, 0, 0)),
            pl.BlockSpec((1, d), lambda bi, i: (0, 0)),
            pl.BlockSpec((d, n), lambda bi, i: (0, 0)),
            pl.BlockSpec((4, LANES), lambda bi, i: (0, 0)),
            pl.BlockSpec((1, tm, LANES), row),
            pl.BlockSpec((1, tm, LANES), row),
        ],
        out_specs=[
            pl.BlockSpec((1, NSA_HEADS, tm, LANES), lambda bi, i: (bi, 0, i, 0)),
            pl.BlockSpec((1, tm, 2 * LANES), row),
            pl.BlockSpec((1, tm, 4 * LANES), row),
            pl.BlockSpec((1, tm, 9 * LANES), row),
            pl.BlockSpec((1, tm, LANES), row),
            pl.BlockSpec((1, tm, 2 * d), row),
        ],
        out_shape=[
            jax.ShapeDtypeStruct((b, NSA_HEADS, s, LANES), BF16),
            jax.ShapeDtypeStruct((b, s, 2 * LANES), F32),
            jax.ShapeDtypeStruct((b, s, 4 * LANES), BF16),
            jax.ShapeDtypeStruct((b, s, 9 * LANES), BF16),
            jax.ShapeDtypeStruct((b, s, LANES), F32),
            jax.ShapeDtypeStruct((b, s, 2 * d), BF16),
        ],
        compiler_params=_params("parallel", "parallel"),
        name="proj",
    )(x, mod6, g1, w_perm, gains, cos_t, sin_t)


def _cmp_kernel(c_ref, pe_ref, w1_ref, w2_ref, gain_ref, cos_ref, sin_ref, kc_ref, vc_ref):
    nb = c_ref.shape[2]
    half = c_ref.shape[3]
    lane = lax.broadcasted_iota(I32, (nb, LANES), 1)
    lo = lane < HEAD_DIM
    first = (lane & (HEAD_DIM - 1)) < (HEAD_DIM // 2)
    for kind in range(2):
        out = jnp.zeros((nb, LANES), F32)
        for hd in range(2):
            c = c_ref[0, kind * 2 + hd]
            top = _dot((c + pe_ref[kind, 0:1, :]).astype(BF16), w1_ref[kind, 0:half, :])
            bot = _dot((c + pe_ref[kind, 1:2, :]).astype(BF16), w1_ref[kind, half:2 * half, :])
            hid = top + pltpu.roll(bot, nb - 1, 0)
            hid = hid * jax.nn.sigmoid(hid)
            out = out + _dot(hid.astype(BF16), w2_ref[kind, hd])
        if kind == 0:
            out = _norm_rope(out, gain_ref[...], cos_ref[0], sin_ref[0], lo, first)
            kc_ref[0] = out.astype(BF16)
        else:
            vc_ref[0] = out.astype(BF16)


def _cmp_call(chunks, pe, w1, w2ext, gain_k, ccos, csin):
    b, _, nb, half = chunks.shape
    return pl.pallas_call(
        _cmp_kernel,
        grid=(b,),
        in_specs=[
            pl.BlockSpec((1, 4, nb, half), lambda bi: (bi, 0, 0, 0)),
            pl.BlockSpec((2, 2, half), lambda bi: (0, 0, 0)),
            pl.BlockSpec((2, 2 * half, CMP_HIDDEN), lambda bi: (0, 0, 0)),
            pl.BlockSpec((2, 2, CMP_HIDDEN, LANES), lambda bi: (0, 0, 0, 0)),
            pl.BlockSpec((1, LANES), lambda bi: (0, 0)),
            pl.BlockSpec((1, nb, LANES), lambda bi: (bi, 0, 0)),
            pl.BlockSpec((1, nb, LANES), lambda bi: (bi, 0, 0)),
        ],
        out_specs=[
            pl.BlockSpec((1, nb, LANES), lambda bi: (bi, 0, 0)),
            pl.BlockSpec((1, nb, LANES), lambda bi: (bi, 0, 0)),
        ],
        out_shape=[
            jax.ShapeDtypeStruct((b, nb, LANES), BF16),
            jax.ShapeDtypeStruct((b, nb, LANES), BF16),
        ],
        compiler_params=_params("parallel"),
        name="cmp",
    )(chunks, pe, w1, w2ext, gain_k, ccos, csin)


def _softmax_cols(s):
    p = jnp.exp(s - jnp.max(s, axis=0, keepdims=True))
    return p, jnp.sum(p, axis=0, keepdims=True)


def _nsa_kernel(q_ref, kc_ref, vc_ref, ksl_ref, vsl_ref, kw_ref, vw_ref, ga_ref, ov_ref, e_ref,
                o_ref, m_scr, l_scr, acc_scr):
    tq = NSA_TQ
    tk = NSA_TK
    cols = NSA_GROUP * tq
    n_blk = e_ref.shape[2]
    kh = pl.program_id(1)
    t0 = pl.program_id(2) * tq
    q4 = q_ref[0].reshape(cols, LANES)
    tcol = t0 + lax.broadcasted_iota(I32, (1, tq), 1)
    tcol4 = t0 + (lax.broadcasted_iota(I32, (1, cols), 1) & (tq - 1))

    s = _dot_nt(kc_ref[0], q4)
    nb = s.shape[0]
    cmp_end = lax.broadcasted_iota(I32, (nb, 1), 0) * CMP_STRIDE + (CMP_BLOCK - 1)
    valid = cmp_end <= tcol4
    e, den = _softmax_cols(jnp.where(valid, s, NEG_INF))
    p = jnp.where(valid, e / den, 0.0)
    o_cmp = _dot_tn(vc_ref[0], p.astype(BF16))

    psum = p[:, 0:tq] + p[:, tq:2 * tq] + p[:, 2 * tq:3 * tq] + p[:, 3 * tq:4 * tq]
    p_hi, p_lo = _split(psum)
    ov = ov_ref[...]
    imp = _dot(ov, p_hi) + _dot(ov, p_lo)
    blk = lax.broadcasted_iota(I32, (n_blk, 1), 0)
    blk_f = blk.astype(F32)
    rel = (tcol >> 6) - blk
    forced = (blk == 0) | ((rel >= 0) & (rel < N_LOCAL_SEL))
    score = jnp.where(rel < 0, NEG_INF, jnp.where(forced, FORCE_SCORE, imp))
    sel = jnp.zeros((n_blk, tq), F32)
    for _ in range(N_SEL):
        best = jnp.max(score, axis=0, keepdims=True)
        first = jnp.min(jnp.where(score == best, blk_f, float(n_blk)), axis=0, keepdims=True)
        pick = blk_f == first
        sel = jnp.where(pick, 1.0, sel)
        score = jnp.where(pick, PICKED, score)
    selb = sel.astype(BF16)

    nwin = WINDOW + tq
    ws = pl.multiple_of(jnp.maximum(t0 - WINDOW, 0), tq)
    sw = _dot_nt(kw_ref[0, pl.ds(ws, nwin), :], q4)
    dist = tcol4 - (ws + lax.broadcasted_iota(I32, (nwin, 1), 0))
    pw, lw = _softmax_cols(jnp.where((dist >= 0) & (dist <= WINDOW), sw, MASKED))
    o_win = _dot_tn(vw_ref[0, pl.ds(ws, nwin), :], pw.astype(BF16)) / lw

    m_scr[...] = jnp.full(m_scr.shape, MASKED, F32)
    l_scr[...] = jnp.zeros(l_scr.shape, F32)
    acc_scr[...] = jnp.zeros(acc_scr.shape, F32)
    kiota = lax.broadcasted_iota(I32, (tk, 1), 0)

    def body(c, carry):
        ks = pl.multiple_of(c * tk, tk)
        sc = _dot_nt(ksl_ref[0, pl.ds(ks, tk), :], q4)
        chosen = _dot(e_ref[c], selb)
        ok = (chosen > 0.5) & ((ks + kiota) <= tcol)
        bias = jnp.where(ok, 0.0, MASKED)
        sc = sc + jnp.concatenate([bias] * NSA_GROUP, axis=1)
        m_old = m_scr[...]
        m_new = jnp.maximum(m_old, jnp.max(sc, axis=0, keepdims=True))
        alpha = jnp.exp(m_old - m_new)
        pe = jnp.exp(sc - m_new)
        l_scr[...] = alpha * l_scr[...] + jnp.sum(pe, axis=0, keepdims=True)
        acc_scr[...] = alpha * acc_scr[...] + _dot_tn(vsl_ref[0, pl.ds(ks, tk), :], pe.astype(BF16))
        m_scr[...] = m_new
        return carry

    lax.fori_loop(0, (t0 + tq - 1) // tk + 1, body, 0)
    o_slc = acc_scr[...] / l_scr[...]

    gat = ga_ref[0].T
    is0 = kh == 0
    heads = []
    for g in range(NSA_GROUP):
        c = slice(g * tq, (g + 1) * tq)
        og = jnp.zeros((LANES, tq), F32)
        for gi, ob in enumerate((o_cmp, o_slc, o_win)):
            c0 = gi * NSA_HEADS + g
            c1 = c0 + NSA_GROUP
            gate = jnp.where(is0, gat[c0:c0 + 1, :], gat[c1:c1 + 1, :])
            og = og + gate * ob[:, c]
        heads.append(jnp.where(is0, og[0:HEAD_DIM], og[HEAD_DIM:LANES]))
    for pair in range(2):
        tile = jnp.concatenate([heads[2 * pair], heads[2 * pair + 1]], axis=0)
        o_ref[0, :, pair * LANES:(pair + 1) * LANES] = tile.T.astype(BF16)


def _nsa_call(qext, kc, vc, kva, ga, ov_t, e3):
    b, _, s, _ = qext.shape
    tq = NSA_TQ
    nb = kc.shape[1]
    cols = NSA_GROUP * tq
    full = lambda j: pl.BlockSpec((1, s, LANES), lambda bi, k, i: (bi, 0, j))
    return pl.pallas_call(
        _nsa_kernel,
        grid=(b, NSA_KV_HEADS, s // tq),
        in_specs=[
            pl.BlockSpec((1, NSA_GROUP, tq, LANES), lambda bi, k, i: (bi, k, i, 0)),
            pl.BlockSpec((1, nb, LANES), lambda bi, k, i: (bi, 0, 0)),
            pl.BlockSpec((1, nb, LANES), lambda bi, k, i: (bi, 0, 0)),
            full(0), full(1), full(2), full(3),
            pl.BlockSpec((1, tq, LANES), lambda bi, k, i: (bi, i, 0)),
            pl.BlockSpec(ov_t.shape, lambda bi, k, i: (0, 0)),
            pl.BlockSpec(e3.shape, lambda bi, k, i: (0, 0, 0)),
        ],
        out_specs=pl.BlockSpec((1, tq, 2 * LANES), lambda bi, k, i: (bi, i, k)),
        out_shape=jax.ShapeDtypeStruct((b, s, A_Q), BF16),
        scratch_shapes=[
            pltpu.VMEM((1, cols), F32),
            pltpu.VMEM((1, cols), F32),
            pltpu.VMEM((LANES, cols), F32),
        ],
        compiler_params=_params("parallel", "parallel", "parallel"),
        name="nsa",
    )(qext, kc, vc, kva, kva, kva, kva, ga, ov_t, e3)


def _dil_kernel(q_ref, k_ref, v_ref, o_ref, lse_ref):
    t = DIL_T
    g = pl.program_id(1)
    seg_tiles = jnp.where(g == 0, 16, jnp.where(g == 1, 4, 1))
    lo = lax.broadcasted_iota(I32, (t, LANES), 1) < HEAD_DIM
    kj = lax.broadcasted_iota(I32, (t, 2 * t), 0)
    qi = lax.broadcasted_iota(I32, (t, 2 * t), 1) & (t - 1)
    top = lax.broadcasted_iota(I32, (LANES, t), 0) < HEAD_DIM

    def tile(i, carry):
        has_prev = (i & (seg_tiles - 1)) != 0
        cs = pl.multiple_of(i * t, t)
        ps = pl.multiple_of(jnp.maximum(i - 1, 0) * t, t)
        q = q_ref[0, 0, pl.ds(cs, t), :]
        zero = jnp.zeros_like(q)
        q2 = jnp.concatenate([jnp.where(lo, q, zero), jnp.where(lo, zero, q)], axis=0)
        sp = jnp.where((kj >= qi) & has_prev, _dot_nt(k_ref[0, 0, pl.ds(ps, t), :], q2), MASKED)
        sc = jnp.where(qi >= kj, _dot_nt(k_ref[0, 0, pl.ds(cs, t), :], q2), MASKED)
        m = jnp.maximum(jnp.max(sp, axis=0, keepdims=True), jnp.max(sc, axis=0, keepdims=True))
        pp = jnp.exp(sp - m)
        pc = jnp.exp(sc - m)
        l = jnp.sum(pp, axis=0, keepdims=True) + jnp.sum(pc, axis=0, keepdims=True)
        o = (_dot_tn(v_ref[0, 0, pl.ds(ps, t), :], pp.astype(BF16))
             + _dot_tn(v_ref[0, 0, pl.ds(cs, t), :], pc.astype(BF16))) / l
        lse = m + jnp.log(l)
        o_ref[0, 0, pl.ds(cs, t), :] = jnp.where(top, o[:, 0:t], o[:, t:2 * t]).T.astype(BF16)
        lse_ref[0, 0, pl.ds(cs, t), :] = jnp.where(top, lse[:, 0:t], lse[:, t:2 * t]).T
        return carry

    lax.fori_loop(0, q_ref.shape[2] // t, tile, 0)


def _dil_call(dil_in):
    b, _, s, _ = dil_in.shape
    blk = (1, 1, s, LANES)
    return pl.pallas_call(
        _dil_kernel,
        grid=(b, DIL_GROUPS),
        in_specs=[
            pl.BlockSpec(blk, lambda bi, g: (bi, g, 0, 0)),
            pl.BlockSpec(blk, lambda bi, g: (bi, DIL_GROUPS + g, 0, 0)),
            pl.BlockSpec(blk, lambda bi, g: (bi, 2 * DIL_GROUPS + g, 0, 0)),
        ],
        out_specs=[
            pl.BlockSpec(blk, lambda bi, g: (bi, g, 0, 0)),
            pl.BlockSpec(blk, lambda bi, g: (bi, g, 0, 0)),
        ],
        out_shape=[
            jax.ShapeDtypeStruct((b, DIL_GROUPS, s, LANES), BF16),
            jax.ShapeDtypeStruct((b, DIL_GROUPS, s, LANES), F32),
        ],
        compiler_params=_params("parallel", "parallel"),
        name="dil",
    )(dil_in, dil_in, dil_in)


def _merge_kernel(x_ref, oa_ref, od_ref, lse_ref, gm_ref, mod_ref, wa_ref, wb_ref, wo_ref, g2_ref,
                  rhi_ref, rlo_ref, rb_ref, x1_ref, h2_ref, route_ref):
    tm = x_ref.shape[0]
    d = x_ref.shape[1]
    l0, l1, l2 = lse_ref[0], lse_ref[1], lse_ref[2]
    mx = jnp.maximum(jnp.maximum(l0, l1), l2)
    e0, e1, e2 = jnp.exp(l0 - mx), jnp.exp(l1 - mx), jnp.exp(l2 - mx)
    den = e0 + e1 + e2
    ob = jnp.concatenate([
        (od_ref[0].astype(F32) * (e0 / den)).astype(BF16),
        (od_ref[1].astype(F32) * (e1 / den)).astype(BF16),
        (od_ref[2].astype(F32) * (e2 / den)).astype(BF16)], axis=1)
    ya = _dot(oa_ref[...], wa_ref[...])
    yb = _dot(ob, wb_ref[...])
    y = gm_ref[:, 0:d].astype(F32) * ya + gm_ref[:, d:2 * d].astype(F32) * yb
    z = _dot(y.astype(BF16), wo_ref[...])
    x1 = x_ref[...] + mod_ref[0, 2:3, :] * z
    x1_ref[...] = x1

    ms = jnp.mean(x1 * x1, axis=-1, keepdims=True)
    h2 = x1 * lax.rsqrt(ms + EPS) * g2_ref[...]
    h2 = h2 * (1.0 + mod_ref[0, 4:5, :]) + mod_ref[0, 3:4, :]
    for s in range(ROW_TILE[0]):
        h2_ref[:, s, :] = h2[:, s * LANES:(s + 1) * LANES]

    hi, lo = _split(h2)
    rhi = rhi_ref[...]
    logits = _dot(hi, rhi) + _dot(lo, rhi) + _dot(hi, rlo_ref[...]) + rb_ref[...]
    lane = lax.broadcasted_iota(I32, (tm, LANES), 1)
    gl = jnp.where(lane < N_GROUPS, logits, NEG_INF)
    gmax = jnp.max(gl, axis=-1, keepdims=True)
    g_w = 1.0 / jnp.sum(jnp.exp(gl - gmax), axis=-1, keepdims=True)
    g_idx = jnp.min(jnp.where(gl == gmax, lane, LANES), axis=-1, keepdims=True)
    e_lo = N_GROUPS + EXPERTS_PER_GROUP * g_idx
    el = jnp.where((lane >= e_lo) & (lane < e_lo + EXPERTS_PER_GROUP), logits, NEG_INF)
    m1 = jnp.max(el, axis=-1, keepdims=True)
    i1 = jnp.min(jnp.where(el == m1, lane, LANES), axis=-1, keepdims=True)
    el2 = jnp.where(lane == i1, PICKED, el)
    m2 = jnp.max(el2, axis=-1, keepdims=True)
    i2 = jnp.min(jnp.where(el2 == m2, lane, LANES), axis=-1, keepdims=True)
    ex = jnp.exp(m2 - m1)
    w1 = g_w * (1.0 / (1.0 + ex))
    w2 = g_w * (ex / (1.0 + ex))
    route = jnp.where(lane == 0, (i1 - N_GROUPS).astype(F32),
                      jnp.where(lane == 1, (i2 - N_GROUPS).astype(F32),
                                jnp.where(lane == 2, w1, jnp.where(lane == 3, w2, 0.0))))
    route_ref[...] = route


def _merge_call(x2, oa, od, lse, gm, mod6, wa, wb, wo, g2, rhi, rlo, rb, s):
    t, d = x2.shape
    tm = MERGE_TM
    per_b = s // tm
    row = lambda i: (i, 0)
    const = lambda i: (0, 0)
    return pl.pallas_call(
        _merge_kernel,
        grid=(t // tm,),
        in_specs=[
            pl.BlockSpec((tm, d), row),
            pl.BlockSpec((tm, A_Q), row),
            pl.BlockSpec((DIL_GROUPS, tm, LANES), lambda i: (0, i, 0)),
            pl.BlockSpec((DIL_GROUPS, tm, LANES), lambda i: (0, i, 0)),
            pl.BlockSpec((tm, 2 * d), row),
            pl.BlockSpec((1, 6, d), lambda i: (i // per_b, 0, 0)),
            pl.BlockSpec(wa.shape, const),
            pl.BlockSpec(wb.shape, const),
            pl.BlockSpec(wo.shape, const),
            pl.BlockSpec((1, d), const),
            pl.BlockSpec(rhi.shape, const),
            pl.BlockSpec(rlo.shape, const),
            pl.BlockSpec((1, LANES), const),
        ],
        out_specs=[
            pl.BlockSpec((tm, d), row),
            pl.BlockSpec((tm,) + ROW_TILE, lambda i: (i, 0, 0)),
            pl.BlockSpec((tm, LANES), row),
        ],
        out_shape=[
            jax.ShapeDtypeStruct((t, d), F32),
            jax.ShapeDtypeStruct((t,) + ROW_TILE, F32),
            jax.ShapeDtypeStruct((t, LANES), F32),
        ],
        compiler_params=_params("parallel"),
        name="merge",
    )(x2, oa, od, lse, gm, mod6, wa, wb, wo, g2, rhi, rlo, rb)


def _rank_kernel(route_ref, rank_ref, count_ref, carry_scr):
    tm = route_ref.shape[0]

    @pl.when(pl.program_id(0) == 0)
    def _():
        carry_scr[...] = jnp.zeros(carry_scr.shape, F32)

    route = route_ref[...]
    lane = lax.broadcasted_iota(I32, (tm, LANES), 1)
    e1 = route[:, 0:1].astype(I32)
    e2 = route[:, 1:2].astype(I32)
    hit1 = lane == e1
    hit2 = lane == e2
    cnt = jnp.where(hit1 | hit2, 1.0, 0.0)
    r = lax.broadcasted_iota(I32, (tm, tm), 0)
    c = lax.broadcasted_iota(I32, (tm, tm), 1)
    below = jnp.where(c < r, 1.0, 0.0).astype(BF16)
    before = _dot(below, cnt.astype(BF16)) + carry_scr[...]
    r1 = jnp.sum(jnp.where(hit1, before, 0.0), axis=-1, keepdims=True)
    r2 = jnp.sum(jnp.where(hit2, before, 0.0), axis=-1, keepdims=True)
    rank_ref[...] = jnp.where(lane == 0, r1, jnp.where(lane == 1, r2, 0.0))
    carry_scr[...] = carry_scr[...] + jnp.sum(cnt, axis=0, keepdims=True)
    count_ref[...] = carry_scr[...]


def _rank_call(route):
    t = route.shape[0]
    tm = RANK_TM
    return pl.pallas_call(
        _rank_kernel,
        grid=(t // tm,),
        in_specs=[pl.BlockSpec((tm, LANES), lambda i: (i, 0))],
        out_specs=[
            pl.BlockSpec((tm, LANES), lambda i: (i, 0)),
            pl.BlockSpec((1, LANES), lambda i: (0, 0)),
        ],
        out_shape=[
            jax.ShapeDtypeStruct((t, LANES), F32),
            jax.ShapeDtypeStruct((1, LANES), F32),
        ],
        scratch_shapes=[pltpu.VMEM((1, LANES), F32)],
        compiler_params=_params("arbitrary"),
        name="rank",
    )(route)


def _issue_unrolled(n, fn):
    def body(j, carry):
        for u in range(DMA_UNROLL):
            fn(j * DMA_UNROLL + u)
        return carry
    lax.fori_loop(0, n // DMA_UNROLL, body, 0)


def _dispatch_kernel(cnt_ref, pstart_ref, dest_hbm, h_hbm, xs_hbm, idx_smem, zbuf, isem, dsem, zsem):
    tm = MOE_TM
    nd = TOP_K * tm
    i = pl.program_id(0)
    n = pl.num_programs(0)
    slot = i % 2
    nxt = 1 - slot
    n_slots = xs_hbm.shape[0]

    def idx_copy(step, sl):
        return pltpu.make_async_copy(dest_hbm.at[step], idx_smem.at[sl], isem.at[sl])

    def step_rows(sl):
        return pltpu.make_async_copy(h_hbm.at[pl.ds(0, nd)], xs_hbm.at[pl.ds(0, nd)], dsem.at[sl])

    @pl.when(i == 0)
    def _():
        idx_copy(0, 0).start()
        zbuf[...] = jnp.zeros(zbuf.shape, F32)

        def zero_row(row):
            return pltpu.make_async_copy(zbuf.at[0], xs_hbm.at[row], zsem)

        def per_expert(e, used):
            base = pstart_ref[e]
            cnt = cnt_ref[e]
            pad = (cnt + MOE_BLOCK - 1) // MOE_BLOCK * MOE_BLOCK

            def start(r, c):
                zero_row(base + r).start()
                return c

            def wait(r, c):
                zero_row(base + r).wait()
                return c
            lax.fori_loop(cnt, pad, start, 0)
            lax.fori_loop(cnt, pad, wait, 0)
            return used + pad
        used = lax.fori_loop(0, N_EXPERTS, per_expert, 0)

        def zero_block(blk):
            rows = pl.ds(pl.multiple_of(blk * MOE_BLOCK, MOE_BLOCK), MOE_BLOCK)
            return pltpu.make_async_copy(zbuf, xs_hbm.at[rows], zsem)

        def start_blk(blk, c):
            zero_block(blk).start()
            return c

        def wait_blk(blk, c):
            zero_block(blk).wait()
            return c
        lax.fori_loop(used // MOE_BLOCK, n_slots // MOE_BLOCK, start_blk, 0)
        lax.fori_loop(used // MOE_BLOCK, n_slots // MOE_BLOCK, wait_blk, 0)

    idx_copy(i, slot).wait()

    @pl.when(i + 1 < n)
    def _():
        idx_copy(i + 1, nxt).start()

    t0 = i * tm

    def one(r):
        tok = t0 + (r & (tm - 1))
        pltpu.make_async_copy(h_hbm.at[tok], xs_hbm.at[idx_smem[slot, r]], dsem.at[slot]).start()
    _issue_unrolled(nd, one)

    @pl.when(i > 0)
    def _():
        step_rows(nxt).wait()

    @pl.when(i == n - 1)
    def _():
        step_rows(slot).wait()


def _dispatch_call(counts, pad_starts, dest_tiles, h2, n_slots):
    n_steps = dest_tiles.shape[0]
    grid_spec = pltpu.PrefetchScalarGridSpec(
        num_scalar_prefetch=2,
        grid=(n_steps,),
        in_specs=[pl.BlockSpec(memory_space=pl.ANY), pl.BlockSpec(memory_space=pl.ANY)],
        out_specs=pl.BlockSpec(memory_space=pl.ANY),
        scratch_shapes=[
            pltpu.SMEM((2, TOP_K * MOE_TM), I32),
            pltpu.VMEM((MOE_BLOCK,) + ROW_TILE, F32),
            pltpu.SemaphoreType.DMA((2,)),
            pltpu.SemaphoreType.DMA((2,)),
            pltpu.SemaphoreType.DMA(()),
        ],
    )
    return pl.pallas_call(
        _dispatch_kernel,
        grid_spec=grid_spec,
        out_shape=jax.ShapeDtypeStruct((n_slots,) + ROW_TILE, F32),
        compiler_params=_params("arbitrary"),
        name="dispatch",
    )(counts, pad_starts, dest_tiles, h2)


def _expert_kernel(be_ref, x_ref, wg_ref, wu_ref, wd_ref, y_ref):
    del be_ref
    ns = ROW_TILE[0]
    xb = jnp.concatenate([x_ref[:, s, :] for s in range(ns)], axis=1).astype(BF16)
    gate = _dot(xb, wg_ref[0])
    up = _dot(xb, wu_ref[0])
    hid = (gate * jax.nn.sigmoid(gate) * up).astype(BF16)
    y = _dot(hid, wd_ref[0])
    for s in range(ns):
        y_ref[:, s, :] = y[:, s * LANES:(s + 1) * LANES]


def _expert_call(blk_expert, xs, wg, wu, wd):
    n_blk = blk_expert.shape[0]
    d = wg.shape[1]
    blk = (MOE_BLOCK,) + ROW_TILE
    grid_spec = pltpu.PrefetchScalarGridSpec(
        num_scalar_prefetch=1,
        grid=(n_blk,),
        in_specs=[
            pl.BlockSpec(blk, lambda i, be: (i, 0, 0)),
            pl.BlockSpec((1, d, EXPERT_FF), lambda i, be: (be[i], 0, 0)),
            pl.BlockSpec((1, d, EXPERT_FF), lambda i, be: (be[i], 0, 0)),
            pl.BlockSpec((1, EXPERT_FF, d), lambda i, be: (be[i], 0, 0)),
        ],
        out_specs=pl.BlockSpec(blk, lambda i, be: (i, 0, 0)),
    )
    return pl.pallas_call(
        _expert_kernel,
        grid_spec=grid_spec,
        out_shape=jax.ShapeDtypeStruct(xs.shape, F32),
        compiler_params=_params("arbitrary"),
        name="experts",
    )(blk_expert, xs, wg, wu, wd)


def _combine_kernel(dest_hbm, y_hbm, x1_ref, route_ref, mod_ref, o_ref, idx_smem, ybuf, isem, dsem):
    tm = MOE_TM
    nd = TOP_K * tm
    i = pl.program_id(0)
    n = pl.num_programs(0)
    slot = i % 2
    nxt = 1 - slot

    def idx_copy(step, sl):
        return pltpu.make_async_copy(dest_hbm.at[step], idx_smem.at[sl], isem.at[sl])

    def step_rows(sl):
        return pltpu.make_async_copy(y_hbm.at[pl.ds(0, nd)], ybuf.at[sl], dsem.at[sl])

    def issue_rows(sl):
        def one(r):
            pltpu.make_async_copy(y_hbm.at[idx_smem[sl, r]], ybuf.at[sl, r], dsem.at[sl]).start()
        _issue_unrolled(nd, one)

    @pl.when(i == 0)
    def _():
        idx_copy(0, 0).start()
        idx_copy(0, 0).wait()
        issue_rows(0)

        @pl.when(n > 1)
        def _():
            idx_copy(1, 1).start()

    @pl.when(i + 1 < n)
    def _():
        idx_copy(i + 1, nxt).wait()
        issue_rows(nxt)

    @pl.when(i + 2 < n)
    def _():
        idx_copy(i + 2, slot).start()

    step_rows(slot).wait()
    route = route_ref[...]
    w1 = route[:, 2:3]
    w2 = route[:, 3:4]
    for s in range(ROW_TILE[0]):
        cols = slice(s * LANES, (s + 1) * LANES)
        y = w1 * ybuf[slot, 0:tm, s, :] + w2 * ybuf[slot, tm:nd, s, :]
        o_ref[:, cols] = x1_ref[:, cols] + mod_ref[0, 5:6, cols] * y


def _combine_call(dest_tiles, y_slots, x1, route, mod6, s):
    t, d = x1.shape
    tm = MOE_TM
    per_b = s // tm
    return pl.pallas_call(
        _combine_kernel,
        grid=(t // tm,),
        in_specs=[
            pl.BlockSpec(memory_space=pl.ANY),
            pl.BlockSpec(memory_space=pl.ANY),
            pl.BlockSpec((tm, d), lambda i: (i, 0)),
            pl.BlockSpec((tm, LANES), lambda i: (i, 0)),
            pl.BlockSpec((1, 6, d), lambda i: (i // per_b, 0, 0)),
        ],
        out_specs=pl.BlockSpec((tm, d), lambda i: (i, 0)),
        out_shape=jax.ShapeDtypeStruct((t, d), F32),
        scratch_shapes=[
            pltpu.SMEM((2, TOP_K * tm), I32),
            pltpu.VMEM((2, TOP_K * tm) + ROW_TILE, F32),
            pltpu.SemaphoreType.DMA((2,)),
            pltpu.SemaphoreType.DMA((2,)),
        ],
        compiler_params=_params("arbitrary"),
        name="combine",
    )(dest_tiles, y_slots, x1, route, mod6)


def _rope_tiles(pos):
    inv = ROPE_THETA ** (-jnp.arange(0, HEAD_DIM, 2, dtype=F32) / HEAD_DIM)
    ang = pos.astype(F32)[..., None] * inv
    cos, sin = jnp.cos(ang), jnp.sin(ang)
    return (jnp.concatenate([cos, cos, cos, cos], axis=-1),
            jnp.concatenate([-sin, sin, -sin, sin], axis=-1))


def _selection_constants(s):
    n_c = s // CMP_STRIDE
    n_s = s // SEL_BLOCK
    cs = np.arange(n_c) * CMP_STRIDE
    ss = np.arange(n_s) * SEL_BLOCK
    ov = np.clip(np.minimum(cs[:, None] + CMP_BLOCK, ss[None, :] + SEL_BLOCK)
                 - np.maximum(cs[:, None], ss[None, :]), 0, None).astype(np.float32) / CMP_BLOCK
    key_blk = np.arange(s) // SEL_BLOCK
    e = (key_blk[:, None] == np.arange(n_s)[None, :]).astype(np.float32)
    return jnp.asarray(ov.T, BF16), jnp.asarray(e.reshape(s // NSA_TK, NSA_TK, n_s), BF16)


def _to_residue_major(t, dil):
    b, s, w = t.shape
    return t.reshape(b, s // dil, dil, w).transpose(0, 2, 1, 3).reshape(b, s, w)


def _from_residue_major(t, dil):
    b, s, w = t.shape
    return t.reshape(b, dil, s // dil, w).transpose(0, 2, 1, 3).reshape(b, s, w)


def _mixer_and_router(x, c, positions, w_ada, b_ada, norm1_g, norm2_g, w_in, nsa_q_norm, nsa_k_norm,
                      cmp_pe_k, cmp_w1_k, cmp_w2_k, cmp_pe_v, cmp_w1_v, cmp_w2_v, dil_q_norm,
                      dil_k_norm, w_up_a, w_up_b, w_out, w_group, b_group, w_router, b_router):
    b, s, d = x.shape
    scale = HEAD_DIM ** -0.5
    mod6 = _mod_call(c, w_ada, b_ada).reshape(b, 6, d)

    c1 = A_Q
    c2 = c1 + 6 * A_KV
    c3 = c2 + 3 * NSA_HEADS
    c4 = c3 + 3 * DIL_W
    w_perm = jnp.concatenate([
        w_in[:, :c2], w_in[:, c3:c4],
        jnp.pad(w_in[:, c2:c3], ((0, 0), (0, LANES - 3 * NSA_HEADS))), w_in[:, c4:]], axis=1).astype(BF16)
    two = lambda g: jnp.concatenate([g, g]).astype(F32)
    gains = jnp.stack([two(nsa_q_norm) * scale, two(nsa_k_norm), two(dil_q_norm) * scale, two(dil_k_norm)])
    cos_t, sin_t = _rope_tiles(positions)
    qext, kvc, kva, dil, ga, gm = _proj_call(x, mod6, norm1_g.reshape(1, d), w_perm, gains, cos_t, sin_t)

    n_chunk = s // CMP_STRIDE
    half = CMP_STRIDE * HEAD_DIM
    chunks = kvc.reshape(b, s, 4, HEAD_DIM).transpose(0, 2, 1, 3).reshape(b, 4, n_chunk, half)
    pe = jnp.stack([cmp_pe_k.reshape(2, half), cmp_pe_v.reshape(2, half)])
    w1 = jnp.stack([cmp_w1_k, cmp_w1_v]).astype(BF16)
    zeros = jnp.zeros((CMP_HIDDEN, HEAD_DIM), F32)
    ext = lambda w: jnp.stack([jnp.concatenate([w, zeros], 1), jnp.concatenate([zeros, w], 1)])
    w2ext = jnp.stack([ext(cmp_w2_k), ext(cmp_w2_v)]).astype(BF16)
    cmp_pos = jnp.pad(positions[:, CMP_BLOCK - 1::CMP_STRIDE], ((0, 0), (0, 1)))
    ccos, csin = _rope_tiles(cmp_pos)
    kc, vc = _cmp_call(chunks, pe, w1, w2ext, two(nsa_k_norm).reshape(1, LANES), ccos, csin)

    ov_t, e3 = _selection_constants(s)
    o_a = _nsa_call(qext, kc, vc, kva, ga, ov_t, e3)

    parts = []
    for kind in range(3):
        for gi, (_, dl) in enumerate(DIL_PAIRS):
            col = (kind * DIL_GROUPS + gi) * LANES
            parts.append(_to_residue_major(dil[:, :, col:col + LANES], dl))
    o_dil, lse_dil = _dil_call(jnp.stack(parts, axis=1))
    od = jnp.stack([_from_residue_major(o_dil[:, gi], dl) for gi, (_, dl) in enumerate(DIL_PAIRS)])
    lse = jnp.stack([_from_residue_major(lse_dil[:, gi], dl) for gi, (_, dl) in enumerate(DIL_PAIRS)])

    t = b * s
    w_r = jnp.concatenate([w_group, w_router.transpose(1, 0, 2).reshape(d, N_EXPERTS)], axis=1)
    w_r = jnp.pad(w_r, ((0, 0), (0, LANES - w_r.shape[1])))
    rhi, rlo = _split(w_r)
    rb = jnp.pad(jnp.concatenate([b_group, b_router.reshape(-1)]), (0, LANES - N_GROUPS - N_EXPERTS))
    return _merge_call(
        x.reshape(t, d), o_a.reshape(t, A_Q), od.reshape(DIL_GROUPS, t, LANES),
        lse.reshape(DIL_GROUPS, t, LANES), gm.reshape(t, 2 * d), mod6,
        w_up_a.astype(BF16), w_up_b.astype(BF16), w_out.astype(BF16), norm2_g.reshape(1, d),
        rhi, rlo, rb.reshape(1, LANES).astype(F32), s), mod6


def _moe(x1, h2, route, mod6, w_e_gate, w_e_up, w_e_down, s):
    t, _ = x1.shape
    rank, counts = _rank_call(route)
    expert = route[:, 0:TOP_K].astype(I32)
    counts = counts[0, :N_EXPERTS].astype(I32)
    padded = (counts + MOE_BLOCK - 1) // MOE_BLOCK * MOE_BLOCK
    pad_ends = jnp.cumsum(padded)
    pad_starts = pad_ends - padded
    dest = pad_starts[expert] + rank[:, 0:TOP_K].astype(I32)
    n_slots = t * TOP_K + N_EXPERTS * MOE_BLOCK
    n_blk = n_slots // MOE_BLOCK
    blk_start = jnp.arange(n_blk, dtype=I32) * MOE_BLOCK
    blk_expert = jnp.minimum(jnp.sum((pad_ends[None, :] <= blk_start[:, None]).astype(I32), axis=1),
                             N_EXPERTS - 1)
    tm = MOE_TM
    dest_tiles = dest.reshape(t // tm, tm, TOP_K).transpose(0, 2, 1).reshape(t // tm, TOP_K * tm)
    xs = _dispatch_call(counts, pad_starts.astype(I32), dest_tiles, h2, n_slots)
    y_slots = _expert_call(blk_expert, xs, w_e_gate.astype(BF16), w_e_up.astype(BF16),
                           w_e_down.astype(BF16))
    return _combine_call(dest_tiles, y_slots, x1, route, mod6, s)


def kernel(x, c, positions, w_ada, b_ada, norm1_g, norm2_g, w_in, nsa_q_norm, nsa_k_norm, cmp_pe_k,
           cmp_w1_k, cmp_w2_k, cmp_pe_v, cmp_w1_v, cmp_w2_v, dil_q_norm, dil_k_norm, w_up_a, w_up_b,
           w_out, w_group, b_group, w_router, b_router, w_e_gate, w_e_up, w_e_down):
    b, s, d = x.shape
    assert w_ada.shape[0] == 1 and d == D_MODEL and s % NSA_TK == 0
    (x1, h2, route), mod6 = _mixer_and_router(
        x, c, positions, w_ada[0], b_ada[0], norm1_g[0], norm2_g[0], w_in[0], nsa_q_norm[0],
        nsa_k_norm[0], cmp_pe_k[0], cmp_w1_k[0], cmp_w2_k[0], cmp_pe_v[0], cmp_w1_v[0], cmp_w2_v[0],
        dil_q_norm[0], dil_k_norm[0], w_up_a[0], w_up_b[0], w_out[0], w_group[0], b_group[0],
        w_router[0], b_router[0])
    out = _moe(x1, h2, route, mod6, w_e_gate[0], w_e_up[0], w_e_down[0], s)
    return out.reshape(b, s, d)
```

```python
import functools

import jax
import jax.numpy as jnp
import numpy as np
from jax import lax
from jax.experimental import pallas as pl
from jax.experimental.pallas import tpu as pltpu

F32 = jnp.float32
BF16 = jnp.bfloat16
I32 = jnp.int32

D_MODEL = 1024
HEAD_DIM = 64
LANES = 128
ROW_TILE = (D_MODEL // LANES, LANES)
ROPE_THETA = 10000.0
EPS = 1e-6
NEG_INF = -1e30
FORCE_SCORE = 1e9
MASKED = -1e30
PICKED = -3e38

NSA_HEADS = 8
NSA_KV_HEADS = 2
NSA_GROUP = 4
CMP_BLOCK = 32
CMP_STRIDE = 16
CMP_HIDDEN = 256
SEL_BLOCK = 64
N_SEL = 8
N_LOCAL_SEL = 2
WINDOW = 512
DIL_PAIRS = ((128, 1), (512, 4), (2048, 16))
DIL_GROUPS = 3
A_Q = NSA_HEADS * HEAD_DIM
A_KV = NSA_KV_HEADS * HEAD_DIM
DIL_W = 2 * DIL_GROUPS * HEAD_DIM
N_GROUPS = 4
EXPERTS_PER_GROUP = 8
N_EXPERTS = 32
TOP_K = 2
EXPERT_FF = 512
MOE_BLOCK = 256

VMEM_LIMIT = 56 * 1024 * 1024

T_QA = 0
T_KVC = 4
T_KVA = 6
T_DIL = 10
T_GA = 19
T_GM = 20
N_TILES = 36

PROJ_TM = 256
NSA_TQ = 128
NSA_TK = 512
NSA_STREAMS = 1
DIL_T = 128
MERGE_TM = 256
RANK_TM = 512
MOE_TM = 256
DMA_UNROLL = 8


def _dot(a, b):
    return jnp.dot(a, b, preferred_element_type=F32)


def _dot_nt(a, b):
    return lax.dot_general(a, b, (((1,), (1,)), ((), ())), preferred_element_type=F32)


def _dot_tn(a, b):
    return lax.dot_general(a, b, (((0,), (0,)), ((), ())), preferred_element_type=F32)


def _split(a):
    hi = a.astype(BF16)
    lo = (a - hi.astype(F32)).astype(BF16)
    return hi, lo


def _params(*sem):
    return pltpu.CompilerParams(dimension_semantics=sem, vmem_limit_bytes=VMEM_LIMIT)


def _mod_kernel(c_ref, whi_ref, wlo_ref, b_ref, o_ref):
    c = c_ref[...]
    ca = c * jax.nn.sigmoid(c)
    hi, lo = _split(ca)
    whi = whi_ref[...]
    o_ref[...] = _dot(hi, whi) + _dot(lo, whi) + _dot(hi, wlo_ref[...]) + b_ref[...]


def _mod_call(c, w_ada, b_ada):
    b, d = c.shape
    n = w_ada.shape[1]
    whi, wlo = _split(w_ada)
    tn = 1024
    return pl.pallas_call(
        _mod_kernel,
        grid=(n // tn,),
        in_specs=[
            pl.BlockSpec((b, d), lambda j: (0, 0)),
            pl.BlockSpec((d, tn), lambda j: (0, j)),
            pl.BlockSpec((d, tn), lambda j: (0, j)),
            pl.BlockSpec((1, tn), lambda j: (0, j)),
        ],
        out_specs=pl.BlockSpec((b, tn), lambda j: (0, j)),
        out_shape=jax.ShapeDtypeStruct((b, n), F32),
        compiler_params=_params("parallel"),
        name="mod",
    )(c, whi, wlo, b_ada.reshape(1, n))


def _norm_rope(a, gain, cos, sin_signed, lo, first):
    sq = a * a
    s0 = jnp.sum(jnp.where(lo, sq, 0.0), axis=-1, keepdims=True)
    s1 = jnp.sum(jnp.where(lo, 0.0, sq), axis=-1, keepdims=True)
    r = jnp.where(lo, lax.rsqrt(s0 * (1.0 / HEAD_DIM) + EPS), lax.rsqrt(s1 * (1.0 / HEAD_DIM) + EPS))
    y = a * r * gain
    rot = jnp.where(first, pltpu.roll(y, 96, 1), pltpu.roll(y, 32, 1))
    return y * cos + rot * sin_signed


def _proj_kernel(x_ref, mod_ref, g1_ref, w_ref, gains_ref, cos_ref, sin_ref,
                 qext_ref, kvc_ref, kva_ref, dil_ref, ga_ref, gm_ref):
    tm = x_ref.shape[1]
    x = x_ref[0]
    ms = jnp.mean(x * x, axis=-1, keepdims=True)
    y = x * lax.rsqrt(ms + EPS) * g1_ref[...]
    sh1 = mod_ref[0, 0:1, :]
    sc1 = mod_ref[0, 1:2, :]
    h = (y * (1.0 + sc1) + sh1).astype(BF16)
    cos = cos_ref[0]
    sin = sin_ref[0]
    lane = lax.broadcasted_iota(I32, (tm, LANES), 1)
    lo = lane < HEAD_DIM
    first = (lane & (HEAD_DIM - 1)) < (HEAD_DIM // 2)
    nr = functools.partial(_norm_rope, cos=cos, sin_signed=sin, lo=lo, first=first)

    for c in range(N_TILES // 2):
        acc = _dot(h, w_ref[:, c * 2 * LANES:(c + 1) * 2 * LANES])
        for half in range(2):
            t = 2 * c + half
            a = acc[:, half * LANES:(half + 1) * LANES]
            if t < T_KVC:
                yq = nr(a, gains_ref[0:1, :])
                rq = pltpu.roll(yq, HEAD_DIM, 1)
                if t // 2 == 0:
                    e0 = jnp.where(lo, yq, 0.0)
                    e1 = jnp.where(lo, rq, 0.0)
                else:
                    e0 = jnp.where(lo, 0.0, rq)
                    e1 = jnp.where(lo, 0.0, yq)
                qext_ref[0, 2 * t] = e0.astype(BF16)
                qext_ref[0, 2 * t + 1] = e1.astype(BF16)
            elif t < T_KVA:
                kvc_ref[0, :, (t - T_KVC) * LANES:(t - T_KVC + 1) * LANES] = a
            elif t < T_DIL:
                j = t - T_KVA
                v = nr(a, gains_ref[1:2, :]) if j % 2 == 0 else a
                kva_ref[0, :, j * LANES:(j + 1) * LANES] = v.astype(BF16)
            elif t < T_GA:
                j = t - T_DIL
                if j < 3:
                    v = nr(a, gains_ref[2:3, :])
                elif j < 6:
                    v = nr(a, gains_ref[3:4, :])
                else:
                    v = a
                dil_ref[0, :, j * LANES:(j + 1) * LANES] = v.astype(BF16)
            elif t < T_GM:
                ga_ref[0] = jax.nn.sigmoid(a)
            else:
                j = t - T_GM
                gm_ref[0, :, j * LANES:(j + 1) * LANES] = jax.nn.sigmoid(a).astype(BF16)


def _proj_call(x, mod6, g1, w_perm, gains, cos_t, sin_t):
    b, s, d = x.shape
    tm = PROJ_TM
    n = N_TILES * LANES
    row = lambda bi, i: (bi, i, 0)
    return pl.pallas_call(
        _proj_kernel,
        grid=(b, s // tm),
        in_specs=[
            pl.BlockSpec((1, tm, d), row),
            pl.BlockSpec((1, 6, d), lambda bi, i: (bi, 0, 0)),
            pl.BlockSpec((1, d), lambda bi, i: (0, 0)),
            pl.BlockSpec((d, n), lambda bi, i: (0, 0)),
            pl.BlockSpec((4, LANES), lambda bi, i: (0, 0)),
            pl.BlockSpec((1, tm, LANES), row),
            pl.BlockSpec((1, tm, LANES), row),
        ],
        out_specs=[
            pl.BlockSpec((1, NSA_HEADS, tm, LANES), lambda bi, i: (bi, 0, i, 0)),
            pl.BlockSpec((1, tm, 2 * LANES), row),
            pl.BlockSpec((1, tm, 4 * LANES), row),
            pl.BlockSpec((1, tm, 9 * LANES), row),
            pl.BlockSpec((1, tm, LANES), row),
            pl.BlockSpec((1, tm, 2 * d), row),
        ],
        out_shape=[
            jax.ShapeDtypeStruct((b, NSA_HEADS, s, LANES), BF16),
            jax.ShapeDtypeStruct((b, s, 2 * LANES), F32),
            jax.ShapeDtypeStruct((b, s, 4 * LANES), BF16),
            jax.ShapeDtypeStruct((b, s, 9 * LANES), BF16),
            jax.ShapeDtypeStruct((b, s, LANES), F32),
            jax.ShapeDtypeStruct((b, s, 2 * d), BF16),
        ],
        compiler_params=_params("parallel", "parallel"),
        name="proj",
    )(x, mod6, g1, w_perm, gains, cos_t, sin_t)


def _cmp_kernel(c_ref, pe_ref, w1_ref, w2_ref, gain_ref, cos_ref, sin_ref, kc_ref, vc_ref):
    nb = c_ref.shape[2]
    half = c_ref.shape[3]
    lane = lax.broadcasted_iota(I32, (nb, LANES), 1)
    lo = lane < HEAD_DIM
    first = (lane & (HEAD_DIM - 1)) < (HEAD_DIM // 2)
    for kind in range(2):
        out = jnp.zeros((nb, LANES), F32)
        for hd in range(2):
            c = c_ref[0, kind * 2 + hd]
            top = _dot((c + pe_ref[kind, 0:1, :]).astype(BF16), w1_ref[kind, 0:half, :])
            bot = _dot((c + pe_ref[kind, 1:2, :]).astype(BF16), w1_ref[kind, half:2 * half, :])
            hid = top + pltpu.roll(bot, nb - 1, 0)
            hid = hid * jax.nn.sigmoid(hid)
            out = out + _dot(hid.astype(BF16), w2_ref[kind, hd])
        if kind == 0:
            out = _norm_rope(out, gain_ref[...], cos_ref[0], sin_ref[0], lo, first)
            kc_ref[0] = out.astype(BF16)
        else:
            vc_ref[0] = out.astype(BF16)


def _cmp_call(chunks, pe, w1, w2ext, gain_k, ccos, csin):
    b, _, nb, half = chunks.shape
    return pl.pallas_call(
        _cmp_kernel,
        grid=(b,),
        in_specs=[
            pl.BlockSpec((1, 4, nb, half), lambda bi: (bi, 0, 0, 0)),
            pl.BlockSpec((2, 2, half), lambda bi: (0, 0, 0)),
            pl.BlockSpec((2, 2 * half, CMP_HIDDEN), lambda bi: (0, 0, 0)),
            pl.BlockSpec((2, 2, CMP_HIDDEN, LANES), lambda bi: (0, 0, 0, 0)),
            pl.BlockSpec((1, LANES), lambda bi: (0, 0)),
            pl.BlockSpec((1, nb, LANES), lambda bi: (bi, 0, 0)),
            pl.BlockSpec((1, nb, LANES), lambda bi: (bi, 0, 0)),
        ],
        out_specs=[
            pl.BlockSpec((1, nb, LANES), lambda bi: (bi, 0, 0)),
            pl.BlockSpec((1, nb, LANES), lambda bi: (bi, 0, 0)),
        ],
        out_shape=[
            jax.ShapeDtypeStruct((b, nb, LANES), BF16),
            jax.ShapeDtypeStruct((b, nb, LANES), BF16),
        ],
        compiler_params=_params("parallel"),
        name="cmp",
    )(chunks, pe, w1, w2ext, gain_k, ccos, csin)


def _softmax_cols(s):
    p = jnp.exp(s - jnp.max(s, axis=0, keepdims=True))
    return p, jnp.sum(p, axis=0, keepdims=True)


def _nsa_kernel(q_ref, kc_ref, vc_ref, ksl_ref, vsl_ref, kw_ref, vw_ref, ga_ref, ov_ref, wb_ref,
                o_ref, m_scr, l_scr, acc_scr, brow_scr):
    tq = NSA_TQ
    tk = NSA_TK
    cols = NSA_GROUP * tq
    width = cols // NSA_STREAMS
    n_blk = ov_ref.shape[0]
    bpc = tk // SEL_BLOCK
    kh = pl.program_id(1)
    t0 = pl.multiple_of(pl.program_id(2) * tq, tq)
    q4 = q_ref[0].reshape(cols, LANES)
    qh = [q4[h * width:(h + 1) * width] for h in range(NSA_STREAMS)]
    tcol = t0 + lax.broadcasted_iota(I32, (1, tq), 1)
    tcol4 = t0 + (lax.broadcasted_iota(I32, (1, cols), 1) & (tq - 1))

    s = _dot_nt(kc_ref[0], q4)
    nb = s.shape[0]
    cmp_end = lax.broadcasted_iota(I32, (nb, 1), 0) * CMP_STRIDE + (CMP_BLOCK - 1)
    valid = cmp_end <= tcol4
    e, den = _softmax_cols(jnp.where(valid, s, NEG_INF))
    p = jnp.where(valid, e / den, 0.0)
    o_cmp = _dot_tn(vc_ref[0], p.astype(BF16))

    psum = p[:, 0:tq] + p[:, tq:2 * tq] + p[:, 2 * tq:3 * tq] + p[:, 3 * tq:4 * tq]
    p_hi, p_lo = _split(psum)
    ov = ov_ref[...]
    imp = _dot(ov, p_hi) + _dot(ov, p_lo)
    blk = lax.broadcasted_iota(I32, (n_blk, 1), 0)
    blk_f = blk.astype(F32)
    rel = (tcol >> 6) - blk
    forced = (blk == 0) | ((rel >= 0) & (rel < N_LOCAL_SEL))
    score = jnp.where(rel < 0, NEG_INF, jnp.where(forced, FORCE_SCORE, imp))
    sel_bias = jnp.full((n_blk, tq), MASKED, F32)
    for _ in range(N_SEL):
        best = jnp.max(score, axis=0, keepdims=True)
        first = jnp.min(jnp.where(score == best, blk_f, float(n_blk)), axis=0, keepdims=True)
        pick = blk_f == first
        sel_bias = jnp.where(pick, 0.0, sel_bias)
        score = jnp.where(pick, PICKED, score)
    brow_scr[...] = sel_bias

    nwin = WINDOW + tq
    in_seq = lax.broadcasted_iota(I32, (nwin, 1), 0) >= (WINDOW - t0)
    wbias = jnp.where(in_seq, wb_ref[...], MASKED)
    wbias = jnp.concatenate([wbias] * (width // tq), axis=1)
    kw = kw_ref[0, pl.ds(t0, nwin), :]
    vw = vw_ref[0, pl.ds(t0, nwin), :]
    o_win = []
    for h in range(NSA_STREAMS):
        pw, lw = _softmax_cols(_dot_nt(kw, qh[h]) + wbias)
        o_win.append(_dot_tn(vw, pw.astype(BF16)) / lw)
    o_win = jnp.concatenate(o_win, axis=1)

    m_scr[...] = jnp.full(m_scr.shape, MASKED, F32)
    l_scr[...] = jnp.zeros(l_scr.shape, F32)
    acc_scr[...] = jnp.zeros(acc_scr.shape, F32)

    def block_bias(c):
        rows = brow_scr[pl.ds(pl.multiple_of(c * bpc, bpc), bpc), :]
        return jnp.broadcast_to(rows[:, None, :], (bpc, SEL_BLOCK, tq)).reshape(tk, tq)

    def chunk(c, bias):
        ks = pl.multiple_of(c * tk, tk)
        kb = ksl_ref[0, pl.ds(ks, tk), :]
        vb = vsl_ref[0, pl.ds(ks, tk), :]
        bias = jnp.concatenate([bias] * (width // tq), axis=1)
        for h in range(NSA_STREAMS):
            cs = slice(h * width, (h + 1) * width)
            sc = _dot_nt(kb, qh[h]) + bias
            m_old = m_scr[:, cs]
            m_new = jnp.maximum(m_old, jnp.max(sc, axis=0, keepdims=True))
            alpha = jnp.exp(m_old - m_new)
            pe = jnp.exp(sc - m_new)
            l_scr[:, cs] = alpha * l_scr[:, cs] + jnp.sum(pe, axis=0, keepdims=True)
            acc_scr[:, cs] = alpha * acc_scr[:, cs] + _dot_tn(vb, pe.astype(BF16))
            m_scr[:, cs] = m_new

    def body(c, carry):
        chunk(c, block_bias(c))
        return carry

    diag = t0 // tk
    lax.fori_loop(0, diag, body, 0)
    key_pos = diag * tk + lax.broadcasted_iota(I32, (tk, 1), 0)
    chunk(diag, block_bias(diag) + jnp.where(key_pos <= tcol, 0.0, MASKED))
    o_slc = acc_scr[...] / l_scr[...]

    gat = ga_ref[0].T
    is0 = kh == 0
    heads = []
    for g in range(NSA_GROUP):
        c = slice(g * tq, (g + 1) * tq)
        og = jnp.zeros((LANES, tq), F32)
        for gi, ob in enumerate((o_cmp, o_slc, o_win)):
            c0 = gi * NSA_HEADS + g
            c1 = c0 + NSA_GROUP
            gate = jnp.where(is0, gat[c0:c0 + 1, :], gat[c1:c1 + 1, :])
            og = og + gate * ob[:, c]
        heads.append(jnp.where(is0, og[0:HEAD_DIM], og[HEAD_DIM:LANES]))
    for pair in range(2):
        tile = jnp.concatenate([heads[2 * pair], heads[2 * pair + 1]], axis=0)
        o_ref[0, :, pair * LANES:(pair + 1) * LANES] = tile.T.astype(BF16)


def _nsa_call(qext, kc, vc, kva, kwp, vwp, ga, ov_t, wbias):
    b, _, s, _ = qext.shape
    tq = NSA_TQ
    nb = kc.shape[1]
    cols = NSA_GROUP * tq
    full = lambda j: pl.BlockSpec((1, s, LANES), lambda bi, k, i: (bi, 0, j))
    padded = pl.BlockSpec((1, s + WINDOW, LANES), lambda bi, k, i: (bi, 0, 0))
    return pl.pallas_call(
        _nsa_kernel,
        grid=(b, NSA_KV_HEADS, s // tq),
        in_specs=[
            pl.BlockSpec((1, NSA_GROUP, tq, LANES), lambda bi, k, i: (bi, k, i, 0)),
            pl.BlockSpec((1, nb, LANES), lambda bi, k, i: (bi, 0, 0)),
            pl.BlockSpec((1, nb, LANES), lambda bi, k, i: (bi, 0, 0)),
            full(0), full(1), padded, padded,
            pl.BlockSpec((1, tq, LANES), lambda bi, k, i: (bi, i, 0)),
            pl.BlockSpec(ov_t.shape, lambda bi, k, i: (0, 0)),
            pl.BlockSpec(wbias.shape, lambda bi, k, i: (0, 0)),
        ],
        out_specs=pl.BlockSpec((1, tq, 2 * LANES), lambda bi, k, i: (bi, i, k)),
        out_shape=jax.ShapeDtypeStruct((b, s, A_Q), BF16),
        scratch_shapes=[
            pltpu.VMEM((1, cols), F32),
            pltpu.VMEM((1, cols), F32),
            pltpu.VMEM((LANES, cols), F32),
            pltpu.VMEM(ov_t.shape[0:1] + (tq,), F32),
        ],
        compiler_params=_params("parallel", "parallel", "parallel"),
        name="nsa",
    )(qext, kc, vc, kva, kva, kwp, vwp, ga, ov_t, wbias)


def _dil_kernel(q_ref, k_ref, v_ref, o_ref, lse_ref):
    t = DIL_T
    g = pl.program_id(1)
    seg_tiles = jnp.where(g == 0, 16, jnp.where(g == 1, 4, 1))
    lo = lax.broadcasted_iota(I32, (t, LANES), 1) < HEAD_DIM
    kj = lax.broadcasted_iota(I32, (t, 2 * t), 0)
    qi = lax.broadcasted_iota(I32, (t, 2 * t), 1) & (t - 1)
    top = lax.broadcasted_iota(I32, (LANES, t), 0) < HEAD_DIM

    def tile(i, carry):
        has_prev = (i & (seg_tiles - 1)) != 0
        cs = pl.multiple_of(i * t, t)
        ps = pl.multiple_of(jnp.maximum(i - 1, 0) * t, t)
        q = q_ref[0, 0, pl.ds(cs, t), :]
        zero = jnp.zeros_like(q)
        q2 = jnp.concatenate([jnp.where(lo, q, zero), jnp.where(lo, zero, q)], axis=0)
        sp = jnp.where((kj >= qi) & has_prev, _dot_nt(k_ref[0, 0, pl.ds(ps, t), :], q2), MASKED)
        sc = jnp.where(qi >= kj, _dot_nt(k_ref[0, 0, pl.ds(cs, t), :], q2), MASKED)
        m = jnp.maximum(jnp.max(sp, axis=0, keepdims=True), jnp.max(sc, axis=0, keepdims=True))
        pp = jnp.exp(sp - m)
        pc = jnp.exp(sc - m)
        l = jnp.sum(pp, axis=0, keepdims=True) + jnp.sum(pc, axis=0, keepdims=True)
        o = (_dot_tn(v_ref[0, 0, pl.ds(ps, t), :], pp.astype(BF16))
             + _dot_tn(v_ref[0, 0, pl.ds(cs, t), :], pc.astype(BF16))) / l
        lse = m + jnp.log(l)
        o_ref[0, 0, pl.ds(cs, t), :] = jnp.where(top, o[:, 0:t], o[:, t:2 * t]).T.astype(BF16)
        lse_ref[0, 0, pl.ds(cs, t), :] = jnp.where(top, lse[:, 0:t], lse[:, t:2 * t]).T
        return carry

    lax.fori_loop(0, q_ref.shape[2] // t, tile, 0)


def _dil_call(dil_in):
    b, _, s, _ = dil_in.shape
    blk = (1, 1, s, LANES)
    return pl.pallas_call(
        _dil_kernel,
        grid=(b, DIL_GROUPS),
        in_specs=[
            pl.BlockSpec(blk, lambda bi, g: (bi, g, 0, 0)),
            pl.BlockSpec(blk, lambda bi, g: (bi, DIL_GROUPS + g, 0, 0)),
            pl.BlockSpec(blk, lambda bi, g: (bi, 2 * DIL_GROUPS + g, 0, 0)),
        ],
        out_specs=[
            pl.BlockSpec(blk, lambda bi, g: (bi, g, 0, 0)),
            pl.BlockSpec(blk, lambda bi, g: (bi, g, 0, 0)),
        ],
        out_shape=[
            jax.ShapeDtypeStruct((b, DIL_GROUPS, s, LANES), BF16),
            jax.ShapeDtypeStruct((b, DIL_GROUPS, s, LANES), F32),
        ],
        compiler_params=_params("parallel", "parallel"),
        name="dil",
    )(dil_in, dil_in, dil_in)


def _merge_kernel(x_ref, oa_ref, od_ref, lse_ref, gm_ref, mod_ref, wa_ref, wb_ref, wo_ref, g2_ref,
                  rhi_ref, rlo_ref, rb_ref, x1_ref, h2_ref, route_ref):
    tm = x_ref.shape[0]
    d = x_ref.shape[1]
    l0, l1, l2 = lse_ref[0], lse_ref[1], lse_ref[2]
    mx = jnp.maximum(jnp.maximum(l0, l1), l2)
    e0, e1, e2 = jnp.exp(l0 - mx), jnp.exp(l1 - mx), jnp.exp(l2 - mx)
    den = e0 + e1 + e2
    ob = jnp.concatenate([
        (od_ref[0].astype(F32) * (e0 / den)).astype(BF16),
        (od_ref[1].astype(F32) * (e1 / den)).astype(BF16),
        (od_ref[2].astype(F32) * (e2 / den)).astype(BF16)], axis=1)
    ya = _dot(oa_ref[...], wa_ref[...])
    yb = _dot(ob, wb_ref[...])
    y = gm_ref[:, 0:d].astype(F32) * ya + gm_ref[:, d:2 * d].astype(F32) * yb
    z = _dot(y.astype(BF16), wo_ref[...])
    x1 = x_ref[...] + mod_ref[0, 2:3, :] * z
    x1_ref[...] = x1

    ms = jnp.mean(x1 * x1, axis=-1, keepdims=True)
    h2 = x1 * lax.rsqrt(ms + EPS) * g2_ref[...]
    h2 = h2 * (1.0 + mod_ref[0, 4:5, :]) + mod_ref[0, 3:4, :]
    for s in range(ROW_TILE[0]):
        h2_ref[:, s, :] = h2[:, s * LANES:(s + 1) * LANES]

    hi, lo = _split(h2)
    rhi = rhi_ref[...]
    logits = _dot(hi, rhi) + _dot(lo, rhi) + _dot(hi, rlo_ref[...]) + rb_ref[...]
    lane = lax.broadcasted_iota(I32, (tm, LANES), 1)
    gl = jnp.where(lane < N_GROUPS, logits, NEG_INF)
    gmax = jnp.max(gl, axis=-1, keepdims=True)
    g_w = 1.0 / jnp.sum(jnp.exp(gl - gmax), axis=-1, keepdims=True)
    g_idx = jnp.min(jnp.where(gl == gmax, lane, LANES), axis=-1, keepdims=True)
    e_lo = N_GROUPS + EXPERTS_PER_GROUP * g_idx
    el = jnp.where((lane >= e_lo) & (lane < e_lo + EXPERTS_PER_GROUP), logits, NEG_INF)
    m1 = jnp.max(el, axis=-1, keepdims=True)
    i1 = jnp.min(jnp.where(el == m1, lane, LANES), axis=-1, keepdims=True)
    el2 = jnp.where(lane == i1, PICKED, el)
    m2 = jnp.max(el2, axis=-1, keepdims=True)
    i2 = jnp.min(jnp.where(el2 == m2, lane, LANES), axis=-1, keepdims=True)
    ex = jnp.exp(m2 - m1)
    w1 = g_w * (1.0 / (1.0 + ex))
    w2 = g_w * (ex / (1.0 + ex))
    route = jnp.where(lane == 0, (i1 - N_GROUPS).astype(F32),
                      jnp.where(lane == 1, (i2 - N_GROUPS).astype(F32),
                                jnp.where(lane == 2, w1, jnp.where(lane == 3, w2, 0.0))))
    route_ref[...] = route


def _merge_call(x2, oa, od, lse, gm, mod6, wa, wb, wo, g2, rhi, rlo, rb, s):
    t, d = x2.shape
    tm = MERGE_TM
    per_b = s // tm
    row = lambda i: (i, 0)
    const = lambda i: (0, 0)
    return pl.pallas_call(
        _merge_kernel,
        grid=(t // tm,),
        in_specs=[
            pl.BlockSpec((tm, d), row),
            pl.BlockSpec((tm, A_Q), row),
            pl.BlockSpec((DIL_GROUPS, tm, LANES), lambda i: (0, i, 0)),
            pl.BlockSpec((DIL_GROUPS, tm, LANES), lambda i: (0, i, 0)),
            pl.BlockSpec((tm, 2 * d), row),
            pl.BlockSpec((1, 6, d), lambda i: (i // per_b, 0, 0)),
            pl.BlockSpec(wa.shape, const),
            pl.BlockSpec(wb.shape, const),
            pl.BlockSpec(wo.shape, const),
            pl.BlockSpec((1, d), const),
            pl.BlockSpec(rhi.shape, const),
            pl.BlockSpec(rlo.shape, const),
            pl.BlockSpec((1, LANES), const),
        ],
        out_specs=[
            pl.BlockSpec((tm, d), row),
            pl.BlockSpec((tm,) + ROW_TILE, lambda i: (i, 0, 0)),
            pl.BlockSpec((tm, LANES), row),
        ],
        out_shape=[
            jax.ShapeDtypeStruct((t, d), F32),
            jax.ShapeDtypeStruct((t,) + ROW_TILE, F32),
            jax.ShapeDtypeStruct((t, LANES), F32),
        ],
        compiler_params=_params("parallel"),
        name="merge",
    )(x2, oa, od, lse, gm, mod6, wa, wb, wo, g2, rhi, rlo, rb)


def _rank_kernel(route_ref, rank_ref, count_ref, carry_scr):
    tm = route_ref.shape[0]

    @pl.when(pl.program_id(0) == 0)
    def _():
        carry_scr[...] = jnp.zeros(carry_scr.shape, F32)

    route = route_ref[...]
    lane = lax.broadcasted_iota(I32, (tm, LANES), 1)
    e1 = route[:, 0:1].astype(I32)
    e2 = route[:, 1:2].astype(I32)
    hit1 = lane == e1
    hit2 = lane == e2
    cnt = jnp.where(hit1 | hit2, 1.0, 0.0)
    r = lax.broadcasted_iota(I32, (tm, tm), 0)
    c = lax.broadcasted_iota(I32, (tm, tm), 1)
    below = jnp.where(c < r, 1.0, 0.0).astype(BF16)
    before = _dot(below, cnt.astype(BF16)) + carry_scr[...]
    r1 = jnp.sum(jnp.where(hit1, before, 0.0), axis=-1, keepdims=True)
    r2 = jnp.sum(jnp.where(hit2, before, 0.0), axis=-1, keepdims=True)
    rank_ref[...] = jnp.where(lane == 0, r1, jnp.where(lane == 1, r2, 0.0))
    carry_scr[...] = carry_scr[...] + jnp.sum(cnt, axis=0, keepdims=True)
    count_ref[...] = carry_scr[...]


def _rank_call(route):
    t = route.shape[0]
    tm = RANK_TM
    return pl.pallas_call(
        _rank_kernel,
        grid=(t // tm,),
        in_specs=[pl.BlockSpec((tm, LANES), lambda i: (i, 0))],
        out_specs=[
            pl.BlockSpec((tm, LANES), lambda i: (i, 0)),
            pl.BlockSpec((1, LANES), lambda i: (0, 0)),
        ],
        out_shape=[
            jax.ShapeDtypeStruct((t, LANES), F32),
            jax.ShapeDtypeStruct((1, LANES), F32),
        ],
        scratch_shapes=[pltpu.VMEM((1, LANES), F32)],
        compiler_params=_params("arbitrary"),
        name="rank",
    )(route)


def _issue_unrolled(n, fn):
    def body(j, carry):
        for u in range(DMA_UNROLL):
            fn(j * DMA_UNROLL + u)
        return carry
    lax.fori_loop(0, n // DMA_UNROLL, body, 0)


def _dispatch_kernel(cnt_ref, pstart_ref, dest_hbm, h_hbm, xs_hbm, idx_smem, hbuf, zbuf,
                     isem, hsem, dsem, zsem):
    tm = MOE_TM
    nd = TOP_K * tm
    i = pl.program_id(0)
    n = pl.num_programs(0)
    slot = i % 2
    buf = i % 3
    n_slots = xs_hbm.shape[0]

    def idx_copy(step, sl):
        return pltpu.make_async_copy(dest_hbm.at[step], idx_smem.at[sl], isem.at[sl])

    def tile_copy(step, bf):
        rows = pl.ds(pl.multiple_of(step * tm, tm), tm)
        return pltpu.make_async_copy(h_hbm.at[rows], hbuf.at[bf], hsem.at[bf])

    def wait_scatter(bf):
        for _ in range(TOP_K):
            pltpu.make_async_copy(hbuf.at[bf], xs_hbm.at[pl.ds(0, tm)], dsem.at[bf]).wait()

    @pl.when(i == 0)
    def _():
        idx_copy(0, 0).start()
        tile_copy(0, 0).start()

        @pl.when(n > 1)
        def _():
            tile_copy(1, 1).start()
        zbuf[...] = jnp.zeros(zbuf.shape, F32)

        def zero_row(row):
            return pltpu.make_async_copy(zbuf.at[0], xs_hbm.at[row], zsem)

        def per_expert(e, used):
            base = pstart_ref[e]
            cnt = cnt_ref[e]
            pad = (cnt + MOE_BLOCK - 1) // MOE_BLOCK * MOE_BLOCK

            def start(r, c):
                zero_row(base + r).start()
                return c

            def wait(r, c):
                zero_row(base + r).wait()
                return c
            lax.fori_loop(cnt, pad, start, 0)
            lax.fori_loop(cnt, pad, wait, 0)
            return used + pad
        used = lax.fori_loop(0, N_EXPERTS, per_expert, 0)

        def zero_block(blk):
            rows = pl.ds(pl.multiple_of(blk * MOE_BLOCK, MOE_BLOCK), MOE_BLOCK)
            return pltpu.make_async_copy(zbuf, xs_hbm.at[rows], zsem)

        def start_blk(blk, c):
            zero_block(blk).start()
            return c

        def wait_blk(blk, c):
            zero_block(blk).wait()
            return c
        lax.fori_loop(used // MOE_BLOCK, n_slots // MOE_BLOCK, start_blk, 0)
        lax.fori_loop(used // MOE_BLOCK, n_slots // MOE_BLOCK, wait_blk, 0)

    idx_copy(i, slot).wait()

    @pl.when(i + 1 < n)
    def _():
        idx_copy(i + 1, 1 - slot).start()

    tile_copy(i, buf).wait()

    def one(r):
        src = hbuf.at[buf, r & (tm - 1)]
        pltpu.make_async_copy(src, xs_hbm.at[idx_smem[slot, r]], dsem.at[buf]).start()
    _issue_unrolled(nd, one)

    @pl.when(i > 0)
    def _():
        wait_scatter((i + 2) % 3)

    @pl.when(i + 2 < n)
    def _():
        tile_copy(i + 2, (i + 2) % 3).start()

    @pl.when(i == n - 1)
    def _():
        wait_scatter(buf)


def _dispatch_call(counts, pad_starts, dest_tiles, h2, n_slots):
    n_steps = dest_tiles.shape[0]
    grid_spec = pltpu.PrefetchScalarGridSpec(
        num_scalar_prefetch=2,
        grid=(n_steps,),
        in_specs=[pl.BlockSpec(memory_space=pl.ANY), pl.BlockSpec(memory_space=pl.ANY)],
        out_specs=pl.BlockSpec(memory_space=pl.ANY),
        scratch_shapes=[
            pltpu.SMEM((2, TOP_K * MOE_TM), I32),
            pltpu.VMEM((3, MOE_TM) + ROW_TILE, F32),
            pltpu.VMEM((MOE_BLOCK,) + ROW_TILE, F32),
            pltpu.SemaphoreType.DMA((2,)),
            pltpu.SemaphoreType.DMA((3,)),
            pltpu.SemaphoreType.DMA((3,)),
            pltpu.SemaphoreType.DMA(()),
        ],
    )
    return pl.pallas_call(
        _dispatch_kernel,
        grid_spec=grid_spec,
        out_shape=jax.ShapeDtypeStruct((n_slots,) + ROW_TILE, F32),
        compiler_params=_params("arbitrary"),
        name="dispatch",
    )(counts, pad_starts, dest_tiles, h2)


def _expert_kernel(be_ref, x_ref, wg_ref, wu_ref, wd_ref, y_ref):
    del be_ref
    ns = ROW_TILE[0]
    xb = jnp.concatenate([x_ref[:, s, :] for s in range(ns)], axis=1).astype(BF16)
    gate = _dot(xb, wg_ref[0])
    up = _dot(xb, wu_ref[0])
    hid = (gate * jax.nn.sigmoid(gate) * up).astype(BF16)
    y = _dot(hid, wd_ref[0])
    for s in range(ns):
        y_ref[:, s, :] = y[:, s * LANES:(s + 1) * LANES]


def _expert_call(blk_expert, xs, wg, wu, wd):
    n_blk = blk_expert.shape[0]
    d = wg.shape[1]
    blk = (MOE_BLOCK,) + ROW_TILE
    grid_spec = pltpu.PrefetchScalarGridSpec(
        num_scalar_prefetch=1,
        grid=(n_blk,),
        in_specs=[
            pl.BlockSpec(blk, lambda i, be: (i, 0, 0)),
            pl.BlockSpec((1, d, EXPERT_FF), lambda i, be: (be[i], 0, 0)),
            pl.BlockSpec((1, d, EXPERT_FF), lambda i, be: (be[i], 0, 0)),
            pl.BlockSpec((1, EXPERT_FF, d), lambda i, be: (be[i], 0, 0)),
        ],
        out_specs=pl.BlockSpec(blk, lambda i, be: (i, 0, 0)),
    )
    return pl.pallas_call(
        _expert_kernel,
        grid_spec=grid_spec,
        out_shape=jax.ShapeDtypeStruct(xs.shape, F32),
        compiler_params=_params("arbitrary"),
        name="experts",
    )(blk_expert, xs, wg, wu, wd)


def _combine_kernel(dest_hbm, y_hbm, x1_ref, route_ref, mod_ref, o_ref, idx_smem, ybuf, isem, dsem):
    tm = MOE_TM
    nd = TOP_K * tm
    i = pl.program_id(0)
    n = pl.num_programs(0)
    slot = i % 2
    nxt = 1 - slot

    def idx_copy(step, sl):
        return pltpu.make_async_copy(dest_hbm.at[step], idx_smem.at[sl], isem.at[sl])

    def step_rows(sl):
        return pltpu.make_async_copy(y_hbm.at[pl.ds(0, nd)], ybuf.at[sl], dsem.at[sl])

    def issue_rows(sl):
        def one(r):
            pltpu.make_async_copy(y_hbm.at[idx_smem[sl, r]], ybuf.at[sl, r], dsem.at[sl]).start()
        _issue_unrolled(nd, one)

    @pl.when(i == 0)
    def _():
        idx_copy(0, 0).start()
        idx_copy(0, 0).wait()
        issue_rows(0)

        @pl.when(n > 1)
        def _():
            idx_copy(1, 1).start()

    @pl.when(i + 1 < n)
    def _():
        idx_copy(i + 1, nxt).wait()
        issue_rows(nxt)

    @pl.when(i + 2 < n)
    def _():
        idx_copy(i + 2, slot).start()

    step_rows(slot).wait()
    route = route_ref[...]
    w1 = route[:, 2:3]
    w2 = route[:, 3:4]
    for s in range(ROW_TILE[0]):
        cols = slice(s * LANES, (s + 1) * LANES)
        y = w1 * ybuf[slot, 0:tm, s, :] + w2 * ybuf[slot, tm:nd, s, :]
        o_ref[:, cols] = x1_ref[:, cols] + mod_ref[0, 5:6, cols] * y


def _combine_call(dest_tiles, y_slots, x1, route, mod6, s):
    t, d = x1.shape
    tm = MOE_TM
    per_b = s // tm
    return pl.pallas_call(
        _combine_kernel,
        grid=(t // tm,),
        in_specs=[
            pl.BlockSpec(memory_space=pl.ANY),
            pl.BlockSpec(memory_space=pl.ANY),
            pl.BlockSpec((tm, d), lambda i: (i, 0)),
            pl.BlockSpec((tm, LANES), lambda i: (i, 0)),
            pl.BlockSpec((1, 6, d), lambda i: (i // per_b, 0, 0)),
        ],
        out_specs=pl.BlockSpec((tm, d), lambda i: (i, 0)),
        out_shape=jax.ShapeDtypeStruct((t, d), F32),
        scratch_shapes=[
            pltpu.SMEM((2, TOP_K * tm), I32),
            pltpu.VMEM((2, TOP_K * tm) + ROW_TILE, F32),
            pltpu.SemaphoreType.DMA((2,)),
            pltpu.SemaphoreType.DMA((2,)),
        ],
        compiler_params=_params("arbitrary"),
        name="combine",
    )(dest_tiles, y_slots, x1, route, mod6)


def _rope_tiles(pos):
    inv = ROPE_THETA ** (-jnp.arange(0, HEAD_DIM, 2, dtype=F32) / HEAD_DIM)
    ang = pos.astype(F32)[..., None] * inv
    cos, sin = jnp.cos(ang), jnp.sin(ang)
    return (jnp.concatenate([cos, cos, cos, cos], axis=-1),
            jnp.concatenate([-sin, sin, -sin, sin], axis=-1))


def _selection_constants(s):
    n_c = s // CMP_STRIDE
    n_s = s // SEL_BLOCK
    cs = np.arange(n_c) * CMP_STRIDE
    ss = np.arange(n_s) * SEL_BLOCK
    ov = np.clip(np.minimum(cs[:, None] + CMP_BLOCK, ss[None, :] + SEL_BLOCK)
                 - np.maximum(cs[:, None], ss[None, :]), 0, None).astype(np.float32) / CMP_BLOCK
    dist = (np.arange(NSA_TQ)[None, :] + WINDOW) - np.arange(WINDOW + NSA_TQ)[:, None]
    wbias = np.where((dist >= 0) & (dist <= WINDOW), 0.0, MASKED).astype(np.float32)
    return jnp.asarray(ov.T, BF16), jnp.asarray(wbias)


def _to_residue_major(t, dil):
    b, s, w = t.shape
    return t.reshape(b, s // dil, dil, w).transpose(0, 2, 1, 3).reshape(b, s, w)


def _from_residue_major(t, dil):
    b, s, w = t.shape
    return t.reshape(b, dil, s // dil, w).transpose(0, 2, 1, 3).reshape(b, s, w)


def _mixer_and_router(x, c, positions, w_ada, b_ada, norm1_g, norm2_g, w_in, nsa_q_norm, nsa_k_norm,
                      cmp_pe_k, cmp_w1_k, cmp_w2_k, cmp_pe_v, cmp_w1_v, cmp_w2_v, dil_q_norm,
                      dil_k_norm, w_up_a, w_up_b, w_out, w_group, b_group, w_router, b_router):
    b, s, d = x.shape
    scale = HEAD_DIM ** -0.5
    mod6 = _mod_call(c, w_ada, b_ada).reshape(b, 6, d)

    c1 = A_Q
    c2 = c1 + 6 * A_KV
    c3 = c2 + 3 * NSA_HEADS
    c4 = c3 + 3 * DIL_W
    w_perm = jnp.concatenate([
        w_in[:, :c2], w_in[:, c3:c4],
        jnp.pad(w_in[:, c2:c3], ((0, 0), (0, LANES - 3 * NSA_HEADS))), w_in[:, c4:]], axis=1).astype(BF16)
    two = lambda g: jnp.concatenate([g, g]).astype(F32)
    gains = jnp.stack([two(nsa_q_norm) * scale, two(nsa_k_norm), two(dil_q_norm) * scale, two(dil_k_norm)])
    cos_t, sin_t = _rope_tiles(positions)
    qext, kvc, kva, dil, ga, gm = _proj_call(x, mod6, norm1_g.reshape(1, d), w_perm, gains, cos_t, sin_t)

    n_chunk = s // CMP_STRIDE
    half = CMP_STRIDE * HEAD_DIM
    chunks = kvc.reshape(b, s, 4, HEAD_DIM).transpose(0, 2, 1, 3).reshape(b, 4, n_chunk, half)
    pe = jnp.stack([cmp_pe_k.reshape(2, half), cmp_pe_v.reshape(2, half)])
    w1 = jnp.stack([cmp_w1_k, cmp_w1_v]).astype(BF16)
    zeros = jnp.zeros((CMP_HIDDEN, HEAD_DIM), F32)
    ext = lambda w: jnp.stack([jnp.concatenate([w, zeros], 1), jnp.concatenate([zeros, w], 1)])
    w2ext = jnp.stack([ext(cmp_w2_k), ext(cmp_w2_v)]).astype(BF16)
    cmp_pos = jnp.pad(positions[:, CMP_BLOCK - 1::CMP_STRIDE], ((0, 0), (0, 1)))
    ccos, csin = _rope_tiles(cmp_pos)
    kc, vc = _cmp_call(chunks, pe, w1, w2ext, two(nsa_k_norm).reshape(1, LANES), ccos, csin)

    ov_t, wbias = _selection_constants(s)
    front = lambda j: jnp.pad(kva[:, :, j * LANES:(j + 1) * LANES], ((0, 0), (WINDOW, 0), (0, 0)))
    o_a = _nsa_call(qext, kc, vc, kva, front(2), front(3), ga, ov_t, wbias)

    parts = []
    for kind in range(3):
        for gi, (_, dl) in enumerate(DIL_PAIRS):
            col = (kind * DIL_GROUPS + gi) * LANES
            parts.append(_to_residue_major(dil[:, :, col:col + LANES], dl))
    o_dil, lse_dil = _dil_call(jnp.stack(parts, axis=1))
    od = jnp.stack([_from_residue_major(o_dil[:, gi], dl) for gi, (_, dl) in enumerate(DIL_PAIRS)])
    lse = jnp.stack([_from_residue_major(lse_dil[:, gi], dl) for gi, (_, dl) in enumerate(DIL_PAIRS)])

    t = b * s
    w_r = jnp.concatenate([w_group, w_router.transpose(1, 0, 2).reshape(d, N_EXPERTS)], axis=1)
    w_r = jnp.pad(w_r, ((0, 0), (0, LANES - w_r.shape[1])))
    rhi, rlo = _split(w_r)
    rb = jnp.pad(jnp.concatenate([b_group, b_router.reshape(-1)]), (0, LANES - N_GROUPS - N_EXPERTS))
    return _merge_call(
        x.reshape(t, d), o_a.reshape(t, A_Q), od.reshape(DIL_GROUPS, t, LANES),
        lse.reshape(DIL_GROUPS, t, LANES), gm.reshape(t, 2 * d), mod6,
        w_up_a.astype(BF16), w_up_b.astype(BF16), w_out.astype(BF16), norm2_g.reshape(1, d),
        rhi, rlo, rb.reshape(1, LANES).astype(F32), s), mod6


def _moe(x1, h2, route, mod6, w_e_gate, w_e_up, w_e_down, s):
    t, _ = x1.shape
    rank, counts = _rank_call(route)
    expert = route[:, 0:TOP_K].astype(I32)
    counts = counts[0, :N_EXPERTS].astype(I32)
    padded = (counts + MOE_BLOCK - 1) // MOE_BLOCK * MOE_BLOCK
    pad_ends = jnp.cumsum(padded)
    pad_starts = pad_ends - padded
    dest = pad_starts[expert] + rank[:, 0:TOP_K].astype(I32)
    n_slots = t * TOP_K + N_EXPERTS * MOE_BLOCK
    n_blk = n_slots // MOE_BLOCK
    blk_start = jnp.arange(n_blk, dtype=I32) * MOE_BLOCK
    blk_expert = jnp.minimum(jnp.sum((pad_ends[None, :] <= blk_start[:, None]).astype(I32), axis=1),
                             N_EXPERTS - 1)
    tm = MOE_TM
    dest_tiles = dest.reshape(t // tm, tm, TOP_K).transpose(0, 2, 1).reshape(t // tm, TOP_K * tm)
    xs = _dispatch_call(counts, pad_starts.astype(I32), dest_tiles, h2, n_slots)
    y_slots = _expert_call(blk_expert, xs, w_e_gate.astype(BF16), w_e_up.astype(BF16),
                           w_e_down.astype(BF16))
    return _combine_call(dest_tiles, y_slots, x1, route, mod6, s)


def kernel(x, c, positions, w_ada, b_ada, norm1_g, norm2_g, w_in, nsa_q_norm, nsa_k_norm, cmp_pe_k,
           cmp_w1_k, cmp_w2_k, cmp_pe_v, cmp_w1_v, cmp_w2_v, dil_q_norm, dil_k_norm, w_up_a, w_up_b,
           w_out, w_group, b_group, w_router, b_router, w_e_gate, w_e_up, w_e_down):
    b, s, d = x.shape
    assert w_ada.shape[0] == 1 and d == D_MODEL and s % NSA_TK == 0
    (x1, h2, route), mod6 = _mixer_and_router(
        x, c, positions, w_ada[0], b_ada[0], norm1_g[0], norm2_g[0], w_in[0], nsa_q_norm[0],
        nsa_k_norm[0], cmp_pe_k[0], cmp_w1_k[0], cmp_w2_k[0], cmp_pe_v[0], cmp_w1_v[0], cmp_w2_v[0],
        dil_q_norm[0], dil_k_norm[0], w_up_a[0], w_up_b[0], w_out[0], w_group[0], b_group[0],
        w_router[0], b_router[0])
    out = _moe(x1, h2, route, mod6, w_e_gate[0], w_e_up[0], w_e_down[0], s)
    return out.reshape(b, s, d)
```

```python
import functools

import jax
import jax.numpy as jnp
import numpy as np
from jax import lax
from jax.experimental import pallas as pl
from jax.experimental.pallas import tpu as pltpu

F32 = jnp.float32
BF16 = jnp.bfloat16
I32 = jnp.int32

D_MODEL = 1024
HEAD_DIM = 64
LANES = 128
ROW_TILE = (D_MODEL // LANES, LANES)
ROPE_THETA = 10000.0
EPS = 1e-6
LOG2E = 1.4426950408889634
NEG_INF = -1e30
FORCE_SCORE = 1e9
MASKED = -1e30
PICKED = -3e38

NSA_HEADS = 8
NSA_KV_HEADS = 2
NSA_GROUP = 4
CMP_BLOCK = 32
CMP_STRIDE = 16
CMP_HIDDEN = 256
SEL_BLOCK = 64
N_SEL = 8
N_LOCAL_SEL = 2
WINDOW = 512
DIL_PAIRS = ((128, 1), (512, 4), (2048, 16))
DIL_GROUPS = 3
A_Q = NSA_HEADS * HEAD_DIM
A_KV = NSA_KV_HEADS * HEAD_DIM
DIL_W = 2 * DIL_GROUPS * HEAD_DIM
N_GROUPS = 4
EXPERTS_PER_GROUP = 8
N_EXPERTS = 32
TOP_K = 2
EXPERT_FF = 512
MOE_BLOCK = 256

VMEM_LIMIT = 56 * 1024 * 1024

T_QA = 0
T_KVC = 4
T_KVA = 6
T_DIL = 10
T_GA = 19
T_GM = 20
N_TILES = 36

PROJ_TM = 256
NSA_TQ = 128
NSA_TK = 512
DIL_T = 128
DIL_UNROLL = 4
MERGE_TM = 256
RANK_TM = 512
MOE_TM = 256
DMA_UNROLL = 8


def _dot(a, b):
    return jnp.dot(a, b, preferred_element_type=F32)


def _dot_nt(a, b):
    return lax.dot_general(a, b, (((1,), (1,)), ((), ())), preferred_element_type=F32)


def _dot_tn(a, b):
    return lax.dot_general(a, b, (((0,), (0,)), ((), ())), preferred_element_type=F32)


def _split(a):
    hi = a.astype(BF16)
    lo = (a - hi.astype(F32)).astype(BF16)
    return hi, lo


def _load_token_tiles(ref, first, n, lead=()):
    ns = ROW_TILE[0]
    return jnp.concatenate(
        [ref[lead + (pl.ds(first * ns + s, n, stride=ns), slice(None))] for s in range(ns)], axis=1)


def _store_token_tiles(ref, rows):
    ns = ROW_TILE[0]
    for s in range(ns):
        ref[pl.ds(s, rows.shape[0], stride=ns), :] = rows[:, s * LANES:(s + 1) * LANES]


def _params(*sem):
    return pltpu.CompilerParams(dimension_semantics=sem, vmem_limit_bytes=VMEM_LIMIT)


def _mod_kernel(c_ref, whi_ref, wlo_ref, b_ref, o_ref):
    c = c_ref[...]
    ca = c * jax.nn.sigmoid(c)
    hi, lo = _split(ca)
    whi = whi_ref[...]
    o_ref[...] = _dot(hi, whi) + _dot(lo, whi) + _dot(hi, wlo_ref[...]) + b_ref[...]


def _mod_call(c, w_ada, b_ada):
    b, d = c.shape
    n = w_ada.shape[1]
    whi, wlo = _split(w_ada)
    tn = 1024
    return pl.pallas_call(
        _mod_kernel,
        grid=(n // tn,),
        in_specs=[
            pl.BlockSpec((b, d), lambda j: (0, 0)),
            pl.BlockSpec((d, tn), lambda j: (0, j)),
            pl.BlockSpec((d, tn), lambda j: (0, j)),
            pl.BlockSpec((1, tn), lambda j: (0, j)),
        ],
        out_specs=pl.BlockSpec((b, tn), lambda j: (0, j)),
        out_shape=jax.ShapeDtypeStruct((b, n), F32),
        compiler_params=_params("parallel"),
        name="mod",
    )(c, whi, wlo, b_ada.reshape(1, n))


def _norm_rope(a, gain, cos, sin_signed, lo, first):
    sq = a * a
    s0 = jnp.sum(jnp.where(lo, sq, 0.0), axis=-1, keepdims=True)
    s1 = jnp.sum(jnp.where(lo, 0.0, sq), axis=-1, keepdims=True)
    r = jnp.where(lo, lax.rsqrt(s0 * (1.0 / HEAD_DIM) + EPS), lax.rsqrt(s1 * (1.0 / HEAD_DIM) + EPS))
    y = a * r * gain
    rot = jnp.where(first, pltpu.roll(y, 96, 1), pltpu.roll(y, 32, 1))
    return y * cos + rot * sin_signed


def _proj_kernel(x_ref, mod_ref, g1_ref, w_ref, gains_ref, cos_ref, sin_ref, kw0_ref, vw0_ref,
                 qext_ref, kvc_ref, ksl_ref, vsl_ref, kwp_ref, vwp_ref, dil_ref, ga_ref, gm_ref):
    del kw0_ref, vw0_ref
    tm = x_ref.shape[1]
    x = x_ref[0]
    ms = jnp.mean(x * x, axis=-1, keepdims=True)
    y = x * lax.rsqrt(ms + EPS) * g1_ref[...]
    sh1 = mod_ref[0, 0:1, :]
    sc1 = mod_ref[0, 1:2, :]
    h = (y * (1.0 + sc1) + sh1).astype(BF16)
    cos = cos_ref[0]
    sin = sin_ref[0]
    lane = lax.broadcasted_iota(I32, (tm, LANES), 1)
    lo = lane < HEAD_DIM
    first = (lane & (HEAD_DIM - 1)) < (HEAD_DIM // 2)
    nr = functools.partial(_norm_rope, cos=cos, sin_signed=sin, lo=lo, first=first)

    for c in range(N_TILES // 2):
        acc = _dot(h, w_ref[:, c * 2 * LANES:(c + 1) * 2 * LANES])
        for half in range(2):
            t = 2 * c + half
            a = acc[:, half * LANES:(half + 1) * LANES]
            if t < T_KVC:
                yq = nr(a, gains_ref[0:1, :])
                rq = pltpu.roll(yq, HEAD_DIM, 1)
                if t // 2 == 0:
                    e0 = jnp.where(lo, yq, 0.0)
                    e1 = jnp.where(lo, rq, 0.0)
                else:
                    e0 = jnp.where(lo, 0.0, rq)
                    e1 = jnp.where(lo, 0.0, yq)
                qext_ref[0, 2 * t] = e0.astype(BF16)
                qext_ref[0, 2 * t + 1] = e1.astype(BF16)
            elif t < T_KVA:
                kvc_ref[0, :, (t - T_KVC) * LANES:(t - T_KVC + 1) * LANES] = a
            elif t < T_DIL:
                j = t - T_KVA
                k_out, v_out = (ksl_ref, vsl_ref) if j < 2 else (kwp_ref, vwp_ref)
                if j % 2 == 0:
                    k_out[0] = nr(a, gains_ref[1:2, :]).astype(BF16)
                else:
                    v_out[0, 0] = jnp.where(lo, a, 1.0).astype(BF16)
                    v_out[0, 1] = jnp.where(lo, pltpu.roll(a, HEAD_DIM, 1), 1.0).astype(BF16)
            elif t < T_GA:
                j = t - T_DIL
                if j < 3:
                    v = nr(a, gains_ref[2:3, :])
                elif j < 6:
                    v = nr(a, gains_ref[3:4, :])
                else:
                    v = a
                dil_ref[0, :, j * LANES:(j + 1) * LANES] = v.astype(BF16)
            elif t < T_GM:
                ga_ref[0] = jax.nn.sigmoid(a)
            else:
                j = t - T_GM
                gm_ref[0, :, j * LANES:(j + 1) * LANES] = jax.nn.sigmoid(a).astype(BF16)


def _proj_call(x, mod6, g1, w_perm, gains, cos_t, sin_t):
    b, s, d = x.shape
    tm = PROJ_TM
    n = N_TILES * LANES
    row = lambda bi, i: (bi, i, 0)
    heads = lambda bi, i: (bi, 0, i, 0)
    pad = WINDOW // tm
    in_specs = [
        pl.BlockSpec((1, tm, d), row),
        pl.BlockSpec((1, 6, d), lambda bi, i: (bi, 0, 0)),
        pl.BlockSpec((1, d), lambda bi, i: (0, 0)),
        pl.BlockSpec((d, n), lambda bi, i: (0, 0)),
        pl.BlockSpec((4, LANES), lambda bi, i: (0, 0)),
        pl.BlockSpec((1, tm, LANES), row),
        pl.BlockSpec((1, tm, LANES), row),
        pl.BlockSpec(memory_space=pl.ANY),
        pl.BlockSpec(memory_space=pl.ANY),
    ]
    out_specs = [
        pl.BlockSpec((1, NSA_HEADS, tm, LANES), heads),
        pl.BlockSpec((1, tm, 2 * LANES), row),
        pl.BlockSpec((1, tm, LANES), row),
        pl.BlockSpec((1, NSA_KV_HEADS, tm, LANES), heads),
        pl.BlockSpec((1, tm, LANES), lambda bi, i: (bi, i + pad, 0)),
        pl.BlockSpec((1, NSA_KV_HEADS, tm, LANES), lambda bi, i: (bi, 0, i + pad, 0)),
        pl.BlockSpec((1, tm, 9 * LANES), row),
        pl.BlockSpec((1, tm, LANES), row),
        pl.BlockSpec((1, tm, 2 * d), row),
    ]
    out_shape = [
        jax.ShapeDtypeStruct((b, NSA_HEADS, s, LANES), BF16),
        jax.ShapeDtypeStruct((b, s, 2 * LANES), F32),
        jax.ShapeDtypeStruct((b, s, LANES), BF16),
        jax.ShapeDtypeStruct((b, NSA_KV_HEADS, s, LANES), BF16),
        jax.ShapeDtypeStruct((b, s + WINDOW, LANES), BF16),
        jax.ShapeDtypeStruct((b, NSA_KV_HEADS, s + WINDOW, LANES), BF16),
        jax.ShapeDtypeStruct((b, s, 9 * LANES), BF16),
        jax.ShapeDtypeStruct((b, s, LANES), F32),
        jax.ShapeDtypeStruct((b, s, 2 * d), BF16),
    ]
    kw0 = jnp.zeros(out_shape[4].shape, BF16)
    vw0 = jnp.zeros(out_shape[5].shape, BF16)
    return pl.pallas_call(
        _proj_kernel,
        grid=(b, s // tm),
        in_specs=in_specs,
        out_specs=out_specs,
        out_shape=out_shape,
        input_output_aliases={7: 4, 8: 5},
        compiler_params=_params("parallel", "parallel"),
        name="proj",
    )(x, mod6, g1, w_perm, gains, cos_t, sin_t, kw0, vw0)


def _cmp_kernel(c_ref, pe_ref, w1_ref, w2_ref, gain_ref, cos_ref, sin_ref, kc_ref, vc_ref):
    nb = c_ref.shape[2]
    half = c_ref.shape[3]
    lane = lax.broadcasted_iota(I32, (nb, LANES), 1)
    lo = lane < HEAD_DIM
    first = (lane & (HEAD_DIM - 1)) < (HEAD_DIM // 2)
    for kind in range(2):
        out = jnp.zeros((nb, LANES), F32)
        for hd in range(2):
            c = c_ref[0, kind * 2 + hd]
            top = _dot((c + pe_ref[kind, 0:1, :]).astype(BF16), w1_ref[kind, 0:half, :])
            bot = _dot((c + pe_ref[kind, 1:2, :]).astype(BF16), w1_ref[kind, half:2 * half, :])
            hid = top + pltpu.roll(bot, nb - 1, 0)
            hid = hid * jax.nn.sigmoid(hid)
            out = out + _dot(hid.astype(BF16), w2_ref[kind, hd])
        if kind == 0:
            out = _norm_rope(out, gain_ref[...], cos_ref[0], sin_ref[0], lo, first)
            kc_ref[0] = out.astype(BF16)
        else:
            vc_ref[0, 0] = jnp.where(lo, out, 0.0).astype(BF16)
            vc_ref[0, 1] = jnp.where(lo, pltpu.roll(out, HEAD_DIM, 1), 0.0).astype(BF16)


def _cmp_call(chunks, pe, w1, w2ext, gain_k, ccos, csin):
    b, _, nb, half = chunks.shape
    return pl.pallas_call(
        _cmp_kernel,
        grid=(b,),
        in_specs=[
            pl.BlockSpec((1, 4, nb, half), lambda bi: (bi, 0, 0, 0)),
            pl.BlockSpec((2, 2, half), lambda bi: (0, 0, 0)),
            pl.BlockSpec((2, 2 * half, CMP_HIDDEN), lambda bi: (0, 0, 0)),
            pl.BlockSpec((2, 2, CMP_HIDDEN, LANES), lambda bi: (0, 0, 0, 0)),
            pl.BlockSpec((1, LANES), lambda bi: (0, 0)),
            pl.BlockSpec((1, nb, LANES), lambda bi: (bi, 0, 0)),
            pl.BlockSpec((1, nb, LANES), lambda bi: (bi, 0, 0)),
        ],
        out_specs=[
            pl.BlockSpec((1, nb, LANES), lambda bi: (bi, 0, 0)),
            pl.BlockSpec((1, NSA_KV_HEADS, nb, LANES), lambda bi: (bi, 0, 0, 0)),
        ],
        out_shape=[
            jax.ShapeDtypeStruct((b, nb, LANES), BF16),
            jax.ShapeDtypeStruct((b, NSA_KV_HEADS, nb, LANES), BF16),
        ],
        compiler_params=_params("parallel"),
        name="cmp",
    )(chunks, pe, w1, w2ext, gain_k, ccos, csin)


def _softmax_cols(s, exp=jnp.exp):
    p = exp(s - jnp.max(s, axis=0, keepdims=True))
    return p, jnp.sum(p, axis=0, keepdims=True)


def _block_max(s, offs, bs):
    m = None
    for j, off in enumerate(offs):
        mj = jnp.max(s[j * bs:(j + 1) * bs], axis=0, keepdims=True) + off
        m = mj if m is None else jnp.maximum(m, mj)
    return m


def _block_exp2(s, offs, bs, m):
    return jnp.concatenate(
        [jnp.exp2(s[j * bs:(j + 1) * bs] - (m - off)) for j, off in enumerate(offs)], axis=0)


def _nsa_kernel(q_ref, kc_ref, vc_ref, ksl_ref, vsl_ref, kw_ref, vw_ref, ga_ref, ov_ref, wb_ref,
                o_ref, m_scr, acc_scr, s0_scr, s1_scr):
    tq = NSA_TQ
    tk = NSA_TK
    hd = HEAD_DIM
    cols = NSA_GROUP * tq
    n_blk = ov_ref.shape[0]
    n_chunks = ksl_ref.shape[1] // tk
    bpc = tk // SEL_BLOCK
    kh = pl.program_id(1)
    t0 = pl.multiple_of(pl.program_id(2) * tq, tq)
    q4 = q_ref[0].reshape(cols, LANES)
    tcol = t0 + lax.broadcasted_iota(I32, (1, tq), 1)
    tcol4 = t0 + (lax.broadcasted_iota(I32, (1, cols), 1) & (tq - 1))
    s_bufs = (s0_scr, s1_scr)
    per_head = lambda b: jnp.concatenate([b] * NSA_GROUP, axis=1)
    older_edge = wb_ref[0]
    causal_edge = wb_ref[1]

    nwin = WINDOW + tq
    kw = kw_ref[0, pl.ds(t0, nwin), :]
    vw = vw_ref[0, 0, pl.ds(t0, nwin), :]
    s_cmp = _dot_nt(kc_ref[0], q4)
    s_win = _dot_nt(kw, q4)
    s0_scr[...] = _dot_nt(ksl_ref[0, 0:tk, :], q4)

    nb = s_cmp.shape[0]
    cmp_end = lax.broadcasted_iota(I32, (nb, 1), 0) * CMP_STRIDE + (CMP_BLOCK - 1)
    valid = cmp_end <= tcol4
    e, den = _softmax_cols(jnp.where(valid, s_cmp, NEG_INF), jnp.exp2)
    p = jnp.where(valid, e / den, 0.0)
    o_cmp = _dot_tn(vc_ref[0, 0], p.astype(BF16))[0:hd]

    psum = p[:, 0:tq] + p[:, tq:2 * tq] + p[:, 2 * tq:3 * tq] + p[:, 3 * tq:4 * tq]
    p_hi, p_lo = _split(psum)
    ov = ov_ref[...]
    imp = _dot(ov, p_hi) + _dot(ov, p_lo)
    blk = lax.broadcasted_iota(I32, (n_blk, 1), 0)
    blk_f = blk.astype(F32)
    rel = (tcol >> 6) - blk
    forced = (blk == 0) | ((rel >= 0) & (rel < N_LOCAL_SEL))
    score = jnp.where(rel < 0, NEG_INF, jnp.where(forced, FORCE_SCORE, imp))
    sel_bias = jnp.full((n_blk, tq), MASKED, F32)
    for _ in range(N_SEL):
        best = jnp.max(score, axis=0, keepdims=True)
        first = jnp.min(jnp.where(score == best, blk_f, float(n_blk)), axis=0, keepdims=True)
        pick = blk_f == first
        sel_bias = jnp.where(pick, 0.0, sel_bias)
        score = jnp.where(pick, PICKED, score)
    sel_bias = per_head(jnp.where(rel < 0, MASKED, sel_bias))

    n_wb = nwin // tq
    sw = jnp.concatenate([s_win[0:tq] + per_head(older_edge), s_win[tq:nwin - tq],
                          s_win[nwin - tq:nwin] + per_head(causal_edge)], axis=0)
    w_offs = [jnp.where(t0 - WINDOW + j * tq >= 0, 0.0, MASKED) for j in range(n_wb)]
    pw = _block_exp2(sw, w_offs, tq, _block_max(sw, w_offs, tq))
    ow = _dot_tn(vw, pw.astype(BF16))
    o_win = ow[0:hd] / ow[hd:hd + 1]

    diag = t0 // tk

    def chunk(c):
        if c + 1 < n_chunks:
            s_bufs[(c + 1) % 2][...] = _dot_nt(ksl_ref[0, (c + 1) * tk:(c + 2) * tk, :], q4)
        buf = s_bufs[c % 2]
        own = pl.ds(pl.multiple_of(jnp.clip(t0 - c * tk, 0, tk - tq), tq), tq)
        buf[own, :] = buf[own, :] + per_head(jnp.where(c == diag, causal_edge, 0.0))
        sc = buf[...]
        offs = [sel_bias[c * bpc + j:c * bpc + j + 1] for j in range(bpc)]
        m_new = _block_max(sc, offs, SEL_BLOCK)
        vb = vsl_ref[0, 0, c * tk:(c + 1) * tk, :]
        if c == 0:
            acc_scr[...] = _dot_tn(vb, _block_exp2(sc, offs, SEL_BLOCK, m_new).astype(BF16))
        else:
            m_old = m_scr[...]
            m_new = jnp.maximum(m_old, m_new)
            pe = _block_exp2(sc, offs, SEL_BLOCK, m_new)
            acc_scr[...] = jnp.exp2(m_old - m_new) * acc_scr[...] + _dot_tn(vb, pe.astype(BF16))
        m_scr[...] = m_new

    chunk(0)
    for c in range(1, n_chunks):
        pl.when(c <= diag)(functools.partial(chunk, c))
    o_slc = acc_scr[0:hd, :] / acc_scr[hd:hd + 1, :]

    gat = ga_ref[0].T
    is0 = kh == 0
    heads = []
    for g in range(NSA_GROUP):
        c = slice(g * tq, (g + 1) * tq)
        og = jnp.zeros((hd, tq), F32)
        for gi, ob in enumerate((o_cmp, o_slc, o_win)):
            c0 = gi * NSA_HEADS + g
            c1 = c0 + NSA_GROUP
            gate = jnp.where(is0, gat[c0:c0 + 1, :], gat[c1:c1 + 1, :])
            og = og + gate * ob[:, c]
        heads.append(og)
    for pair in range(2):
        tile = jnp.concatenate([heads[2 * pair], heads[2 * pair + 1]], axis=0)
        o_ref[0, :, pair * LANES:(pair + 1) * LANES] = tile.T.astype(BF16)


def _nsa_call(qext, kc, vc, ksl, vsl, kwp, vwp, ga, ov_t, wbias):
    b, _, s, _ = qext.shape
    tq = NSA_TQ
    nb = kc.shape[1]
    cols = NSA_GROUP * tq
    shared = lambda rows: pl.BlockSpec((1, rows, LANES), lambda bi, k, i: (bi, 0, 0))
    per_kv = lambda rows: pl.BlockSpec((1, 1, rows, LANES), lambda bi, k, i: (bi, k, 0, 0))
    return pl.pallas_call(
        _nsa_kernel,
        grid=(b, NSA_KV_HEADS, s // tq),
        in_specs=[
            pl.BlockSpec((1, NSA_GROUP, tq, LANES), lambda bi, k, i: (bi, k, i, 0)),
            shared(nb), per_kv(nb),
            shared(s), per_kv(s),
            shared(s + WINDOW), per_kv(s + WINDOW),
            pl.BlockSpec((1, tq, LANES), lambda bi, k, i: (bi, i, 0)),
            pl.BlockSpec(ov_t.shape, lambda bi, k, i: (0, 0)),
            pl.BlockSpec(wbias.shape, lambda bi, k, i: (0, 0, 0)),
        ],
        out_specs=pl.BlockSpec((1, tq, 2 * LANES), lambda bi, k, i: (bi, i, k)),
        out_shape=jax.ShapeDtypeStruct((b, s, A_Q), BF16),
        scratch_shapes=[
            pltpu.VMEM((1, cols), F32),
            pltpu.VMEM((LANES, cols), F32),
            pltpu.VMEM((NSA_TK, cols), F32),
            pltpu.VMEM((NSA_TK, cols), F32),
        ],
        compiler_params=_params("parallel", "parallel", "parallel"),
        name="nsa",
    )(qext, kc, vc, ksl, vsl, kwp, vwp, ga, ov_t, wbias)


def _dil_kernel(q_ref, k_ref, v_ref, o_ref, lse_ref):
    t = DIL_T
    g = pl.program_id(1)
    seg_tiles = jnp.where(g == 0, 16, jnp.where(g == 1, 4, 1))
    lo = lax.broadcasted_iota(I32, (t, LANES), 1) < HEAD_DIM
    kj = lax.broadcasted_iota(I32, (t, 2 * t), 0)
    qi = lax.broadcasted_iota(I32, (t, 2 * t), 1) & (t - 1)
    top = lax.broadcasted_iota(I32, (LANES, t), 0) < HEAD_DIM

    def scores(i):
        has_prev = (i & (seg_tiles - 1)) != 0
        cs = pl.multiple_of(i * t, t)
        ps = pl.multiple_of(jnp.maximum(i - 1, 0) * t, t)
        q = q_ref[0, 0, pl.ds(cs, t), :]
        zero = jnp.zeros_like(q)
        q2 = jnp.concatenate([jnp.where(lo, q, zero), jnp.where(lo, zero, q)], axis=0)
        sp = _dot_nt(k_ref[0, 0, pl.ds(ps, t), :], q2)
        sc = _dot_nt(k_ref[0, 0, pl.ds(cs, t), :], q2)
        return has_prev, ps, cs, sp, sc

    def attend(has_prev, ps, cs, sp, sc):
        sp = jnp.where((kj >= qi) & has_prev, sp, MASKED)
        sc = jnp.where(qi >= kj, sc, MASKED)
        m = jnp.maximum(jnp.max(sp, axis=0, keepdims=True), jnp.max(sc, axis=0, keepdims=True))
        pp = jnp.exp(sp - m)
        pc = jnp.exp(sc - m)
        l = jnp.sum(pp, axis=0, keepdims=True) + jnp.sum(pc, axis=0, keepdims=True)
        o = (_dot_tn(v_ref[0, 0, pl.ds(ps, t), :], pp.astype(BF16))
             + _dot_tn(v_ref[0, 0, pl.ds(cs, t), :], pc.astype(BF16))) / l
        lse = m + jnp.log(l)
        o_ref[0, 0, pl.ds(cs, t), :] = jnp.where(top, o[:, 0:t], o[:, t:2 * t]).T.astype(BF16)
        lse_ref[0, 0, pl.ds(cs, t), :] = jnp.where(top, lse[:, 0:t], lse[:, t:2 * t]).T

    def tiles(j, carry):
        group = [scores(j * DIL_UNROLL + u) for u in range(DIL_UNROLL)]
        for args in group:
            attend(*args)
        return carry

    lax.fori_loop(0, q_ref.shape[2] // (t * DIL_UNROLL), tiles, 0)


def _dil_call(dil_in):
    b, _, s, _ = dil_in.shape
    blk = (1, 1, s, LANES)
    return pl.pallas_call(
        _dil_kernel,
        grid=(b, DIL_GROUPS),
        in_specs=[
            pl.BlockSpec(blk, lambda bi, g: (bi, g, 0, 0)),
            pl.BlockSpec(blk, lambda bi, g: (bi, DIL_GROUPS + g, 0, 0)),
            pl.BlockSpec(blk, lambda bi, g: (bi, 2 * DIL_GROUPS + g, 0, 0)),
        ],
        out_specs=[
            pl.BlockSpec(blk, lambda bi, g: (bi, g, 0, 0)),
            pl.BlockSpec(blk, lambda bi, g: (bi, g, 0, 0)),
        ],
        out_shape=[
            jax.ShapeDtypeStruct((b, DIL_GROUPS, s, LANES), BF16),
            jax.ShapeDtypeStruct((b, DIL_GROUPS, s, LANES), F32),
        ],
        compiler_params=_params("parallel", "parallel"),
        name="dil",
    )(dil_in, dil_in, dil_in)


def _merge_kernel(x_ref, oa_ref, od_ref, lse_ref, gm_ref, mod_ref, wa_ref, wb_ref, wo_ref, g2_ref,
                  rhi_ref, rlo_ref, rb_ref, x1_ref, h2_ref, route_ref):
    tm = x_ref.shape[0]
    d = x_ref.shape[1]
    l0, l1, l2 = lse_ref[0], lse_ref[1], lse_ref[2]
    mx = jnp.maximum(jnp.maximum(l0, l1), l2)
    e0, e1, e2 = jnp.exp(l0 - mx), jnp.exp(l1 - mx), jnp.exp(l2 - mx)
    den = e0 + e1 + e2
    ob = jnp.concatenate([
        (od_ref[0].astype(F32) * (e0 / den)).astype(BF16),
        (od_ref[1].astype(F32) * (e1 / den)).astype(BF16),
        (od_ref[2].astype(F32) * (e2 / den)).astype(BF16)], axis=1)
    ya = _dot(oa_ref[...], wa_ref[...])
    yb = _dot(ob, wb_ref[...])
    y = gm_ref[:, 0:d].astype(F32) * ya + gm_ref[:, d:2 * d].astype(F32) * yb
    z = _dot(y.astype(BF16), wo_ref[...])
    x1 = x_ref[...] + mod_ref[0, 2:3, :] * z
    x1_ref[...] = x1

    ms = jnp.mean(x1 * x1, axis=-1, keepdims=True)
    h2 = x1 * lax.rsqrt(ms + EPS) * g2_ref[...]
    h2 = h2 * (1.0 + mod_ref[0, 4:5, :]) + mod_ref[0, 3:4, :]
    _store_token_tiles(h2_ref, h2)

    hi, lo = _split(h2)
    rhi = rhi_ref[...]
    logits = _dot(hi, rhi) + _dot(lo, rhi) + _dot(hi, rlo_ref[...]) + rb_ref[...]
    lane = lax.broadcasted_iota(I32, (tm, LANES), 1)
    gl = jnp.where(lane < N_GROUPS, logits, NEG_INF)
    gmax = jnp.max(gl, axis=-1, keepdims=True)
    g_w = 1.0 / jnp.sum(jnp.exp(gl - gmax), axis=-1, keepdims=True)
    g_idx = jnp.min(jnp.where(gl == gmax, lane, LANES), axis=-1, keepdims=True)
    e_lo = N_GROUPS + EXPERTS_PER_GROUP * g_idx
    el = jnp.where((lane >= e_lo) & (lane < e_lo + EXPERTS_PER_GROUP), logits, NEG_INF)
    m1 = jnp.max(el, axis=-1, keepdims=True)
    i1 = jnp.min(jnp.where(el == m1, lane, LANES), axis=-1, keepdims=True)
    el2 = jnp.where(lane == i1, PICKED, el)
    m2 = jnp.max(el2, axis=-1, keepdims=True)
    i2 = jnp.min(jnp.where(el2 == m2, lane, LANES), axis=-1, keepdims=True)
    ex = jnp.exp(m2 - m1)
    w1 = g_w * (1.0 / (1.0 + ex))
    w2 = g_w * (ex / (1.0 + ex))
    route = jnp.where(lane == 0, (i1 - N_GROUPS).astype(F32),
                      jnp.where(lane == 1, (i2 - N_GROUPS).astype(F32),
                                jnp.where(lane == 2, w1, jnp.where(lane == 3, w2, 0.0))))
    route_ref[...] = route


def _merge_call(x2, oa, od, lse, gm, mod6, wa, wb, wo, g2, rhi, rlo, rb, s):
    t, d = x2.shape
    tm = MERGE_TM
    per_b = s // tm
    row = lambda i: (i, 0)
    const = lambda i: (0, 0)
    return pl.pallas_call(
        _merge_kernel,
        grid=(t // tm,),
        in_specs=[
            pl.BlockSpec((tm, d), row),
            pl.BlockSpec((tm, A_Q), row),
            pl.BlockSpec((DIL_GROUPS, tm, LANES), lambda i: (0, i, 0)),
            pl.BlockSpec((DIL_GROUPS, tm, LANES), lambda i: (0, i, 0)),
            pl.BlockSpec((tm, 2 * d), row),
            pl.BlockSpec((1, 6, d), lambda i: (i // per_b, 0, 0)),
            pl.BlockSpec(wa.shape, const),
            pl.BlockSpec(wb.shape, const),
            pl.BlockSpec(wo.shape, const),
            pl.BlockSpec((1, d), const),
            pl.BlockSpec(rhi.shape, const),
            pl.BlockSpec(rlo.shape, const),
            pl.BlockSpec((1, LANES), const),
        ],
        out_specs=[
            pl.BlockSpec((tm, d), row),
            pl.BlockSpec((tm * ROW_TILE[0], LANES), row),
            pl.BlockSpec((tm, LANES), row),
        ],
        out_shape=[
            jax.ShapeDtypeStruct((t, d), F32),
            jax.ShapeDtypeStruct((t * ROW_TILE[0], LANES), F32),
            jax.ShapeDtypeStruct((t, LANES), F32),
        ],
        compiler_params=_params("parallel"),
        name="merge",
    )(x2, oa, od, lse, gm, mod6, wa, wb, wo, g2, rhi, rlo, rb)


def _rank_kernel(route_ref, rank_ref, count_ref, carry_scr):
    tm = route_ref.shape[0]

    @pl.when(pl.program_id(0) == 0)
    def _():
        carry_scr[...] = jnp.zeros(carry_scr.shape, F32)

    route = route_ref[...]
    lane = lax.broadcasted_iota(I32, (tm, LANES), 1)
    e1 = route[:, 0:1].astype(I32)
    e2 = route[:, 1:2].astype(I32)
    hit1 = lane == e1
    hit2 = lane == e2
    cnt = jnp.where(hit1 | hit2, 1.0, 0.0)
    r = lax.broadcasted_iota(I32, (tm, tm), 0)
    c = lax.broadcasted_iota(I32, (tm, tm), 1)
    below = jnp.where(c < r, 1.0, 0.0).astype(BF16)
    before = _dot(below, cnt.astype(BF16)) + carry_scr[...]
    r1 = jnp.sum(jnp.where(hit1, before, 0.0), axis=-1, keepdims=True)
    r2 = jnp.sum(jnp.where(hit2, before, 0.0), axis=-1, keepdims=True)
    rank_ref[...] = jnp.where(lane == 0, r1, jnp.where(lane == 1, r2, 0.0))
    carry_scr[...] = carry_scr[...] + jnp.sum(cnt, axis=0, keepdims=True)
    count_ref[...] = carry_scr[...]


def _rank_call(route):
    t = route.shape[0]
    tm = RANK_TM
    return pl.pallas_call(
        _rank_kernel,
        grid=(t // tm,),
        in_specs=[pl.BlockSpec((tm, LANES), lambda i: (i, 0))],
        out_specs=[
            pl.BlockSpec((tm, LANES), lambda i: (i, 0)),
            pl.BlockSpec((1, LANES), lambda i: (0, 0)),
        ],
        out_shape=[
            jax.ShapeDtypeStruct((t, LANES), F32),
            jax.ShapeDtypeStruct((1, LANES), F32),
        ],
        scratch_shapes=[pltpu.VMEM((1, LANES), F32)],
        compiler_params=_params("arbitrary"),
        name="rank",
    )(route)


def _issue_unrolled(n, fn):
    def body(j, carry):
        for u in range(DMA_UNROLL):
            fn(j * DMA_UNROLL + u)
        return carry
    lax.fori_loop(0, n // DMA_UNROLL, body, 0)


def _dispatch_kernel(cnt_ref, pstart_ref, dest_hbm, h_hbm, xs_hbm, idx_smem, hbuf, zbuf,
                     isem, hsem, dsem, zsem):
    tm = MOE_TM
    nd = TOP_K * tm
    i = pl.program_id(0)
    n = pl.num_programs(0)
    slot = i % 2
    buf = i % 3
    n_slots = xs_hbm.shape[0]

    def idx_copy(step, sl):
        return pltpu.make_async_copy(dest_hbm.at[step], idx_smem.at[sl], isem.at[sl])

    def tile_copy(step, bf):
        rows = pl.ds(pl.multiple_of(step * tm, tm), tm)
        return pltpu.make_async_copy(h_hbm.at[rows], hbuf.at[bf], hsem.at[bf])

    def wait_scatter(bf):
        for _ in range(TOP_K):
            pltpu.make_async_copy(hbuf.at[bf], xs_hbm.at[pl.ds(0, tm)], dsem.at[bf]).wait()

    @pl.when(i == 0)
    def _():
        idx_copy(0, 0).start()
        tile_copy(0, 0).start()

        @pl.when(n > 1)
        def _():
            tile_copy(1, 1).start()
        zbuf[...] = jnp.zeros(zbuf.shape, F32)

        def zero_row(row):
            return pltpu.make_async_copy(zbuf.at[0], xs_hbm.at[row], zsem)

        def per_expert(e, used):
            base = pstart_ref[e]
            cnt = cnt_ref[e]
            pad = (cnt + MOE_BLOCK - 1) // MOE_BLOCK * MOE_BLOCK

            def start(r, c):
                zero_row(base + r).start()
                return c

            def wait(r, c):
                zero_row(base + r).wait()
                return c
            lax.fori_loop(cnt, pad, start, 0)
            lax.fori_loop(cnt, pad, wait, 0)
            return used + pad
        used = lax.fori_loop(0, N_EXPERTS, per_expert, 0)

        def zero_block(blk):
            rows = pl.ds(pl.multiple_of(blk * MOE_BLOCK, MOE_BLOCK), MOE_BLOCK)
            return pltpu.make_async_copy(zbuf, xs_hbm.at[rows], zsem)

        def start_blk(blk, c):
            zero_block(blk).start()
            return c

        def wait_blk(blk, c):
            zero_block(blk).wait()
            return c
        lax.fori_loop(used // MOE_BLOCK, n_slots // MOE_BLOCK, start_blk, 0)
        lax.fori_loop(used // MOE_BLOCK, n_slots // MOE_BLOCK, wait_blk, 0)

    idx_copy(i, slot).wait()

    @pl.when(i + 1 < n)
    def _():
        idx_copy(i + 1, 1 - slot).start()

    tile_copy(i, buf).wait()

    def one(r):
        src = hbuf.at[buf, r & (tm - 1)]
        pltpu.make_async_copy(src, xs_hbm.at[idx_smem[slot, r]], dsem.at[buf]).start()
    _issue_unrolled(nd, one)

    @pl.when(i > 0)
    def _():
        wait_scatter((i + 2) % 3)

    @pl.when(i + 2 < n)
    def _():
        tile_copy(i + 2, (i + 2) % 3).start()

    @pl.when(i == n - 1)
    def _():
        wait_scatter(buf)


def _dispatch_call(counts, pad_starts, dest_tiles, h2, n_slots):
    n_steps = dest_tiles.shape[0]
    grid_spec = pltpu.PrefetchScalarGridSpec(
        num_scalar_prefetch=2,
        grid=(n_steps,),
        in_specs=[pl.BlockSpec(memory_space=pl.ANY), pl.BlockSpec(memory_space=pl.ANY)],
        out_specs=pl.BlockSpec(memory_space=pl.ANY),
        scratch_shapes=[
            pltpu.SMEM((2, TOP_K * MOE_TM), I32),
            pltpu.VMEM((3, MOE_TM) + ROW_TILE, F32),
            pltpu.VMEM((MOE_BLOCK,) + ROW_TILE, F32),
            pltpu.SemaphoreType.DMA((2,)),
            pltpu.SemaphoreType.DMA((3,)),
            pltpu.SemaphoreType.DMA((3,)),
            pltpu.SemaphoreType.DMA(()),
        ],
    )
    return pl.pallas_call(
        _dispatch_kernel,
        grid_spec=grid_spec,
        out_shape=jax.ShapeDtypeStruct((n_slots,) + ROW_TILE, F32),
        compiler_params=_params("arbitrary"),
        name="dispatch",
    )(counts, pad_starts, dest_tiles, h2)


def _expert_kernel(be_ref, x_ref, wg_ref, wu_ref, wd_ref, y_ref):
    del be_ref
    xb = _load_token_tiles(x_ref, 0, MOE_BLOCK).astype(BF16)
    gate = _dot(xb, wg_ref[0])
    up = _dot(xb, wu_ref[0])
    hid = (gate * jax.nn.sigmoid(gate) * up).astype(BF16)
    _store_token_tiles(y_ref, _dot(hid, wd_ref[0]))


def _expert_call(blk_expert, xs, wg, wu, wd):
    n_blk = blk_expert.shape[0]
    d = wg.shape[1]
    blk = (MOE_BLOCK * ROW_TILE[0], LANES)
    grid_spec = pltpu.PrefetchScalarGridSpec(
        num_scalar_prefetch=1,
        grid=(n_blk,),
        in_specs=[
            pl.BlockSpec(blk, lambda i, be: (i, 0)),
            pl.BlockSpec((1, d, EXPERT_FF), lambda i, be: (be[i], 0, 0)),
            pl.BlockSpec((1, d, EXPERT_FF), lambda i, be: (be[i], 0, 0)),
            pl.BlockSpec((1, EXPERT_FF, d), lambda i, be: (be[i], 0, 0)),
        ],
        out_specs=pl.BlockSpec(blk, lambda i, be: (i, 0)),
    )
    return pl.pallas_call(
        _expert_kernel,
        grid_spec=grid_spec,
        out_shape=jax.ShapeDtypeStruct(xs.shape, F32),
        compiler_params=_params("arbitrary"),
        name="experts",
    )(blk_expert, xs, wg, wu, wd)


def _combine_kernel(dest_hbm, y_hbm, x1_ref, route_ref, mod_ref, o_ref, idx_smem, ybuf, isem, dsem):
    tm = MOE_TM
    nd = TOP_K * tm
    i = pl.program_id(0)
    n = pl.num_programs(0)
    slot = i % 2
    nxt = 1 - slot

    def idx_copy(step, sl):
        return pltpu.make_async_copy(dest_hbm.at[step], idx_smem.at[sl], isem.at[sl])

    ns = ROW_TILE[0]

    def step_rows(sl):
        return pltpu.make_async_copy(y_hbm.at[pl.ds(0, nd * ns)], ybuf.at[sl], dsem.at[sl])

    def issue_rows(sl):
        def one(r):
            src = y_hbm.at[pl.ds(pl.multiple_of(idx_smem[sl, r] * ns, ns), ns)]
            dst = ybuf.at[sl, pl.ds(pl.multiple_of(r * ns, ns), ns)]
            pltpu.make_async_copy(src, dst, dsem.at[sl]).start()
        _issue_unrolled(nd, one)

    @pl.when(i == 0)
    def _():
        idx_copy(0, 0).start()
        idx_copy(0, 0).wait()
        issue_rows(0)

        @pl.when(n > 1)
        def _():
            idx_copy(1, 1).start()

    @pl.when(i + 1 < n)
    def _():
        idx_copy(i + 1, nxt).wait()
        issue_rows(nxt)

    @pl.when(i + 2 < n)
    def _():
        idx_copy(i + 2, slot).start()

    step_rows(slot).wait()
    route = route_ref[...]
    w1 = route[:, 2:3]
    w2 = route[:, 3:4]
    y = (w1 * _load_token_tiles(ybuf, 0, tm, lead=(slot,))
         + w2 * _load_token_tiles(ybuf, tm, tm, lead=(slot,)))
    o_ref[...] = x1_ref[...] + mod_ref[0, 5:6, :] * y


def _combine_call(dest_tiles, y_slots, x1, route, mod6, s):
    t, d = x1.shape
    tm = MOE_TM
    per_b = s // tm
    return pl.pallas_call(
        _combine_kernel,
        grid=(t // tm,),
        in_specs=[
            pl.BlockSpec(memory_space=pl.ANY),
            pl.BlockSpec(memory_space=pl.ANY),
            pl.BlockSpec((tm, d), lambda i: (i, 0)),
            pl.BlockSpec((tm, LANES), lambda i: (i, 0)),
            pl.BlockSpec((1, 6, d), lambda i: (i // per_b, 0, 0)),
        ],
        out_specs=pl.BlockSpec((tm, d), lambda i: (i, 0)),
        out_shape=jax.ShapeDtypeStruct((t, d), F32),
        scratch_shapes=[
            pltpu.SMEM((2, TOP_K * tm), I32),
            pltpu.VMEM((2, TOP_K * tm * ROW_TILE[0], LANES), F32),
            pltpu.SemaphoreType.DMA((2,)),
            pltpu.SemaphoreType.DMA((2,)),
        ],
        compiler_params=_params("arbitrary"),
        name="combine",
    )(dest_tiles, y_slots, x1, route, mod6)


def _rope_tiles(pos):
    inv = ROPE_THETA ** (-jnp.arange(0, HEAD_DIM, 2, dtype=F32) / HEAD_DIM)
    ang = pos.astype(F32)[..., None] * inv
    cos, sin = jnp.cos(ang), jnp.sin(ang)
    return (jnp.concatenate([cos, cos, cos, cos], axis=-1),
            jnp.concatenate([-sin, sin, -sin, sin], axis=-1))


def _selection_constants(s):
    n_c = s // CMP_STRIDE
    n_s = s // SEL_BLOCK
    cs = np.arange(n_c) * CMP_STRIDE
    ss = np.arange(n_s) * SEL_BLOCK
    ov = np.clip(np.minimum(cs[:, None] + CMP_BLOCK, ss[None, :] + SEL_BLOCK)
                 - np.maximum(cs[:, None], ss[None, :]), 0, None).astype(np.float32) / CMP_BLOCK
    kq = np.arange(NSA_TQ)[:, None] - np.arange(NSA_TQ)[None, :]
    wbias = np.stack([np.where(kq >= 0, 0.0, MASKED), np.where(kq <= 0, 0.0, MASKED)]).astype(np.float32)
    return jnp.asarray(ov.T, BF16), jnp.asarray(wbias)


def _to_residue_major(t, dil):
    b, s, w = t.shape
    return t.reshape(b, s // dil, dil, w).transpose(0, 2, 1, 3).reshape(b, s, w)


def _from_residue_major(t, dil):
    b, s, w = t.shape
    return t.reshape(b, dil, s // dil, w).transpose(0, 2, 1, 3).reshape(b, s, w)


def _mixer_and_router(x, c, positions, w_ada, b_ada, norm1_g, norm2_g, w_in, nsa_q_norm, nsa_k_norm,
                      cmp_pe_k, cmp_w1_k, cmp_w2_k, cmp_pe_v, cmp_w1_v, cmp_w2_v, dil_q_norm,
                      dil_k_norm, w_up_a, w_up_b, w_out, w_group, b_group, w_router, b_router):
    b, s, d = x.shape
    scale = HEAD_DIM ** -0.5
    mod6 = _mod_call(c, w_ada, b_ada).reshape(b, 6, d)

    c1 = A_Q
    c2 = c1 + 6 * A_KV
    c3 = c2 + 3 * NSA_HEADS
    c4 = c3 + 3 * DIL_W
    w_perm = jnp.concatenate([
        w_in[:, :c2], w_in[:, c3:c4],
        jnp.pad(w_in[:, c2:c3], ((0, 0), (0, LANES - 3 * NSA_HEADS))), w_in[:, c4:]], axis=1).astype(BF16)
    two = lambda g: jnp.concatenate([g, g]).astype(F32)
    gains = jnp.stack([two(nsa_q_norm) * (scale * LOG2E), two(nsa_k_norm), two(dil_q_norm) * scale,
                       two(dil_k_norm)])
    cos_t, sin_t = _rope_tiles(positions)
    qext, kvc, ksl, vsl, kwp, vwp, dil, ga, gm = _proj_call(
        x, mod6, norm1_g.reshape(1, d), w_perm, gains, cos_t, sin_t)

    n_chunk = s // CMP_STRIDE
    half = CMP_STRIDE * HEAD_DIM
    chunks = kvc.reshape(b, s, 4, HEAD_DIM).transpose(0, 2, 1, 3).reshape(b, 4, n_chunk, half)
    pe = jnp.stack([cmp_pe_k.reshape(2, half), cmp_pe_v.reshape(2, half)])
    w1 = jnp.stack([cmp_w1_k, cmp_w1_v]).astype(BF16)
    zeros = jnp.zeros((CMP_HIDDEN, HEAD_DIM), F32)
    ext = lambda w: jnp.stack([jnp.concatenate([w, zeros], 1), jnp.concatenate([zeros, w], 1)])
    w2ext = jnp.stack([ext(cmp_w2_k), ext(cmp_w2_v)]).astype(BF16)
    cmp_pos = jnp.pad(positions[:, CMP_BLOCK - 1::CMP_STRIDE], ((0, 0), (0, 1)))
    ccos, csin = _rope_tiles(cmp_pos)
    kc, vc = _cmp_call(chunks, pe, w1, w2ext, two(nsa_k_norm).reshape(1, LANES), ccos, csin)

    ov_t, wbias = _selection_constants(s)
    o_a = _nsa_call(qext, kc, vc, ksl, vsl, kwp, vwp, ga, ov_t, wbias)

    parts = []
    for kind in range(3):
        for gi, (_, dl) in enumerate(DIL_PAIRS):
            col = (kind * DIL_GROUPS + gi) * LANES
            parts.append(_to_residue_major(dil[:, :, col:col + LANES], dl))
    o_dil, lse_dil = _dil_call(jnp.stack(parts, axis=1))
    od = jnp.stack([_from_residue_major(o_dil[:, gi], dl) for gi, (_, dl) in enumerate(DIL_PAIRS)])
    lse = jnp.stack([_from_residue_major(lse_dil[:, gi], dl) for gi, (_, dl) in enumerate(DIL_PAIRS)])

    t = b * s
    w_r = jnp.concatenate([w_group, w_router.transpose(1, 0, 2).reshape(d, N_EXPERTS)], axis=1)
    w_r = jnp.pad(w_r, ((0, 0), (0, LANES - w_r.shape[1])))
    rhi, rlo = _split(w_r)
    rb = jnp.pad(jnp.concatenate([b_group, b_router.reshape(-1)]), (0, LANES - N_GROUPS - N_EXPERTS))
    return _merge_call(
        x.reshape(t, d), o_a.reshape(t, A_Q), od.reshape(DIL_GROUPS, t, LANES),
        lse.reshape(DIL_GROUPS, t, LANES), gm.reshape(t, 2 * d), mod6,
        w_up_a.astype(BF16), w_up_b.astype(BF16), w_out.astype(BF16), norm2_g.reshape(1, d),
        rhi, rlo, rb.reshape(1, LANES).astype(F32), s), mod6


def _moe(x1, h2, route, mod6, w_e_gate, w_e_up, w_e_down, s):
    t, _ = x1.shape
    rank, counts = _rank_call(route)
    expert = route[:, 0:TOP_K].astype(I32)
    counts = counts[0, :N_EXPERTS].astype(I32)
    padded = (counts + MOE_BLOCK - 1) // MOE_BLOCK * MOE_BLOCK
    pad_ends = jnp.cumsum(padded)
    pad_starts = pad_ends - padded
    dest = pad_starts[expert] + rank[:, 0:TOP_K].astype(I32)
    n_slots = t * TOP_K + N_EXPERTS * MOE_BLOCK
    n_blk = n_slots // MOE_BLOCK
    blk_start = jnp.arange(n_blk, dtype=I32) * MOE_BLOCK
    blk_expert = jnp.minimum(jnp.sum((pad_ends[None, :] <= blk_start[:, None]).astype(I32), axis=1),
                             N_EXPERTS - 1)
    tm = MOE_TM
    dest_tiles = dest.reshape(t // tm, tm, TOP_K).transpose(0, 2, 1).reshape(t // tm, TOP_K * tm)
    xs = _dispatch_call(counts, pad_starts.astype(I32), dest_tiles, h2.reshape((t,) + ROW_TILE), n_slots)
    y_slots = _expert_call(blk_expert, xs.reshape(n_slots * ROW_TILE[0], LANES), w_e_gate.astype(BF16),
                           w_e_up.astype(BF16), w_e_down.astype(BF16))
    return _combine_call(dest_tiles, y_slots, x1, route, mod6, s)


def kernel(x, c, positions, w_ada, b_ada, norm1_g, norm2_g, w_in, nsa_q_norm, nsa_k_norm, cmp_pe_k,
           cmp_w1_k, cmp_w2_k, cmp_pe_v, cmp_w1_v, cmp_w2_v, dil_q_norm, dil_k_norm, w_up_a, w_up_b,
           w_out, w_group, b_group, w_router, b_router, w_e_gate, w_e_up, w_e_down):
    b, s, d = x.shape
    assert w_ada.shape[0] == 1 and d == D_MODEL and s % NSA_TK == 0
    (x1, h2, route), mod6 = _mixer_and_router(
        x, c, positions, w_ada[0], b_ada[0], norm1_g[0], norm2_g[0], w_in[0], nsa_q_norm[0],
        nsa_k_norm[0], cmp_pe_k[0], cmp_w1_k[0], cmp_w2_k[0], cmp_pe_v[0], cmp_w1_v[0], cmp_w2_v[0],
        dil_q_norm[0], dil_k_norm[0], w_up_a[0], w_up_b[0], w_out[0], w_group[0], b_group[0],
        w_router[0], b_router[0])
    out = _moe(x1, h2, route, mod6, w_e_gate[0], w_e_up[0], w_e_down[0], s)
    return out.reshape(b, s, d)
```

```python
import functools

import jax
import jax.numpy as jnp
import numpy as np
from jax import lax
from jax.experimental import pallas as pl
from jax.experimental.pallas import tpu as pltpu

F32 = jnp.float32
BF16 = jnp.bfloat16
I32 = jnp.int32

D_MODEL = 1024
HEAD_DIM = 64
LANES = 128
ROW_TILE = (D_MODEL // LANES, LANES)
ROPE_THETA = 10000.0
EPS = 1e-6
LOG2E = 1.4426950408889634
NEG_INF = -1e30
FORCE_SCORE = 1e9
MASKED = -1e30
PICKED = -3e38

NSA_HEADS = 8
NSA_KV_HEADS = 2
NSA_GROUP = 4
CMP_BLOCK = 32
CMP_STRIDE = 16
CMP_HIDDEN = 256
SEL_BLOCK = 64
N_SEL = 8
N_LOCAL_SEL = 2
WINDOW = 512
DIL_PAIRS = ((128, 1), (512, 4), (2048, 16))
DIL_GROUPS = 3
A_Q = NSA_HEADS * HEAD_DIM
A_KV = NSA_KV_HEADS * HEAD_DIM
DIL_W = 2 * DIL_GROUPS * HEAD_DIM
N_GROUPS = 4
EXPERTS_PER_GROUP = 8
N_EXPERTS = 32
TOP_K = 2
EXPERT_FF = 512
MOE_BLOCK = 256

VMEM_LIMIT = 56 * 1024 * 1024

T_QA = 0
T_KVC = 4
T_KVA = 6
T_DIL = 10
T_GA = 19
T_GM = 20
N_TILES = 36

PROJ_TM = 256
NSA_TQ = 128
NSA_TK = 512
DIL_T = 128
DIL_UNROLL = 4
MERGE_TM = 512
RANK_TM = 512
MOE_TM = 256
DMA_UNROLL = 8


def _dot(a, b):
    return jnp.dot(a, b, preferred_element_type=F32)


def _dot_nt(a, b):
    return lax.dot_general(a, b, (((1,), (1,)), ((), ())), preferred_element_type=F32)


def _dot_tn(a, b):
    return lax.dot_general(a, b, (((0,), (0,)), ((), ())), preferred_element_type=F32)


def _split(a):
    hi = a.astype(BF16)
    lo = (a - hi.astype(F32)).astype(BF16)
    return hi, lo


def _load_token_tiles(ref, first, n, lead=()):
    ns = ROW_TILE[0]
    return jnp.concatenate(
        [ref[lead + (pl.ds(first * ns + s, n, stride=ns), slice(None))] for s in range(ns)], axis=1)


def _store_token_tiles(ref, rows, first=0):
    ns = ROW_TILE[0]
    for s in range(ns):
        ref[pl.ds(first * ns + s, rows.shape[0], stride=ns), :] = rows[:, s * LANES:(s + 1) * LANES]


def _params(*sem):
    return pltpu.CompilerParams(dimension_semantics=sem, vmem_limit_bytes=VMEM_LIMIT)


def _mod_kernel(c_ref, whi_ref, wlo_ref, b_ref, o_ref):
    c = c_ref[...]
    ca = c * jax.nn.sigmoid(c)
    hi, lo = _split(ca)
    whi = whi_ref[...]
    o_ref[...] = _dot(hi, whi) + _dot(lo, whi) + _dot(hi, wlo_ref[...]) + b_ref[...]


def _mod_call(c, w_ada, b_ada):
    b, d = c.shape
    n = w_ada.shape[1]
    whi, wlo = _split(w_ada)
    tn = 1024
    return pl.pallas_call(
        _mod_kernel,
        grid=(n // tn,),
        in_specs=[
            pl.BlockSpec((b, d), lambda j: (0, 0)),
            pl.BlockSpec((d, tn), lambda j: (0, j)),
            pl.BlockSpec((d, tn), lambda j: (0, j)),
            pl.BlockSpec((1, tn), lambda j: (0, j)),
        ],
        out_specs=pl.BlockSpec((b, tn), lambda j: (0, j)),
        out_shape=jax.ShapeDtypeStruct((b, n), F32),
        compiler_params=_params("parallel"),
        name="mod",
    )(c, whi, wlo, b_ada.reshape(1, n))


def _norm_rope(a, gain, cos, sin_signed, lo, first):
    sq = a * a
    s0 = jnp.sum(jnp.where(lo, sq, 0.0), axis=-1, keepdims=True)
    s1 = jnp.sum(jnp.where(lo, 0.0, sq), axis=-1, keepdims=True)
    r = jnp.where(lo, lax.rsqrt(s0 * (1.0 / HEAD_DIM) + EPS), lax.rsqrt(s1 * (1.0 / HEAD_DIM) + EPS))
    y = a * r * gain
    rot = jnp.where(first, pltpu.roll(y, 96, 1), pltpu.roll(y, 32, 1))
    return y * cos + rot * sin_signed


def _proj_kernel(x_ref, mod_ref, g1_ref, w_ref, gains_ref, cos_ref, sin_ref, kw0_ref, vw0_ref,
                 qext_ref, kvc_ref, ksl_ref, vsl_ref, kwp_ref, vwp_ref, dil0_ref, dil1_ref, dil2_ref,
                 ga_ref, gm_ref, perm_scr):
    del kw0_ref, vw0_ref
    tm = x_ref.shape[1]
    x = x_ref[0]
    ms = jnp.mean(x * x, axis=-1, keepdims=True)
    y = x * lax.rsqrt(ms + EPS) * g1_ref[...]
    sh1 = mod_ref[0, 0:1, :]
    sc1 = mod_ref[0, 1:2, :]
    h = (y * (1.0 + sc1) + sh1).astype(BF16)
    cos = cos_ref[0]
    sin = sin_ref[0]
    lane = lax.broadcasted_iota(I32, (tm, LANES), 1)
    lo = lane < HEAD_DIM
    first = (lane & (HEAD_DIM - 1)) < (HEAD_DIM // 2)
    nr = functools.partial(_norm_rope, cos=cos, sin_signed=sin, lo=lo, first=first)

    for c in range(N_TILES // 2):
        acc = _dot(h, w_ref[:, c * 2 * LANES:(c + 1) * 2 * LANES])
        for half in range(2):
            t = 2 * c + half
            a = acc[:, half * LANES:(half + 1) * LANES]
            if t < T_KVC:
                yq = nr(a, gains_ref[0:1, :])
                rq = pltpu.roll(yq, HEAD_DIM, 1)
                if t // 2 == 0:
                    e0 = jnp.where(lo, yq, 0.0)
                    e1 = jnp.where(lo, rq, 0.0)
                else:
                    e0 = jnp.where(lo, 0.0, rq)
                    e1 = jnp.where(lo, 0.0, yq)
                qext_ref[0, 2 * t] = e0.astype(BF16)
                qext_ref[0, 2 * t + 1] = e1.astype(BF16)
            elif t < T_KVA:
                kvc_ref[0, :, (t - T_KVC) * LANES:(t - T_KVC + 1) * LANES] = a
            elif t < T_DIL:
                j = t - T_KVA
                k_out, v_out = (ksl_ref, vsl_ref) if j < 2 else (kwp_ref, vwp_ref)
                if j % 2 == 0:
                    k_out[0] = nr(a, gains_ref[1:2, :]).astype(BF16)
                else:
                    v_out[0, 0] = jnp.where(lo, a, 1.0).astype(BF16)
                    v_out[0, 1] = jnp.where(lo, pltpu.roll(a, HEAD_DIM, 1), 1.0).astype(BF16)
            elif t < T_GA:
                j = t - T_DIL
                if j < 3:
                    v = nr(a, gains_ref[2:3, :])
                elif j < 6:
                    v = nr(a, gains_ref[3:4, :])
                else:
                    v = a
                kind, gi = divmod(j, DIL_GROUPS)
                dl = DIL_PAIRS[gi][1]
                if dl == 1:
                    dil0_ref[0, kind] = v.astype(BF16)
                else:
                    out = (dil0_ref, dil1_ref, dil2_ref)[gi]
                    perm_scr[...] = v
                    for r in range(dl):
                        out[0, kind, r] = perm_scr[pl.ds(r, tm // dl, stride=dl), :].astype(BF16)
            elif t < T_GM:
                ga_ref[0] = jax.nn.sigmoid(a)
            else:
                j = t - T_GM
                gm_ref[0, :, j * LANES:(j + 1) * LANES] = jax.nn.sigmoid(a).astype(BF16)


def _proj_call(x, mod6, g1, w_perm, gains, cos_t, sin_t):
    b, s, d = x.shape
    tm = PROJ_TM
    n = N_TILES * LANES
    row = lambda bi, i: (bi, i, 0)
    heads = lambda bi, i: (bi, 0, i, 0)
    pad = WINDOW // tm
    in_specs = [
        pl.BlockSpec((1, tm, d), row),
        pl.BlockSpec((1, 6, d), lambda bi, i: (bi, 0, 0)),
        pl.BlockSpec((1, d), lambda bi, i: (0, 0)),
        pl.BlockSpec((d, n), lambda bi, i: (0, 0)),
        pl.BlockSpec((4, LANES), lambda bi, i: (0, 0)),
        pl.BlockSpec((1, tm, LANES), row),
        pl.BlockSpec((1, tm, LANES), row),
        pl.BlockSpec(memory_space=pl.ANY),
        pl.BlockSpec(memory_space=pl.ANY),
    ]
    out_specs = [
        pl.BlockSpec((1, NSA_HEADS, tm, LANES), heads),
        pl.BlockSpec((1, tm, 2 * LANES), row),
        pl.BlockSpec((1, tm, LANES), row),
        pl.BlockSpec((1, NSA_KV_HEADS, tm, LANES), heads),
        pl.BlockSpec((1, tm, LANES), lambda bi, i: (bi, i + pad, 0)),
        pl.BlockSpec((1, NSA_KV_HEADS, tm, LANES), lambda bi, i: (bi, 0, i + pad, 0)),
        pl.BlockSpec((1, 3, tm, LANES), heads),
        *[pl.BlockSpec((1, 3, dl, tm // dl, LANES), lambda bi, i: (bi, 0, 0, i, 0)) for _, dl in DIL_PAIRS[1:]],
        pl.BlockSpec((1, tm, LANES), row),
        pl.BlockSpec((1, tm, 2 * d), row),
    ]
    out_shape = [
        jax.ShapeDtypeStruct((b, NSA_HEADS, s, LANES), BF16),
        jax.ShapeDtypeStruct((b, s, 2 * LANES), F32),
        jax.ShapeDtypeStruct((b, s, LANES), BF16),
        jax.ShapeDtypeStruct((b, NSA_KV_HEADS, s, LANES), BF16),
        jax.ShapeDtypeStruct((b, s + WINDOW, LANES), BF16),
        jax.ShapeDtypeStruct((b, NSA_KV_HEADS, s + WINDOW, LANES), BF16),
        jax.ShapeDtypeStruct((b, 3, s, LANES), BF16),
        *[jax.ShapeDtypeStruct((b, 3, dl, s // dl, LANES), BF16) for _, dl in DIL_PAIRS[1:]],
        jax.ShapeDtypeStruct((b, s, LANES), F32),
        jax.ShapeDtypeStruct((b, s, 2 * d), BF16),
    ]
    kw0 = jnp.zeros(out_shape[4].shape, BF16)
    vw0 = jnp.zeros(out_shape[5].shape, BF16)
    return pl.pallas_call(
        _proj_kernel,
        grid=(b, s // tm),
        in_specs=in_specs,
        out_specs=out_specs,
        out_shape=out_shape,
        scratch_shapes=[pltpu.VMEM((tm, LANES), F32)],
        input_output_aliases={7: 4, 8: 5},
        compiler_params=_params("parallel", "parallel"),
        name="proj",
    )(x, mod6, g1, w_perm, gains, cos_t, sin_t, kw0, vw0)


def _cmp_kernel(c_ref, pe_ref, w1_ref, w2_ref, gain_ref, cos_ref, sin_ref, kc_ref, vc_ref):
    nb = c_ref.shape[2]
    half = c_ref.shape[3]
    lane = lax.broadcasted_iota(I32, (nb, LANES), 1)
    lo = lane < HEAD_DIM
    first = (lane & (HEAD_DIM - 1)) < (HEAD_DIM // 2)
    for kind in range(2):
        out = jnp.zeros((nb, LANES), F32)
        for hd in range(2):
            c = c_ref[0, kind * 2 + hd]
            top = _dot((c + pe_ref[kind, 0:1, :]).astype(BF16), w1_ref[kind, 0:half, :])
            bot = _dot((c + pe_ref[kind, 1:2, :]).astype(BF16), w1_ref[kind, half:2 * half, :])
            hid = top + pltpu.roll(bot, nb - 1, 0)
            hid = hid * jax.nn.sigmoid(hid)
            out = out + _dot(hid.astype(BF16), w2_ref[kind, hd])
        if kind == 0:
            out = _norm_rope(out, gain_ref[...], cos_ref[0], sin_ref[0], lo, first)
            kc_ref[0] = out.astype(BF16)
        else:
            vc_ref[0, 0] = jnp.where(lo, out, 0.0).astype(BF16)
            vc_ref[0, 1] = jnp.where(lo, pltpu.roll(out, HEAD_DIM, 1), 0.0).astype(BF16)


def _cmp_call(chunks, pe, w1, w2ext, gain_k, ccos, csin):
    b, _, nb, half = chunks.shape
    return pl.pallas_call(
        _cmp_kernel,
        grid=(b,),
        in_specs=[
            pl.BlockSpec((1, 4, nb, half), lambda bi: (bi, 0, 0, 0)),
            pl.BlockSpec((2, 2, half), lambda bi: (0, 0, 0)),
            pl.BlockSpec((2, 2 * half, CMP_HIDDEN), lambda bi: (0, 0, 0)),
            pl.BlockSpec((2, 2, CMP_HIDDEN, LANES), lambda bi: (0, 0, 0, 0)),
            pl.BlockSpec((1, LANES), lambda bi: (0, 0)),
            pl.BlockSpec((1, nb, LANES), lambda bi: (bi, 0, 0)),
            pl.BlockSpec((1, nb, LANES), lambda bi: (bi, 0, 0)),
        ],
        out_specs=[
            pl.BlockSpec((1, nb, LANES), lambda bi: (bi, 0, 0)),
            pl.BlockSpec((1, NSA_KV_HEADS, nb, LANES), lambda bi: (bi, 0, 0, 0)),
        ],
        out_shape=[
            jax.ShapeDtypeStruct((b, nb, LANES), BF16),
            jax.ShapeDtypeStruct((b, NSA_KV_HEADS, nb, LANES), BF16),
        ],
        compiler_params=_params("parallel"),
        name="cmp",
    )(chunks, pe, w1, w2ext, gain_k, ccos, csin)


def _softmax_cols(s, exp=jnp.exp):
    p = exp(s - jnp.max(s, axis=0, keepdims=True))
    return p, jnp.sum(p, axis=0, keepdims=True)


def _block_max(s, offs, bs):
    m = None
    for j, off in enumerate(offs):
        mj = jnp.max(s[j * bs:(j + 1) * bs], axis=0, keepdims=True) + off
        m = mj if m is None else jnp.maximum(m, mj)
    return m


def _block_exp2(s, offs, bs, m):
    return jnp.concatenate(
        [jnp.exp2(s[j * bs:(j + 1) * bs] - (m - off)) for j, off in enumerate(offs)], axis=0)


def _nsa_kernel(q_ref, kc_ref, vc_ref, ksl_ref, vsl_ref, kw_ref, vw_ref, ga_ref, ov_ref, wb_ref,
                o_ref, m_scr, acc_scr, s0_scr, s1_scr):
    tq = NSA_TQ
    tk = NSA_TK
    hd = HEAD_DIM
    cols = NSA_GROUP * tq
    n_blk = ov_ref.shape[0]
    n_chunks = ksl_ref.shape[1] // tk
    bpc = tk // SEL_BLOCK
    kh = pl.program_id(1)
    t0 = pl.multiple_of(pl.program_id(2) * tq, tq)
    q4 = q_ref[0].reshape(cols, LANES)
    tcol = t0 + lax.broadcasted_iota(I32, (1, tq), 1)
    tcol4 = t0 + (lax.broadcasted_iota(I32, (1, cols), 1) & (tq - 1))
    s_bufs = (s0_scr, s1_scr)
    per_head = lambda b: jnp.concatenate([b] * NSA_GROUP, axis=1)
    older_edge = wb_ref[0]
    causal_edge = wb_ref[1]

    nwin = WINDOW + tq
    kw = kw_ref[0, pl.ds(t0, nwin), :]
    vw = vw_ref[0, 0, pl.ds(t0, nwin), :]
    s_cmp = _dot_nt(kc_ref[0], q4)
    s_win = _dot_nt(kw, q4)
    s0_scr[...] = _dot_nt(ksl_ref[0, 0:tk, :], q4)

    nb = s_cmp.shape[0]
    cmp_end = lax.broadcasted_iota(I32, (nb, 1), 0) * CMP_STRIDE + (CMP_BLOCK - 1)
    valid = cmp_end <= tcol4
    e, den = _softmax_cols(jnp.where(valid, s_cmp, NEG_INF), jnp.exp2)
    p = jnp.where(valid, e / den, 0.0)
    o_cmp = _dot_tn(vc_ref[0, 0], p.astype(BF16))[0:hd]

    psum = p[:, 0:tq] + p[:, tq:2 * tq] + p[:, 2 * tq:3 * tq] + p[:, 3 * tq:4 * tq]
    p_hi, p_lo = _split(psum)
    ov = ov_ref[...]
    imp = _dot(ov, p_hi) + _dot(ov, p_lo)
    blk = lax.broadcasted_iota(I32, (n_blk, 1), 0)
    blk_f = blk.astype(F32)
    rel = (tcol >> 6) - blk
    forced = (blk == 0) | ((rel >= 0) & (rel < N_LOCAL_SEL))
    score = jnp.where(rel < 0, NEG_INF, jnp.where(forced, FORCE_SCORE, imp))
    sel_bias = jnp.full((n_blk, tq), MASKED, F32)
    for _ in range(N_SEL):
        best = jnp.max(score, axis=0, keepdims=True)
        first = jnp.min(jnp.where(score == best, blk_f, float(n_blk)), axis=0, keepdims=True)
        pick = blk_f == first
        sel_bias = jnp.where(pick, 0.0, sel_bias)
        score = jnp.where(pick, PICKED, score)
    sel_bias = per_head(jnp.where(rel < 0, MASKED, sel_bias))

    n_wb = nwin // tq
    sw = jnp.concatenate([s_win[0:tq] + per_head(older_edge), s_win[tq:nwin - tq],
                          s_win[nwin - tq:nwin] + per_head(causal_edge)], axis=0)
    w_offs = [jnp.where(t0 - WINDOW + j * tq >= 0, 0.0, MASKED) for j in range(n_wb)]
    pw = _block_exp2(sw, w_offs, tq, _block_max(sw, w_offs, tq))
    ow = _dot_tn(vw, pw.astype(BF16))
    o_win = ow[0:hd] / ow[hd:hd + 1]

    diag = t0 // tk

    def chunk(c):
        if c + 1 < n_chunks:
            s_bufs[(c + 1) % 2][...] = _dot_nt(ksl_ref[0, (c + 1) * tk:(c + 2) * tk, :], q4)
        buf = s_bufs[c % 2]
        own = pl.ds(pl.multiple_of(jnp.clip(t0 - c * tk, 0, tk - tq), tq), tq)
        buf[own, :] = buf[own, :] + per_head(jnp.where(c == diag, causal_edge, 0.0))
        sc = buf[...]
        offs = [sel_bias[c * bpc + j:c * bpc + j + 1] for j in range(bpc)]
        m_new = _block_max(sc, offs, SEL_BLOCK)
        vb = vsl_ref[0, 0, c * tk:(c + 1) * tk, :]
        if c == 0:
            acc_scr[...] = _dot_tn(vb, _block_exp2(sc, offs, SEL_BLOCK, m_new).astype(BF16))
        else:
            m_old = m_scr[...]
            m_new = jnp.maximum(m_old, m_new)
            pe = _block_exp2(sc, offs, SEL_BLOCK, m_new)
            acc_scr[...] = jnp.exp2(m_old - m_new) * acc_scr[...] + _dot_tn(vb, pe.astype(BF16))
        m_scr[...] = m_new

    chunk(0)
    for c in range(1, n_chunks):
        pl.when(c <= diag)(functools.partial(chunk, c))
    o_slc = acc_scr[0:hd, :] / acc_scr[hd:hd + 1, :]

    gat = ga_ref[0].T
    is0 = kh == 0
    heads = []
    for g in range(NSA_GROUP):
        c = slice(g * tq, (g + 1) * tq)
        og = jnp.zeros((hd, tq), F32)
        for gi, ob in enumerate((o_cmp, o_slc, o_win)):
            c0 = gi * NSA_HEADS + g
            c1 = c0 + NSA_GROUP
            gate = jnp.where(is0, gat[c0:c0 + 1, :], gat[c1:c1 + 1, :])
            og = og + gate * ob[:, c]
        heads.append(og)
    for pair in range(2):
        tile = jnp.concatenate([heads[2 * pair], heads[2 * pair + 1]], axis=0)
        o_ref[0, :, pair * LANES:(pair + 1) * LANES] = tile.T.astype(BF16)


def _nsa_call(qext, kc, vc, ksl, vsl, kwp, vwp, ga, ov_t, wbias):
    b, _, s, _ = qext.shape
    tq = NSA_TQ
    nb = kc.shape[1]
    cols = NSA_GROUP * tq
    shared = lambda rows: pl.BlockSpec((1, rows, LANES), lambda bi, k, i: (bi, 0, 0))
    per_kv = lambda rows: pl.BlockSpec((1, 1, rows, LANES), lambda bi, k, i: (bi, k, 0, 0))
    return pl.pallas_call(
        _nsa_kernel,
        grid=(b, NSA_KV_HEADS, s // tq),
        in_specs=[
            pl.BlockSpec((1, NSA_GROUP, tq, LANES), lambda bi, k, i: (bi, k, i, 0)),
            shared(nb), per_kv(nb),
            shared(s), per_kv(s),
            shared(s + WINDOW), per_kv(s + WINDOW),
            pl.BlockSpec((1, tq, LANES), lambda bi, k, i: (bi, i, 0)),
            pl.BlockSpec(ov_t.shape, lambda bi, k, i: (0, 0)),
            pl.BlockSpec(wbias.shape, lambda bi, k, i: (0, 0, 0)),
        ],
        out_specs=pl.BlockSpec((1, tq, 2 * LANES), lambda bi, k, i: (bi, i, k)),
        out_shape=jax.ShapeDtypeStruct((b, s, A_Q), BF16),
        scratch_shapes=[
            pltpu.VMEM((1, cols), F32),
            pltpu.VMEM((LANES, cols), F32),
            pltpu.VMEM((NSA_TK, cols), F32),
            pltpu.VMEM((NSA_TK, cols), F32),
        ],
        compiler_params=_params("parallel", "parallel", "parallel"),
        name="nsa",
    )(qext, kc, vc, ksl, vsl, kwp, vwp, ga, ov_t, wbias)


def _dil_kernel(q_ref, k_ref, v_ref, o_ref, lse_ref, *, seg_tiles):
    t = DIL_T
    lo = lax.broadcasted_iota(I32, (t, LANES), 1) < HEAD_DIM
    kj = lax.broadcasted_iota(I32, (t, 2 * t), 0)
    qi = lax.broadcasted_iota(I32, (t, 2 * t), 1) & (t - 1)
    top = lax.broadcasted_iota(I32, (LANES, t), 0) < HEAD_DIM

    def scores(i):
        has_prev = (i & (seg_tiles - 1)) != 0
        cs = pl.multiple_of(i * t, t)
        ps = pl.multiple_of(jnp.maximum(i - 1, 0) * t, t)
        q = q_ref[0, 0, pl.ds(cs, t), :]
        zero = jnp.zeros_like(q)
        q2 = jnp.concatenate([jnp.where(lo, q, zero), jnp.where(lo, zero, q)], axis=0)
        sp = _dot_nt(k_ref[0, 0, pl.ds(ps, t), :], q2)
        sc = _dot_nt(k_ref[0, 0, pl.ds(cs, t), :], q2)
        return has_prev, ps, cs, sp, sc

    def attend(has_prev, ps, cs, sp, sc):
        sp = jnp.where((kj >= qi) & has_prev, sp, MASKED)
        sc = jnp.where(qi >= kj, sc, MASKED)
        m = jnp.maximum(jnp.max(sp, axis=0, keepdims=True), jnp.max(sc, axis=0, keepdims=True))
        pp = jnp.exp(sp - m)
        pc = jnp.exp(sc - m)
        l = jnp.sum(pp, axis=0, keepdims=True) + jnp.sum(pc, axis=0, keepdims=True)
        o = (_dot_tn(v_ref[0, 0, pl.ds(ps, t), :], pp.astype(BF16))
             + _dot_tn(v_ref[0, 0, pl.ds(cs, t), :], pc.astype(BF16))) / l
        lse = m + jnp.log(l)
        o_ref[0, pl.ds(cs, t), :] = jnp.where(top, o[:, 0:t], o[:, t:2 * t]).T
        lse_ref[0, pl.ds(cs, t), :] = jnp.where(top, lse[:, 0:t], lse[:, t:2 * t]).T

    def tiles(j, carry):
        group = [scores(j * DIL_UNROLL + u) for u in range(DIL_UNROLL)]
        for args in group:
            attend(*args)
        return carry

    lax.fori_loop(0, q_ref.shape[2] // (t * DIL_UNROLL), tiles, 0)


def _dil_call(qkv, dl):
    b, _, s, _ = qkv.shape
    kind = lambda j: pl.BlockSpec((1, 1, s, LANES), lambda bi: (bi, j, 0, 0))
    out = pl.BlockSpec((1, s, LANES), lambda bi: (bi, 0, 0))
    return pl.pallas_call(
        functools.partial(_dil_kernel, seg_tiles=s // dl // DIL_T),
        grid=(b,),
        in_specs=[kind(0), kind(1), kind(2)],
        out_specs=[out, out],
        out_shape=[jax.ShapeDtypeStruct((b, s, LANES), F32)] * 2,
        compiler_params=_params("parallel"),
        name=f"dil{dl}",
    )(qkv, qkv, qkv)


def _merge_kernel(x_ref, oa_ref, od0_ref, od1_ref, od2_ref, ls0_ref, ls1_ref, ls2_ref, gm_ref, mod_ref,
                  wa_ref, wb_ref, wo_ref, g2_ref, rcat_ref, rb_ref, x1_ref, h2_ref, route_ref, perm_scr):
    tm = x_ref.shape[0]
    d = x_ref.shape[1]
    hm = tm // 2
    for n, ref in enumerate((od1_ref, od2_ref, ls1_ref, ls2_ref)):
        dl = ref.shape[1]
        for r in range(dl):
            perm_scr[n, pl.ds(r, tm // dl, stride=dl), :] = ref[0, r]

    def up(rows):
        l0, l1, l2 = ls0_ref[rows], perm_scr[2, rows], perm_scr[3, rows]
        mx = jnp.maximum(jnp.maximum(l0, l1), l2)
        e0, e1, e2 = jnp.exp(l0 - mx), jnp.exp(l1 - mx), jnp.exp(l2 - mx)
        den = e0 + e1 + e2
        ob = jnp.concatenate([(od0_ref[rows] * (e0 / den)).astype(BF16),
                              (perm_scr[0, rows] * (e1 / den)).astype(BF16),
                              (perm_scr[1, rows] * (e2 / den)).astype(BF16)], axis=1)
        return _dot(oa_ref[rows], wa_ref[...]), _dot(ob, wb_ref[...])

    def out_proj(rows, ya, yb):
        y = gm_ref[rows, 0:d].astype(F32) * ya + gm_ref[rows, d:2 * d].astype(F32) * yb
        return _dot(y.astype(BF16), wo_ref[...])

    def residual_and_logits(h, rows, z):
        x1 = x_ref[rows] + mod_ref[0, 2:3, :] * z
        x1_ref[rows] = x1
        ms = jnp.mean(x1 * x1, axis=-1, keepdims=True)
        h2 = x1 * lax.rsqrt(ms + EPS) * g2_ref[...]
        h2 = h2 * (1.0 + mod_ref[0, 4:5, :]) + mod_ref[0, 3:4, :]
        _store_token_tiles(h2_ref, h2, first=h * hm)
        hi, lo = _split(h2)
        both = _dot(hi, rcat_ref[...])
        return both[:, 0:LANES] + both[:, LANES:2 * LANES] + _dot(lo, rcat_ref[:, 0:LANES]) + rb_ref[...]

    def route(rows, logits):
        lane = lax.broadcasted_iota(I32, (hm, LANES), 1)
        gl = jnp.where(lane < N_GROUPS, logits, NEG_INF)
        gmax = jnp.max(gl, axis=-1, keepdims=True)
        g_w = 1.0 / jnp.sum(jnp.exp(gl - gmax), axis=-1, keepdims=True)
        g_idx = jnp.min(jnp.where(gl == gmax, lane, LANES), axis=-1, keepdims=True)
        e_lo = N_GROUPS + EXPERTS_PER_GROUP * g_idx
        el = jnp.where((lane >= e_lo) & (lane < e_lo + EXPERTS_PER_GROUP), logits, NEG_INF)
        m1 = jnp.max(el, axis=-1, keepdims=True)
        i1 = jnp.min(jnp.where(el == m1, lane, LANES), axis=-1, keepdims=True)
        el2 = jnp.where(lane == i1, PICKED, el)
        m2 = jnp.max(el2, axis=-1, keepdims=True)
        i2 = jnp.min(jnp.where(el2 == m2, lane, LANES), axis=-1, keepdims=True)
        ex = jnp.exp(m2 - m1)
        w1 = g_w * (1.0 / (1.0 + ex))
        w2 = g_w * (ex / (1.0 + ex))
        route_ref[rows] = jnp.where(
            lane == 0, (i1 - N_GROUPS).astype(F32),
            jnp.where(lane == 1, (i2 - N_GROUPS).astype(F32),
                      jnp.where(lane == 2, w1, jnp.where(lane == 3, w2, 0.0))))

    halves = (slice(0, hm), slice(hm, tm))
    ups = [up(rows) for rows in halves]
    zs = [out_proj(rows, *u) for rows, u in zip(halves, ups)]
    logits = [residual_and_logits(h, rows, z) for h, (rows, z) in enumerate(zip(halves, zs))]
    for rows, lg in zip(halves, logits):
        route(rows, lg)


def _merge_call(x2, oa, ods, lses, gm, mod6, wa, wb, wo, g2, rcat, rb, s):
    t, d = x2.shape
    b = t // s
    tm = MERGE_TM
    per_b = s // tm
    row = lambda i: (i, 0)
    const = lambda i: (0, 0)

    def by_class(dl):
        return pl.BlockSpec((1, dl, tm // dl, LANES), lambda i: (i // per_b, 0, i % per_b, 0))

    dls = [dl for _, dl in DIL_PAIRS]
    assert dls[0] == 1
    group_specs = [pl.BlockSpec((tm, LANES), row)] + [by_class(dl) for dl in dls[1:]]
    views = lambda arrs: [arrs[0].reshape(t, LANES)] + [
        a.reshape(b, dl, s // dl, LANES) for a, dl in zip(arrs[1:], dls[1:])]
    return pl.pallas_call(
        _merge_kernel,
        grid=(t // tm,),
        in_specs=[
            pl.BlockSpec((tm, d), row),
            pl.BlockSpec((tm, A_Q), row),
            *group_specs, *group_specs,
            pl.BlockSpec((tm, 2 * d), row),
            pl.BlockSpec((1, 6, d), lambda i: (i // per_b, 0, 0)),
            pl.BlockSpec(wa.shape, const),
            pl.BlockSpec(wb.shape, const),
            pl.BlockSpec(wo.shape, const),
            pl.BlockSpec((1, d), const),
            pl.BlockSpec(rcat.shape, const),
            pl.BlockSpec((1, LANES), const),
        ],
        out_specs=[
            pl.BlockSpec((tm, d), row),
            pl.BlockSpec((tm * ROW_TILE[0], LANES), row),
            pl.BlockSpec((tm, LANES), row),
        ],
        out_shape=[
            jax.ShapeDtypeStruct((t, d), F32),
            jax.ShapeDtypeStruct((t * ROW_TILE[0], LANES), F32),
            jax.ShapeDtypeStruct((t, LANES), F32),
        ],
        scratch_shapes=[pltpu.VMEM((2 * (len(dls) - 1), tm, LANES), F32)],
        compiler_params=_params("parallel"),
        name="merge",
    )(x2, oa, *views(ods), *views(lses), gm, mod6, wa, wb, wo, g2, rcat, rb)


def _rank_kernel(route_ref, rank_ref, count_ref, carry_scr):
    tm = route_ref.shape[0]

    @pl.when(pl.program_id(0) == 0)
    def _():
        carry_scr[...] = jnp.zeros(carry_scr.shape, F32)

    route = route_ref[...]
    lane = lax.broadcasted_iota(I32, (tm, LANES), 1)
    e1 = route[:, 0:1].astype(I32)
    e2 = route[:, 1:2].astype(I32)
    hit1 = lane == e1
    hit2 = lane == e2
    cnt = jnp.where(hit1 | hit2, 1.0, 0.0)
    r = lax.broadcasted_iota(I32, (tm, tm), 0)
    c = lax.broadcasted_iota(I32, (tm, tm), 1)
    below = jnp.where(c < r, 1.0, 0.0).astype(BF16)
    before = _dot(below, cnt.astype(BF16)) + carry_scr[...]
    r1 = jnp.sum(jnp.where(hit1, before, 0.0), axis=-1, keepdims=True)
    r2 = jnp.sum(jnp.where(hit2, before, 0.0), axis=-1, keepdims=True)
    rank_ref[...] = jnp.where(lane == 0, r1, jnp.where(lane == 1, r2, 0.0))
    carry_scr[...] = carry_scr[...] + jnp.sum(cnt, axis=0, keepdims=True)
    count_ref[...] = carry_scr[...]


def _rank_call(route):
    t = route.shape[0]
    tm = RANK_TM
    return pl.pallas_call(
        _rank_kernel,
        grid=(t // tm,),
        in_specs=[pl.BlockSpec((tm, LANES), lambda i: (i, 0))],
        out_specs=[
            pl.BlockSpec((tm, LANES), lambda i: (i, 0)),
            pl.BlockSpec((1, LANES), lambda i: (0, 0)),
        ],
        out_shape=[
            jax.ShapeDtypeStruct((t, LANES), F32),
            jax.ShapeDtypeStruct((1, LANES), F32),
        ],
        scratch_shapes=[pltpu.VMEM((1, LANES), F32)],
        compiler_params=_params("arbitrary"),
        name="rank",
    )(route)


def _issue_unrolled(n, fn):
    def body(j, carry):
        for u in range(DMA_UNROLL):
            fn(j * DMA_UNROLL + u)
        return carry
    lax.fori_loop(0, n // DMA_UNROLL, body, 0)


def _dispatch_kernel(cnt_ref, pstart_ref, dest_hbm, h_hbm, xs_hbm, idx_smem, hbuf, zbuf,
                     isem, hsem, dsem, zsem):
    tm = MOE_TM
    nd = TOP_K * tm
    i = pl.program_id(0)
    n = pl.num_programs(0)
    slot = i % 2
    buf = i % 3
    n_slots = xs_hbm.shape[0]

    def idx_copy(step, sl):
        return pltpu.make_async_copy(dest_hbm.at[step], idx_smem.at[sl], isem.at[sl])

    def tile_copy(step, bf):
        rows = pl.ds(pl.multiple_of(step * tm, tm), tm)
        return pltpu.make_async_copy(h_hbm.at[rows], hbuf.at[bf], hsem.at[bf])

    def wait_scatter(bf):
        for _ in range(TOP_K):
            pltpu.make_async_copy(hbuf.at[bf], xs_hbm.at[pl.ds(0, tm)], dsem.at[bf]).wait()

    @pl.when(i == 0)
    def _():
        idx_copy(0, 0).start()
        tile_copy(0, 0).start()

        @pl.when(n > 1)
        def _():
            tile_copy(1, 1).start()
        zbuf[...] = jnp.zeros(zbuf.shape, F32)

        def zero_row(row):
            return pltpu.make_async_copy(zbuf.at[0], xs_hbm.at[row], zsem)

        def per_expert(e, used):
            base = pstart_ref[e]
            cnt = cnt_ref[e]
            pad = (cnt + MOE_BLOCK - 1) // MOE_BLOCK * MOE_BLOCK

            def start(r, c):
                zero_row(base + r).start()
                return c

            def wait(r, c):
                zero_row(base + r).wait()
                return c
            lax.fori_loop(cnt, pad, start, 0)
            lax.fori_loop(cnt, pad, wait, 0)
            return used + pad
        used = lax.fori_loop(0, N_EXPERTS, per_expert, 0)

        def zero_block(blk):
            rows = pl.ds(pl.multiple_of(blk * MOE_BLOCK, MOE_BLOCK), MOE_BLOCK)
            return pltpu.make_async_copy(zbuf, xs_hbm.at[rows], zsem)

        def start_blk(blk, c):
            zero_block(blk).start()
            return c

        def wait_blk(blk, c):
            zero_block(blk).wait()
            return c
        lax.fori_loop(used // MOE_BLOCK, n_slots // MOE_BLOCK, start_blk, 0)
        lax.fori_loop(used // MOE_BLOCK, n_slots // MOE_BLOCK, wait_blk, 0)

    idx_copy(i, slot).wait()

    @pl.when(i + 1 < n)
    def _():
        idx_copy(i + 1, 1 - slot).start()

    tile_copy(i, buf).wait()

    def one(r):
        src = hbuf.at[buf, r & (tm - 1)]
        pltpu.make_async_copy(src, xs_hbm.at[idx_smem[slot, r]], dsem.at[buf]).start()
    _issue_unrolled(nd, one)

    @pl.when(i > 0)
    def _():
        wait_scatter((i + 2) % 3)

    @pl.when(i + 2 < n)
    def _():
        tile_copy(i + 2, (i + 2) % 3).start()

    @pl.when(i == n - 1)
    def _():
        wait_scatter(buf)


def _dispatch_call(counts, pad_starts, dest_tiles, h2, n_slots):
    n_steps = dest_tiles.shape[0]
    grid_spec = pltpu.PrefetchScalarGridSpec(
        num_scalar_prefetch=2,
        grid=(n_steps,),
        in_specs=[pl.BlockSpec(memory_space=pl.ANY), pl.BlockSpec(memory_space=pl.ANY)],
        out_specs=pl.BlockSpec(memory_space=pl.ANY),
        scratch_shapes=[
            pltpu.SMEM((2, TOP_K * MOE_TM), I32),
            pltpu.VMEM((3, MOE_TM) + ROW_TILE, F32),
            pltpu.VMEM((MOE_BLOCK,) + ROW_TILE, F32),
            pltpu.SemaphoreType.DMA((2,)),
            pltpu.SemaphoreType.DMA((3,)),
            pltpu.SemaphoreType.DMA((3,)),
            pltpu.SemaphoreType.DMA(()),
        ],
    )
    return pl.pallas_call(
        _dispatch_kernel,
        grid_spec=grid_spec,
        out_shape=jax.ShapeDtypeStruct((n_slots,) + ROW_TILE, F32),
        compiler_params=_params("arbitrary"),
        name="dispatch",
    )(counts, pad_starts, dest_tiles, h2)


def _expert_kernel(be_ref, x_ref, wg_ref, wu_ref, wd_ref, y_ref, wg_scr, wu_scr, wd_scr):
    i = pl.program_id(0)

    @pl.when((i == 0) | (be_ref[i] != be_ref[jnp.maximum(i - 1, 0)]))
    def _():
        wg_scr[...] = wg_ref[0].astype(BF16)
        wu_scr[...] = wu_ref[0].astype(BF16)
        wd_scr[...] = wd_ref[0].astype(BF16)

    xb = _load_token_tiles(x_ref, 0, MOE_BLOCK).astype(BF16)
    gate = _dot(xb, wg_scr[...])
    up = _dot(xb, wu_scr[...])
    hid = (gate * jax.nn.sigmoid(gate) * up).astype(BF16)
    _store_token_tiles(y_ref, _dot(hid, wd_scr[...]))


def _expert_call(blk_expert, xs, wg, wu, wd):
    n_blk = blk_expert.shape[0]
    d = wg.shape[1]
    blk = (MOE_BLOCK * ROW_TILE[0], LANES)
    grid_spec = pltpu.PrefetchScalarGridSpec(
        num_scalar_prefetch=1,
        grid=(n_blk,),
        in_specs=[
            pl.BlockSpec(blk, lambda i, be: (i, 0)),
            pl.BlockSpec((1, d, EXPERT_FF), lambda i, be: (be[i], 0, 0)),
            pl.BlockSpec((1, d, EXPERT_FF), lambda i, be: (be[i], 0, 0)),
            pl.BlockSpec((1, EXPERT_FF, d), lambda i, be: (be[i], 0, 0)),
        ],
        out_specs=pl.BlockSpec(blk, lambda i, be: (i, 0)),
        scratch_shapes=[
            pltpu.VMEM((d, EXPERT_FF), BF16),
            pltpu.VMEM((d, EXPERT_FF), BF16),
            pltpu.VMEM((EXPERT_FF, d), BF16),
        ],
    )
    return pl.pallas_call(
        _expert_kernel,
        grid_spec=grid_spec,
        out_shape=jax.ShapeDtypeStruct(xs.shape, F32),
        compiler_params=_params("arbitrary"),
        name="experts",
    )(blk_expert, xs, wg, wu, wd)


def _combine_kernel(dest_hbm, y_hbm, x1_ref, route_ref, mod_ref, o_ref, idx_smem, ybuf, isem, dsem):
    tm = MOE_TM
    nd = TOP_K * tm
    i = pl.program_id(0)
    n = pl.num_programs(0)
    slot = i % 2
    nxt = 1 - slot

    def idx_copy(step, sl):
        return pltpu.make_async_copy(dest_hbm.at[step], idx_smem.at[sl], isem.at[sl])

    ns = ROW_TILE[0]

    def step_rows(sl):
        return pltpu.make_async_copy(y_hbm.at[pl.ds(0, nd * ns)], ybuf.at[sl], dsem.at[sl])

    def issue_rows(sl):
        def one(r):
            src = y_hbm.at[pl.ds(pl.multiple_of(idx_smem[sl, r] * ns, ns), ns)]
            dst = ybuf.at[sl, pl.ds(pl.multiple_of(r * ns, ns), ns)]
            pltpu.make_async_copy(src, dst, dsem.at[sl]).start()
        _issue_unrolled(nd, one)

    @pl.when(i == 0)
    def _():
        idx_copy(0, 0).start()
        idx_copy(0, 0).wait()
        issue_rows(0)

        @pl.when(n > 1)
        def _():
            idx_copy(1, 1).start()

    @pl.when(i + 1 < n)
    def _():
        idx_copy(i + 1, nxt).wait()
        issue_rows(nxt)

    @pl.when(i + 2 < n)
    def _():
        idx_copy(i + 2, slot).start()

    step_rows(slot).wait()
    route = route_ref[...]
    w1 = route[:, 2:3]
    w2 = route[:, 3:4]
    y = (w1 * _load_token_tiles(ybuf, 0, tm, lead=(slot,))
         + w2 * _load_token_tiles(ybuf, tm, tm, lead=(slot,)))
    o_ref[...] = x1_ref[...] + mod_ref[0, 5:6, :] * y


def _combine_call(dest_tiles, y_slots, x1, route, mod6, s):
    t, d = x1.shape
    tm = MOE_TM
    per_b = s // tm
    return pl.pallas_call(
        _combine_kernel,
        grid=(t // tm,),
        in_specs=[
            pl.BlockSpec(memory_space=pl.ANY),
            pl.BlockSpec(memory_space=pl.ANY),
            pl.BlockSpec((tm, d), lambda i: (i, 0)),
            pl.BlockSpec((tm, LANES), lambda i: (i, 0)),
            pl.BlockSpec((1, 6, d), lambda i: (i // per_b, 0, 0)),
        ],
        out_specs=pl.BlockSpec((tm, d), lambda i: (i, 0)),
        out_shape=jax.ShapeDtypeStruct((t, d), F32),
        scratch_shapes=[
            pltpu.SMEM((2, TOP_K * tm), I32),
            pltpu.VMEM((2, TOP_K * tm * ROW_TILE[0], LANES), F32),
            pltpu.SemaphoreType.DMA((2,)),
            pltpu.SemaphoreType.DMA((2,)),
        ],
        compiler_params=_params("arbitrary"),
        name="combine",
    )(dest_tiles, y_slots, x1, route, mod6)


def _rope_tiles(pos):
    inv = ROPE_THETA ** (-jnp.arange(0, HEAD_DIM, 2, dtype=F32) / HEAD_DIM)
    ang = pos.astype(F32)[..., None] * inv
    cos, sin = jnp.cos(ang), jnp.sin(ang)
    return (jnp.concatenate([cos, cos, cos, cos], axis=-1),
            jnp.concatenate([-sin, sin, -sin, sin], axis=-1))


def _selection_constants(s):
    n_c = s // CMP_STRIDE
    n_s = s // SEL_BLOCK
    cs = np.arange(n_c) * CMP_STRIDE
    ss = np.arange(n_s) * SEL_BLOCK
    ov = np.clip(np.minimum(cs[:, None] + CMP_BLOCK, ss[None, :] + SEL_BLOCK)
                 - np.maximum(cs[:, None], ss[None, :]), 0, None).astype(np.float32) / CMP_BLOCK
    kq = np.arange(NSA_TQ)[:, None] - np.arange(NSA_TQ)[None, :]
    wbias = np.stack([np.where(kq >= 0, 0.0, MASKED), np.where(kq <= 0, 0.0, MASKED)]).astype(np.float32)
    return jnp.asarray(ov.T, BF16), jnp.asarray(wbias)


def _mixer_and_router(x, c, positions, w_ada, b_ada, norm1_g, norm2_g, w_in, nsa_q_norm, nsa_k_norm,
                      cmp_pe_k, cmp_w1_k, cmp_w2_k, cmp_pe_v, cmp_w1_v, cmp_w2_v, dil_q_norm,
                      dil_k_norm, w_up_a, w_up_b, w_out, w_group, b_group, w_router, b_router):
    b, s, d = x.shape
    scale = HEAD_DIM ** -0.5
    mod6 = _mod_call(c, w_ada, b_ada).reshape(b, 6, d)

    c1 = A_Q
    c2 = c1 + 6 * A_KV
    c3 = c2 + 3 * NSA_HEADS
    c4 = c3 + 3 * DIL_W
    w_perm = jnp.concatenate([
        w_in[:, :c2], w_in[:, c3:c4],
        jnp.pad(w_in[:, c2:c3], ((0, 0), (0, LANES - 3 * NSA_HEADS))), w_in[:, c4:]], axis=1).astype(BF16)
    two = lambda g: jnp.concatenate([g, g]).astype(F32)
    gains = jnp.stack([two(nsa_q_norm) * (scale * LOG2E), two(nsa_k_norm), two(dil_q_norm) * scale,
                       two(dil_k_norm)])
    cos_t, sin_t = _rope_tiles(positions)
    qext, kvc, ksl, vsl, kwp, vwp, dil0, dil1, dil2, ga, gm = _proj_call(
        x, mod6, norm1_g.reshape(1, d), w_perm, gains, cos_t, sin_t)

    n_chunk = s // CMP_STRIDE
    half = CMP_STRIDE * HEAD_DIM
    chunks = kvc.reshape(b, s, 4, HEAD_DIM).transpose(0, 2, 1, 3).reshape(b, 4, n_chunk, half)
    pe = jnp.stack([cmp_pe_k.reshape(2, half), cmp_pe_v.reshape(2, half)])
    w1 = jnp.stack([cmp_w1_k, cmp_w1_v]).astype(BF16)
    zeros = jnp.zeros((CMP_HIDDEN, HEAD_DIM), F32)
    ext = lambda w: jnp.stack([jnp.concatenate([w, zeros], 1), jnp.concatenate([zeros, w], 1)])
    w2ext = jnp.stack([ext(cmp_w2_k), ext(cmp_w2_v)]).astype(BF16)
    cmp_pos = jnp.pad(positions[:, CMP_BLOCK - 1::CMP_STRIDE], ((0, 0), (0, 1)))
    ccos, csin = _rope_tiles(cmp_pos)
    kc, vc = _cmp_call(chunks, pe, w1, w2ext, two(nsa_k_norm).reshape(1, LANES), ccos, csin)

    ov_t, wbias = _selection_constants(s)
    o_a = _nsa_call(qext, kc, vc, ksl, vsl, kwp, vwp, ga, ov_t, wbias)

    ods, lses = zip(*[_dil_call(qkv.reshape(b, 3, s, LANES), dl)
                      for qkv, (_, dl) in zip((dil0, dil1, dil2), DIL_PAIRS)])

    t = b * s
    w_r = jnp.concatenate([w_group, w_router.transpose(1, 0, 2).reshape(d, N_EXPERTS)], axis=1)
    w_r = jnp.pad(w_r, ((0, 0), (0, LANES - w_r.shape[1])))
    rcat = jnp.concatenate(_split(w_r), axis=1)
    rb = jnp.pad(jnp.concatenate([b_group, b_router.reshape(-1)]), (0, LANES - N_GROUPS - N_EXPERTS))
    return _merge_call(
        x.reshape(t, d), o_a.reshape(t, A_Q), ods, lses, gm.reshape(t, 2 * d), mod6,
        w_up_a.astype(BF16), w_up_b.astype(BF16), w_out.astype(BF16), norm2_g.reshape(1, d),
        rcat, rb.reshape(1, LANES).astype(F32), s), mod6


def _moe(x1, h2, route, mod6, w_e_gate, w_e_up, w_e_down, s):
    t, _ = x1.shape
    rank, counts = _rank_call(route)
    expert = route[:, 0:TOP_K].astype(I32)
    counts = counts[0, :N_EXPERTS].astype(I32)
    padded = (counts + MOE_BLOCK - 1) // MOE_BLOCK * MOE_BLOCK
    pad_ends = jnp.cumsum(padded)
    pad_starts = pad_ends - padded
    dest = pad_starts[expert] + rank[:, 0:TOP_K].astype(I32)
    n_slots = t * TOP_K + N_EXPERTS * MOE_BLOCK
    n_blk = n_slots // MOE_BLOCK
    blk_start = jnp.arange(n_blk, dtype=I32) * MOE_BLOCK
    blk_expert = jnp.minimum(jnp.sum((pad_ends[None, :] <= blk_start[:, None]).astype(I32), axis=1),
                             N_EXPERTS - 1)
    tm = MOE_TM
    dest_tiles = dest.reshape(t // tm, tm, TOP_K).transpose(0, 2, 1).reshape(t // tm, TOP_K * tm)
    xs = _dispatch_call(counts, pad_starts.astype(I32), dest_tiles, h2.reshape((t,) + ROW_TILE), n_slots)
    y_slots = _expert_call(blk_expert, xs.reshape(n_slots * ROW_TILE[0], LANES), w_e_gate, w_e_up, w_e_down)
    return _combine_call(dest_tiles, y_slots, x1, route, mod6, s)


def kernel(x, c, positions, w_ada, b_ada, norm1_g, norm2_g, w_in, nsa_q_norm, nsa_k_norm, cmp_pe_k,
           cmp_w1_k, cmp_w2_k, cmp_pe_v, cmp_w1_v, cmp_w2_v, dil_q_norm, dil_k_norm, w_up_a, w_up_b,
           w_out, w_group, b_group, w_router, b_router, w_e_gate, w_e_up, w_e_down):
    b, s, d = x.shape
    assert w_ada.shape[0] == 1 and d == D_MODEL and s % NSA_TK == 0
    (x1, h2, route), mod6 = _mixer_and_router(
        x, c, positions, w_ada[0], b_ada[0], norm1_g[0], norm2_g[0], w_in[0], nsa_q_norm[0],
        nsa_k_norm[0], cmp_pe_k[0], cmp_w1_k[0], cmp_w2_k[0], cmp_pe_v[0], cmp_w1_v[0], cmp_w2_v[0],
        dil_q_norm[0], dil_k_norm[0], w_up_a[0], w_up_b[0], w_out[0], w_group[0], b_group[0],
        w_router[0], b_router[0])
    out = _moe(x1, h2, route, mod6, w_e_gate[0], w_e_up[0], w_e_down[0], s)
    return out.reshape(b, s, d)
```

```python
import functools

import jax
import jax.numpy as jnp
import numpy as np
from jax import lax
from jax.experimental import pallas as pl
from jax.experimental.pallas import tpu as pltpu

F32 = jnp.float32
BF16 = jnp.bfloat16
I32 = jnp.int32

D_MODEL = 1024
HEAD_DIM = 64
LANES = 128
ROW_TILE = (D_MODEL // LANES, LANES)
ROPE_THETA = 10000.0
EPS = 1e-6
LOG2E = 1.4426950408889634
NEG_INF = -1e30
FORCE_SCORE = 1e9
MASKED = -1e30
PICKED = -3e38

NSA_HEADS = 8
NSA_KV_HEADS = 2
NSA_GROUP = 4
CMP_BLOCK = 32
CMP_STRIDE = 16
CMP_HIDDEN = 256
SEL_BLOCK = 64
N_SEL = 8
N_LOCAL_SEL = 2
WINDOW = 512
DIL_PAIRS = ((128, 1), (512, 4), (2048, 16))
DIL_GROUPS = 3
A_Q = NSA_HEADS * HEAD_DIM
A_KV = NSA_KV_HEADS * HEAD_DIM
DIL_W = 2 * DIL_GROUPS * HEAD_DIM
N_GROUPS = 4
EXPERTS_PER_GROUP = 8
N_EXPERTS = 32
TOP_K = 2
EXPERT_FF = 512
MOE_BLOCK = 256

VMEM_LIMIT = 56 * 1024 * 1024

T_QA = 0
T_KVC = 4
T_KVA = 6
T_DIL = 10
T_GA = 19
T_GM = 20
N_TILES = 36

PROJ_TM = 256
NSA_TQ = 128
NSA_TK = 512
NSA_NQ = 2
DIL_T = 128
DIL_UNROLL = 4
MERGE_TM = 512
RANK_TM = 512
MOE_TM = 256


def _dot(a, b):
    return jnp.dot(a, b, preferred_element_type=F32)


def _dot_nt(a, b):
    return lax.dot_general(a, b, (((1,), (1,)), ((), ())), preferred_element_type=F32)


def _dot_tn(a, b):
    return lax.dot_general(a, b, (((0,), (0,)), ((), ())), preferred_element_type=F32)


def _split(a):
    hi = a.astype(BF16)
    lo = (a - hi.astype(F32)).astype(BF16)
    return hi, lo


def _load_token_tiles(ref, first, n, lead=()):
    ns = ROW_TILE[0]
    return jnp.concatenate(
        [ref[lead + (pl.ds(first * ns + s, n, stride=ns), slice(None))] for s in range(ns)], axis=1)


def _store_token_tiles(ref, rows, first=0):
    ns = ROW_TILE[0]
    for s in range(ns):
        ref[pl.ds(first * ns + s, rows.shape[0], stride=ns), :] = rows[:, s * LANES:(s + 1) * LANES]


def _params(*sem):
    return pltpu.CompilerParams(dimension_semantics=sem, vmem_limit_bytes=VMEM_LIMIT)


def _mod_kernel(c_ref, whi_ref, wlo_ref, b_ref, o_ref):
    c = c_ref[...]
    ca = c * jax.nn.sigmoid(c)
    hi, lo = _split(ca)
    whi = whi_ref[...]
    o_ref[...] = _dot(hi, whi) + _dot(lo, whi) + _dot(hi, wlo_ref[...]) + b_ref[...]


def _mod_call(c, w_ada, b_ada):
    b, d = c.shape
    n = w_ada.shape[1]
    whi, wlo = _split(w_ada)
    tn = 1024
    return pl.pallas_call(
        _mod_kernel,
        grid=(n // tn,),
        in_specs=[
            pl.BlockSpec((b, d), lambda j: (0, 0)),
            pl.BlockSpec((d, tn), lambda j: (0, j)),
            pl.BlockSpec((d, tn), lambda j: (0, j)),
            pl.BlockSpec((1, tn), lambda j: (0, j)),
        ],
        out_specs=pl.BlockSpec((b, tn), lambda j: (0, j)),
        out_shape=jax.ShapeDtypeStruct((b, n), F32),
        compiler_params=_params("parallel"),
        name="mod",
    )(c, whi, wlo, b_ada.reshape(1, n))


def _norm_rope(a, gain, cos, sin_signed, lo, first):
    sq = a * a
    s0 = jnp.sum(jnp.where(lo, sq, 0.0), axis=-1, keepdims=True)
    s1 = jnp.sum(jnp.where(lo, 0.0, sq), axis=-1, keepdims=True)
    r = jnp.where(lo, lax.rsqrt(s0 * (1.0 / HEAD_DIM) + EPS), lax.rsqrt(s1 * (1.0 / HEAD_DIM) + EPS))
    y = a * r * gain
    rot = jnp.where(first, pltpu.roll(y, 96, 1), pltpu.roll(y, 32, 1))
    return y * cos + rot * sin_signed


def _proj_kernel(x_ref, mod_ref, g1_ref, w_ref, gains_ref, cos_ref, sin_ref, kw0_ref, vw0_ref,
                 qext_ref, kvc_ref, ksl_ref, vsl_ref, kwp_ref, vwp_ref, dil0_ref, dil1_ref, dil2_ref,
                 ga_ref, gm_ref, perm_scr):
    del kw0_ref, vw0_ref
    tm = x_ref.shape[1]
    x = x_ref[0]
    ms = jnp.mean(x * x, axis=-1, keepdims=True)
    y = x * lax.rsqrt(ms + EPS) * g1_ref[...]
    sh1 = mod_ref[0, 0:1, :]
    sc1 = mod_ref[0, 1:2, :]
    h = (y * (1.0 + sc1) + sh1).astype(BF16)
    lane = lax.broadcasted_iota(I32, (tm, LANES), 1)
    lo = lane < HEAD_DIM
    first = (lane & (HEAD_DIM - 1)) < (HEAD_DIM // 2)
    cos = jnp.concatenate([cos_ref[0]] * 4, axis=1)
    sin = jnp.concatenate([sin_ref[0]] * 4, axis=1)
    sin = jnp.where(first, -sin, sin)
    nr = functools.partial(_norm_rope, cos=cos, sin_signed=sin, lo=lo, first=first)

    for c in range(N_TILES // 2):
        acc = _dot(h, w_ref[:, c * 2 * LANES:(c + 1) * 2 * LANES])
        for half in range(2):
            t = 2 * c + half
            a = acc[:, half * LANES:(half + 1) * LANES]
            if t < T_KVC:
                yq = nr(a, gains_ref[0:1, :])
                rq = pltpu.roll(yq, HEAD_DIM, 1)
                if t // 2 == 0:
                    e0 = jnp.where(lo, yq, 0.0)
                    e1 = jnp.where(lo, rq, 0.0)
                else:
                    e0 = jnp.where(lo, 0.0, rq)
                    e1 = jnp.where(lo, 0.0, yq)
                qext_ref[0, 2 * t] = e0.astype(BF16)
                qext_ref[0, 2 * t + 1] = e1.astype(BF16)
            elif t < T_KVA:
                kvc_ref[0, :, (t - T_KVC) * LANES:(t - T_KVC + 1) * LANES] = a
            elif t < T_DIL:
                j = t - T_KVA
                k_out, v_out = (ksl_ref, vsl_ref) if j < 2 else (kwp_ref, vwp_ref)
                if j % 2 == 0:
                    k_out[0] = nr(a, gains_ref[1:2, :]).astype(BF16)
                else:
                    v_out[0, 0] = jnp.where(lo, a, 1.0).astype(BF16)
                    v_out[0, 1] = jnp.where(lo, pltpu.roll(a, HEAD_DIM, 1), 1.0).astype(BF16)
            elif t < T_GA:
                j = t - T_DIL
                if j < 3:
                    v = nr(a, gains_ref[2:3, :])
                elif j < 6:
                    v = nr(a, gains_ref[3:4, :])
                else:
                    v = a
                kind, gi = divmod(j, DIL_GROUPS)
                dl = DIL_PAIRS[gi][1]
                if dl == 1:
                    dil0_ref[0, kind] = v.astype(BF16)
                else:
                    out = (dil0_ref, dil1_ref, dil2_ref)[gi]
                    perm_scr[...] = v
                    for r in range(dl):
                        out[0, kind, r] = perm_scr[pl.ds(r, tm // dl, stride=dl), :].astype(BF16)
            elif t < T_GM:
                ga_ref[0] = jax.nn.sigmoid(a)
            else:
                j = t - T_GM
                gm_ref[0, :, j * LANES:(j + 1) * LANES] = jax.nn.sigmoid(a).astype(BF16)


def _proj_call(x, mod6, g1, w_perm, gains, cos_t, sin_t):
    b, s, d = x.shape
    tm = PROJ_TM
    n = N_TILES * LANES
    row = lambda bi, i: (bi, i, 0)
    heads = lambda bi, i: (bi, 0, i, 0)
    pad = WINDOW // tm
    in_specs = [
        pl.BlockSpec((1, tm, d), row),
        pl.BlockSpec((1, 6, d), lambda bi, i: (bi, 0, 0)),
        pl.BlockSpec((1, d), lambda bi, i: (0, 0)),
        pl.BlockSpec((d, n), lambda bi, i: (0, 0)),
        pl.BlockSpec((4, LANES), lambda bi, i: (0, 0)),
        pl.BlockSpec((1, tm, HEAD_DIM // 2), row),
        pl.BlockSpec((1, tm, HEAD_DIM // 2), row),
        pl.BlockSpec(memory_space=pl.ANY),
        pl.BlockSpec(memory_space=pl.ANY),
    ]
    out_specs = [
        pl.BlockSpec((1, NSA_HEADS, tm, LANES), heads),
        pl.BlockSpec((1, tm, 2 * LANES), row),
        pl.BlockSpec((1, tm, LANES), row),
        pl.BlockSpec((1, NSA_KV_HEADS, tm, LANES), heads),
        pl.BlockSpec((1, tm, LANES), lambda bi, i: (bi, i + pad, 0)),
        pl.BlockSpec((1, NSA_KV_HEADS, tm, LANES), lambda bi, i: (bi, 0, i + pad, 0)),
        pl.BlockSpec((1, 3, tm, LANES), heads),
        *[pl.BlockSpec((1, 3, dl, tm // dl, LANES), lambda bi, i: (bi, 0, 0, i, 0)) for _, dl in DIL_PAIRS[1:]],
        pl.BlockSpec((1, tm, LANES), row),
        pl.BlockSpec((1, tm, 2 * d), row),
    ]
    out_shape = [
        jax.ShapeDtypeStruct((b, NSA_HEADS, s, LANES), BF16),
        jax.ShapeDtypeStruct((b, s, 2 * LANES), F32),
        jax.ShapeDtypeStruct((b, s, LANES), BF16),
        jax.ShapeDtypeStruct((b, NSA_KV_HEADS, s, LANES), BF16),
        jax.ShapeDtypeStruct((b, s + WINDOW, LANES), BF16),
        jax.ShapeDtypeStruct((b, NSA_KV_HEADS, s + WINDOW, LANES), BF16),
        jax.ShapeDtypeStruct((b, 3, s, LANES), BF16),
        *[jax.ShapeDtypeStruct((b, 3, dl, s // dl, LANES), BF16) for _, dl in DIL_PAIRS[1:]],
        jax.ShapeDtypeStruct((b, s, LANES), F32),
        jax.ShapeDtypeStruct((b, s, 2 * d), BF16),
    ]
    kw0 = jnp.zeros(out_shape[4].shape, BF16)
    vw0 = jnp.zeros(out_shape[5].shape, BF16)
    return pl.pallas_call(
        _proj_kernel,
        grid=(b, s // tm),
        in_specs=in_specs,
        out_specs=out_specs,
        out_shape=out_shape,
        scratch_shapes=[pltpu.VMEM((tm, LANES), F32)],
        input_output_aliases={7: 4, 8: 5},
        compiler_params=_params("parallel", "parallel"),
        name="proj",
    )(x, mod6, g1, w_perm, gains, cos_t, sin_t, kw0, vw0)


def _cmp_kernel(c_ref, pe_ref, w1_ref, w2_ref, gain_ref, cos_ref, sin_ref, kc_ref, vc_ref):
    nb = c_ref.shape[2]
    half = c_ref.shape[3]
    lane = lax.broadcasted_iota(I32, (nb, LANES), 1)
    lo = lane < HEAD_DIM
    first = (lane & (HEAD_DIM - 1)) < (HEAD_DIM // 2)
    for kind in range(2):
        out = jnp.zeros((nb, LANES), F32)
        for hd in range(2):
            c = c_ref[0, kind * 2 + hd]
            top = _dot((c + pe_ref[kind, 0:1, :]).astype(BF16), w1_ref[kind, 0:half, :])
            bot = _dot((c + pe_ref[kind, 1:2, :]).astype(BF16), w1_ref[kind, half:2 * half, :])
            hid = top + pltpu.roll(bot, nb - 1, 0)
            hid = hid * jax.nn.sigmoid(hid)
            out = out + _dot(hid.astype(BF16), w2_ref[kind, hd])
        if kind == 0:
            out = _norm_rope(out, gain_ref[...], cos_ref[0], sin_ref[0], lo, first)
            kc_ref[0] = out.astype(BF16)
        else:
            vc_ref[0, 0] = jnp.where(lo, out, 0.0).astype(BF16)
            vc_ref[0, 1] = jnp.where(lo, pltpu.roll(out, HEAD_DIM, 1), 0.0).astype(BF16)


def _cmp_call(chunks, pe, w1, w2ext, gain_k, ccos, csin):
    b, _, nb, half = chunks.shape
    return pl.pallas_call(
        _cmp_kernel,
        grid=(b,),
        in_specs=[
            pl.BlockSpec((1, 4, nb, half), lambda bi: (bi, 0, 0, 0)),
            pl.BlockSpec((2, 2, half), lambda bi: (0, 0, 0)),
            pl.BlockSpec((2, 2 * half, CMP_HIDDEN), lambda bi: (0, 0, 0)),
            pl.BlockSpec((2, 2, CMP_HIDDEN, LANES), lambda bi: (0, 0, 0, 0)),
            pl.BlockSpec((1, LANES), lambda bi: (0, 0)),
            pl.BlockSpec((1, nb, LANES), lambda bi: (bi, 0, 0)),
            pl.BlockSpec((1, nb, LANES), lambda bi: (bi, 0, 0)),
        ],
        out_specs=[
            pl.BlockSpec((1, nb, LANES), lambda bi: (bi, 0, 0)),
            pl.BlockSpec((1, NSA_KV_HEADS, nb, LANES), lambda bi: (bi, 0, 0, 0)),
        ],
        out_shape=[
            jax.ShapeDtypeStruct((b, nb, LANES), BF16),
            jax.ShapeDtypeStruct((b, NSA_KV_HEADS, nb, LANES), BF16),
        ],
        compiler_params=_params("parallel"),
        name="cmp",
    )(chunks, pe, w1, w2ext, gain_k, ccos, csin)


def _softmax_cols(s, exp=jnp.exp):
    p = exp(s - jnp.max(s, axis=0, keepdims=True))
    return p, jnp.sum(p, axis=0, keepdims=True)


def _block_max(s, offs, bs):
    m = None
    for j, off in enumerate(offs):
        mj = jnp.max(s[j * bs:(j + 1) * bs], axis=0, keepdims=True) + off
        m = mj if m is None else jnp.maximum(m, mj)
    return m


def _block_exp2(s, offs, bs, m):
    return jnp.concatenate(
        [jnp.exp2(s[j * bs:(j + 1) * bs] - (m - off)) for j, off in enumerate(offs)], axis=0)


def _nsa_kernel(q_ref, kc_ref, vc_ref, ksl_ref, vsl_ref, kw_ref, vw_ref, ga_ref, ov_ref, wb_ref,
                o_ref, m_scr, acc_scr, *s_scrs):
    tq = NSA_TQ
    tk = NSA_TK
    hd = HEAD_DIM
    cols = NSA_GROUP * tq
    tiles = range(NSA_NQ)
    s_buf = lambda u, c: s_scrs[2 * u + c % 2]
    n_blk = ov_ref.shape[0]
    n_chunks = ksl_ref.shape[1] // tk
    bpc = tk // SEL_BLOCK
    kh = pl.program_id(1)
    base = pl.multiple_of(pl.program_id(2) * (NSA_NQ * tq), NSA_NQ * tq)
    t0 = [base + u * tq for u in tiles]
    q4 = [q_ref[0, :, u * tq:(u + 1) * tq, :].reshape(cols, LANES) for u in tiles]
    lane_q = lax.broadcasted_iota(I32, (1, tq), 1)
    lane_q4 = lax.broadcasted_iota(I32, (1, cols), 1) & (tq - 1)
    per_head = lambda b: jnp.concatenate([b] * NSA_GROUP, axis=1)
    older_edge = wb_ref[0]
    causal_edge = wb_ref[1]
    nwin = WINDOW + tq
    n_wb = nwin // tq
    blk = lax.broadcasted_iota(I32, (n_blk, 1), 0)
    blk_f = blk.astype(F32)
    ov = ov_ref[...]

    s_cmp = [_dot_nt(kc_ref[0], q4[u]) for u in tiles]
    s_win = [_dot_nt(kw_ref[0, pl.ds(t0[u], nwin), :], q4[u]) for u in tiles]
    for u in tiles:
        s_buf(u, 0)[...] = _dot_nt(ksl_ref[0, 0:tk, :], q4[u])

    o_cmp, sel_bias = [], []
    for u in tiles:
        nb = s_cmp[u].shape[0]
        cmp_end = lax.broadcasted_iota(I32, (nb, 1), 0) * CMP_STRIDE + (CMP_BLOCK - 1)
        valid = cmp_end <= t0[u] + lane_q4
        e, den = _softmax_cols(jnp.where(valid, s_cmp[u], NEG_INF), jnp.exp2)
        p = jnp.where(valid, e / den, 0.0)
        o_cmp.append(_dot_tn(vc_ref[0, 0], p.astype(BF16))[0:hd])
        psum = p[:, 0:tq] + p[:, tq:2 * tq] + p[:, 2 * tq:3 * tq] + p[:, 3 * tq:4 * tq]
        p_hi, p_lo = _split(psum)
        imp = _dot(ov, p_hi) + _dot(ov, p_lo)
        rel = ((t0[u] + lane_q) >> 6) - blk
        forced = (blk == 0) | ((rel >= 0) & (rel < N_LOCAL_SEL))
        score = jnp.where(rel < 0, NEG_INF, jnp.where(forced, FORCE_SCORE, imp))
        bias = jnp.full((n_blk, tq), MASKED, F32)
        for _ in range(N_SEL):
            best = jnp.max(score, axis=0, keepdims=True)
            first = jnp.min(jnp.where(score == best, blk_f, float(n_blk)), axis=0, keepdims=True)
            pick = blk_f == first
            bias = jnp.where(pick, 0.0, bias)
            score = jnp.where(pick, PICKED, score)
        sel_bias.append(per_head(jnp.where(rel < 0, MASKED, bias)))

    o_win = []
    for u in tiles:
        sw = jnp.concatenate([s_win[u][0:tq] + per_head(older_edge), s_win[u][tq:nwin - tq],
                              s_win[u][nwin - tq:nwin] + per_head(causal_edge)], axis=0)
        w_offs = [jnp.where(t0[u] - WINDOW + j * tq >= 0, 0.0, MASKED) for j in range(n_wb)]
        pw = _block_exp2(sw, w_offs, tq, _block_max(sw, w_offs, tq))
        ow = _dot_tn(vw_ref[0, 0, pl.ds(t0[u], nwin), :], pw.astype(BF16))
        o_win.append(ow[0:hd] / ow[hd:hd + 1])

    diag = base // tk

    def chunk(c):
        if c + 1 < n_chunks:
            for u in tiles:
                s_buf(u, c + 1)[...] = _dot_nt(ksl_ref[0, (c + 1) * tk:(c + 2) * tk, :], q4[u])
        vb = vsl_ref[0, 0, c * tk:(c + 1) * tk, :]
        for u in tiles:
            own = pl.ds(pl.multiple_of(jnp.clip(t0[u] - c * tk, 0, tk - tq), tq), tq)
            buf = s_buf(u, c)
            buf[own, :] = buf[own, :] + per_head(jnp.where(c == diag, causal_edge, 0.0))
            sc = buf[...]
            offs = [sel_bias[u][c * bpc + j:c * bpc + j + 1] for j in range(bpc)]
            m_new = _block_max(sc, offs, SEL_BLOCK)
            if c == 0:
                acc_scr[u] = _dot_tn(vb, _block_exp2(sc, offs, SEL_BLOCK, m_new).astype(BF16))
            else:
                m_old = m_scr[u]
                m_new = jnp.maximum(m_old, m_new)
                pe = _block_exp2(sc, offs, SEL_BLOCK, m_new)
                acc_scr[u] = jnp.exp2(m_old - m_new) * acc_scr[u] + _dot_tn(vb, pe.astype(BF16))
            m_scr[u] = m_new

    chunk(0)
    for c in range(1, n_chunks):
        pl.when(c <= diag)(functools.partial(chunk, c))

    is0 = kh == 0
    for u in tiles:
        rows = slice(u * tq, (u + 1) * tq)
        o_slc = acc_scr[u, 0:hd, :] / acc_scr[u, hd:hd + 1, :]
        gat = ga_ref[0, rows, :].T
        heads = []
        for g in range(NSA_GROUP):
            c = slice(g * tq, (g + 1) * tq)
            og = jnp.zeros((hd, tq), F32)
            for gi, ob in enumerate((o_cmp[u], o_slc, o_win[u])):
                c0 = gi * NSA_HEADS + g
                c1 = c0 + NSA_GROUP
                gate = jnp.where(is0, gat[c0:c0 + 1, :], gat[c1:c1 + 1, :])
                og = og + gate * ob[:, c]
            heads.append(og)
        for pair in range(2):
            tile = jnp.concatenate([heads[2 * pair], heads[2 * pair + 1]], axis=0)
            o_ref[0, rows, pair * LANES:(pair + 1) * LANES] = tile.T.astype(BF16)


def _nsa_call(qext, kc, vc, ksl, vsl, kwp, vwp, ga, ov_t, wbias):
    b, _, s, _ = qext.shape
    tq = NSA_TQ * NSA_NQ
    nb = kc.shape[1]
    cols = NSA_GROUP * NSA_TQ
    assert NSA_TK % tq == 0
    shared = lambda rows: pl.BlockSpec((1, rows, LANES), lambda bi, k, i: (bi, 0, 0))
    per_kv = lambda rows: pl.BlockSpec((1, 1, rows, LANES), lambda bi, k, i: (bi, k, 0, 0))
    return pl.pallas_call(
        _nsa_kernel,
        grid=(b, NSA_KV_HEADS, s // tq),
        in_specs=[
            pl.BlockSpec((1, NSA_GROUP, tq, LANES), lambda bi, k, i: (bi, k, i, 0)),
            shared(nb), per_kv(nb),
            shared(s), per_kv(s),
            shared(s + WINDOW), per_kv(s + WINDOW),
            pl.BlockSpec((1, tq, LANES), lambda bi, k, i: (bi, i, 0)),
            pl.BlockSpec(ov_t.shape, lambda bi, k, i: (0, 0)),
            pl.BlockSpec(wbias.shape, lambda bi, k, i: (0, 0, 0)),
        ],
        out_specs=pl.BlockSpec((1, tq, 2 * LANES), lambda bi, k, i: (bi, i, k)),
        out_shape=jax.ShapeDtypeStruct((b, s, A_Q), BF16),
        scratch_shapes=[
            pltpu.VMEM((NSA_NQ, 1, cols), F32),
            pltpu.VMEM((NSA_NQ, LANES, cols), F32),
            *[pltpu.VMEM((NSA_TK, cols), F32)] * (2 * NSA_NQ),
        ],
        compiler_params=_params("parallel", "parallel", "parallel"),
        name="nsa",
    )(qext, kc, vc, ksl, vsl, kwp, vwp, ga, ov_t, wbias)


def _dil_kernel(q_ref, k_ref, v_ref, o_ref, lse_ref, *, seg_tiles):
    t = DIL_T
    lo = lax.broadcasted_iota(I32, (t, LANES), 1) < HEAD_DIM
    kj = lax.broadcasted_iota(I32, (t, 2 * t), 0)
    qi = lax.broadcasted_iota(I32, (t, 2 * t), 1) & (t - 1)
    top = lax.broadcasted_iota(I32, (LANES, t), 0) < HEAD_DIM

    def scores(i):
        has_prev = (i & (seg_tiles - 1)) != 0
        cs = pl.multiple_of(i * t, t)
        ps = pl.multiple_of(jnp.maximum(i - 1, 0) * t, t)
        q = q_ref[0, 0, pl.ds(cs, t), :]
        zero = jnp.zeros_like(q)
        q2 = jnp.concatenate([jnp.where(lo, q, zero), jnp.where(lo, zero, q)], axis=0)
        sp = _dot_nt(k_ref[0, 0, pl.ds(ps, t), :], q2)
        sc = _dot_nt(k_ref[0, 0, pl.ds(cs, t), :], q2)
        return has_prev, ps, cs, sp, sc

    def attend(has_prev, ps, cs, sp, sc):
        sp = jnp.where((kj >= qi) & has_prev, sp, MASKED)
        sc = jnp.where(qi >= kj, sc, MASKED)
        m = jnp.maximum(jnp.max(sp, axis=0, keepdims=True), jnp.max(sc, axis=0, keepdims=True))
        pp = jnp.exp(sp - m)
        pc = jnp.exp(sc - m)
        l = jnp.sum(pp, axis=0, keepdims=True) + jnp.sum(pc, axis=0, keepdims=True)
        o = (_dot_tn(v_ref[0, 0, pl.ds(ps, t), :], pp.astype(BF16))
             + _dot_tn(v_ref[0, 0, pl.ds(cs, t), :], pc.astype(BF16))) / l
        lse = m + jnp.log(l)
        o_ref[0, pl.ds(cs, t), :] = jnp.where(top, o[:, 0:t], o[:, t:2 * t]).T
        lse_ref[0, pl.ds(cs, t), :] = jnp.where(top, lse[:, 0:t], lse[:, t:2 * t]).T

    def tiles(j, carry):
        group = [scores(j * DIL_UNROLL + u) for u in range(DIL_UNROLL)]
        for args in group:
            attend(*args)
        return carry

    lax.fori_loop(0, q_ref.shape[2] // (t * DIL_UNROLL), tiles, 0)


def _dil_call(qkv, dl):
    b, _, s, _ = qkv.shape
    kind = lambda j: pl.BlockSpec((1, 1, s, LANES), lambda bi: (bi, j, 0, 0))
    out = pl.BlockSpec((1, s, LANES), lambda bi: (bi, 0, 0))
    return pl.pallas_call(
        functools.partial(_dil_kernel, seg_tiles=s // dl // DIL_T),
        grid=(b,),
        in_specs=[kind(0), kind(1), kind(2)],
        out_specs=[out, out],
        out_shape=[jax.ShapeDtypeStruct((b, s, LANES), F32)] * 2,
        compiler_params=_params("parallel"),
        name=f"dil{dl}",
    )(qkv, qkv, qkv)


def _merge_kernel(x_ref, oa_ref, od0_ref, od1_ref, od2_ref, ls0_ref, ls1_ref, ls2_ref, gm_ref, mod_ref,
                  wa_ref, wb_ref, wo_ref, g2_ref, rcat_ref, rb_ref, x1_ref, h2_ref, route_ref, perm_scr):
    tm = x_ref.shape[0]
    d = x_ref.shape[1]
    hm = tm // 2
    for n, ref in enumerate((od1_ref, od2_ref, ls1_ref, ls2_ref)):
        dl = ref.shape[1]
        for r in range(dl):
            perm_scr[n, pl.ds(r, tm // dl, stride=dl), :] = ref[0, r]

    def up(rows):
        l0, l1, l2 = ls0_ref[rows], perm_scr[2, rows], perm_scr[3, rows]
        mx = jnp.maximum(jnp.maximum(l0, l1), l2)
        e0, e1, e2 = jnp.exp(l0 - mx), jnp.exp(l1 - mx), jnp.exp(l2 - mx)
        den = e0 + e1 + e2
        ob = jnp.concatenate([(od0_ref[rows] * (e0 / den)).astype(BF16),
                              (perm_scr[0, rows] * (e1 / den)).astype(BF16),
                              (perm_scr[1, rows] * (e2 / den)).astype(BF16)], axis=1)
        return _dot(oa_ref[rows], wa_ref[...]), _dot(ob, wb_ref[...])

    def out_proj(rows, ya, yb):
        y = gm_ref[rows, 0:d].astype(F32) * ya + gm_ref[rows, d:2 * d].astype(F32) * yb
        return _dot(y.astype(BF16), wo_ref[...])

    def residual_and_logits(h, rows, z):
        x1 = x_ref[rows] + mod_ref[0, 2:3, :] * z
        x1_ref[rows] = x1
        ms = jnp.mean(x1 * x1, axis=-1, keepdims=True)
        h2 = x1 * lax.rsqrt(ms + EPS) * g2_ref[...]
        h2 = h2 * (1.0 + mod_ref[0, 4:5, :]) + mod_ref[0, 3:4, :]
        _store_token_tiles(h2_ref, h2, first=h * hm)
        hi, lo = _split(h2)
        both = _dot(hi, rcat_ref[...])
        return both[:, 0:LANES] + both[:, LANES:2 * LANES] + _dot(lo, rcat_ref[:, 0:LANES]) + rb_ref[...]

    def route(rows, logits):
        lane = lax.broadcasted_iota(I32, (hm, LANES), 1)
        gl = jnp.where(lane < N_GROUPS, logits, NEG_INF)
        gmax = jnp.max(gl, axis=-1, keepdims=True)
        g_w = 1.0 / jnp.sum(jnp.exp(gl - gmax), axis=-1, keepdims=True)
        g_idx = jnp.min(jnp.where(gl == gmax, lane, LANES), axis=-1, keepdims=True)
        e_lo = N_GROUPS + EXPERTS_PER_GROUP * g_idx
        el = jnp.where((lane >= e_lo) & (lane < e_lo + EXPERTS_PER_GROUP), logits, NEG_INF)
        m1 = jnp.max(el, axis=-1, keepdims=True)
        i1 = jnp.min(jnp.where(el == m1, lane, LANES), axis=-1, keepdims=True)
        el2 = jnp.where(lane == i1, PICKED, el)
        m2 = jnp.max(el2, axis=-1, keepdims=True)
        i2 = jnp.min(jnp.where(el2 == m2, lane, LANES), axis=-1, keepdims=True)
        ex = jnp.exp(m2 - m1)
        w1 = g_w * (1.0 / (1.0 + ex))
        w2 = g_w * (ex / (1.0 + ex))
        route_ref[rows] = jnp.where(
            lane == 0, (i1 - N_GROUPS).astype(F32),
            jnp.where(lane == 1, (i2 - N_GROUPS).astype(F32),
                      jnp.where(lane == 2, w1, jnp.where(lane == 3, w2, 0.0))))

    halves = (slice(0, hm), slice(hm, tm))
    ups = [up(rows) for rows in halves]
    zs = [out_proj(rows, *u) for rows, u in zip(halves, ups)]
    logits = [residual_and_logits(h, rows, z) for h, (rows, z) in enumerate(zip(halves, zs))]
    for rows, lg in zip(halves, logits):
        route(rows, lg)


def _merge_call(x2, oa, ods, lses, gm, mod6, wa, wb, wo, g2, rcat, rb, s):
    t, d = x2.shape
    b = t // s
    tm = MERGE_TM
    per_b = s // tm
    row = lambda i: (i, 0)
    const = lambda i: (0, 0)

    def by_class(dl):
        return pl.BlockSpec((1, dl, tm // dl, LANES), lambda i: (i // per_b, 0, i % per_b, 0))

    dls = [dl for _, dl in DIL_PAIRS]
    assert dls[0] == 1
    group_specs = [pl.BlockSpec((tm, LANES), row)] + [by_class(dl) for dl in dls[1:]]
    views = lambda arrs: [arrs[0].reshape(t, LANES)] + [
        a.reshape(b, dl, s // dl, LANES) for a, dl in zip(arrs[1:], dls[1:])]
    return pl.pallas_call(
        _merge_kernel,
        grid=(t // tm,),
        in_specs=[
            pl.BlockSpec((tm, d), row),
            pl.BlockSpec((tm, A_Q), row),
            *group_specs, *group_specs,
            pl.BlockSpec((tm, 2 * d), row),
            pl.BlockSpec((1, 6, d), lambda i: (i // per_b, 0, 0)),
            pl.BlockSpec(wa.shape, const),
            pl.BlockSpec(wb.shape, const),
            pl.BlockSpec(wo.shape, const),
            pl.BlockSpec((1, d), const),
            pl.BlockSpec(rcat.shape, const),
            pl.BlockSpec((1, LANES), const),
        ],
        out_specs=[
            pl.BlockSpec((tm, d), row),
            pl.BlockSpec((tm * ROW_TILE[0], LANES), row),
            pl.BlockSpec((tm, LANES), row),
        ],
        out_shape=[
            jax.ShapeDtypeStruct((t, d), F32),
            jax.ShapeDtypeStruct((t * ROW_TILE[0], LANES), F32),
            jax.ShapeDtypeStruct((t, LANES), F32),
        ],
        scratch_shapes=[pltpu.VMEM((2 * (len(dls) - 1), tm, LANES), F32)],
        compiler_params=_params("parallel"),
        name="merge",
    )(x2, oa, *views(ods), *views(lses), gm, mod6, wa, wb, wo, g2, rcat, rb)


def _rank_kernel(route_ref, rank_ref, count_ref, carry_scr):
    tm = route_ref.shape[0]

    @pl.when(pl.program_id(0) == 0)
    def _():
        carry_scr[...] = jnp.zeros(carry_scr.shape, F32)

    route = route_ref[...]
    lane = lax.broadcasted_iota(I32, (tm, LANES), 1)
    e1 = route[:, 0:1].astype(I32)
    e2 = route[:, 1:2].astype(I32)
    hit1 = lane == e1
    hit2 = lane == e2
    cnt = jnp.where(hit1 | hit2, 1.0, 0.0)
    r = lax.broadcasted_iota(I32, (tm, tm), 0)
    c = lax.broadcasted_iota(I32, (tm, tm), 1)
    below = jnp.where(c < r, 1.0, 0.0).astype(BF16)
    before = _dot(below, cnt.astype(BF16)) + carry_scr[...]
    r1 = jnp.sum(jnp.where(hit1, before, 0.0), axis=-1, keepdims=True)
    r2 = jnp.sum(jnp.where(hit2, before, 0.0), axis=-1, keepdims=True)
    rank_ref[...] = jnp.where(lane == 0, r1, jnp.where(lane == 1, r2, 0.0))
    carry_scr[...] = carry_scr[...] + jnp.sum(cnt, axis=0, keepdims=True)
    count_ref[...] = carry_scr[...]


def _rank_call(route):
    t = route.shape[0]
    tm = RANK_TM
    return pl.pallas_call(
        _rank_kernel,
        grid=(t // tm,),
        in_specs=[pl.BlockSpec((tm, LANES), lambda i: (i, 0))],
        out_specs=[
            pl.BlockSpec((tm, LANES), lambda i: (i, 0)),
            pl.BlockSpec((1, LANES), lambda i: (0, 0)),
        ],
        out_shape=[
            jax.ShapeDtypeStruct((t, LANES), F32),
            jax.ShapeDtypeStruct((1, LANES), F32),
        ],
        scratch_shapes=[pltpu.VMEM((1, LANES), F32)],
        compiler_params=_params("arbitrary"),
        name="rank",
    )(route)


def _for_static_slot(slot, fn):
    for par in range(2):
        pl.when(slot == par)(functools.partial(fn, par))


def _dispatch_kernel(cnt_ref, pstart_ref, dest_hbm, h_hbm, xs_hbm, idx_smem, hbuf, zbuf,
                     isem, hsem, dsem, zsem):
    tm = MOE_TM
    nd = TOP_K * tm
    i = pl.program_id(0)
    n = pl.num_programs(0)
    slot = i % 2
    n_slots = xs_hbm.shape[0]

    def idx_copy(step, sl):
        return pltpu.make_async_copy(dest_hbm.at[step], idx_smem.at[sl], isem.at[sl])

    def tile_copy(step, bf):
        rows = pl.ds(pl.multiple_of(step * tm, tm), tm)
        return pltpu.make_async_copy(h_hbm.at[rows], hbuf.at[bf], hsem.at[bf])

    def wait_scatter(bf):
        for _ in range(TOP_K):
            pltpu.make_async_copy(hbuf.at[bf], xs_hbm.at[pl.ds(0, tm)], dsem.at[bf]).wait()

    @pl.when(i == 0)
    def _():
        idx_copy(0, 0).start()
        tile_copy(0, 0).start()
        zbuf[...] = jnp.zeros(zbuf.shape, F32)

        def zero_row(row):
            return pltpu.make_async_copy(zbuf.at[0], xs_hbm.at[row], zsem)

        def per_expert(e, used):
            base = pstart_ref[e]
            cnt = cnt_ref[e]
            pad = (cnt + MOE_BLOCK - 1) // MOE_BLOCK * MOE_BLOCK

            def start(r, c):
                zero_row(base + r).start()
                return c

            def wait(r, c):
                zero_row(base + r).wait()
                return c
            lax.fori_loop(cnt, pad, start, 0)
            lax.fori_loop(cnt, pad, wait, 0)
            return used + pad
        used = lax.fori_loop(0, N_EXPERTS, per_expert, 0)

        def zero_block(blk):
            rows = pl.ds(pl.multiple_of(blk * MOE_BLOCK, MOE_BLOCK), MOE_BLOCK)
            return pltpu.make_async_copy(zbuf, xs_hbm.at[rows], zsem)

        def start_blk(blk, c):
            zero_block(blk).start()
            return c

        def wait_blk(blk, c):
            zero_block(blk).wait()
            return c
        lax.fori_loop(used // MOE_BLOCK, n_slots // MOE_BLOCK, start_blk, 0)
        lax.fori_loop(used // MOE_BLOCK, n_slots // MOE_BLOCK, wait_blk, 0)

    @pl.when(i > 0)
    def _():
        wait_scatter(1 - slot)

    @pl.when(i + 1 < n)
    def _():
        idx_copy(i + 1, 1 - slot).start()
        tile_copy(i + 1, 1 - slot).start()

    idx_copy(i, slot).wait()
    tile_copy(i, slot).wait()

    def scatter_rows(par):
        for r in range(nd):
            pltpu.make_async_copy(hbuf.at[par, r % tm], xs_hbm.at[idx_smem[par, r]], dsem.at[par]).start()
    _for_static_slot(slot, scatter_rows)

    @pl.when(i == n - 1)
    def _():
        wait_scatter(slot)


def _dispatch_call(counts, pad_starts, dest_tiles, h2, n_slots):
    n_steps = dest_tiles.shape[0]
    grid_spec = pltpu.PrefetchScalarGridSpec(
        num_scalar_prefetch=2,
        grid=(n_steps,),
        in_specs=[pl.BlockSpec(memory_space=pl.ANY), pl.BlockSpec(memory_space=pl.ANY)],
        out_specs=pl.BlockSpec(memory_space=pl.ANY),
        scratch_shapes=[
            pltpu.SMEM((2, TOP_K * MOE_TM), I32),
            pltpu.VMEM((2, MOE_TM) + ROW_TILE, F32),
            pltpu.VMEM((MOE_BLOCK,) + ROW_TILE, F32),
            pltpu.SemaphoreType.DMA((2,)),
            pltpu.SemaphoreType.DMA((2,)),
            pltpu.SemaphoreType.DMA((2,)),
            pltpu.SemaphoreType.DMA(()),
        ],
    )
    return pl.pallas_call(
        _dispatch_kernel,
        grid_spec=grid_spec,
        out_shape=jax.ShapeDtypeStruct((n_slots,) + ROW_TILE, F32),
        compiler_params=_params("arbitrary"),
        name="dispatch",
    )(counts, pad_starts, dest_tiles, h2)


def _expert_kernel(be_ref, x_ref, wg_ref, wu_ref, wd_ref, y_ref, wg_scr, wu_scr, wd_scr):
    i = pl.program_id(0)

    @pl.when((i == 0) | (be_ref[i] != be_ref[jnp.maximum(i - 1, 0)]))
    def _():
        wg_scr[...] = wg_ref[0].astype(BF16)
        wu_scr[...] = wu_ref[0].astype(BF16)
        wd_scr[...] = wd_ref[0].astype(BF16)

    xb = _load_token_tiles(x_ref, 0, MOE_BLOCK).astype(BF16)
    gate = _dot(xb, wg_scr[...])
    up = _dot(xb, wu_scr[...])
    hid = (gate * jax.nn.sigmoid(gate) * up).astype(BF16)
    _store_token_tiles(y_ref, _dot(hid, wd_scr[...]))


def _expert_call(blk_expert, xs, wg, wu, wd):
    n_blk = blk_expert.shape[0]
    d = wg.shape[1]
    blk = (MOE_BLOCK * ROW_TILE[0], LANES)
    grid_spec = pltpu.PrefetchScalarGridSpec(
        num_scalar_prefetch=1,
        grid=(n_blk,),
        in_specs=[
            pl.BlockSpec(blk, lambda i, be: (i, 0)),
            pl.BlockSpec((1, d, EXPERT_FF), lambda i, be: (be[i], 0, 0)),
            pl.BlockSpec((1, d, EXPERT_FF), lambda i, be: (be[i], 0, 0)),
            pl.BlockSpec((1, EXPERT_FF, d), lambda i, be: (be[i], 0, 0)),
        ],
        out_specs=pl.BlockSpec(blk, lambda i, be: (i, 0)),
        scratch_shapes=[
            pltpu.VMEM((d, EXPERT_FF), BF16),
            pltpu.VMEM((d, EXPERT_FF), BF16),
            pltpu.VMEM((EXPERT_FF, d), BF16),
        ],
    )
    return pl.pallas_call(
        _expert_kernel,
        grid_spec=grid_spec,
        out_shape=jax.ShapeDtypeStruct(xs.shape, F32),
        compiler_params=_params("arbitrary"),
        name="experts",
    )(blk_expert, xs, wg, wu, wd)


def _combine_kernel(dest_hbm, y_hbm, x1_ref, route_ref, mod_ref, o_ref, idx_smem, ybuf, isem, dsem):
    tm = MOE_TM
    nd = TOP_K * tm
    i = pl.program_id(0)
    n = pl.num_programs(0)
    slot = i % 2
    nxt = 1 - slot

    def idx_copy(step, sl):
        return pltpu.make_async_copy(dest_hbm.at[step], idx_smem.at[sl], isem.at[sl])

    ns = ROW_TILE[0]

    def step_rows(sl):
        return pltpu.make_async_copy(y_hbm.at[pl.ds(0, nd * ns)], ybuf.at[sl], dsem.at[sl])

    def issue_rows(par):
        for r in range(nd):
            src = y_hbm.at[pl.ds(pl.multiple_of(idx_smem[par, r] * ns, ns), ns)]
            pltpu.make_async_copy(src, ybuf.at[par, pl.ds(r * ns, ns)], dsem.at[par]).start()

    @pl.when(i == 0)
    def _():
        idx_copy(0, 0).start()
        idx_copy(0, 0).wait()
        issue_rows(0)

        @pl.when(n > 1)
        def _():
            idx_copy(1, 1).start()

    @pl.when(i + 1 < n)
    def _():
        idx_copy(i + 1, nxt).wait()
        _for_static_slot(nxt, issue_rows)

    @pl.when(i + 2 < n)
    def _():
        idx_copy(i + 2, slot).start()

    step_rows(slot).wait()
    route = route_ref[...]
    w1 = route[:, 2:3]
    w2 = route[:, 3:4]
    y = (w1 * _load_token_tiles(ybuf, 0, tm, lead=(slot,))
         + w2 * _load_token_tiles(ybuf, tm, tm, lead=(slot,)))
    o_ref[...] = x1_ref[...] + mod_ref[0, 5:6, :] * y


def _combine_call(dest_tiles, y_slots, x1, route, mod6, s):
    t, d = x1.shape
    tm = MOE_TM
    per_b = s // tm
    return pl.pallas_call(
        _combine_kernel,
        grid=(t // tm,),
        in_specs=[
            pl.BlockSpec(memory_space=pl.ANY),
            pl.BlockSpec(memory_space=pl.ANY),
            pl.BlockSpec((tm, d), lambda i: (i, 0)),
            pl.BlockSpec((tm, LANES), lambda i: (i, 0)),
            pl.BlockSpec((1, 6, d), lambda i: (i // per_b, 0, 0)),
        ],
        out_specs=pl.BlockSpec((tm, d), lambda i: (i, 0)),
        out_shape=jax.ShapeDtypeStruct((t, d), F32),
        scratch_shapes=[
            pltpu.SMEM((2, TOP_K * tm), I32),
            pltpu.VMEM((2, TOP_K * tm * ROW_TILE[0], LANES), F32),
            pltpu.SemaphoreType.DMA((2,)),
            pltpu.SemaphoreType.DMA((2,)),
        ],
        compiler_params=_params("arbitrary"),
        name="combine",
    )(dest_tiles, y_slots, x1, route, mod6)


def _rope_tables(pos):
    inv = ROPE_THETA ** (-jnp.arange(0, HEAD_DIM, 2, dtype=F32) / HEAD_DIM)
    ang = pos.astype(F32)[..., None] * inv
    return jnp.cos(ang), jnp.sin(ang)


def _rope_tiles(pos):
    cos, sin = _rope_tables(pos)
    return (jnp.concatenate([cos, cos, cos, cos], axis=-1),
            jnp.concatenate([-sin, sin, -sin, sin], axis=-1))


def _selection_constants(s):
    n_c = s // CMP_STRIDE
    n_s = s // SEL_BLOCK
    cs = np.arange(n_c) * CMP_STRIDE
    ss = np.arange(n_s) * SEL_BLOCK
    ov = np.clip(np.minimum(cs[:, None] + CMP_BLOCK, ss[None, :] + SEL_BLOCK)
                 - np.maximum(cs[:, None], ss[None, :]), 0, None).astype(np.float32) / CMP_BLOCK
    kq = np.arange(NSA_TQ)[:, None] - np.arange(NSA_TQ)[None, :]
    wbias = np.stack([np.where(kq >= 0, 0.0, MASKED), np.where(kq <= 0, 0.0, MASKED)]).astype(np.float32)
    return jnp.asarray(ov.T, BF16), jnp.asarray(wbias)


def _mixer_and_router(x, c, positions, w_ada, b_ada, norm1_g, norm2_g, w_in, nsa_q_norm, nsa_k_norm,
                      cmp_pe_k, cmp_w1_k, cmp_w2_k, cmp_pe_v, cmp_w1_v, cmp_w2_v, dil_q_norm,
                      dil_k_norm, w_up_a, w_up_b, w_out, w_group, b_group, w_router, b_router):
    b, s, d = x.shape
    scale = HEAD_DIM ** -0.5
    mod6 = _mod_call(c, w_ada, b_ada).reshape(b, 6, d)

    c1 = A_Q
    c2 = c1 + 6 * A_KV
    c3 = c2 + 3 * NSA_HEADS
    c4 = c3 + 3 * DIL_W
    w_perm = jnp.concatenate([
        w_in[:, :c2], w_in[:, c3:c4],
        jnp.pad(w_in[:, c2:c3], ((0, 0), (0, LANES - 3 * NSA_HEADS))), w_in[:, c4:]], axis=1).astype(BF16)
    two = lambda g: jnp.concatenate([g, g]).astype(F32)
    gains = jnp.stack([two(nsa_q_norm) * (scale * LOG2E), two(nsa_k_norm), two(dil_q_norm) * scale,
                       two(dil_k_norm)])
    cos_t, sin_t = _rope_tables(positions)
    qext, kvc, ksl, vsl, kwp, vwp, dil0, dil1, dil2, ga, gm = _proj_call(
        x, mod6, norm1_g.reshape(1, d), w_perm, gains, cos_t, sin_t)

    n_chunk = s // CMP_STRIDE
    half = CMP_STRIDE * HEAD_DIM
    chunks = kvc.reshape(b, s, 4, HEAD_DIM).transpose(0, 2, 1, 3).reshape(b, 4, n_chunk, half)
    pe = jnp.stack([cmp_pe_k.reshape(2, half), cmp_pe_v.reshape(2, half)])
    w1 = jnp.stack([cmp_w1_k, cmp_w1_v]).astype(BF16)
    zeros = jnp.zeros((CMP_HIDDEN, HEAD_DIM), F32)
    ext = lambda w: jnp.stack([jnp.concatenate([w, zeros], 1), jnp.concatenate([zeros, w], 1)])
    w2ext = jnp.stack([ext(cmp_w2_k), ext(cmp_w2_v)]).astype(BF16)
    cmp_pos = jnp.pad(positions[:, CMP_BLOCK - 1::CMP_STRIDE], ((0, 0), (0, 1)))
    ccos, csin = _rope_tiles(cmp_pos)
    kc, vc = _cmp_call(chunks, pe, w1, w2ext, two(nsa_k_norm).reshape(1, LANES), ccos, csin)

    ov_t, wbias = _selection_constants(s)
    o_a = _nsa_call(qext, kc, vc, ksl, vsl, kwp, vwp, ga, ov_t, wbias)

    ods, lses = zip(*[_dil_call(qkv.reshape(b, 3, s, LANES), dl)
                      for qkv, (_, dl) in zip((dil0, dil1, dil2), DIL_PAIRS)])

    t = b * s
    w_r = jnp.concatenate([w_group, w_router.transpose(1, 0, 2).reshape(d, N_EXPERTS)], axis=1)
    w_r = jnp.pad(w_r, ((0, 0), (0, LANES - w_r.shape[1])))
    rcat = jnp.concatenate(_split(w_r), axis=1)
    rb = jnp.pad(jnp.concatenate([b_group, b_router.reshape(-1)]), (0, LANES - N_GROUPS - N_EXPERTS))
    return _merge_call(
        x.reshape(t, d), o_a.reshape(t, A_Q), ods, lses, gm.reshape(t, 2 * d), mod6,
        w_up_a.astype(BF16), w_up_b.astype(BF16), w_out.astype(BF16), norm2_g.reshape(1, d),
        rcat, rb.reshape(1, LANES).astype(F32), s), mod6


def _moe(x1, h2, route, mod6, w_e_gate, w_e_up, w_e_down, s):
    t, _ = x1.shape
    rank, counts = _rank_call(route)
    expert = route[:, 0:TOP_K].astype(I32)
    counts = counts[0, :N_EXPERTS].astype(I32)
    padded = (counts + MOE_BLOCK - 1) // MOE_BLOCK * MOE_BLOCK
    pad_ends = jnp.cumsum(padded)
    pad_starts = pad_ends - padded
    dest = pad_starts[expert] + rank[:, 0:TOP_K].astype(I32)
    n_slots = t * TOP_K + N_EXPERTS * MOE_BLOCK
    n_blk = n_slots // MOE_BLOCK
    blk_start = jnp.arange(n_blk, dtype=I32) * MOE_BLOCK
    blk_expert = jnp.minimum(jnp.sum((pad_ends[None, :] <= blk_start[:, None]).astype(I32), axis=1),
                             N_EXPERTS - 1)
    tm = MOE_TM
    dest_tiles = dest.reshape(t // tm, tm, TOP_K).transpose(0, 2, 1).reshape(t // tm, TOP_K * tm)
    xs = _dispatch_call(counts, pad_starts.astype(I32), dest_tiles, h2.reshape((t,) + ROW_TILE), n_slots)
    y_slots = _expert_call(blk_expert, xs.reshape(n_slots * ROW_TILE[0], LANES), w_e_gate, w_e_up, w_e_down)
    return _combine_call(dest_tiles, y_slots, x1, route, mod6, s)


def kernel(x, c, positions, w_ada, b_ada, norm1_g, norm2_g, w_in, nsa_q_norm, nsa_k_norm, cmp_pe_k,
           cmp_w1_k, cmp_w2_k, cmp_pe_v, cmp_w1_v, cmp_w2_v, dil_q_norm, dil_k_norm, w_up_a, w_up_b,
           w_out, w_group, b_group, w_router, b_router, w_e_gate, w_e_up, w_e_down):
    b, s, d = x.shape
    assert w_ada.shape[0] == 1 and d == D_MODEL and s % NSA_TK == 0
    (x1, h2, route), mod6 = _mixer_and_router(
        x, c, positions, w_ada[0], b_ada[0], norm1_g[0], norm2_g[0], w_in[0], nsa_q_norm[0],
        nsa_k_norm[0], cmp_pe_k[0], cmp_w1_k[0], cmp_w2_k[0], cmp_pe_v[0], cmp_w1_v[0], cmp_w2_v[0],
        dil_q_norm[0], dil_k_norm[0], w_up_a[0], w_up_b[0], w_out[0], w_group[0], b_group[0],
        w_router[0], b_router[0])
    out = _moe(x1, h2, route, mod6, w_e_gate[0], w_e_up[0], w_e_down[0], s)
    return out.reshape(b, s, d)
```

```python
import functools

import jax
import jax.numpy as jnp
import numpy as np
from jax import lax
from jax.experimental import pallas as pl
from jax.experimental.pallas import tpu as pltpu

F32 = jnp.float32
BF16 = jnp.bfloat16
I32 = jnp.int32

D_MODEL = 1024
HEAD_DIM = 64
LANES = 128
ROW_TILE = (D_MODEL // LANES, LANES)
ROPE_THETA = 10000.0
EPS = 1e-6
LOG2E = 1.4426950408889634
NEG_INF = -1e30
FORCE_SCORE = 1e9
MASKED = -1e30
PICKED = -3e38

NSA_HEADS = 8
NSA_KV_HEADS = 2
NSA_GROUP = 4
CMP_BLOCK = 32
CMP_STRIDE = 16
CMP_HIDDEN = 256
SEL_BLOCK = 64
N_SEL = 8
N_LOCAL_SEL = 2
WINDOW = 512
DIL_PAIRS = ((128, 1), (512, 4), (2048, 16))
DIL_GROUPS = 3
A_Q = NSA_HEADS * HEAD_DIM
A_KV = NSA_KV_HEADS * HEAD_DIM
DIL_W = 2 * DIL_GROUPS * HEAD_DIM
N_GROUPS = 4
EXPERTS_PER_GROUP = 8
N_EXPERTS = 32
TOP_K = 2
EXPERT_FF = 512
MOE_BLOCK = 512

VMEM_LIMIT = 56 * 1024 * 1024

T_QA = 0
T_KVC = 4
T_KVA = 6
T_DIL = 10
T_GA = 19
T_GM = 20
N_TILES = 36

PROJ_TM = 512
NSA_TQ = 128
NSA_TK = 512
NSA_NQ = 2
DIL_T = 128
DIL_UNROLL = 4
MERGE_TM = 512
RANK_TM = 512
MOE_TM = 256


def _dot(a, b):
    return jnp.dot(a, b, preferred_element_type=F32)


def _dot_nt(a, b):
    return lax.dot_general(a, b, (((1,), (1,)), ((), ())), preferred_element_type=F32)


def _dot_tn(a, b):
    return lax.dot_general(a, b, (((0,), (0,)), ((), ())), preferred_element_type=F32)


def _split(a):
    hi = a.astype(BF16)
    lo = (a - hi.astype(F32)).astype(BF16)
    return hi, lo


def _load_token_tiles(ref, first, n, lead=()):
    ns = ROW_TILE[0]
    return jnp.concatenate(
        [ref[lead + (pl.ds(first * ns + s, n, stride=ns), slice(None))] for s in range(ns)], axis=1)


def _store_token_tiles(ref, rows, first=0):
    ns = ROW_TILE[0]
    for s in range(ns):
        ref[pl.ds(first * ns + s, rows.shape[0], stride=ns), :] = rows[:, s * LANES:(s + 1) * LANES]


def _params(*sem):
    return pltpu.CompilerParams(dimension_semantics=sem, vmem_limit_bytes=VMEM_LIMIT)


def _mod_kernel(c_ref, whi_ref, wlo_ref, b_ref, o_ref):
    c = c_ref[...]
    ca = c * jax.nn.sigmoid(c)
    hi, lo = _split(ca)
    whi = whi_ref[...]
    o_ref[...] = _dot(hi, whi) + _dot(lo, whi) + _dot(hi, wlo_ref[...]) + b_ref[...]


def _mod_call(c, w_ada, b_ada):
    b, d = c.shape
    n = w_ada.shape[1]
    whi, wlo = _split(w_ada)
    tn = 1024
    return pl.pallas_call(
        _mod_kernel,
        grid=(n // tn,),
        in_specs=[
            pl.BlockSpec((b, d), lambda j: (0, 0)),
            pl.BlockSpec((d, tn), lambda j: (0, j)),
            pl.BlockSpec((d, tn), lambda j: (0, j)),
            pl.BlockSpec((1, tn), lambda j: (0, j)),
        ],
        out_specs=pl.BlockSpec((b, tn), lambda j: (0, j)),
        out_shape=jax.ShapeDtypeStruct((b, n), F32),
        compiler_params=_params("parallel"),
        name="mod",
    )(c, whi, wlo, b_ada.reshape(1, n))


def _norm_rope(a, gain, cos, sin_signed, lo, first):
    sq = a * a
    s0 = jnp.sum(jnp.where(lo, sq, 0.0), axis=-1, keepdims=True)
    s1 = jnp.sum(jnp.where(lo, 0.0, sq), axis=-1, keepdims=True)
    r = jnp.where(lo, lax.rsqrt(s0 * (1.0 / HEAD_DIM) + EPS), lax.rsqrt(s1 * (1.0 / HEAD_DIM) + EPS))
    y = a * r * gain
    rot = jnp.where(first, pltpu.roll(y, 96, 1), pltpu.roll(y, 32, 1))
    return y * cos + rot * sin_signed


def _proj_kernel(x_ref, mod_ref, g1_ref, w_ref, gains_ref, cos_ref, sin_ref, kw0_ref, vw0_ref,
                 qext_ref, kvc_ref, ksl_ref, vsl_ref, kwp_ref, vwp_ref, dil0_ref, dil1_ref, dil2_ref,
                 ga_ref, gm_ref, perm_scr):
    del kw0_ref, vw0_ref
    tm = x_ref.shape[1]
    x = x_ref[0]
    ms = jnp.mean(x * x, axis=-1, keepdims=True)
    y = x * lax.rsqrt(ms + EPS) * g1_ref[...]
    sh1 = mod_ref[0, 0:1, :]
    sc1 = mod_ref[0, 1:2, :]
    h = (y * (1.0 + sc1) + sh1).astype(BF16)
    lane = lax.broadcasted_iota(I32, (tm, LANES), 1)
    lo = lane < HEAD_DIM
    first = (lane & (HEAD_DIM - 1)) < (HEAD_DIM // 2)
    cos = jnp.concatenate([cos_ref[0]] * 4, axis=1)
    sin = jnp.concatenate([sin_ref[0]] * 4, axis=1)
    sin = jnp.where(first, -sin, sin)
    nr = functools.partial(_norm_rope, cos=cos, sin_signed=sin, lo=lo, first=first)

    for c in range(N_TILES // 2):
        acc = _dot(h, w_ref[:, c * 2 * LANES:(c + 1) * 2 * LANES])
        for half in range(2):
            t = 2 * c + half
            a = acc[:, half * LANES:(half + 1) * LANES]
            if t < T_KVC:
                yq = nr(a, gains_ref[0:1, :])
                rq = pltpu.roll(yq, HEAD_DIM, 1)
                if t // 2 == 0:
                    e0 = jnp.where(lo, yq, 0.0)
                    e1 = jnp.where(lo, rq, 0.0)
                else:
                    e0 = jnp.where(lo, 0.0, rq)
                    e1 = jnp.where(lo, 0.0, yq)
                qext_ref[0, 2 * t] = e0.astype(BF16)
                qext_ref[0, 2 * t + 1] = e1.astype(BF16)
            elif t < T_KVA:
                kvc_ref[0, :, (t - T_KVC) * LANES:(t - T_KVC + 1) * LANES] = a
            elif t < T_DIL:
                j = t - T_KVA
                k_out, v_out = (ksl_ref, vsl_ref) if j < 2 else (kwp_ref, vwp_ref)
                if j % 2 == 0:
                    k_out[0] = nr(a, gains_ref[1:2, :]).astype(BF16)
                else:
                    v_out[0, 0] = jnp.where(lo, a, 1.0).astype(BF16)
                    v_out[0, 1] = jnp.where(lo, pltpu.roll(a, HEAD_DIM, 1), 1.0).astype(BF16)
            elif t < T_GA:
                j = t - T_DIL
                if j < 3:
                    v = nr(a, gains_ref[2:3, :])
                elif j < 6:
                    v = nr(a, gains_ref[3:4, :])
                else:
                    v = a
                kind, gi = divmod(j, DIL_GROUPS)
                dl = DIL_PAIRS[gi][1]
                if dl == 1:
                    dil0_ref[0, kind] = v.astype(BF16)
                else:
                    out = (dil0_ref, dil1_ref, dil2_ref)[gi]
                    perm_scr[...] = v
                    for r in range(dl):
                        out[0, kind, r] = perm_scr[pl.ds(r, tm // dl, stride=dl), :].astype(BF16)
            elif t < T_GM:
                ga_ref[0] = jax.nn.sigmoid(a)
            else:
                j = t - T_GM
                gm_ref[0, :, j * LANES:(j + 1) * LANES] = jax.nn.sigmoid(a).astype(BF16)


def _proj_call(x, mod6, g1, w_perm, gains, cos_t, sin_t):
    b, s, d = x.shape
    tm = PROJ_TM
    n = N_TILES * LANES
    row = lambda bi, i: (bi, i, 0)
    heads = lambda bi, i: (bi, 0, i, 0)
    pad = WINDOW // tm
    in_specs = [
        pl.BlockSpec((1, tm, d), row),
        pl.BlockSpec((1, 6, d), lambda bi, i: (bi, 0, 0)),
        pl.BlockSpec((1, d), lambda bi, i: (0, 0)),
        pl.BlockSpec((d, n), lambda bi, i: (0, 0)),
        pl.BlockSpec((4, LANES), lambda bi, i: (0, 0)),
        pl.BlockSpec((1, tm, HEAD_DIM // 2), row),
        pl.BlockSpec((1, tm, HEAD_DIM // 2), row),
        pl.BlockSpec(memory_space=pl.ANY),
        pl.BlockSpec(memory_space=pl.ANY),
    ]
    out_specs = [
        pl.BlockSpec((1, NSA_HEADS, tm, LANES), heads),
        pl.BlockSpec((1, tm, 2 * LANES), row),
        pl.BlockSpec((1, tm, LANES), row),
        pl.BlockSpec((1, NSA_KV_HEADS, tm, LANES), heads),
        pl.BlockSpec((1, tm, LANES), lambda bi, i: (bi, i + pad, 0)),
        pl.BlockSpec((1, NSA_KV_HEADS, tm, LANES), lambda bi, i: (bi, 0, i + pad, 0)),
        pl.BlockSpec((1, 3, tm, LANES), heads),
        *[pl.BlockSpec((1, 3, dl, tm // dl, LANES), lambda bi, i: (bi, 0, 0, i, 0)) for _, dl in DIL_PAIRS[1:]],
        pl.BlockSpec((1, tm, LANES), row),
        pl.BlockSpec((1, tm, 2 * d), row),
    ]
    out_shape = [
        jax.ShapeDtypeStruct((b, NSA_HEADS, s, LANES), BF16),
        jax.ShapeDtypeStruct((b, s, 2 * LANES), F32),
        jax.ShapeDtypeStruct((b, s, LANES), BF16),
        jax.ShapeDtypeStruct((b, NSA_KV_HEADS, s, LANES), BF16),
        jax.ShapeDtypeStruct((b, s + WINDOW, LANES), BF16),
        jax.ShapeDtypeStruct((b, NSA_KV_HEADS, s + WINDOW, LANES), BF16),
        jax.ShapeDtypeStruct((b, 3, s, LANES), BF16),
        *[jax.ShapeDtypeStruct((b, 3, dl, s // dl, LANES), BF16) for _, dl in DIL_PAIRS[1:]],
        jax.ShapeDtypeStruct((b, s, LANES), F32),
        jax.ShapeDtypeStruct((b, s, 2 * d), BF16),
    ]
    kw0 = jnp.zeros(out_shape[4].shape, BF16)
    vw0 = jnp.zeros(out_shape[5].shape, BF16)
    return pl.pallas_call(
        _proj_kernel,
        grid=(b, s // tm),
        in_specs=in_specs,
        out_specs=out_specs,
        out_shape=out_shape,
        scratch_shapes=[pltpu.VMEM((tm, LANES), F32)],
        input_output_aliases={7: 4, 8: 5},
        compiler_params=_params("parallel", "parallel"),
        name="proj",
    )(x, mod6, g1, w_perm, gains, cos_t, sin_t, kw0, vw0)


def _cmp_kernel(c_ref, pe_ref, w1_ref, w2_ref, gain_ref, cos_ref, sin_ref, kc_ref, vc_ref):
    nb = c_ref.shape[2]
    half = c_ref.shape[3]
    lane = lax.broadcasted_iota(I32, (nb, LANES), 1)
    lo = lane < HEAD_DIM
    first = (lane & (HEAD_DIM - 1)) < (HEAD_DIM // 2)
    for kind in range(2):
        out = jnp.zeros((nb, LANES), F32)
        for hd in range(2):
            c = c_ref[0, kind * 2 + hd]
            top = _dot((c + pe_ref[kind, 0:1, :]).astype(BF16), w1_ref[kind, 0:half, :])
            bot = _dot((c + pe_ref[kind, 1:2, :]).astype(BF16), w1_ref[kind, half:2 * half, :])
            hid = top + pltpu.roll(bot, nb - 1, 0)
            hid = hid * jax.nn.sigmoid(hid)
            out = out + _dot(hid.astype(BF16), w2_ref[kind, hd])
        if kind == 0:
            out = _norm_rope(out, gain_ref[...], cos_ref[0], sin_ref[0], lo, first)
            kc_ref[0] = out.astype(BF16)
        else:
            vc_ref[0, 0] = jnp.where(lo, out, 0.0).astype(BF16)
            vc_ref[0, 1] = jnp.where(lo, pltpu.roll(out, HEAD_DIM, 1), 0.0).astype(BF16)


def _cmp_call(chunks, pe, w1, w2ext, gain_k, ccos, csin):
    b, _, nb, half = chunks.shape
    return pl.pallas_call(
        _cmp_kernel,
        grid=(b,),
        in_specs=[
            pl.BlockSpec((1, 4, nb, half), lambda bi: (bi, 0, 0, 0)),
            pl.BlockSpec((2, 2, half), lambda bi: (0, 0, 0)),
            pl.BlockSpec((2, 2 * half, CMP_HIDDEN), lambda bi: (0, 0, 0)),
            pl.BlockSpec((2, 2, CMP_HIDDEN, LANES), lambda bi: (0, 0, 0, 0)),
            pl.BlockSpec((1, LANES), lambda bi: (0, 0)),
            pl.BlockSpec((1, nb, LANES), lambda bi: (bi, 0, 0)),
            pl.BlockSpec((1, nb, LANES), lambda bi: (bi, 0, 0)),
        ],
        out_specs=[
            pl.BlockSpec((1, nb, LANES), lambda bi: (bi, 0, 0)),
            pl.BlockSpec((1, NSA_KV_HEADS, nb, LANES), lambda bi: (bi, 0, 0, 0)),
        ],
        out_shape=[
            jax.ShapeDtypeStruct((b, nb, LANES), BF16),
            jax.ShapeDtypeStruct((b, NSA_KV_HEADS, nb, LANES), BF16),
        ],
        compiler_params=_params("parallel"),
        name="cmp",
    )(chunks, pe, w1, w2ext, gain_k, ccos, csin)


def _softmax_cols(s, exp=jnp.exp):
    p = exp(s - jnp.max(s, axis=0, keepdims=True))
    return p, jnp.sum(p, axis=0, keepdims=True)


def _block_max(s, offs, bs):
    m = None
    for j, off in enumerate(offs):
        mj = jnp.max(s[j * bs:(j + 1) * bs], axis=0, keepdims=True) + off
        m = mj if m is None else jnp.maximum(m, mj)
    return m


def _block_exp2(s, offs, bs, m):
    return jnp.concatenate(
        [jnp.exp2(s[j * bs:(j + 1) * bs] - (m - off)) for j, off in enumerate(offs)], axis=0)


def _nsa_kernel(q_ref, kc_ref, vc_ref, ksl_ref, vsl_ref, kw_ref, vw_ref, ga_ref, ov_ref, wb_ref,
                o_ref, m_scr, acc_scr, *s_scrs):
    tq = NSA_TQ
    tk = NSA_TK
    hd = HEAD_DIM
    cols = NSA_GROUP * tq
    tiles = range(NSA_NQ)
    s_buf = lambda u, c: s_scrs[2 * u + c % 2]
    n_blk = ov_ref.shape[0]
    n_chunks = ksl_ref.shape[1] // tk
    bpc = tk // SEL_BLOCK
    kh = pl.program_id(1)
    base = pl.multiple_of(pl.program_id(2) * (NSA_NQ * tq), NSA_NQ * tq)
    t0 = [base + u * tq for u in tiles]
    q4 = [q_ref[0, :, u * tq:(u + 1) * tq, :].reshape(cols, LANES) for u in tiles]
    lane_q = lax.broadcasted_iota(I32, (1, tq), 1)
    lane_q4 = lax.broadcasted_iota(I32, (1, cols), 1) & (tq - 1)
    per_head = lambda b: jnp.concatenate([b] * NSA_GROUP, axis=1)
    older_edge = wb_ref[0]
    causal_edge = wb_ref[1]
    nwin = WINDOW + tq
    n_wb = nwin // tq
    blk = lax.broadcasted_iota(I32, (n_blk, 1), 0)
    blk_f = blk.astype(F32)
    ov = ov_ref[...]

    s_cmp = [_dot_nt(kc_ref[0], q4[u]) for u in tiles]
    s_win = [_dot_nt(kw_ref[0, pl.ds(t0[u], nwin), :], q4[u]) for u in tiles]
    for u in tiles:
        s_buf(u, 0)[...] = _dot_nt(ksl_ref[0, 0:tk, :], q4[u])

    o_cmp, sel_bias = [], []
    for u in tiles:
        nb = s_cmp[u].shape[0]
        cmp_end = lax.broadcasted_iota(I32, (nb, 1), 0) * CMP_STRIDE + (CMP_BLOCK - 1)
        valid = cmp_end <= t0[u] + lane_q4
        e, den = _softmax_cols(jnp.where(valid, s_cmp[u], NEG_INF), jnp.exp2)
        p = jnp.where(valid, e / den, 0.0)
        o_cmp.append(_dot_tn(vc_ref[0, 0], p.astype(BF16))[0:hd])
        psum = p[:, 0:tq] + p[:, tq:2 * tq] + p[:, 2 * tq:3 * tq] + p[:, 3 * tq:4 * tq]
        p_hi, p_lo = _split(psum)
        imp = _dot(ov, p_hi) + _dot(ov, p_lo)
        rel = ((t0[u] + lane_q) >> 6) - blk
        forced = (blk == 0) | ((rel >= 0) & (rel < N_LOCAL_SEL))
        score = jnp.where(rel < 0, NEG_INF, jnp.where(forced, FORCE_SCORE, imp))
        bias = jnp.full((n_blk, tq), MASKED, F32)
        for _ in range(N_SEL):
            best = jnp.max(score, axis=0, keepdims=True)
            first = jnp.min(jnp.where(score == best, blk_f, float(n_blk)), axis=0, keepdims=True)
            pick = blk_f == first
            bias = jnp.where(pick, 0.0, bias)
            score = jnp.where(pick, PICKED, score)
        sel_bias.append(per_head(jnp.where(rel < 0, MASKED, bias)))

    o_win = []
    for u in tiles:
        sw = jnp.concatenate([s_win[u][0:tq] + per_head(older_edge), s_win[u][tq:nwin - tq],
                              s_win[u][nwin - tq:nwin] + per_head(causal_edge)], axis=0)
        w_offs = [jnp.where(t0[u] - WINDOW + j * tq >= 0, 0.0, MASKED) for j in range(n_wb)]
        pw = _block_exp2(sw, w_offs, tq, _block_max(sw, w_offs, tq))
        ow = _dot_tn(vw_ref[0, 0, pl.ds(t0[u], nwin), :], pw.astype(BF16))
        o_win.append(ow[0:hd] / ow[hd:hd + 1])

    diag = base // tk

    def chunk(c):
        if c + 1 < n_chunks:
            for u in tiles:
                s_buf(u, c + 1)[...] = _dot_nt(ksl_ref[0, (c + 1) * tk:(c + 2) * tk, :], q4[u])
        vb = vsl_ref[0, 0, c * tk:(c + 1) * tk, :]
        for u in tiles:
            own = pl.ds(pl.multiple_of(jnp.clip(t0[u] - c * tk, 0, tk - tq), tq), tq)
            buf = s_buf(u, c)
            buf[own, :] = buf[own, :] + per_head(jnp.where(c == diag, causal_edge, 0.0))
            sc = buf[...]
            offs = [sel_bias[u][c * bpc + j:c * bpc + j + 1] for j in range(bpc)]
            m_new = _block_max(sc, offs, SEL_BLOCK)
            if c == 0:
                acc_scr[u] = _dot_tn(vb, _block_exp2(sc, offs, SEL_BLOCK, m_new).astype(BF16))
            else:
                m_old = m_scr[u]
                m_new = jnp.maximum(m_old, m_new)
                pe = _block_exp2(sc, offs, SEL_BLOCK, m_new)
                acc_scr[u] = jnp.exp2(m_old - m_new) * acc_scr[u] + _dot_tn(vb, pe.astype(BF16))
            m_scr[u] = m_new

    chunk(0)
    for c in range(1, n_chunks):
        pl.when(c <= diag)(functools.partial(chunk, c))

    is0 = kh == 0
    for u in tiles:
        rows = slice(u * tq, (u + 1) * tq)
        o_slc = acc_scr[u, 0:hd, :] / acc_scr[u, hd:hd + 1, :]
        gat = ga_ref[0, rows, :].T
        heads = []
        for g in range(NSA_GROUP):
            c = slice(g * tq, (g + 1) * tq)
            og = jnp.zeros((hd, tq), F32)
            for gi, ob in enumerate((o_cmp[u], o_slc, o_win[u])):
                c0 = gi * NSA_HEADS + g
                c1 = c0 + NSA_GROUP
                gate = jnp.where(is0, gat[c0:c0 + 1, :], gat[c1:c1 + 1, :])
                og = og + gate * ob[:, c]
            heads.append(og)
        for pair in range(2):
            tile = jnp.concatenate([heads[2 * pair], heads[2 * pair + 1]], axis=0)
            o_ref[0, rows, pair * LANES:(pair + 1) * LANES] = tile.T.astype(BF16)


def _nsa_call(qext, kc, vc, ksl, vsl, kwp, vwp, ga, ov_t, wbias):
    b, _, s, _ = qext.shape
    tq = NSA_TQ * NSA_NQ
    nb = kc.shape[1]
    cols = NSA_GROUP * NSA_TQ
    assert NSA_TK % tq == 0
    shared = lambda rows: pl.BlockSpec((1, rows, LANES), lambda bi, k, i: (bi, 0, 0))
    per_kv = lambda rows: pl.BlockSpec((1, 1, rows, LANES), lambda bi, k, i: (bi, k, 0, 0))
    return pl.pallas_call(
        _nsa_kernel,
        grid=(b, NSA_KV_HEADS, s // tq),
        in_specs=[
            pl.BlockSpec((1, NSA_GROUP, tq, LANES), lambda bi, k, i: (bi, k, i, 0)),
            shared(nb), per_kv(nb),
            shared(s), per_kv(s),
            shared(s + WINDOW), per_kv(s + WINDOW),
            pl.BlockSpec((1, tq, LANES), lambda bi, k, i: (bi, i, 0)),
            pl.BlockSpec(ov_t.shape, lambda bi, k, i: (0, 0)),
            pl.BlockSpec(wbias.shape, lambda bi, k, i: (0, 0, 0)),
        ],
        out_specs=pl.BlockSpec((1, tq, 2 * LANES), lambda bi, k, i: (bi, i, k)),
        out_shape=jax.ShapeDtypeStruct((b, s, A_Q), BF16),
        scratch_shapes=[
            pltpu.VMEM((NSA_NQ, 1, cols), F32),
            pltpu.VMEM((NSA_NQ, LANES, cols), F32),
            *[pltpu.VMEM((NSA_TK, cols), F32)] * (2 * NSA_NQ),
        ],
        compiler_params=_params("parallel", "parallel", "parallel"),
        name="nsa",
    )(qext, kc, vc, ksl, vsl, kwp, vwp, ga, ov_t, wbias)


def _dil_kernel(q_ref, k_ref, v_ref, o_ref, lse_ref, *, seg_tiles):
    t = DIL_T
    lo = lax.broadcasted_iota(I32, (t, LANES), 1) < HEAD_DIM
    kj = lax.broadcasted_iota(I32, (t, 2 * t), 0)
    qi = lax.broadcasted_iota(I32, (t, 2 * t), 1) & (t - 1)
    top = lax.broadcasted_iota(I32, (LANES, t), 0) < HEAD_DIM

    def scores(i):
        has_prev = (i & (seg_tiles - 1)) != 0
        cs = pl.multiple_of(i * t, t)
        ps = pl.multiple_of(jnp.maximum(i - 1, 0) * t, t)
        q = q_ref[0, 0, pl.ds(cs, t), :]
        zero = jnp.zeros_like(q)
        q2 = jnp.concatenate([jnp.where(lo, q, zero), jnp.where(lo, zero, q)], axis=0)
        sp = _dot_nt(k_ref[0, 0, pl.ds(ps, t), :], q2)
        sc = _dot_nt(k_ref[0, 0, pl.ds(cs, t), :], q2)
        return has_prev, ps, cs, sp, sc

    def attend(has_prev, ps, cs, sp, sc):
        sp = jnp.where((kj >= qi) & has_prev, sp, MASKED)
        sc = jnp.where(qi >= kj, sc, MASKED)
        m = jnp.maximum(jnp.max(sp, axis=0, keepdims=True), jnp.max(sc, axis=0, keepdims=True))
        pp = jnp.exp(sp - m)
        pc = jnp.exp(sc - m)
        l = jnp.sum(pp, axis=0, keepdims=True) + jnp.sum(pc, axis=0, keepdims=True)
        o = (_dot_tn(v_ref[0, 0, pl.ds(ps, t), :], pp.astype(BF16))
             + _dot_tn(v_ref[0, 0, pl.ds(cs, t), :], pc.astype(BF16))) / l
        lse = m + jnp.log(l)
        o_ref[0, pl.ds(cs, t), :] = jnp.where(top, o[:, 0:t], o[:, t:2 * t]).T
        lse_ref[0, pl.ds(cs, t), :] = jnp.where(top, lse[:, 0:t], lse[:, t:2 * t]).T

    def tiles(j, carry):
        group = [scores(j * DIL_UNROLL + u) for u in range(DIL_UNROLL)]
        for args in group:
            attend(*args)
        return carry

    lax.fori_loop(0, q_ref.shape[2] // (t * DIL_UNROLL), tiles, 0)


def _dil_call(qkv, dl):
    b, _, s, _ = qkv.shape
    kind = lambda j: pl.BlockSpec((1, 1, s, LANES), lambda bi: (bi, j, 0, 0))
    out = pl.BlockSpec((1, s, LANES), lambda bi: (bi, 0, 0))
    return pl.pallas_call(
        functools.partial(_dil_kernel, seg_tiles=s // dl // DIL_T),
        grid=(b,),
        in_specs=[kind(0), kind(1), kind(2)],
        out_specs=[out, out],
        out_shape=[jax.ShapeDtypeStruct((b, s, LANES), F32)] * 2,
        compiler_params=_params("parallel"),
        name=f"dil{dl}",
    )(qkv, qkv, qkv)


def _merge_kernel(x_ref, oa_ref, od0_ref, od1_ref, od2_ref, ls0_ref, ls1_ref, ls2_ref, gm_ref, mod_ref,
                  wa_ref, wb_ref, wo_ref, g2_ref, rcat_ref, rb_ref, x1_ref, h2_ref, route_ref, perm_scr):
    tm = x_ref.shape[0]
    d = x_ref.shape[1]
    hm = tm // 2
    for n, ref in enumerate((od1_ref, od2_ref, ls1_ref, ls2_ref)):
        dl = ref.shape[1]
        for r in range(dl):
            perm_scr[n, pl.ds(r, tm // dl, stride=dl), :] = ref[0, r]

    def up(rows):
        l0, l1, l2 = ls0_ref[rows], perm_scr[2, rows], perm_scr[3, rows]
        mx = jnp.maximum(jnp.maximum(l0, l1), l2)
        e0, e1, e2 = jnp.exp(l0 - mx), jnp.exp(l1 - mx), jnp.exp(l2 - mx)
        den = e0 + e1 + e2
        ob = jnp.concatenate([(od0_ref[rows] * (e0 / den)).astype(BF16),
                              (perm_scr[0, rows] * (e1 / den)).astype(BF16),
                              (perm_scr[1, rows] * (e2 / den)).astype(BF16)], axis=1)
        return _dot(oa_ref[rows], wa_ref[...]), _dot(ob, wb_ref[...])

    def out_proj(rows, ya, yb):
        y = gm_ref[rows, 0:d].astype(F32) * ya + gm_ref[rows, d:2 * d].astype(F32) * yb
        return _dot(y.astype(BF16), wo_ref[...])

    def residual_and_logits(h, rows, z):
        x1 = x_ref[rows] + mod_ref[0, 2:3, :] * z
        x1_ref[rows] = x1
        ms = jnp.mean(x1 * x1, axis=-1, keepdims=True)
        h2 = x1 * lax.rsqrt(ms + EPS) * g2_ref[...]
        h2 = h2 * (1.0 + mod_ref[0, 4:5, :]) + mod_ref[0, 3:4, :]
        _store_token_tiles(h2_ref, h2, first=h * hm)
        hi, lo = _split(h2)
        both = _dot(hi, rcat_ref[...])
        return both[:, 0:LANES] + both[:, LANES:2 * LANES] + _dot(lo, rcat_ref[:, 0:LANES]) + rb_ref[...]

    def route(rows, logits):
        lane = lax.broadcasted_iota(I32, (hm, LANES), 1)
        gl = jnp.where(lane < N_GROUPS, logits, NEG_INF)
        gmax = jnp.max(gl, axis=-1, keepdims=True)
        g_w = 1.0 / jnp.sum(jnp.exp(gl - gmax), axis=-1, keepdims=True)
        g_idx = jnp.min(jnp.where(gl == gmax, lane, LANES), axis=-1, keepdims=True)
        e_lo = N_GROUPS + EXPERTS_PER_GROUP * g_idx
        el = jnp.where((lane >= e_lo) & (lane < e_lo + EXPERTS_PER_GROUP), logits, NEG_INF)
        m1 = jnp.max(el, axis=-1, keepdims=True)
        i1 = jnp.min(jnp.where(el == m1, lane, LANES), axis=-1, keepdims=True)
        el2 = jnp.where(lane == i1, PICKED, el)
        m2 = jnp.max(el2, axis=-1, keepdims=True)
        i2 = jnp.min(jnp.where(el2 == m2, lane, LANES), axis=-1, keepdims=True)
        ex = jnp.exp(m2 - m1)
        w1 = g_w * (1.0 / (1.0 + ex))
        w2 = g_w * (ex / (1.0 + ex))
        route_ref[rows] = jnp.where(
            lane == 0, (i1 - N_GROUPS).astype(F32),
            jnp.where(lane == 1, (i2 - N_GROUPS).astype(F32),
                      jnp.where(lane == 2, w1, jnp.where(lane == 3, w2, 0.0))))

    halves = (slice(0, hm), slice(hm, tm))
    ups = [up(rows) for rows in halves]
    zs = [out_proj(rows, *u) for rows, u in zip(halves, ups)]
    logits = [residual_and_logits(h, rows, z) for h, (rows, z) in enumerate(zip(halves, zs))]
    for rows, lg in zip(halves, logits):
        route(rows, lg)


def _merge_call(x2, oa, ods, lses, gm, mod6, wa, wb, wo, g2, rcat, rb, s):
    t, d = x2.shape
    b = t // s
    tm = MERGE_TM
    per_b = s // tm
    row = lambda i: (i, 0)
    const = lambda i: (0, 0)

    def by_class(dl):
        return pl.BlockSpec((1, dl, tm // dl, LANES), lambda i: (i // per_b, 0, i % per_b, 0))

    dls = [dl for _, dl in DIL_PAIRS]
    assert dls[0] == 1
    group_specs = [pl.BlockSpec((tm, LANES), row)] + [by_class(dl) for dl in dls[1:]]
    views = lambda arrs: [arrs[0].reshape(t, LANES)] + [
        a.reshape(b, dl, s // dl, LANES) for a, dl in zip(arrs[1:], dls[1:])]
    return pl.pallas_call(
        _merge_kernel,
        grid=(t // tm,),
        in_specs=[
            pl.BlockSpec((tm, d), row),
            pl.BlockSpec((tm, A_Q), row),
            *group_specs, *group_specs,
            pl.BlockSpec((tm, 2 * d), row),
            pl.BlockSpec((1, 6, d), lambda i: (i // per_b, 0, 0)),
            pl.BlockSpec(wa.shape, const),
            pl.BlockSpec(wb.shape, const),
            pl.BlockSpec(wo.shape, const),
            pl.BlockSpec((1, d), const),
            pl.BlockSpec(rcat.shape, const),
            pl.BlockSpec((1, LANES), const),
        ],
        out_specs=[
            pl.BlockSpec((tm, d), row),
            pl.BlockSpec((tm * ROW_TILE[0], LANES), row),
            pl.BlockSpec((tm, LANES), row),
        ],
        out_shape=[
            jax.ShapeDtypeStruct((t, d), F32),
            jax.ShapeDtypeStruct((t * ROW_TILE[0], LANES), F32),
            jax.ShapeDtypeStruct((t, LANES), F32),
        ],
        scratch_shapes=[pltpu.VMEM((2 * (len(dls) - 1), tm, LANES), F32)],
        compiler_params=_params("parallel"),
        name="merge",
    )(x2, oa, *views(ods), *views(lses), gm, mod6, wa, wb, wo, g2, rcat, rb)


def _rank_kernel(route_ref, rank_ref, count_ref, carry_scr):
    tm = route_ref.shape[0]

    @pl.when(pl.program_id(0) == 0)
    def _():
        carry_scr[...] = jnp.zeros(carry_scr.shape, F32)

    route = route_ref[...]
    lane = lax.broadcasted_iota(I32, (tm, LANES), 1)
    e1 = route[:, 0:1].astype(I32)
    e2 = route[:, 1:2].astype(I32)
    hit1 = lane == e1
    hit2 = lane == e2
    cnt = jnp.where(hit1 | hit2, 1.0, 0.0)
    r = lax.broadcasted_iota(I32, (tm, tm), 0)
    c = lax.broadcasted_iota(I32, (tm, tm), 1)
    below = jnp.where(c < r, 1.0, 0.0).astype(BF16)
    before = _dot(below, cnt.astype(BF16)) + carry_scr[...]
    r1 = jnp.sum(jnp.where(hit1, before, 0.0), axis=-1, keepdims=True)
    r2 = jnp.sum(jnp.where(hit2, before, 0.0), axis=-1, keepdims=True)
    rank_ref[...] = jnp.where(lane == 0, r1, jnp.where(lane == 1, r2, 0.0))
    carry_scr[...] = carry_scr[...] + jnp.sum(cnt, axis=0, keepdims=True)
    count_ref[...] = carry_scr[...]


def _rank_call(route):
    t = route.shape[0]
    tm = RANK_TM
    return pl.pallas_call(
        _rank_kernel,
        grid=(t // tm,),
        in_specs=[pl.BlockSpec((tm, LANES), lambda i: (i, 0))],
        out_specs=[
            pl.BlockSpec((tm, LANES), lambda i: (i, 0)),
            pl.BlockSpec((1, LANES), lambda i: (0, 0)),
        ],
        out_shape=[
            jax.ShapeDtypeStruct((t, LANES), F32),
            jax.ShapeDtypeStruct((1, LANES), F32),
        ],
        scratch_shapes=[pltpu.VMEM((1, LANES), F32)],
        compiler_params=_params("arbitrary"),
        name="rank",
    )(route)


def _slot_kernel(route_ref, rank_ref, start_ref, slot_ref):
    tm = route_ref.shape[0]
    lane = lax.broadcasted_iota(I32, (tm, LANES), 1)
    starts = start_ref[...]
    slots = []
    for k in range(TOP_K):
        hit = lane == route_ref[:, k:k + 1].astype(I32)
        slots.append(jnp.sum(jnp.where(hit, starts, 0.0), axis=-1, keepdims=True) + rank_ref[:, k:k + 1])
    slot_ref[...] = jnp.where(lane == 0, slots[0], jnp.where(lane == 1, slots[1], 0.0))


def _slot_call(route, rank, pad_starts):
    t = route.shape[0]
    tm = RANK_TM
    tile = pl.BlockSpec((tm, LANES), lambda i: (i, 0))
    return pl.pallas_call(
        _slot_kernel,
        grid=(t // tm,),
        in_specs=[tile, tile, pl.BlockSpec((1, LANES), lambda i: (0, 0))],
        out_specs=tile,
        out_shape=jax.ShapeDtypeStruct((t, LANES), F32),
        compiler_params=_params("parallel"),
        name="slots",
    )(route, rank, pad_starts)


def _for_static_slot(slot, fn):
    for par in range(2):
        pl.when(slot == par)(functools.partial(fn, par))


def _dispatch_kernel(cnt_ref, pstart_ref, dest_hbm, h_hbm, xs_hbm, idx_smem, hbuf, zbuf,
                     isem, hsem, dsem, zsem):
    tm = MOE_TM
    nd = TOP_K * tm
    i = pl.program_id(0)
    n = pl.num_programs(0)
    slot = i % 2
    n_slots = xs_hbm.shape[0]

    def idx_copy(step, sl):
        return pltpu.make_async_copy(dest_hbm.at[step], idx_smem.at[sl], isem.at[sl])

    def tile_copy(step, bf):
        rows = pl.ds(pl.multiple_of(step * tm, tm), tm)
        return pltpu.make_async_copy(h_hbm.at[rows], hbuf.at[bf], hsem.at[bf])

    def wait_scatter(bf):
        for _ in range(TOP_K):
            pltpu.make_async_copy(hbuf.at[bf], xs_hbm.at[pl.ds(0, tm)], dsem.at[bf]).wait()

    @pl.when(i == 0)
    def _():
        idx_copy(0, 0).start()
        tile_copy(0, 0).start()
        zbuf[...] = jnp.zeros(zbuf.shape, F32)

        def zero_gaps(wait):
            def per_expert(e, used):
                cnt = cnt_ref[e]
                gap = (MOE_BLOCK - cnt % MOE_BLOCK) % MOE_BLOCK
                row = pstart_ref[e] + cnt
                for k in range(MOE_BLOCK.bit_length() - 1):
                    size = 1 << k
                    copy = pltpu.make_async_copy(zbuf.at[pl.ds(0, size)], xs_hbm.at[pl.ds(row, size)], zsem)
                    pl.when(((gap >> k) & 1) == 1)(copy.wait if wait else copy.start)
                    row = row + (gap & size)
                return used + cnt + gap
            return lax.fori_loop(0, N_EXPERTS, per_expert, 0)
        zero_gaps(wait=False)
        used = zero_gaps(wait=True)

        def zero_block(blk):
            rows = pl.ds(pl.multiple_of(blk * MOE_BLOCK, MOE_BLOCK), MOE_BLOCK)
            return pltpu.make_async_copy(zbuf, xs_hbm.at[rows], zsem)

        def start_blk(blk, c):
            zero_block(blk).start()
            return c

        def wait_blk(blk, c):
            zero_block(blk).wait()
            return c
        lax.fori_loop(used // MOE_BLOCK, n_slots // MOE_BLOCK, start_blk, 0)
        lax.fori_loop(used // MOE_BLOCK, n_slots // MOE_BLOCK, wait_blk, 0)

    @pl.when(i > 0)
    def _():
        wait_scatter(1 - slot)

    @pl.when(i + 1 < n)
    def _():
        idx_copy(i + 1, 1 - slot).start()
        tile_copy(i + 1, 1 - slot).start()

    idx_copy(i, slot).wait()
    tile_copy(i, slot).wait()

    def scatter_rows(par):
        for r in range(nd):
            pltpu.make_async_copy(hbuf.at[par, r % tm], xs_hbm.at[idx_smem[par, r]], dsem.at[par]).start()
    _for_static_slot(slot, scatter_rows)

    @pl.when(i == n - 1)
    def _():
        wait_scatter(slot)


def _dispatch_call(counts, pad_starts, dest_tiles, h2, n_slots):
    n_steps = dest_tiles.shape[0]
    grid_spec = pltpu.PrefetchScalarGridSpec(
        num_scalar_prefetch=2,
        grid=(n_steps,),
        in_specs=[pl.BlockSpec(memory_space=pl.ANY), pl.BlockSpec(memory_space=pl.ANY)],
        out_specs=pl.BlockSpec(memory_space=pl.ANY),
        scratch_shapes=[
            pltpu.SMEM((2, TOP_K * MOE_TM), I32),
            pltpu.VMEM((2, MOE_TM) + ROW_TILE, F32),
            pltpu.VMEM((MOE_BLOCK,) + ROW_TILE, F32),
            pltpu.SemaphoreType.DMA((2,)),
            pltpu.SemaphoreType.DMA((2,)),
            pltpu.SemaphoreType.DMA((2,)),
            pltpu.SemaphoreType.DMA(()),
        ],
    )
    return pl.pallas_call(
        _dispatch_kernel,
        grid_spec=grid_spec,
        out_shape=jax.ShapeDtypeStruct((n_slots,) + ROW_TILE, F32),
        compiler_params=_params("arbitrary"),
        name="dispatch",
    )(counts, pad_starts, dest_tiles, h2)


def _expert_kernel(be_ref, x_ref, wg_ref, wu_ref, wd_ref, y_ref, wg_scr, wu_scr, wd_scr):
    i = pl.program_id(0)

    @pl.when((i == 0) | (be_ref[i] != be_ref[jnp.maximum(i - 1, 0)]))
    def _():
        wg_scr[...] = wg_ref[0].astype(BF16)
        wu_scr[...] = wu_ref[0].astype(BF16)
        wd_scr[...] = wd_ref[0].astype(BF16)

    xb = _load_token_tiles(x_ref, 0, MOE_BLOCK).astype(BF16)
    gate = _dot(xb, wg_scr[...])
    up = _dot(xb, wu_scr[...])
    hid = (gate * jax.nn.sigmoid(gate) * up).astype(BF16)
    _store_token_tiles(y_ref, _dot(hid, wd_scr[...]))


def _expert_call(blk_expert, xs, wg, wu, wd):
    n_blk = blk_expert.shape[0]
    d = wg.shape[1]
    blk = (MOE_BLOCK * ROW_TILE[0], LANES)
    grid_spec = pltpu.PrefetchScalarGridSpec(
        num_scalar_prefetch=1,
        grid=(n_blk,),
        in_specs=[
            pl.BlockSpec(blk, lambda i, be: (i, 0)),
            pl.BlockSpec((1, d, EXPERT_FF), lambda i, be: (be[i], 0, 0)),
            pl.BlockSpec((1, d, EXPERT_FF), lambda i, be: (be[i], 0, 0)),
            pl.BlockSpec((1, EXPERT_FF, d), lambda i, be: (be[i], 0, 0)),
        ],
        out_specs=pl.BlockSpec(blk, lambda i, be: (i, 0)),
        scratch_shapes=[
            pltpu.VMEM((d, EXPERT_FF), BF16),
            pltpu.VMEM((d, EXPERT_FF), BF16),
            pltpu.VMEM((EXPERT_FF, d), BF16),
        ],
    )
    return pl.pallas_call(
        _expert_kernel,
        grid_spec=grid_spec,
        out_shape=jax.ShapeDtypeStruct(xs.shape, F32),
        compiler_params=_params("arbitrary"),
        name="experts",
    )(blk_expert, xs, wg, wu, wd)


def _combine_kernel(dest_hbm, y_hbm, x1_ref, route_ref, mod_ref, o_ref, idx_smem, ybuf, isem, dsem):
    tm = MOE_TM
    nd = TOP_K * tm
    i = pl.program_id(0)
    n = pl.num_programs(0)
    slot = i % 2
    nxt = 1 - slot

    def idx_copy(step, sl):
        return pltpu.make_async_copy(dest_hbm.at[step], idx_smem.at[sl], isem.at[sl])

    ns = ROW_TILE[0]

    def step_rows(sl):
        return pltpu.make_async_copy(y_hbm.at[pl.ds(0, nd * ns)], ybuf.at[sl], dsem.at[sl])

    def issue_rows(par):
        for r in range(nd):
            src = y_hbm.at[pl.ds(pl.multiple_of(idx_smem[par, r] * ns, ns), ns)]
            pltpu.make_async_copy(src, ybuf.at[par, pl.ds(r * ns, ns)], dsem.at[par]).start()

    @pl.when(i == 0)
    def _():
        idx_copy(0, 0).start()
        idx_copy(0, 0).wait()
        issue_rows(0)

        @pl.when(n > 1)
        def _():
            idx_copy(1, 1).start()

    @pl.when(i + 1 < n)
    def _():
        idx_copy(i + 1, nxt).wait()
        _for_static_slot(nxt, issue_rows)

    @pl.when(i + 2 < n)
    def _():
        idx_copy(i + 2, slot).start()

    step_rows(slot).wait()
    route = route_ref[...]
    w1 = route[:, 2:3]
    w2 = route[:, 3:4]
    y = (w1 * _load_token_tiles(ybuf, 0, tm, lead=(slot,))
         + w2 * _load_token_tiles(ybuf, tm, tm, lead=(slot,)))
    o_ref[...] = x1_ref[...] + mod_ref[0, 5:6, :] * y


def _combine_call(dest_tiles, y_slots, x1, route, mod6, s):
    t, d = x1.shape
    tm = MOE_TM
    per_b = s // tm
    return pl.pallas_call(
        _combine_kernel,
        grid=(t // tm,),
        in_specs=[
            pl.BlockSpec(memory_space=pl.ANY),
            pl.BlockSpec(memory_space=pl.ANY),
            pl.BlockSpec((tm, d), lambda i: (i, 0)),
            pl.BlockSpec((tm, LANES), lambda i: (i, 0)),
            pl.BlockSpec((1, 6, d), lambda i: (i // per_b, 0, 0)),
        ],
        out_specs=pl.BlockSpec((tm, d), lambda i: (i, 0)),
        out_shape=jax.ShapeDtypeStruct((t, d), F32),
        scratch_shapes=[
            pltpu.SMEM((2, TOP_K * tm), I32),
            pltpu.VMEM((2, TOP_K * tm * ROW_TILE[0], LANES), F32),
            pltpu.SemaphoreType.DMA((2,)),
            pltpu.SemaphoreType.DMA((2,)),
        ],
        compiler_params=_params("arbitrary"),
        name="combine",
    )(dest_tiles, y_slots, x1, route, mod6)


def _rope_tables(pos):
    inv = ROPE_THETA ** (-jnp.arange(0, HEAD_DIM, 2, dtype=F32) / HEAD_DIM)
    ang = pos.astype(F32)[..., None] * inv
    return jnp.cos(ang), jnp.sin(ang)


def _rope_tiles(pos):
    cos, sin = _rope_tables(pos)
    return (jnp.concatenate([cos, cos, cos, cos], axis=-1),
            jnp.concatenate([-sin, sin, -sin, sin], axis=-1))


def _selection_constants(s):
    n_c = s // CMP_STRIDE
    n_s = s // SEL_BLOCK
    cs = np.arange(n_c) * CMP_STRIDE
    ss = np.arange(n_s) * SEL_BLOCK
    ov = np.clip(np.minimum(cs[:, None] + CMP_BLOCK, ss[None, :] + SEL_BLOCK)
                 - np.maximum(cs[:, None], ss[None, :]), 0, None).astype(np.float32) / CMP_BLOCK
    kq = np.arange(NSA_TQ)[:, None] - np.arange(NSA_TQ)[None, :]
    wbias = np.stack([np.where(kq >= 0, 0.0, MASKED), np.where(kq <= 0, 0.0, MASKED)]).astype(np.float32)
    return jnp.asarray(ov.T, BF16), jnp.asarray(wbias)


def _mixer_and_router(x, c, positions, w_ada, b_ada, norm1_g, norm2_g, w_in, nsa_q_norm, nsa_k_norm,
                      cmp_pe_k, cmp_w1_k, cmp_w2_k, cmp_pe_v, cmp_w1_v, cmp_w2_v, dil_q_norm,
                      dil_k_norm, w_up_a, w_up_b, w_out, w_group, b_group, w_router, b_router):
    b, s, d = x.shape
    scale = HEAD_DIM ** -0.5
    mod6 = _mod_call(c, w_ada, b_ada).reshape(b, 6, d)

    c1 = A_Q
    c2 = c1 + 6 * A_KV
    c3 = c2 + 3 * NSA_HEADS
    c4 = c3 + 3 * DIL_W
    w_perm = jnp.concatenate([
        w_in[:, :c2], w_in[:, c3:c4],
        jnp.pad(w_in[:, c2:c3], ((0, 0), (0, LANES - 3 * NSA_HEADS))), w_in[:, c4:]], axis=1).astype(BF16)
    two = lambda g: jnp.concatenate([g, g]).astype(F32)
    gains = jnp.stack([two(nsa_q_norm) * (scale * LOG2E), two(nsa_k_norm), two(dil_q_norm) * scale,
                       two(dil_k_norm)])
    cos_t, sin_t = _rope_tables(positions)
    qext, kvc, ksl, vsl, kwp, vwp, dil0, dil1, dil2, ga, gm = _proj_call(
        x, mod6, norm1_g.reshape(1, d), w_perm, gains, cos_t, sin_t)

    n_chunk = s // CMP_STRIDE
    half = CMP_STRIDE * HEAD_DIM
    chunks = kvc.reshape(b, s, 4, HEAD_DIM).transpose(0, 2, 1, 3).reshape(b, 4, n_chunk, half)
    pe = jnp.stack([cmp_pe_k.reshape(2, half), cmp_pe_v.reshape(2, half)])
    w1 = jnp.stack([cmp_w1_k, cmp_w1_v]).astype(BF16)
    zeros = jnp.zeros((CMP_HIDDEN, HEAD_DIM), F32)
    ext = lambda w: jnp.stack([jnp.concatenate([w, zeros], 1), jnp.concatenate([zeros, w], 1)])
    w2ext = jnp.stack([ext(cmp_w2_k), ext(cmp_w2_v)]).astype(BF16)
    cmp_pos = jnp.pad(positions[:, CMP_BLOCK - 1::CMP_STRIDE], ((0, 0), (0, 1)))
    ccos, csin = _rope_tiles(cmp_pos)
    kc, vc = _cmp_call(chunks, pe, w1, w2ext, two(nsa_k_norm).reshape(1, LANES), ccos, csin)

    ov_t, wbias = _selection_constants(s)
    o_a = _nsa_call(qext, kc, vc, ksl, vsl, kwp, vwp, ga, ov_t, wbias)

    ods, lses = zip(*[_dil_call(qkv.reshape(b, 3, s, LANES), dl)
                      for qkv, (_, dl) in zip((dil0, dil1, dil2), DIL_PAIRS)])

    t = b * s
    w_r = jnp.concatenate([w_group, w_router.transpose(1, 0, 2).reshape(d, N_EXPERTS)], axis=1)
    w_r = jnp.pad(w_r, ((0, 0), (0, LANES - w_r.shape[1])))
    rcat = jnp.concatenate(_split(w_r), axis=1)
    rb = jnp.pad(jnp.concatenate([b_group, b_router.reshape(-1)]), (0, LANES - N_GROUPS - N_EXPERTS))
    return _merge_call(
        x.reshape(t, d), o_a.reshape(t, A_Q), ods, lses, gm.reshape(t, 2 * d), mod6,
        w_up_a.astype(BF16), w_up_b.astype(BF16), w_out.astype(BF16), norm2_g.reshape(1, d),
        rcat, rb.reshape(1, LANES).astype(F32), s), mod6


def _moe(x1, h2, route, mod6, w_e_gate, w_e_up, w_e_down, s):
    t, _ = x1.shape
    rank, counts = _rank_call(route)
    counts = counts[0, :N_EXPERTS].astype(I32)
    padded = (counts + MOE_BLOCK - 1) // MOE_BLOCK * MOE_BLOCK
    pad_ends = jnp.cumsum(padded)
    pad_starts = pad_ends - padded
    starts_row = jnp.pad(pad_starts.astype(F32), (0, LANES - N_EXPERTS)).reshape(1, LANES)
    dest = _slot_call(route, rank, starts_row)[:, 0:TOP_K].astype(I32)
    n_slots = t * TOP_K + N_EXPERTS * MOE_BLOCK
    n_blk = n_slots // MOE_BLOCK
    blk_start = jnp.arange(n_blk, dtype=I32) * MOE_BLOCK
    blk_expert = jnp.minimum(jnp.sum((pad_ends[None, :] <= blk_start[:, None]).astype(I32), axis=1),
                             N_EXPERTS - 1)
    tm = MOE_TM
    dest_tiles = dest.reshape(t // tm, tm, TOP_K).transpose(0, 2, 1).reshape(t // tm, TOP_K * tm)
    xs = _dispatch_call(counts, pad_starts.astype(I32), dest_tiles, h2.reshape((t,) + ROW_TILE), n_slots)
    y_slots = _expert_call(blk_expert, xs.reshape(n_slots * ROW_TILE[0], LANES), w_e_gate, w_e_up, w_e_down)
    return _combine_call(dest_tiles, y_slots, x1, route, mod6, s)


def kernel(x, c, positions, w_ada, b_ada, norm1_g, norm2_g, w_in, nsa_q_norm, nsa_k_norm, cmp_pe_k,
           cmp_w1_k, cmp_w2_k, cmp_pe_v, cmp_w1_v, cmp_w2_v, dil_q_norm, dil_k_norm, w_up_a, w_up_b,
           w_out, w_group, b_group, w_router, b_router, w_e_gate, w_e_up, w_e_down):
    b, s, d = x.shape
    assert w_ada.shape[0] == 1 and d == D_MODEL and s % NSA_TK == 0
    (x1, h2, route), mod6 = _mixer_and_router(
        x, c, positions, w_ada[0], b_ada[0], norm1_g[0], norm2_g[0], w_in[0], nsa_q_norm[0],
        nsa_k_norm[0], cmp_pe_k[0], cmp_w1_k[0], cmp_w2_k[0], cmp_pe_v[0], cmp_w1_v[0], cmp_w2_v[0],
        dil_q_norm[0], dil_k_norm[0], w_up_a[0], w_up_b[0], w_out[0], w_group[0], b_group[0],
        w_router[0], b_router[0])
    out = _moe(x1, h2, route, mod6, w_e_gate[0], w_e_up[0], w_e_down[0], s)
    return out.reshape(b, s, d)
```

```python
import functools

import jax
import jax.numpy as jnp
import numpy as np
from jax import lax
from jax.experimental import pallas as pl
from jax.experimental.pallas import tpu as pltpu

F32 = jnp.float32
BF16 = jnp.bfloat16
I32 = jnp.int32

D_MODEL = 1024
HEAD_DIM = 64
LANES = 128
ROW_TILE = (D_MODEL // LANES, LANES)
ROPE_THETA = 10000.0
EPS = 1e-6
LOG2E = 1.4426950408889634
NEG_INF = -1e30
FORCE_SCORE = 1e9
MASKED = -1e30
PICKED = -3e38

NSA_HEADS = 8
NSA_KV_HEADS = 2
NSA_GROUP = 4
CMP_BLOCK = 32
CMP_STRIDE = 16
CMP_HIDDEN = 256
SEL_BLOCK = 64
N_SEL = 8
N_LOCAL_SEL = 2
WINDOW = 512
DIL_PAIRS = ((128, 1), (512, 4), (2048, 16))
DIL_GROUPS = 3
A_Q = NSA_HEADS * HEAD_DIM
A_KV = NSA_KV_HEADS * HEAD_DIM
DIL_W = 2 * DIL_GROUPS * HEAD_DIM
N_GROUPS = 4
EXPERTS_PER_GROUP = 8
N_EXPERTS = 32
TOP_K = 2
ROUTE_ROWS = 8
EXPERT_FF = 512
MOE_BLOCK = 512

VMEM_LIMIT = 56 * 1024 * 1024

T_QA = 0
T_KVC = 4
T_KVA = 6
T_DIL = 10
T_GA = 19
T_GM = 20
N_TILES = 36

PROJ_TM = 512
NSA_TQ = 128
NSA_TK = 512
NSA_NQ = 2
DIL_T = 128
DIL_UNROLL = 4
MERGE_TM = 512
RANK_TM = 512
MOE_TM = 256


def _dot(a, b):
    return jnp.dot(a, b, preferred_element_type=F32)


def _dot_nt(a, b):
    return lax.dot_general(a, b, (((1,), (1,)), ((), ())), preferred_element_type=F32)


def _dot_tn(a, b):
    return lax.dot_general(a, b, (((0,), (0,)), ((), ())), preferred_element_type=F32)


def _split(a):
    hi = a.astype(BF16)
    lo = (a - hi.astype(F32)).astype(BF16)
    return hi, lo


def _load_token_tiles(ref, first, n, lead=()):
    ns = ROW_TILE[0]
    return jnp.concatenate(
        [ref[lead + (pl.ds(first * ns + s, n, stride=ns), slice(None))] for s in range(ns)], axis=1)


def _store_token_tiles(ref, rows, first=0):
    ns = ROW_TILE[0]
    for s in range(ns):
        ref[pl.ds(first * ns + s, rows.shape[0], stride=ns), :] = rows[:, s * LANES:(s + 1) * LANES]


def _params(*sem):
    return pltpu.CompilerParams(dimension_semantics=sem, vmem_limit_bytes=VMEM_LIMIT)


def _mod_kernel(c_ref, whi_ref, wlo_ref, b_ref, o_ref):
    c = c_ref[...]
    ca = c * jax.nn.sigmoid(c)
    hi, lo = _split(ca)
    whi = whi_ref[...]
    o_ref[...] = _dot(hi, whi) + _dot(lo, whi) + _dot(hi, wlo_ref[...]) + b_ref[...]


def _mod_call(c, w_ada, b_ada):
    b, d = c.shape
    n = w_ada.shape[1]
    whi, wlo = _split(w_ada)
    tn = 1024
    return pl.pallas_call(
        _mod_kernel,
        grid=(n // tn,),
        in_specs=[
            pl.BlockSpec((b, d), lambda j: (0, 0)),
            pl.BlockSpec((d, tn), lambda j: (0, j)),
            pl.BlockSpec((d, tn), lambda j: (0, j)),
            pl.BlockSpec((1, tn), lambda j: (0, j)),
        ],
        out_specs=pl.BlockSpec((b, tn), lambda j: (0, j)),
        out_shape=jax.ShapeDtypeStruct((b, n), F32),
        compiler_params=_params("parallel"),
        name="mod",
    )(c, whi, wlo, b_ada.reshape(1, n))


def _norm_rope(a, gain, cos, sin_signed, lo, first):
    sq = a * a
    s0 = jnp.sum(jnp.where(lo, sq, 0.0), axis=-1, keepdims=True)
    s1 = jnp.sum(jnp.where(lo, 0.0, sq), axis=-1, keepdims=True)
    r = jnp.where(lo, lax.rsqrt(s0 * (1.0 / HEAD_DIM) + EPS), lax.rsqrt(s1 * (1.0 / HEAD_DIM) + EPS))
    y = a * r * gain
    rot = jnp.where(first, pltpu.roll(y, 96, 1), pltpu.roll(y, 32, 1))
    return y * cos + rot * sin_signed


def _proj_kernel(x_ref, mod_ref, g1_ref, w_ref, gains_ref, cos_ref, sin_ref, kw0_ref, vw0_ref,
                 qext_ref, kvc_ref, ksl_ref, vsl_ref, kwp_ref, vwp_ref, dil0_ref, dil1_ref, dil2_ref,
                 ga_ref, gm_ref, perm_scr):
    del kw0_ref, vw0_ref
    tm = x_ref.shape[1]
    x = x_ref[0]
    ms = jnp.mean(x * x, axis=-1, keepdims=True)
    y = x * lax.rsqrt(ms + EPS) * g1_ref[...]
    sh1 = mod_ref[0, 0:1, :]
    sc1 = mod_ref[0, 1:2, :]
    h = (y * (1.0 + sc1) + sh1).astype(BF16)
    lane = lax.broadcasted_iota(I32, (tm, LANES), 1)
    lo = lane < HEAD_DIM
    first = (lane & (HEAD_DIM - 1)) < (HEAD_DIM // 2)
    cos = jnp.concatenate([cos_ref[0]] * 4, axis=1)
    sin = jnp.concatenate([sin_ref[0]] * 4, axis=1)
    sin = jnp.where(first, -sin, sin)
    nr = functools.partial(_norm_rope, cos=cos, sin_signed=sin, lo=lo, first=first)

    for c in range(N_TILES // 2):
        acc = _dot(h, w_ref[:, c * 2 * LANES:(c + 1) * 2 * LANES])
        for half in range(2):
            t = 2 * c + half
            a = acc[:, half * LANES:(half + 1) * LANES]
            if t < T_KVC:
                yq = nr(a, gains_ref[0:1, :])
                rq = pltpu.roll(yq, HEAD_DIM, 1)
                if t // 2 == 0:
                    e0 = jnp.where(lo, yq, 0.0)
                    e1 = jnp.where(lo, rq, 0.0)
                else:
                    e0 = jnp.where(lo, 0.0, rq)
                    e1 = jnp.where(lo, 0.0, yq)
                qext_ref[0, 2 * t] = e0.astype(BF16)
                qext_ref[0, 2 * t + 1] = e1.astype(BF16)
            elif t < T_KVA:
                kvc_ref[0, :, (t - T_KVC) * LANES:(t - T_KVC + 1) * LANES] = a
            elif t < T_DIL:
                j = t - T_KVA
                k_out, v_out = (ksl_ref, vsl_ref) if j < 2 else (kwp_ref, vwp_ref)
                if j % 2 == 0:
                    k_out[0] = nr(a, gains_ref[1:2, :]).astype(BF16)
                else:
                    v_out[0, 0] = jnp.where(lo, a, 1.0).astype(BF16)
                    v_out[0, 1] = jnp.where(lo, pltpu.roll(a, HEAD_DIM, 1), 1.0).astype(BF16)
            elif t < T_GA:
                j = t - T_DIL
                if j < 3:
                    v = nr(a, gains_ref[2:3, :])
                elif j < 6:
                    v = nr(a, gains_ref[3:4, :])
                else:
                    v = a
                kind, gi = divmod(j, DIL_GROUPS)
                dl = DIL_PAIRS[gi][1]
                if dl == 1:
                    dil0_ref[0, kind] = v.astype(BF16)
                else:
                    out = (dil0_ref, dil1_ref, dil2_ref)[gi]
                    perm_scr[...] = v
                    for r in range(dl):
                        out[0, kind, r] = perm_scr[pl.ds(r, tm // dl, stride=dl), :].astype(BF16)
            elif t < T_GM:
                ga_ref[0] = jax.nn.sigmoid(a)
            else:
                j = t - T_GM
                gm_ref[0, :, j * LANES:(j + 1) * LANES] = jax.nn.sigmoid(a).astype(BF16)


def _proj_call(x, mod6, g1, w_perm, gains, cos_t, sin_t):
    b, s, d = x.shape
    tm = PROJ_TM
    n = N_TILES * LANES
    row = lambda bi, i: (bi, i, 0)
    heads = lambda bi, i: (bi, 0, i, 0)
    pad = WINDOW // tm
    in_specs = [
        pl.BlockSpec((1, tm, d), row),
        pl.BlockSpec((1, 6, d), lambda bi, i: (bi, 0, 0)),
        pl.BlockSpec((1, d), lambda bi, i: (0, 0)),
        pl.BlockSpec((d, n), lambda bi, i: (0, 0)),
        pl.BlockSpec((4, LANES), lambda bi, i: (0, 0)),
        pl.BlockSpec((1, tm, HEAD_DIM // 2), row),
        pl.BlockSpec((1, tm, HEAD_DIM // 2), row),
        pl.BlockSpec(memory_space=pl.ANY),
        pl.BlockSpec(memory_space=pl.ANY),
    ]
    out_specs = [
        pl.BlockSpec((1, NSA_HEADS, tm, LANES), heads),
        pl.BlockSpec((1, tm, 2 * LANES), row),
        pl.BlockSpec((1, tm, LANES), row),
        pl.BlockSpec((1, NSA_KV_HEADS, tm, LANES), heads),
        pl.BlockSpec((1, tm, LANES), lambda bi, i: (bi, i + pad, 0)),
        pl.BlockSpec((1, NSA_KV_HEADS, tm, LANES), lambda bi, i: (bi, 0, i + pad, 0)),
        pl.BlockSpec((1, 3, tm, LANES), heads),
        *[pl.BlockSpec((1, 3, dl, tm // dl, LANES), lambda bi, i: (bi, 0, 0, i, 0)) for _, dl in DIL_PAIRS[1:]],
        pl.BlockSpec((1, tm, LANES), row),
        pl.BlockSpec((1, tm, 2 * d), row),
    ]
    out_shape = [
        jax.ShapeDtypeStruct((b, NSA_HEADS, s, LANES), BF16),
        jax.ShapeDtypeStruct((b, s, 2 * LANES), F32),
        jax.ShapeDtypeStruct((b, s, LANES), BF16),
        jax.ShapeDtypeStruct((b, NSA_KV_HEADS, s, LANES), BF16),
        jax.ShapeDtypeStruct((b, s + WINDOW, LANES), BF16),
        jax.ShapeDtypeStruct((b, NSA_KV_HEADS, s + WINDOW, LANES), BF16),
        jax.ShapeDtypeStruct((b, 3, s, LANES), BF16),
        *[jax.ShapeDtypeStruct((b, 3, dl, s // dl, LANES), BF16) for _, dl in DIL_PAIRS[1:]],
        jax.ShapeDtypeStruct((b, s, LANES), F32),
        jax.ShapeDtypeStruct((b, s, 2 * d), BF16),
    ]
    kw0 = jnp.zeros(out_shape[4].shape, BF16)
    vw0 = jnp.zeros(out_shape[5].shape, BF16)
    return pl.pallas_call(
        _proj_kernel,
        grid=(b, s // tm),
        in_specs=in_specs,
        out_specs=out_specs,
        out_shape=out_shape,
        scratch_shapes=[pltpu.VMEM((tm, LANES), F32)],
        input_output_aliases={7: 4, 8: 5},
        compiler_params=_params("parallel", "parallel"),
        name="proj",
    )(x, mod6, g1, w_perm, gains, cos_t, sin_t, kw0, vw0)


def _cmp_kernel(c_ref, pe_ref, w1_ref, w2_ref, gain_ref, cos_ref, sin_ref, kc_ref, vc_ref):
    nb = c_ref.shape[2]
    half = c_ref.shape[3]
    lane = lax.broadcasted_iota(I32, (nb, LANES), 1)
    lo = lane < HEAD_DIM
    first = (lane & (HEAD_DIM - 1)) < (HEAD_DIM // 2)
    for kind in range(2):
        out = jnp.zeros((nb, LANES), F32)
        for hd in range(2):
            c = c_ref[0, kind * 2 + hd]
            top = _dot((c + pe_ref[kind, 0:1, :]).astype(BF16), w1_ref[kind, 0:half, :])
            bot = _dot((c + pe_ref[kind, 1:2, :]).astype(BF16), w1_ref[kind, half:2 * half, :])
            hid = top + pltpu.roll(bot, nb - 1, 0)
            hid = hid * jax.nn.sigmoid(hid)
            out = out + _dot(hid.astype(BF16), w2_ref[kind, hd])
        if kind == 0:
            out = _norm_rope(out, gain_ref[...], cos_ref[0], sin_ref[0], lo, first)
            kc_ref[0] = out.astype(BF16)
        else:
            vc_ref[0, 0] = jnp.where(lo, out, 0.0).astype(BF16)
            vc_ref[0, 1] = jnp.where(lo, pltpu.roll(out, HEAD_DIM, 1), 0.0).astype(BF16)


def _cmp_call(chunks, pe, w1, w2ext, gain_k, ccos, csin):
    b, _, nb, half = chunks.shape
    return pl.pallas_call(
        _cmp_kernel,
        grid=(b,),
        in_specs=[
            pl.BlockSpec((1, 4, nb, half), lambda bi: (bi, 0, 0, 0)),
            pl.BlockSpec((2, 2, half), lambda bi: (0, 0, 0)),
            pl.BlockSpec((2, 2 * half, CMP_HIDDEN), lambda bi: (0, 0, 0)),
            pl.BlockSpec((2, 2, CMP_HIDDEN, LANES), lambda bi: (0, 0, 0, 0)),
            pl.BlockSpec((1, LANES), lambda bi: (0, 0)),
            pl.BlockSpec((1, nb, LANES), lambda bi: (bi, 0, 0)),
            pl.BlockSpec((1, nb, LANES), lambda bi: (bi, 0, 0)),
        ],
        out_specs=[
            pl.BlockSpec((1, nb, LANES), lambda bi: (bi, 0, 0)),
            pl.BlockSpec((1, NSA_KV_HEADS, nb, LANES), lambda bi: (bi, 0, 0, 0)),
        ],
        out_shape=[
            jax.ShapeDtypeStruct((b, nb, LANES), BF16),
            jax.ShapeDtypeStruct((b, NSA_KV_HEADS, nb, LANES), BF16),
        ],
        compiler_params=_params("parallel"),
        name="cmp",
    )(chunks, pe, w1, w2ext, gain_k, ccos, csin)


def _softmax_cols(s, exp=jnp.exp):
    p = exp(s - jnp.max(s, axis=0, keepdims=True))
    return p, jnp.sum(p, axis=0, keepdims=True)


def _block_max(s, offs, bs):
    m = None
    for j, off in enumerate(offs):
        mj = jnp.max(s[j * bs:(j + 1) * bs], axis=0, keepdims=True) + off
        m = mj if m is None else jnp.maximum(m, mj)
    return m


def _block_exp2(s, offs, bs, m):
    return jnp.concatenate(
        [jnp.exp2(s[j * bs:(j + 1) * bs] - (m - off)) for j, off in enumerate(offs)], axis=0)


def _nsa_kernel(q_ref, kc_ref, vc_ref, ksl_ref, vsl_ref, kw_ref, vw_ref, ga_ref, ov_ref, wb_ref,
                o_ref, m_scr, acc_scr, *s_scrs):
    tq = NSA_TQ
    tk = NSA_TK
    hd = HEAD_DIM
    cols = NSA_GROUP * tq
    tiles = range(NSA_NQ)
    s_buf = lambda u, c: s_scrs[2 * u + c % 2]
    n_blk = ov_ref.shape[0]
    n_chunks = ksl_ref.shape[1] // tk
    bpc = tk // SEL_BLOCK
    kh = pl.program_id(1)
    base = pl.multiple_of(pl.program_id(2) * (NSA_NQ * tq), NSA_NQ * tq)
    t0 = [base + u * tq for u in tiles]
    q4 = [q_ref[0, :, u * tq:(u + 1) * tq, :].reshape(cols, LANES) for u in tiles]
    lane_q = lax.broadcasted_iota(I32, (1, tq), 1)
    lane_q4 = lax.broadcasted_iota(I32, (1, cols), 1) & (tq - 1)
    per_head = lambda b: jnp.concatenate([b] * NSA_GROUP, axis=1)
    older_edge = wb_ref[0]
    causal_edge = wb_ref[1]
    nwin = WINDOW + tq
    n_wb = nwin // tq
    blk = lax.broadcasted_iota(I32, (n_blk, 1), 0)
    blk_f = blk.astype(F32)
    ov = ov_ref[...]

    s_cmp = [_dot_nt(kc_ref[0], q4[u]) for u in tiles]
    s_win = [_dot_nt(kw_ref[0, pl.ds(t0[u], nwin), :], q4[u]) for u in tiles]
    for u in tiles:
        s_buf(u, 0)[...] = _dot_nt(ksl_ref[0, 0:tk, :], q4[u])

    o_cmp, sel_bias = [], []
    for u in tiles:
        nb = s_cmp[u].shape[0]
        cmp_end = lax.broadcasted_iota(I32, (nb, 1), 0) * CMP_STRIDE + (CMP_BLOCK - 1)
        valid = cmp_end <= t0[u] + lane_q4
        e, den = _softmax_cols(jnp.where(valid, s_cmp[u], NEG_INF), jnp.exp2)
        p = jnp.where(valid, e / den, 0.0)
        o_cmp.append(_dot_tn(vc_ref[0, 0], p.astype(BF16))[0:hd])
        psum = p[:, 0:tq] + p[:, tq:2 * tq] + p[:, 2 * tq:3 * tq] + p[:, 3 * tq:4 * tq]
        p_hi, p_lo = _split(psum)
        imp = _dot(ov, p_hi) + _dot(ov, p_lo)
        rel = ((t0[u] + lane_q) >> 6) - blk
        forced = (blk == 0) | ((rel >= 0) & (rel < N_LOCAL_SEL))
        score = jnp.where(rel < 0, NEG_INF, jnp.where(forced, FORCE_SCORE, imp))
        bias = jnp.full((n_blk, tq), MASKED, F32)
        for _ in range(N_SEL):
            best = jnp.max(score, axis=0, keepdims=True)
            first = jnp.min(jnp.where(score == best, blk_f, float(n_blk)), axis=0, keepdims=True)
            pick = blk_f == first
            bias = jnp.where(pick, 0.0, bias)
            score = jnp.where(pick, PICKED, score)
        sel_bias.append(per_head(jnp.where(rel < 0, MASKED, bias)))

    o_win = []
    for u in tiles:
        sw = jnp.concatenate([s_win[u][0:tq] + per_head(older_edge), s_win[u][tq:nwin - tq],
                              s_win[u][nwin - tq:nwin] + per_head(causal_edge)], axis=0)
        w_offs = [jnp.where(t0[u] - WINDOW + j * tq >= 0, 0.0, MASKED) for j in range(n_wb)]
        pw = _block_exp2(sw, w_offs, tq, _block_max(sw, w_offs, tq))
        ow = _dot_tn(vw_ref[0, 0, pl.ds(t0[u], nwin), :], pw.astype(BF16))
        o_win.append(ow[0:hd] / ow[hd:hd + 1])

    diag = base // tk

    def chunk(c):
        if c + 1 < n_chunks:
            for u in tiles:
                s_buf(u, c + 1)[...] = _dot_nt(ksl_ref[0, (c + 1) * tk:(c + 2) * tk, :], q4[u])
        vb = vsl_ref[0, 0, c * tk:(c + 1) * tk, :]
        for u in tiles:
            own = pl.ds(pl.multiple_of(jnp.clip(t0[u] - c * tk, 0, tk - tq), tq), tq)
            buf = s_buf(u, c)
            buf[own, :] = buf[own, :] + per_head(jnp.where(c == diag, causal_edge, 0.0))
            sc = buf[...]
            offs = [sel_bias[u][c * bpc + j:c * bpc + j + 1] for j in range(bpc)]
            m_new = _block_max(sc, offs, SEL_BLOCK)
            if c == 0:
                acc_scr[u] = _dot_tn(vb, _block_exp2(sc, offs, SEL_BLOCK, m_new).astype(BF16))
            else:
                m_old = m_scr[u]
                m_new = jnp.maximum(m_old, m_new)
                pe = _block_exp2(sc, offs, SEL_BLOCK, m_new)
                acc_scr[u] = jnp.exp2(m_old - m_new) * acc_scr[u] + _dot_tn(vb, pe.astype(BF16))
            m_scr[u] = m_new

    chunk(0)
    for c in range(1, n_chunks):
        pl.when(c <= diag)(functools.partial(chunk, c))

    is0 = kh == 0
    for u in tiles:
        rows = slice(u * tq, (u + 1) * tq)
        o_slc = acc_scr[u, 0:hd, :] / acc_scr[u, hd:hd + 1, :]
        gat = ga_ref[0, rows, :].T
        heads = []
        for g in range(NSA_GROUP):
            c = slice(g * tq, (g + 1) * tq)
            og = jnp.zeros((hd, tq), F32)
            for gi, ob in enumerate((o_cmp[u], o_slc, o_win[u])):
                c0 = gi * NSA_HEADS + g
                c1 = c0 + NSA_GROUP
                gate = jnp.where(is0, gat[c0:c0 + 1, :], gat[c1:c1 + 1, :])
                og = og + gate * ob[:, c]
            heads.append(og)
        for pair in range(2):
            tile = jnp.concatenate([heads[2 * pair], heads[2 * pair + 1]], axis=0)
            o_ref[0, rows, pair * LANES:(pair + 1) * LANES] = tile.T.astype(BF16)


def _nsa_call(qext, kc, vc, ksl, vsl, kwp, vwp, ga, ov_t, wbias):
    b, _, s, _ = qext.shape
    tq = NSA_TQ * NSA_NQ
    nb = kc.shape[1]
    cols = NSA_GROUP * NSA_TQ
    assert NSA_TK % tq == 0
    shared = lambda rows: pl.BlockSpec((1, rows, LANES), lambda bi, k, i: (bi, 0, 0))
    per_kv = lambda rows: pl.BlockSpec((1, 1, rows, LANES), lambda bi, k, i: (bi, k, 0, 0))
    return pl.pallas_call(
        _nsa_kernel,
        grid=(b, NSA_KV_HEADS, s // tq),
        in_specs=[
            pl.BlockSpec((1, NSA_GROUP, tq, LANES), lambda bi, k, i: (bi, k, i, 0)),
            shared(nb), per_kv(nb),
            shared(s), per_kv(s),
            shared(s + WINDOW), per_kv(s + WINDOW),
            pl.BlockSpec((1, tq, LANES), lambda bi, k, i: (bi, i, 0)),
            pl.BlockSpec(ov_t.shape, lambda bi, k, i: (0, 0)),
            pl.BlockSpec(wbias.shape, lambda bi, k, i: (0, 0, 0)),
        ],
        out_specs=pl.BlockSpec((1, tq, 2 * LANES), lambda bi, k, i: (bi, i, k)),
        out_shape=jax.ShapeDtypeStruct((b, s, A_Q), BF16),
        scratch_shapes=[
            pltpu.VMEM((NSA_NQ, 1, cols), F32),
            pltpu.VMEM((NSA_NQ, LANES, cols), F32),
            *[pltpu.VMEM((NSA_TK, cols), F32)] * (2 * NSA_NQ),
        ],
        compiler_params=_params("parallel", "parallel", "parallel"),
        name="nsa",
    )(qext, kc, vc, ksl, vsl, kwp, vwp, ga, ov_t, wbias)


def _dil_kernel(q_ref, k_ref, v_ref, o_ref, lse_ref, *, seg_tiles):
    t = DIL_T
    lo = lax.broadcasted_iota(I32, (t, LANES), 1) < HEAD_DIM
    kj = lax.broadcasted_iota(I32, (t, 2 * t), 0)
    qi = lax.broadcasted_iota(I32, (t, 2 * t), 1) & (t - 1)
    top = lax.broadcasted_iota(I32, (LANES, t), 0) < HEAD_DIM

    def scores(i):
        has_prev = (i & (seg_tiles - 1)) != 0
        cs = pl.multiple_of(i * t, t)
        ps = pl.multiple_of(jnp.maximum(i - 1, 0) * t, t)
        q = q_ref[0, 0, pl.ds(cs, t), :]
        zero = jnp.zeros_like(q)
        q2 = jnp.concatenate([jnp.where(lo, q, zero), jnp.where(lo, zero, q)], axis=0)
        sp = _dot_nt(k_ref[0, 0, pl.ds(ps, t), :], q2)
        sc = _dot_nt(k_ref[0, 0, pl.ds(cs, t), :], q2)
        return has_prev, ps, cs, sp, sc

    def attend(has_prev, ps, cs, sp, sc):
        sp = jnp.where((kj >= qi) & has_prev, sp, MASKED)
        sc = jnp.where(qi >= kj, sc, MASKED)
        m = jnp.maximum(jnp.max(sp, axis=0, keepdims=True), jnp.max(sc, axis=0, keepdims=True))
        pp = jnp.exp(sp - m)
        pc = jnp.exp(sc - m)
        l = jnp.sum(pp, axis=0, keepdims=True) + jnp.sum(pc, axis=0, keepdims=True)
        o = (_dot_tn(v_ref[0, 0, pl.ds(ps, t), :], pp.astype(BF16))
             + _dot_tn(v_ref[0, 0, pl.ds(cs, t), :], pc.astype(BF16))) / l
        lse = m + jnp.log(l)
        o_ref[0, pl.ds(cs, t), :] = jnp.where(top, o[:, 0:t], o[:, t:2 * t]).T
        lse_ref[0, pl.ds(cs, t), :] = jnp.where(top, lse[:, 0:t], lse[:, t:2 * t]).T

    def tiles(j, carry):
        group = [scores(j * DIL_UNROLL + u) for u in range(DIL_UNROLL)]
        for args in group:
            attend(*args)
        return carry

    lax.fori_loop(0, q_ref.shape[2] // (t * DIL_UNROLL), tiles, 0)


def _dil_call(qkv, dl):
    b, _, s, _ = qkv.shape
    kind = lambda j: pl.BlockSpec((1, 1, s, LANES), lambda bi: (bi, j, 0, 0))
    out = pl.BlockSpec((1, s, LANES), lambda bi: (bi, 0, 0))
    return pl.pallas_call(
        functools.partial(_dil_kernel, seg_tiles=s // dl // DIL_T),
        grid=(b,),
        in_specs=[kind(0), kind(1), kind(2)],
        out_specs=[out, out],
        out_shape=[jax.ShapeDtypeStruct((b, s, LANES), F32)] * 2,
        compiler_params=_params("parallel"),
        name=f"dil{dl}",
    )(qkv, qkv, qkv)


def _merge_kernel(x_ref, oa_ref, od0_ref, od1_ref, od2_ref, ls0_ref, ls1_ref, ls2_ref, gm_ref, mod_ref,
                  wa_ref, wb_ref, wo_ref, g2_ref, rcat_ref, rb_ref, x1_ref, h2_ref, route_ref, perm_scr):
    tm = x_ref.shape[0]
    d = x_ref.shape[1]
    hm = tm // 2
    for n, ref in enumerate((od1_ref, od2_ref, ls1_ref, ls2_ref)):
        dl = ref.shape[1]
        for r in range(dl):
            perm_scr[n, pl.ds(r, tm // dl, stride=dl), :] = ref[0, r]

    def up(rows):
        l0, l1, l2 = ls0_ref[rows], perm_scr[2, rows], perm_scr[3, rows]
        mx = jnp.maximum(jnp.maximum(l0, l1), l2)
        e0, e1, e2 = jnp.exp(l0 - mx), jnp.exp(l1 - mx), jnp.exp(l2 - mx)
        den = e0 + e1 + e2
        ob = jnp.concatenate([(od0_ref[rows] * (e0 / den)).astype(BF16),
                              (perm_scr[0, rows] * (e1 / den)).astype(BF16),
                              (perm_scr[1, rows] * (e2 / den)).astype(BF16)], axis=1)
        return _dot(oa_ref[rows], wa_ref[...]), _dot(ob, wb_ref[...])

    def out_proj(rows, ya, yb):
        y = gm_ref[rows, 0:d].astype(F32) * ya + gm_ref[rows, d:2 * d].astype(F32) * yb
        return _dot(y.astype(BF16), wo_ref[...])

    def residual_and_logits(h, rows, z):
        x1 = x_ref[rows] + mod_ref[0, 2:3, :] * z
        x1_ref[rows] = x1
        ms = jnp.mean(x1 * x1, axis=-1, keepdims=True)
        h2 = x1 * lax.rsqrt(ms + EPS) * g2_ref[...]
        h2 = h2 * (1.0 + mod_ref[0, 4:5, :]) + mod_ref[0, 3:4, :]
        _store_token_tiles(h2_ref, h2, first=h * hm)
        hi, lo = _split(h2)
        both = _dot_nt(rcat_ref[...], hi)
        return both[0:LANES] + both[LANES:2 * LANES] + _dot_nt(rcat_ref[0:LANES, :], lo) + rb_ref[...]

    def route(h, logits):
        row = lax.broadcasted_iota(I32, (LANES, hm), 0).astype(F32)
        gl = jnp.where(row < N_GROUPS, logits, NEG_INF)
        gmax = jnp.max(gl, axis=0, keepdims=True)
        g_w = 1.0 / jnp.sum(jnp.exp(gl - gmax), axis=0, keepdims=True)
        g_idx = jnp.min(jnp.where(gl == gmax, row, float(LANES)), axis=0, keepdims=True)
        e_lo = N_GROUPS + EXPERTS_PER_GROUP * g_idx
        el = jnp.where((row >= e_lo) & (row < e_lo + EXPERTS_PER_GROUP), logits, NEG_INF)
        m1 = jnp.max(el, axis=0, keepdims=True)
        i1 = jnp.min(jnp.where(el == m1, row, float(LANES)), axis=0, keepdims=True)
        el2 = jnp.where(row == i1, PICKED, el)
        m2 = jnp.max(el2, axis=0, keepdims=True)
        i2 = jnp.min(jnp.where(el2 == m2, row, float(LANES)), axis=0, keepdims=True)
        ex = jnp.exp(m2 - m1)
        w1 = g_w * (1.0 / (1.0 + ex))
        w2 = g_w * (ex / (1.0 + ex))
        route_ref[:, h * hm:(h + 1) * hm] = jnp.concatenate(
            [i1 - N_GROUPS, i2 - N_GROUPS, w1, w2, jnp.zeros((ROUTE_ROWS - 4, hm), F32)], axis=0)

    halves = (slice(0, hm), slice(hm, tm))
    ups = [up(rows) for rows in halves]
    zs = [out_proj(rows, *u) for rows, u in zip(halves, ups)]
    logits = [residual_and_logits(h, rows, z) for h, (rows, z) in enumerate(zip(halves, zs))]
    for h, lg in enumerate(logits):
        route(h, lg)


def _merge_call(x2, oa, ods, lses, gm, mod6, wa, wb, wo, g2, rcat, rb, s):
    t, d = x2.shape
    b = t // s
    tm = MERGE_TM
    per_b = s // tm
    row = lambda i: (i, 0)
    const = lambda i: (0, 0)

    def by_class(dl):
        return pl.BlockSpec((1, dl, tm // dl, LANES), lambda i: (i // per_b, 0, i % per_b, 0))

    dls = [dl for _, dl in DIL_PAIRS]
    assert dls[0] == 1
    group_specs = [pl.BlockSpec((tm, LANES), row)] + [by_class(dl) for dl in dls[1:]]
    views = lambda arrs: [arrs[0].reshape(t, LANES)] + [
        a.reshape(b, dl, s // dl, LANES) for a, dl in zip(arrs[1:], dls[1:])]
    return pl.pallas_call(
        _merge_kernel,
        grid=(t // tm,),
        in_specs=[
            pl.BlockSpec((tm, d), row),
            pl.BlockSpec((tm, A_Q), row),
            *group_specs, *group_specs,
            pl.BlockSpec((tm, 2 * d), row),
            pl.BlockSpec((1, 6, d), lambda i: (i // per_b, 0, 0)),
            pl.BlockSpec(wa.shape, const),
            pl.BlockSpec(wb.shape, const),
            pl.BlockSpec(wo.shape, const),
            pl.BlockSpec((1, d), const),
            pl.BlockSpec(rcat.shape, const),
            pl.BlockSpec((LANES, 1), const),
        ],
        out_specs=[
            pl.BlockSpec((tm, d), row),
            pl.BlockSpec((tm * ROW_TILE[0], LANES), row),
            pl.BlockSpec((ROUTE_ROWS, tm), lambda i: (0, i)),
        ],
        out_shape=[
            jax.ShapeDtypeStruct((t, d), F32),
            jax.ShapeDtypeStruct((t * ROW_TILE[0], LANES), F32),
            jax.ShapeDtypeStruct((ROUTE_ROWS, t), F32),
        ],
        scratch_shapes=[pltpu.VMEM((2 * (len(dls) - 1), tm, LANES), F32)],
        compiler_params=_params("parallel"),
        name="merge",
    )(x2, oa, *views(ods), *views(lses), gm, mod6, wa, wb, wo, g2, rcat, rb)


def _expert_hits(route_ref, tm):
    row = lax.broadcasted_iota(I32, (N_EXPERTS, tm), 0).astype(F32)
    return [row == route_ref[k:k + 1, :] for k in range(TOP_K)]


def _rank_kernel(route_ref, rank_ref, count_ref, carry_scr):
    tm = route_ref.shape[1]

    @pl.when(pl.program_id(0) == 0)
    def _():
        carry_scr[...] = jnp.zeros(carry_scr.shape, F32)

    hits = _expert_hits(route_ref, tm)
    cnt = jnp.where(hits[0] | hits[1], 1.0, 0.0)
    r = lax.broadcasted_iota(I32, (tm, tm), 0)
    c = lax.broadcasted_iota(I32, (tm, tm), 1)
    earlier = jnp.where(r < c, 1.0, 0.0).astype(BF16)
    before = _dot(cnt.astype(BF16), earlier) + carry_scr[:, 0:1]
    ranks = [jnp.sum(jnp.where(hit, before, 0.0), axis=0, keepdims=True) for hit in hits]
    rank_ref[...] = jnp.concatenate(ranks + [jnp.zeros((ROUTE_ROWS - TOP_K, tm), F32)], axis=0)
    carry_scr[...] = carry_scr[...] + jnp.sum(cnt, axis=1, keepdims=True)
    count_ref[...] = carry_scr[...]


def _rank_call(route):
    t = route.shape[1]
    tm = RANK_TM
    tile = pl.BlockSpec((ROUTE_ROWS, tm), lambda i: (0, i))
    return pl.pallas_call(
        _rank_kernel,
        grid=(t // tm,),
        in_specs=[tile],
        out_specs=[tile, pl.BlockSpec((N_EXPERTS, LANES), lambda i: (0, 0))],
        out_shape=[
            jax.ShapeDtypeStruct((ROUTE_ROWS, t), F32),
            jax.ShapeDtypeStruct((N_EXPERTS, LANES), F32),
        ],
        scratch_shapes=[pltpu.VMEM((N_EXPERTS, LANES), F32)],
        compiler_params=_params("arbitrary"),
        name="rank",
    )(route)


def _slot_kernel(route_ref, rank_ref, start_ref, slot_ref):
    tm = route_ref.shape[1]
    starts = start_ref[...]
    slots = [jnp.sum(jnp.where(hit, starts, 0.0), axis=0, keepdims=True) + rank_ref[k:k + 1, :]
             for k, hit in enumerate(_expert_hits(route_ref, tm))]
    slot_ref[...] = jnp.concatenate(slots + [jnp.zeros((ROUTE_ROWS - TOP_K, tm), F32)], axis=0)


def _slot_call(route, rank, pad_starts):
    t = route.shape[1]
    tm = RANK_TM
    tile = pl.BlockSpec((ROUTE_ROWS, tm), lambda i: (0, i))
    return pl.pallas_call(
        _slot_kernel,
        grid=(t // tm,),
        in_specs=[tile, tile, pl.BlockSpec((N_EXPERTS, tm), lambda i: (0, 0))],
        out_specs=tile,
        out_shape=jax.ShapeDtypeStruct((ROUTE_ROWS, t), F32),
        compiler_params=_params("parallel"),
        name="slots",
    )(route, rank, jnp.broadcast_to(pad_starts[:, None], (N_EXPERTS, tm)))


def _for_static_slot(slot, fn):
    for par in range(2):
        pl.when(slot == par)(functools.partial(fn, par))


def _dispatch_kernel(cnt_ref, pstart_ref, dest_hbm, h_hbm, xs_hbm, idx_smem, hbuf, zbuf,
                     isem, hsem, dsem, zsem):
    tm = MOE_TM
    nd = TOP_K * tm
    i = pl.program_id(0)
    n = pl.num_programs(0)
    slot = i % 2
    n_slots = xs_hbm.shape[0]

    def idx_copy(step, sl):
        return pltpu.make_async_copy(dest_hbm.at[step], idx_smem.at[sl], isem.at[sl])

    def tile_copy(step, bf):
        rows = pl.ds(pl.multiple_of(step * tm, tm), tm)
        return pltpu.make_async_copy(h_hbm.at[rows], hbuf.at[bf], hsem.at[bf])

    def wait_scatter(bf):
        for _ in range(TOP_K):
            pltpu.make_async_copy(hbuf.at[bf], xs_hbm.at[pl.ds(0, tm)], dsem.at[bf]).wait()

    @pl.when(i == 0)
    def _():
        idx_copy(0, 0).start()
        tile_copy(0, 0).start()
        zbuf[...] = jnp.zeros(zbuf.shape, F32)

        def zero_gaps(wait):
            def per_expert(e, used):
                cnt = cnt_ref[e]
                gap = (MOE_BLOCK - cnt % MOE_BLOCK) % MOE_BLOCK
                row = pstart_ref[e] + cnt
                for k in range(MOE_BLOCK.bit_length() - 1):
                    size = 1 << k
                    copy = pltpu.make_async_copy(zbuf.at[pl.ds(0, size)], xs_hbm.at[pl.ds(row, size)], zsem)
                    pl.when(((gap >> k) & 1) == 1)(copy.wait if wait else copy.start)
                    row = row + (gap & size)
                return used + cnt + gap
            return lax.fori_loop(0, N_EXPERTS, per_expert, 0)
        zero_gaps(wait=False)
        used = zero_gaps(wait=True)

        def zero_block(blk):
            rows = pl.ds(pl.multiple_of(blk * MOE_BLOCK, MOE_BLOCK), MOE_BLOCK)
            return pltpu.make_async_copy(zbuf, xs_hbm.at[rows], zsem)

        def start_blk(blk, c):
            zero_block(blk).start()
            return c

        def wait_blk(blk, c):
            zero_block(blk).wait()
            return c
        lax.fori_loop(used // MOE_BLOCK, n_slots // MOE_BLOCK, start_blk, 0)
        lax.fori_loop(used // MOE_BLOCK, n_slots // MOE_BLOCK, wait_blk, 0)

    @pl.when(i > 0)
    def _():
        wait_scatter(1 - slot)

    @pl.when(i + 1 < n)
    def _():
        idx_copy(i + 1, 1 - slot).start()
        tile_copy(i + 1, 1 - slot).start()

    idx_copy(i, slot).wait()
    tile_copy(i, slot).wait()

    def scatter_rows(par):
        for r in range(nd):
            pltpu.make_async_copy(hbuf.at[par, r % tm], xs_hbm.at[idx_smem[par, r]], dsem.at[par]).start()
    _for_static_slot(slot, scatter_rows)

    @pl.when(i == n - 1)
    def _():
        wait_scatter(slot)


def _dispatch_call(counts, pad_starts, dest_tiles, h2, n_slots):
    n_steps = dest_tiles.shape[0]
    grid_spec = pltpu.PrefetchScalarGridSpec(
        num_scalar_prefetch=2,
        grid=(n_steps,),
        in_specs=[pl.BlockSpec(memory_space=pl.ANY), pl.BlockSpec(memory_space=pl.ANY)],
        out_specs=pl.BlockSpec(memory_space=pl.ANY),
        scratch_shapes=[
            pltpu.SMEM((2, TOP_K * MOE_TM), I32),
            pltpu.VMEM((2, MOE_TM) + ROW_TILE, F32),
            pltpu.VMEM((MOE_BLOCK,) + ROW_TILE, F32),
            pltpu.SemaphoreType.DMA((2,)),
            pltpu.SemaphoreType.DMA((2,)),
            pltpu.SemaphoreType.DMA((2,)),
            pltpu.SemaphoreType.DMA(()),
        ],
    )
    return pl.pallas_call(
        _dispatch_kernel,
        grid_spec=grid_spec,
        out_shape=jax.ShapeDtypeStruct((n_slots,) + ROW_TILE, F32),
        compiler_params=_params("arbitrary"),
        name="dispatch",
    )(counts, pad_starts, dest_tiles, h2)


def _expert_kernel(be_ref, x_ref, wg_ref, wu_ref, wd_ref, y_ref, wg_scr, wu_scr, wd_scr):
    i = pl.program_id(0)

    @pl.when((i == 0) | (be_ref[i] != be_ref[jnp.maximum(i - 1, 0)]))
    def _():
        wg_scr[...] = wg_ref[0].astype(BF16)
        wu_scr[...] = wu_ref[0].astype(BF16)
        wd_scr[...] = wd_ref[0].astype(BF16)

    xb = _load_token_tiles(x_ref, 0, MOE_BLOCK).astype(BF16)
    gate = _dot(xb, wg_scr[...])
    up = _dot(xb, wu_scr[...])
    hid = (gate * jax.nn.sigmoid(gate) * up).astype(BF16)
    _store_token_tiles(y_ref, _dot(hid, wd_scr[...]))


def _expert_call(blk_expert, xs, wg, wu, wd):
    n_blk = blk_expert.shape[0]
    d = wg.shape[1]
    blk = (MOE_BLOCK * ROW_TILE[0], LANES)
    grid_spec = pltpu.PrefetchScalarGridSpec(
        num_scalar_prefetch=1,
        grid=(n_blk,),
        in_specs=[
            pl.BlockSpec(blk, lambda i, be: (i, 0)),
            pl.BlockSpec((1, d, EXPERT_FF), lambda i, be: (be[i], 0, 0)),
            pl.BlockSpec((1, d, EXPERT_FF), lambda i, be: (be[i], 0, 0)),
            pl.BlockSpec((1, EXPERT_FF, d), lambda i, be: (be[i], 0, 0)),
        ],
        out_specs=pl.BlockSpec(blk, lambda i, be: (i, 0)),
        scratch_shapes=[
            pltpu.VMEM((d, EXPERT_FF), BF16),
            pltpu.VMEM((d, EXPERT_FF), BF16),
            pltpu.VMEM((EXPERT_FF, d), BF16),
        ],
    )
    return pl.pallas_call(
        _expert_kernel,
        grid_spec=grid_spec,
        out_shape=jax.ShapeDtypeStruct(xs.shape, F32),
        compiler_params=_params("arbitrary"),
        name="experts",
    )(blk_expert, xs, wg, wu, wd)


def _combine_kernel(dest_hbm, y_hbm, x1_ref, route_ref, mod_ref, o_ref, idx_smem, ybuf, isem, dsem):
    tm = MOE_TM
    nd = TOP_K * tm
    i = pl.program_id(0)
    n = pl.num_programs(0)
    slot = i % 2
    nxt = 1 - slot

    def idx_copy(step, sl):
        return pltpu.make_async_copy(dest_hbm.at[step], idx_smem.at[sl], isem.at[sl])

    ns = ROW_TILE[0]

    def step_rows(sl):
        return pltpu.make_async_copy(y_hbm.at[pl.ds(0, nd * ns)], ybuf.at[sl], dsem.at[sl])

    def issue_rows(par):
        for r in range(nd):
            src = y_hbm.at[pl.ds(pl.multiple_of(idx_smem[par, r] * ns, ns), ns)]
            pltpu.make_async_copy(src, ybuf.at[par, pl.ds(r * ns, ns)], dsem.at[par]).start()

    @pl.when(i == 0)
    def _():
        idx_copy(0, 0).start()
        idx_copy(0, 0).wait()
        issue_rows(0)

        @pl.when(n > 1)
        def _():
            idx_copy(1, 1).start()

    @pl.when(i + 1 < n)
    def _():
        idx_copy(i + 1, nxt).wait()
        _for_static_slot(nxt, issue_rows)

    @pl.when(i + 2 < n)
    def _():
        idx_copy(i + 2, slot).start()

    step_rows(slot).wait()
    route = jnp.concatenate([route_ref[...], jnp.zeros((LANES - ROUTE_ROWS, tm), F32)], axis=0).T
    w1 = route[:, 2:3]
    w2 = route[:, 3:4]
    y = (w1 * _load_token_tiles(ybuf, 0, tm, lead=(slot,))
         + w2 * _load_token_tiles(ybuf, tm, tm, lead=(slot,)))
    o_ref[...] = x1_ref[...] + mod_ref[0, 5:6, :] * y


def _combine_call(dest_tiles, y_slots, x1, route, mod6, s):
    t, d = x1.shape
    tm = MOE_TM
    per_b = s // tm
    return pl.pallas_call(
        _combine_kernel,
        grid=(t // tm,),
        in_specs=[
            pl.BlockSpec(memory_space=pl.ANY),
            pl.BlockSpec(memory_space=pl.ANY),
            pl.BlockSpec((tm, d), lambda i: (i, 0)),
            pl.BlockSpec((ROUTE_ROWS, tm), lambda i: (0, i)),
            pl.BlockSpec((1, 6, d), lambda i: (i // per_b, 0, 0)),
        ],
        out_specs=pl.BlockSpec((tm, d), lambda i: (i, 0)),
        out_shape=jax.ShapeDtypeStruct((t, d), F32),
        scratch_shapes=[
            pltpu.SMEM((2, TOP_K * tm), I32),
            pltpu.VMEM((2, TOP_K * tm * ROW_TILE[0], LANES), F32),
            pltpu.SemaphoreType.DMA((2,)),
            pltpu.SemaphoreType.DMA((2,)),
        ],
        compiler_params=_params("arbitrary"),
        name="combine",
    )(dest_tiles, y_slots, x1, route, mod6)


def _rope_tables(pos):
    inv = ROPE_THETA ** (-jnp.arange(0, HEAD_DIM, 2, dtype=F32) / HEAD_DIM)
    ang = pos.astype(F32)[..., None] * inv
    return jnp.cos(ang), jnp.sin(ang)


def _rope_tiles(pos):
    cos, sin = _rope_tables(pos)
    return (jnp.concatenate([cos, cos, cos, cos], axis=-1),
            jnp.concatenate([-sin, sin, -sin, sin], axis=-1))


def _selection_constants(s):
    n_c = s // CMP_STRIDE
    n_s = s // SEL_BLOCK
    cs = np.arange(n_c) * CMP_STRIDE
    ss = np.arange(n_s) * SEL_BLOCK
    ov = np.clip(np.minimum(cs[:, None] + CMP_BLOCK, ss[None, :] + SEL_BLOCK)
                 - np.maximum(cs[:, None], ss[None, :]), 0, None).astype(np.float32) / CMP_BLOCK
    kq = np.arange(NSA_TQ)[:, None] - np.arange(NSA_TQ)[None, :]
    wbias = np.stack([np.where(kq >= 0, 0.0, MASKED), np.where(kq <= 0, 0.0, MASKED)]).astype(np.float32)
    return jnp.asarray(ov.T, BF16), jnp.asarray(wbias)


def _mixer_and_router(x, c, positions, w_ada, b_ada, norm1_g, norm2_g, w_in, nsa_q_norm, nsa_k_norm,
                      cmp_pe_k, cmp_w1_k, cmp_w2_k, cmp_pe_v, cmp_w1_v, cmp_w2_v, dil_q_norm,
                      dil_k_norm, w_up_a, w_up_b, w_out, w_group, b_group, w_router, b_router):
    b, s, d = x.shape
    scale = HEAD_DIM ** -0.5
    mod6 = _mod_call(c, w_ada, b_ada).reshape(b, 6, d)

    c1 = A_Q
    c2 = c1 + 6 * A_KV
    c3 = c2 + 3 * NSA_HEADS
    c4 = c3 + 3 * DIL_W
    w_perm = jnp.concatenate([
        w_in[:, :c2], w_in[:, c3:c4],
        jnp.pad(w_in[:, c2:c3], ((0, 0), (0, LANES - 3 * NSA_HEADS))), w_in[:, c4:]], axis=1).astype(BF16)
    two = lambda g: jnp.concatenate([g, g]).astype(F32)
    gains = jnp.stack([two(nsa_q_norm) * (scale * LOG2E), two(nsa_k_norm), two(dil_q_norm) * scale,
                       two(dil_k_norm)])
    cos_t, sin_t = _rope_tables(positions)
    qext, kvc, ksl, vsl, kwp, vwp, dil0, dil1, dil2, ga, gm = _proj_call(
        x, mod6, norm1_g.reshape(1, d), w_perm, gains, cos_t, sin_t)

    n_chunk = s // CMP_STRIDE
    half = CMP_STRIDE * HEAD_DIM
    chunks = kvc.reshape(b, s, 4, HEAD_DIM).transpose(0, 2, 1, 3).reshape(b, 4, n_chunk, half)
    pe = jnp.stack([cmp_pe_k.reshape(2, half), cmp_pe_v.reshape(2, half)])
    w1 = jnp.stack([cmp_w1_k, cmp_w1_v]).astype(BF16)
    zeros = jnp.zeros((CMP_HIDDEN, HEAD_DIM), F32)
    ext = lambda w: jnp.stack([jnp.concatenate([w, zeros], 1), jnp.concatenate([zeros, w], 1)])
    w2ext = jnp.stack([ext(cmp_w2_k), ext(cmp_w2_v)]).astype(BF16)
    cmp_pos = jnp.pad(positions[:, CMP_BLOCK - 1::CMP_STRIDE], ((0, 0), (0, 1)))
    ccos, csin = _rope_tiles(cmp_pos)
    kc, vc = _cmp_call(chunks, pe, w1, w2ext, two(nsa_k_norm).reshape(1, LANES), ccos, csin)

    ov_t, wbias = _selection_constants(s)
    o_a = _nsa_call(qext, kc, vc, ksl, vsl, kwp, vwp, ga, ov_t, wbias)

    ods, lses = zip(*[_dil_call(qkv.reshape(b, 3, s, LANES), dl)
                      for qkv, (_, dl) in zip((dil0, dil1, dil2), DIL_PAIRS)])

    t = b * s
    w_r = jnp.concatenate([w_group, w_router.transpose(1, 0, 2).reshape(d, N_EXPERTS)], axis=1)
    w_r = jnp.pad(w_r, ((0, 0), (0, LANES - w_r.shape[1])))
    rcat = jnp.concatenate(_split(w_r.T), axis=0)
    rb = jnp.pad(jnp.concatenate([b_group, b_router.reshape(-1)]), (0, LANES - N_GROUPS - N_EXPERTS))
    return _merge_call(
        x.reshape(t, d), o_a.reshape(t, A_Q), ods, lses, gm.reshape(t, 2 * d), mod6,
        w_up_a.astype(BF16), w_up_b.astype(BF16), w_out.astype(BF16), norm2_g.reshape(1, d),
        rcat, rb.reshape(LANES, 1).astype(F32), s), mod6


def _moe(x1, h2, route, mod6, w_e_gate, w_e_up, w_e_down, s):
    t, _ = x1.shape
    rank, counts = _rank_call(route)
    counts = counts[:, 0].astype(I32)
    padded = (counts + MOE_BLOCK - 1) // MOE_BLOCK * MOE_BLOCK
    pad_ends = jnp.cumsum(padded)
    pad_starts = pad_ends - padded
    dest = _slot_call(route, rank, pad_starts.astype(F32))[0:TOP_K].astype(I32)
    n_slots = t * TOP_K + N_EXPERTS * MOE_BLOCK
    n_blk = n_slots // MOE_BLOCK
    blk_start = jnp.arange(n_blk, dtype=I32) * MOE_BLOCK
    blk_expert = jnp.minimum(jnp.sum((pad_ends[None, :] <= blk_start[:, None]).astype(I32), axis=1),
                             N_EXPERTS - 1)
    tm = MOE_TM
    dest_tiles = dest.reshape(TOP_K, t // tm, tm).transpose(1, 0, 2).reshape(t // tm, TOP_K * tm)
    xs = _dispatch_call(counts, pad_starts.astype(I32), dest_tiles, h2.reshape((t,) + ROW_TILE), n_slots)
    y_slots = _expert_call(blk_expert, xs.reshape(n_slots * ROW_TILE[0], LANES), w_e_gate, w_e_up, w_e_down)
    return _combine_call(dest_tiles, y_slots, x1, route, mod6, s)


def kernel(x, c, positions, w_ada, b_ada, norm1_g, norm2_g, w_in, nsa_q_norm, nsa_k_norm, cmp_pe_k,
           cmp_w1_k, cmp_w2_k, cmp_pe_v, cmp_w1_v, cmp_w2_v, dil_q_norm, dil_k_norm, w_up_a, w_up_b,
           w_out, w_group, b_group, w_router, b_router, w_e_gate, w_e_up, w_e_down):
    b, s, d = x.shape
    assert w_ada.shape[0] == 1 and d == D_MODEL and s % NSA_TK == 0
    (x1, h2, route), mod6 = _mixer_and_router(
        x, c, positions, w_ada[0], b_ada[0], norm1_g[0], norm2_g[0], w_in[0], nsa_q_norm[0],
        nsa_k_norm[0], cmp_pe_k[0], cmp_w1_k[0], cmp_w2_k[0], cmp_pe_v[0], cmp_w1_v[0], cmp_w2_v[0],
        dil_q_norm[0], dil_k_norm[0], w_up_a[0], w_up_b[0], w_out[0], w_group[0], b_group[0],
        w_router[0], b_router[0])
    out = _moe(x1, h2, route, mod6, w_e_gate[0], w_e_up[0], w_e_down[0], s)
    return out.reshape(b, s, d)
```

```python
import functools

import jax
import jax.numpy as jnp
import numpy as np
from jax import lax
from jax.experimental import pallas as pl
from jax.experimental.pallas import tpu as pltpu

F32 = jnp.float32
BF16 = jnp.bfloat16
I32 = jnp.int32

D_MODEL = 1024
HEAD_DIM = 64
LANES = 128
ROW_TILE = (D_MODEL // LANES, LANES)
ROPE_THETA = 10000.0
EPS = 1e-6
LOG2E = 1.4426950408889634
NEG_INF = -1e30
FORCE_SCORE = 1e9
MASKED = -1e30
PICKED = -3e38

NSA_HEADS = 8
NSA_KV_HEADS = 2
NSA_GROUP = 4
CMP_BLOCK = 32
CMP_STRIDE = 16
CMP_HIDDEN = 256
SEL_BLOCK = 64
N_SEL = 8
N_LOCAL_SEL = 2
WINDOW = 512
DIL_PAIRS = ((128, 1), (512, 4), (2048, 16))
DIL_GROUPS = 3
A_Q = NSA_HEADS * HEAD_DIM
A_KV = NSA_KV_HEADS * HEAD_DIM
DIL_W = 2 * DIL_GROUPS * HEAD_DIM
N_GROUPS = 4
EXPERTS_PER_GROUP = 8
N_EXPERTS = 32
TOP_K = 2
ROUTE_ROWS = 8
EXPERT_FF = 512
MOE_BLOCK = 512

VMEM_LIMIT = 56 * 1024 * 1024

T_QA = 0
T_KVC = 4
T_KVA = 6
T_DIL = 10
T_GA = 19
T_GM = 20
N_TILES = 36

PROJ_TM = 512
NSA_TQ = 128
NSA_TK = 512
NSA_NQ = 4
DIL_T = 128
DIL_UNROLL = 4
MERGE_TM = 512
RANK_TM = 512
MOE_TM = 256
DMA_QUEUES = 2


def _dot(a, b):
    return jnp.dot(a, b, preferred_element_type=F32)


def _dot_nt(a, b):
    return lax.dot_general(a, b, (((1,), (1,)), ((), ())), preferred_element_type=F32)


def _dot_tn(a, b):
    return lax.dot_general(a, b, (((0,), (0,)), ((), ())), preferred_element_type=F32)


def _split(a):
    hi = a.astype(BF16)
    lo = (a - hi.astype(F32)).astype(BF16)
    return hi, lo


def _load_token_tiles(ref, first, n, lead=()):
    ns = ROW_TILE[0]
    return jnp.concatenate(
        [ref[lead + (pl.ds(first * ns + s, n, stride=ns), slice(None))] for s in range(ns)], axis=1)


def _store_token_tiles(ref, rows, first=0):
    ns = ROW_TILE[0]
    for s in range(ns):
        ref[pl.ds(first * ns + s, rows.shape[0], stride=ns), :] = rows[:, s * LANES:(s + 1) * LANES]


def _params(*sem):
    return pltpu.CompilerParams(dimension_semantics=sem, vmem_limit_bytes=VMEM_LIMIT)


def _mod_kernel(c_ref, whi_ref, wlo_ref, b_ref, o_ref):
    c = c_ref[...]
    ca = c * jax.nn.sigmoid(c)
    hi, lo = _split(ca)
    whi = whi_ref[...]
    o_ref[...] = _dot(hi, whi) + _dot(lo, whi) + _dot(hi, wlo_ref[...]) + b_ref[...]


def _mod_call(c, w_ada, b_ada):
    b, d = c.shape
    n = w_ada.shape[1]
    whi, wlo = _split(w_ada)
    tn = 1024
    return pl.pallas_call(
        _mod_kernel,
        grid=(n // tn,),
        in_specs=[
            pl.BlockSpec((b, d), lambda j: (0, 0)),
            pl.BlockSpec((d, tn), lambda j: (0, j)),
            pl.BlockSpec((d, tn), lambda j: (0, j)),
            pl.BlockSpec((1, tn), lambda j: (0, j)),
        ],
        out_specs=pl.BlockSpec((b, tn), lambda j: (0, j)),
        out_shape=jax.ShapeDtypeStruct((b, n), F32),
        compiler_params=_params("parallel"),
        name="mod",
    )(c, whi, wlo, b_ada.reshape(1, n))


def _norm_rope(a, gain, cos, sin_signed, lo, first):
    sq = a * a
    s0 = jnp.sum(jnp.where(lo, sq, 0.0), axis=-1, keepdims=True)
    s1 = jnp.sum(jnp.where(lo, 0.0, sq), axis=-1, keepdims=True)
    r = jnp.where(lo, lax.rsqrt(s0 * (1.0 / HEAD_DIM) + EPS), lax.rsqrt(s1 * (1.0 / HEAD_DIM) + EPS))
    y = a * r * gain
    rot = jnp.where(first, pltpu.roll(y, 96, 1), pltpu.roll(y, 32, 1))
    return y * cos + rot * sin_signed


def _proj_kernel(x_ref, mod_ref, g1_ref, w_ref, gains_ref, cos_ref, sin_ref, kw0_ref, vw0_ref,
                 qext_ref, kvc_ref, ksl_ref, vsl_ref, kwp_ref, vwp_ref, dil0_ref, dil1_ref, dil2_ref,
                 ga_ref, gm_ref, perm_scr):
    del kw0_ref, vw0_ref
    tm = x_ref.shape[1]
    x = x_ref[0]
    ms = jnp.mean(x * x, axis=-1, keepdims=True)
    y = x * lax.rsqrt(ms + EPS) * g1_ref[...]
    sh1 = mod_ref[0, 0:1, :]
    sc1 = mod_ref[0, 1:2, :]
    h = (y * (1.0 + sc1) + sh1).astype(BF16)
    lane = lax.broadcasted_iota(I32, (tm, LANES), 1)
    lo = lane < HEAD_DIM
    first = (lane & (HEAD_DIM - 1)) < (HEAD_DIM // 2)
    cos = jnp.concatenate([cos_ref[0]] * 4, axis=1)
    sin = jnp.concatenate([sin_ref[0]] * 4, axis=1)
    sin = jnp.where(first, -sin, sin)
    nr = functools.partial(_norm_rope, cos=cos, sin_signed=sin, lo=lo, first=first)

    for c in range(N_TILES // 2):
        acc = _dot(h, w_ref[:, c * 2 * LANES:(c + 1) * 2 * LANES])
        for half in range(2):
            t = 2 * c + half
            a = acc[:, half * LANES:(half + 1) * LANES]
            if t < T_KVC:
                yq = nr(a, gains_ref[0:1, :])
                rq = pltpu.roll(yq, HEAD_DIM, 1)
                if t // 2 == 0:
                    e0 = jnp.where(lo, yq, 0.0)
                    e1 = jnp.where(lo, rq, 0.0)
                else:
                    e0 = jnp.where(lo, 0.0, rq)
                    e1 = jnp.where(lo, 0.0, yq)
                qext_ref[0, 2 * t] = e0.astype(BF16)
                qext_ref[0, 2 * t + 1] = e1.astype(BF16)
            elif t < T_KVA:
                kvc_ref[0, t - T_KVC] = a
            elif t < T_DIL:
                j = t - T_KVA
                k_out, v_out = (ksl_ref, vsl_ref) if j < 2 else (kwp_ref, vwp_ref)
                if j % 2 == 0:
                    k_out[0] = nr(a, gains_ref[1:2, :]).astype(BF16)
                else:
                    v_out[0, 0] = jnp.where(lo, a, 1.0).astype(BF16)
                    v_out[0, 1] = jnp.where(lo, pltpu.roll(a, HEAD_DIM, 1), 1.0).astype(BF16)
            elif t < T_GA:
                j = t - T_DIL
                if j < 3:
                    v = nr(a, gains_ref[2:3, :])
                elif j < 6:
                    v = nr(a, gains_ref[3:4, :])
                else:
                    v = a
                kind, gi = divmod(j, DIL_GROUPS)
                dl = DIL_PAIRS[gi][1]
                if dl == 1:
                    dil0_ref[0, kind] = v.astype(BF16)
                else:
                    out = (dil0_ref, dil1_ref, dil2_ref)[gi]
                    perm_scr[...] = v
                    for r in range(dl):
                        out[0, kind, r] = perm_scr[pl.ds(r, tm // dl, stride=dl), :].astype(BF16)
            elif t < T_GM:
                ga_ref[0] = jax.nn.sigmoid(a)
            else:
                j = t - T_GM
                gm_ref[0, :, j * LANES:(j + 1) * LANES] = jax.nn.sigmoid(a).astype(BF16)


def _proj_call(x, mod6, g1, w_perm, gains, cos_t, sin_t):
    b, s, d = x.shape
    tm = PROJ_TM
    n = N_TILES * LANES
    row = lambda bi, i: (bi, i, 0)
    heads = lambda bi, i: (bi, 0, i, 0)
    pad = WINDOW // tm
    in_specs = [
        pl.BlockSpec((1, tm, d), row),
        pl.BlockSpec((1, 6, d), lambda bi, i: (bi, 0, 0)),
        pl.BlockSpec((1, d), lambda bi, i: (0, 0)),
        pl.BlockSpec((d, n), lambda bi, i: (0, 0)),
        pl.BlockSpec((4, LANES), lambda bi, i: (0, 0)),
        pl.BlockSpec((1, tm, HEAD_DIM // 2), row),
        pl.BlockSpec((1, tm, HEAD_DIM // 2), row),
        pl.BlockSpec(memory_space=pl.ANY),
        pl.BlockSpec(memory_space=pl.ANY),
    ]
    out_specs = [
        pl.BlockSpec((1, NSA_HEADS, tm, LANES), heads),
        pl.BlockSpec((1, 2, tm, LANES), heads),
        pl.BlockSpec((1, tm, LANES), row),
        pl.BlockSpec((1, NSA_KV_HEADS, tm, LANES), heads),
        pl.BlockSpec((1, tm, LANES), lambda bi, i: (bi, i + pad, 0)),
        pl.BlockSpec((1, NSA_KV_HEADS, tm, LANES), lambda bi, i: (bi, 0, i + pad, 0)),
        pl.BlockSpec((1, 3, tm, LANES), heads),
        *[pl.BlockSpec((1, 3, dl, tm // dl, LANES), lambda bi, i: (bi, 0, 0, i, 0)) for _, dl in DIL_PAIRS[1:]],
        pl.BlockSpec((1, tm, LANES), row),
        pl.BlockSpec((1, tm, 2 * d), row),
    ]
    out_shape = [
        jax.ShapeDtypeStruct((b, NSA_HEADS, s, LANES), BF16),
        jax.ShapeDtypeStruct((b, 2, s, LANES), F32),
        jax.ShapeDtypeStruct((b, s, LANES), BF16),
        jax.ShapeDtypeStruct((b, NSA_KV_HEADS, s, LANES), BF16),
        jax.ShapeDtypeStruct((b, s + WINDOW, LANES), BF16),
        jax.ShapeDtypeStruct((b, NSA_KV_HEADS, s + WINDOW, LANES), BF16),
        jax.ShapeDtypeStruct((b, 3, s, LANES), BF16),
        *[jax.ShapeDtypeStruct((b, 3, dl, s // dl, LANES), BF16) for _, dl in DIL_PAIRS[1:]],
        jax.ShapeDtypeStruct((b, s, LANES), F32),
        jax.ShapeDtypeStruct((b, s, 2 * d), BF16),
    ]
    kw0 = jnp.zeros(out_shape[4].shape, BF16)
    vw0 = jnp.zeros(out_shape[5].shape, BF16)
    return pl.pallas_call(
        _proj_kernel,
        grid=(b, s // tm),
        in_specs=in_specs,
        out_specs=out_specs,
        out_shape=out_shape,
        scratch_shapes=[pltpu.VMEM((tm, LANES), F32)],
        input_output_aliases={7: 4, 8: 5},
        compiler_params=_params("parallel", "parallel"),
        name="proj",
    )(x, mod6, g1, w_perm, gains, cos_t, sin_t, kw0, vw0)


def _cmp_kernel(x_ref, pe_ref, w1_ref, w2_ref, gain_ref, cos_ref, sin_ref, kc_ref, vc_ref):
    nb = kc_ref.shape[1]
    lane = lax.broadcasted_iota(I32, (nb, LANES), 1)
    lo = lane < HEAD_DIM
    first = (lane & (HEAD_DIM - 1)) < (HEAD_DIM // 2)
    for kind in range(2):
        top = jnp.zeros((nb, 2 * CMP_HIDDEN), F32)
        bot = jnp.zeros((nb, 2 * CMP_HIDDEN), F32)
        for r in range(CMP_STRIDE):
            x = x_ref[0, kind, pl.ds(r, nb, stride=CMP_STRIDE), :]
            top = top + _dot((x + pe_ref[kind, r:r + 1, :]).astype(BF16), w1_ref[kind, r])
            bot = bot + _dot((x + pe_ref[kind, CMP_STRIDE + r:CMP_STRIDE + r + 1, :]).astype(BF16),
                             w1_ref[kind, CMP_STRIDE + r])
        hid = top + pltpu.roll(bot, nb - 1, 0)
        hid = (hid * jax.nn.sigmoid(hid)).astype(BF16)
        out = (_dot(hid[:, 0:CMP_HIDDEN], w2_ref[kind, 0])
               + _dot(hid[:, CMP_HIDDEN:2 * CMP_HIDDEN], w2_ref[kind, 1]))
        if kind == 0:
            out = _norm_rope(out, gain_ref[...], cos_ref[0], sin_ref[0], lo, first)
            kc_ref[0] = out.astype(BF16)
        else:
            vc_ref[0, 0] = jnp.where(lo, out, 0.0).astype(BF16)
            vc_ref[0, 1] = jnp.where(lo, pltpu.roll(out, HEAD_DIM, 1), 0.0).astype(BF16)


def _cmp_call(kvc, pe, w1, w2ext, gain_k, ccos, csin):
    b, _, s, _ = kvc.shape
    nb = s // CMP_STRIDE
    return pl.pallas_call(
        _cmp_kernel,
        grid=(b,),
        in_specs=[
            pl.BlockSpec((1, 2, s, LANES), lambda bi: (bi, 0, 0, 0)),
            pl.BlockSpec(pe.shape, lambda bi: (0, 0, 0)),
            pl.BlockSpec(w1.shape, lambda bi: (0, 0, 0, 0)),
            pl.BlockSpec((2, 2, CMP_HIDDEN, LANES), lambda bi: (0, 0, 0, 0)),
            pl.BlockSpec((1, LANES), lambda bi: (0, 0)),
            pl.BlockSpec((1, nb, LANES), lambda bi: (bi, 0, 0)),
            pl.BlockSpec((1, nb, LANES), lambda bi: (bi, 0, 0)),
        ],
        out_specs=[
            pl.BlockSpec((1, nb, LANES), lambda bi: (bi, 0, 0)),
            pl.BlockSpec((1, NSA_KV_HEADS, nb, LANES), lambda bi: (bi, 0, 0, 0)),
        ],
        out_shape=[
            jax.ShapeDtypeStruct((b, nb, LANES), BF16),
            jax.ShapeDtypeStruct((b, NSA_KV_HEADS, nb, LANES), BF16),
        ],
        compiler_params=_params("parallel"),
        name="cmp",
    )(kvc, pe, w1, w2ext, gain_k, ccos, csin)


def _softmax_cols(s, exp=jnp.exp):
    p = exp(s - jnp.max(s, axis=0, keepdims=True))
    return p, jnp.sum(p, axis=0, keepdims=True)


def _block_max(s, offs, bs):
    m = None
    for j, off in enumerate(offs):
        mj = jnp.max(s[j * bs:(j + 1) * bs], axis=0, keepdims=True) + off
        m = mj if m is None else jnp.maximum(m, mj)
    return m


def _block_exp2(s, offs, bs, m):
    return jnp.concatenate(
        [jnp.exp2(s[j * bs:(j + 1) * bs] - (m - off)) for j, off in enumerate(offs)], axis=0)


def _nsa_kernel(q_ref, kc_ref, vc_ref, ksl_ref, vsl_ref, kw_ref, vw_ref, ga_ref, ov_ref, wb_ref,
                o_ref, m_scr, acc_scr, *s_scrs):
    tq = NSA_TQ
    tk = NSA_TK
    hd = HEAD_DIM
    cols = NSA_GROUP * tq
    tiles = range(NSA_NQ)
    s_buf = lambda u, c: s_scrs[2 * u + c % 2]
    n_blk = ov_ref.shape[0]
    n_chunks = ksl_ref.shape[1] // tk
    bpc = tk // SEL_BLOCK
    kh = pl.program_id(1)
    base = pl.multiple_of(pl.program_id(2) * (NSA_NQ * tq), NSA_NQ * tq)
    t0 = [base + u * tq for u in tiles]
    q4 = [q_ref[0, :, u * tq:(u + 1) * tq, :].reshape(cols, LANES) for u in tiles]
    lane_q = lax.broadcasted_iota(I32, (1, tq), 1)
    lane_q4 = lax.broadcasted_iota(I32, (1, cols), 1) & (tq - 1)
    per_head = lambda b: jnp.concatenate([b] * NSA_GROUP, axis=1)
    older_edge = wb_ref[0]
    causal_edge = wb_ref[1]
    nwin = WINDOW + tq
    n_wb = nwin // tq
    blk = lax.broadcasted_iota(I32, (n_blk, 1), 0)
    blk_f = blk.astype(F32)
    ov = ov_ref[...]

    s_cmp = [_dot_nt(kc_ref[0], q4[u]) for u in tiles]
    s_win = [_dot_nt(kw_ref[0, pl.ds(t0[u], nwin), :], q4[u]) for u in tiles]
    for u in tiles:
        s_buf(u, 0)[...] = _dot_nt(ksl_ref[0, 0:tk, :], q4[u])

    o_cmp, sel_bias = [], []
    for u in tiles:
        nb = s_cmp[u].shape[0]
        cmp_end = lax.broadcasted_iota(I32, (nb, 1), 0) * CMP_STRIDE + (CMP_BLOCK - 1)
        valid = cmp_end <= t0[u] + lane_q4
        e, den = _softmax_cols(jnp.where(valid, s_cmp[u], NEG_INF), jnp.exp2)
        p = jnp.where(valid, e / den, 0.0)
        o_cmp.append(_dot_tn(vc_ref[0, 0], p.astype(BF16))[0:hd])
        psum = p[:, 0:tq] + p[:, tq:2 * tq] + p[:, 2 * tq:3 * tq] + p[:, 3 * tq:4 * tq]
        p_hi, p_lo = _split(psum)
        imp = _dot(ov, p_hi) + _dot(ov, p_lo)
        rel = ((t0[u] + lane_q) >> 6) - blk
        forced = (blk == 0) | ((rel >= 0) & (rel < N_LOCAL_SEL))
        score = jnp.where(rel < 0, NEG_INF, jnp.where(forced, FORCE_SCORE, imp))
        bias = jnp.full((n_blk, tq), MASKED, F32)
        for _ in range(N_SEL):
            best = jnp.max(score, axis=0, keepdims=True)
            first = jnp.min(jnp.where(score == best, blk_f, float(n_blk)), axis=0, keepdims=True)
            pick = blk_f == first
            bias = jnp.where(pick, 0.0, bias)
            score = jnp.where(pick, PICKED, score)
        sel_bias.append(per_head(jnp.where(rel < 0, MASKED, bias)))

    o_win = []
    for u in tiles:
        sw = jnp.concatenate([s_win[u][0:tq] + per_head(older_edge), s_win[u][tq:nwin - tq],
                              s_win[u][nwin - tq:nwin] + per_head(causal_edge)], axis=0)
        w_offs = [jnp.where(t0[u] - WINDOW + j * tq >= 0, 0.0, MASKED) for j in range(n_wb)]
        pw = _block_exp2(sw, w_offs, tq, _block_max(sw, w_offs, tq))
        ow = _dot_tn(vw_ref[0, 0, pl.ds(t0[u], nwin), :], pw.astype(BF16))
        o_win.append(ow[0:hd] / ow[hd:hd + 1])

    diag = base // tk

    def chunk(c):
        if c + 1 < n_chunks:
            for u in tiles:
                s_buf(u, c + 1)[...] = _dot_nt(ksl_ref[0, (c + 1) * tk:(c + 2) * tk, :], q4[u])
        vb = vsl_ref[0, 0, c * tk:(c + 1) * tk, :]
        for u in tiles:
            own = pl.ds(pl.multiple_of(jnp.clip(t0[u] - c * tk, 0, tk - tq), tq), tq)
            buf = s_buf(u, c)
            buf[own, :] = buf[own, :] + per_head(jnp.where(c == diag, causal_edge, 0.0))
            sc = buf[...]
            offs = [sel_bias[u][c * bpc + j:c * bpc + j + 1] for j in range(bpc)]
            m_new = _block_max(sc, offs, SEL_BLOCK)
            if c == 0:
                acc_scr[u] = _dot_tn(vb, _block_exp2(sc, offs, SEL_BLOCK, m_new).astype(BF16))
            else:
                m_old = m_scr[u]
                m_new = jnp.maximum(m_old, m_new)
                pe = _block_exp2(sc, offs, SEL_BLOCK, m_new)
                acc_scr[u] = jnp.exp2(m_old - m_new) * acc_scr[u] + _dot_tn(vb, pe.astype(BF16))
            m_scr[u] = m_new

    chunk(0)
    for c in range(1, n_chunks):
        pl.when(c <= diag)(functools.partial(chunk, c))

    is0 = kh == 0
    for u in tiles:
        rows = slice(u * tq, (u + 1) * tq)
        o_slc = acc_scr[u, 0:hd, :] / acc_scr[u, hd:hd + 1, :]
        gat = ga_ref[0, rows, :].T
        heads = []
        for g in range(NSA_GROUP):
            c = slice(g * tq, (g + 1) * tq)
            og = jnp.zeros((hd, tq), F32)
            for gi, ob in enumerate((o_cmp[u], o_slc, o_win[u])):
                c0 = gi * NSA_HEADS + g
                c1 = c0 + NSA_GROUP
                gate = jnp.where(is0, gat[c0:c0 + 1, :], gat[c1:c1 + 1, :])
                og = og + gate * ob[:, c]
            heads.append(og)
        for pair in range(2):
            tile = jnp.concatenate([heads[2 * pair], heads[2 * pair + 1]], axis=0)
            o_ref[0, rows, pair * LANES:(pair + 1) * LANES] = tile.T.astype(BF16)


def _nsa_call(qext, kc, vc, ksl, vsl, kwp, vwp, ga, ov_t, wbias):
    b, _, s, _ = qext.shape
    tq = NSA_TQ * NSA_NQ
    nb = kc.shape[1]
    cols = NSA_GROUP * NSA_TQ
    assert NSA_TK % tq == 0
    shared = lambda rows: pl.BlockSpec((1, rows, LANES), lambda bi, k, i: (bi, 0, 0))
    per_kv = lambda rows: pl.BlockSpec((1, 1, rows, LANES), lambda bi, k, i: (bi, k, 0, 0))
    return pl.pallas_call(
        _nsa_kernel,
        grid=(b, NSA_KV_HEADS, s // tq),
        in_specs=[
            pl.BlockSpec((1, NSA_GROUP, tq, LANES), lambda bi, k, i: (bi, k, i, 0)),
            shared(nb), per_kv(nb),
            shared(s), per_kv(s),
            shared(s + WINDOW), per_kv(s + WINDOW),
            pl.BlockSpec((1, tq, LANES), lambda bi, k, i: (bi, i, 0)),
            pl.BlockSpec(ov_t.shape, lambda bi, k, i: (0, 0)),
            pl.BlockSpec(wbias.shape, lambda bi, k, i: (0, 0, 0)),
        ],
        out_specs=pl.BlockSpec((1, tq, 2 * LANES), lambda bi, k, i: (bi, i, k)),
        out_shape=jax.ShapeDtypeStruct((b, s, A_Q), BF16),
        scratch_shapes=[
            pltpu.VMEM((NSA_NQ, 1, cols), F32),
            pltpu.VMEM((NSA_NQ, LANES, cols), F32),
            *[pltpu.VMEM((NSA_TK, cols), F32)] * (2 * NSA_NQ),
        ],
        compiler_params=_params("parallel", "parallel", "parallel"),
        name="nsa",
    )(qext, kc, vc, ksl, vsl, kwp, vwp, ga, ov_t, wbias)


def _dil_kernel(q_ref, k_ref, v_ref, o_ref, lse_ref, *, seg_tiles):
    t = DIL_T
    lo = lax.broadcasted_iota(I32, (t, LANES), 1) < HEAD_DIM
    kj = lax.broadcasted_iota(I32, (t, 2 * t), 0)
    qi = lax.broadcasted_iota(I32, (t, 2 * t), 1) & (t - 1)
    top = lax.broadcasted_iota(I32, (LANES, t), 0) < HEAD_DIM

    def scores(i):
        has_prev = (i & (seg_tiles - 1)) != 0
        cs = pl.multiple_of(i * t, t)
        ps = pl.multiple_of(jnp.maximum(i - 1, 0) * t, t)
        q = q_ref[0, 0, pl.ds(cs, t), :]
        zero = jnp.zeros_like(q)
        q2 = jnp.concatenate([jnp.where(lo, q, zero), jnp.where(lo, zero, q)], axis=0)
        sp = _dot_nt(k_ref[0, 0, pl.ds(ps, t), :], q2)
        sc = _dot_nt(k_ref[0, 0, pl.ds(cs, t), :], q2)
        return has_prev, ps, cs, sp, sc

    def attend(has_prev, ps, cs, sp, sc):
        sp = jnp.where((kj >= qi) & has_prev, sp, MASKED)
        sc = jnp.where(qi >= kj, sc, MASKED)
        m = jnp.maximum(jnp.max(sp, axis=0, keepdims=True), jnp.max(sc, axis=0, keepdims=True))
        pp = jnp.exp(sp - m)
        pc = jnp.exp(sc - m)
        l = jnp.sum(pp, axis=0, keepdims=True) + jnp.sum(pc, axis=0, keepdims=True)
        o = (_dot_tn(v_ref[0, 0, pl.ds(ps, t), :], pp.astype(BF16))
             + _dot_tn(v_ref[0, 0, pl.ds(cs, t), :], pc.astype(BF16))) / l
        lse = m + jnp.log(l)
        o_ref[0, pl.ds(cs, t), :] = jnp.where(top, o[:, 0:t], o[:, t:2 * t]).T
        lse_ref[0, pl.ds(cs, t), :] = jnp.where(top, lse[:, 0:t], lse[:, t:2 * t]).T

    def tiles(j, carry):
        group = [scores(j * DIL_UNROLL + u) for u in range(DIL_UNROLL)]
        for args in group:
            attend(*args)
        return carry

    lax.fori_loop(0, q_ref.shape[2] // (t * DIL_UNROLL), tiles, 0)


def _dil_call(qkv, dl):
    b, _, s, _ = qkv.shape
    kind = lambda j: pl.BlockSpec((1, 1, s, LANES), lambda bi: (bi, j, 0, 0))
    out = pl.BlockSpec((1, s, LANES), lambda bi: (bi, 0, 0))
    return pl.pallas_call(
        functools.partial(_dil_kernel, seg_tiles=s // dl // DIL_T),
        grid=(b,),
        in_specs=[kind(0), kind(1), kind(2)],
        out_specs=[out, out],
        out_shape=[jax.ShapeDtypeStruct((b, s, LANES), F32)] * 2,
        compiler_params=_params("parallel"),
        name=f"dil{dl}",
    )(qkv, qkv, qkv)


def _merge_kernel(x_ref, oa_ref, od0_ref, od1_ref, od2_ref, ls0_ref, ls1_ref, ls2_ref, gm_ref, mod_ref,
                  wa_ref, wb_ref, wo_ref, g2_ref, rcat_ref, rb_ref, x1_ref, h2_ref, route_ref, perm_scr):
    tm = x_ref.shape[0]
    d = x_ref.shape[1]
    hm = tm // 2
    for n, ref in enumerate((od1_ref, od2_ref, ls1_ref, ls2_ref)):
        dl = ref.shape[1]
        for r in range(dl):
            perm_scr[n, pl.ds(r, tm // dl, stride=dl), :] = ref[0, r]

    def up(rows):
        l0, l1, l2 = ls0_ref[rows], perm_scr[2, rows], perm_scr[3, rows]
        mx = jnp.maximum(jnp.maximum(l0, l1), l2)
        e0, e1, e2 = jnp.exp(l0 - mx), jnp.exp(l1 - mx), jnp.exp(l2 - mx)
        den = e0 + e1 + e2
        ob = jnp.concatenate([(od0_ref[rows] * (e0 / den)).astype(BF16),
                              (perm_scr[0, rows] * (e1 / den)).astype(BF16),
                              (perm_scr[1, rows] * (e2 / den)).astype(BF16)], axis=1)
        return _dot(oa_ref[rows], wa_ref[...]), _dot(ob, wb_ref[...])

    def out_proj(rows, ya, yb):
        y = gm_ref[rows, 0:d].astype(F32) * ya + gm_ref[rows, d:2 * d].astype(F32) * yb
        return _dot(y.astype(BF16), wo_ref[...])

    def residual_and_logits(h, rows, z):
        x1 = x_ref[rows] + mod_ref[0, 2:3, :] * z
        x1_ref[rows] = x1
        ms = jnp.mean(x1 * x1, axis=-1, keepdims=True)
        h2 = x1 * lax.rsqrt(ms + EPS) * g2_ref[...]
        h2 = h2 * (1.0 + mod_ref[0, 4:5, :]) + mod_ref[0, 3:4, :]
        _store_token_tiles(h2_ref, h2, first=h * hm)
        hi, lo = _split(h2)
        both = _dot_nt(rcat_ref[...], hi)
        return both[0:LANES] + both[LANES:2 * LANES] + _dot_nt(rcat_ref[0:LANES, :], lo) + rb_ref[...]

    def route(h, logits):
        row = lax.broadcasted_iota(I32, (LANES, hm), 0).astype(F32)
        gl = jnp.where(row < N_GROUPS, logits, NEG_INF)
        gmax = jnp.max(gl, axis=0, keepdims=True)
        g_w = 1.0 / jnp.sum(jnp.exp(gl - gmax), axis=0, keepdims=True)
        g_idx = jnp.min(jnp.where(gl == gmax, row, float(LANES)), axis=0, keepdims=True)
        e_lo = N_GROUPS + EXPERTS_PER_GROUP * g_idx
        el = jnp.where((row >= e_lo) & (row < e_lo + EXPERTS_PER_GROUP), logits, NEG_INF)
        m1 = jnp.max(el, axis=0, keepdims=True)
        i1 = jnp.min(jnp.where(el == m1, row, float(LANES)), axis=0, keepdims=True)
        el2 = jnp.where(row == i1, PICKED, el)
        m2 = jnp.max(el2, axis=0, keepdims=True)
        i2 = jnp.min(jnp.where(el2 == m2, row, float(LANES)), axis=0, keepdims=True)
        ex = jnp.exp(m2 - m1)
        w1 = g_w * (1.0 / (1.0 + ex))
        w2 = g_w * (ex / (1.0 + ex))
        route_ref[:, h * hm:(h + 1) * hm] = jnp.concatenate(
            [i1 - N_GROUPS, i2 - N_GROUPS, w1, w2, jnp.zeros((ROUTE_ROWS - 4, hm), F32)], axis=0)

    halves = (slice(0, hm), slice(hm, tm))
    ups = [up(rows) for rows in halves]
    zs = [out_proj(rows, *u) for rows, u in zip(halves, ups)]
    logits = [residual_and_logits(h, rows, z) for h, (rows, z) in enumerate(zip(halves, zs))]
    for h, lg in enumerate(logits):
        route(h, lg)


def _merge_call(x2, oa, ods, lses, gm, mod6, wa, wb, wo, g2, rcat, rb, s):
    t, d = x2.shape
    b = t // s
    tm = MERGE_TM
    per_b = s // tm
    row = lambda i: (i, 0)
    const = lambda i: (0, 0)

    def by_class(dl):
        return pl.BlockSpec((1, dl, tm // dl, LANES), lambda i: (i // per_b, 0, i % per_b, 0))

    dls = [dl for _, dl in DIL_PAIRS]
    assert dls[0] == 1
    group_specs = [pl.BlockSpec((tm, LANES), row)] + [by_class(dl) for dl in dls[1:]]
    views = lambda arrs: [arrs[0].reshape(t, LANES)] + [
        a.reshape(b, dl, s // dl, LANES) for a, dl in zip(arrs[1:], dls[1:])]
    return pl.pallas_call(
        _merge_kernel,
        grid=(t // tm,),
        in_specs=[
            pl.BlockSpec((tm, d), row),
            pl.BlockSpec((tm, A_Q), row),
            *group_specs, *group_specs,
            pl.BlockSpec((tm, 2 * d), row),
            pl.BlockSpec((1, 6, d), lambda i: (i // per_b, 0, 0)),
            pl.BlockSpec(wa.shape, const),
            pl.BlockSpec(wb.shape, const),
            pl.BlockSpec(wo.shape, const),
            pl.BlockSpec((1, d), const),
            pl.BlockSpec(rcat.shape, const),
            pl.BlockSpec((LANES, 1), const),
        ],
        out_specs=[
            pl.BlockSpec((tm, d), row),
            pl.BlockSpec((tm * ROW_TILE[0], LANES), row),
            pl.BlockSpec((ROUTE_ROWS, tm), lambda i: (0, i)),
        ],
        out_shape=[
            jax.ShapeDtypeStruct((t, d), F32),
            jax.ShapeDtypeStruct((t * ROW_TILE[0], LANES), F32),
            jax.ShapeDtypeStruct((ROUTE_ROWS, t), F32),
        ],
        scratch_shapes=[pltpu.VMEM((2 * (len(dls) - 1), tm, LANES), F32)],
        compiler_params=_params("parallel"),
        name="merge",
    )(x2, oa, *views(ods), *views(lses), gm, mod6, wa, wb, wo, g2, rcat, rb)


def _expert_hits(route_ref, tm):
    row = lax.broadcasted_iota(I32, (N_EXPERTS, tm), 0).astype(F32)
    return [row == route_ref[k:k + 1, :] for k in range(TOP_K)]


def _rank_kernel(route_ref, rank_ref, count_ref, carry_scr):
    tm = route_ref.shape[1]

    @pl.when(pl.program_id(0) == 0)
    def _():
        carry_scr[...] = jnp.zeros(carry_scr.shape, F32)

    hits = _expert_hits(route_ref, tm)
    cnt = jnp.where(hits[0] | hits[1], 1.0, 0.0)
    r = lax.broadcasted_iota(I32, (tm, tm), 0)
    c = lax.broadcasted_iota(I32, (tm, tm), 1)
    earlier = jnp.where(r < c, 1.0, 0.0).astype(BF16)
    before = _dot(cnt.astype(BF16), earlier) + carry_scr[:, 0:1]
    ranks = [jnp.sum(jnp.where(hit, before, 0.0), axis=0, keepdims=True) for hit in hits]
    rank_ref[...] = jnp.concatenate(ranks + [jnp.zeros((ROUTE_ROWS - TOP_K, tm), F32)], axis=0)
    carry_scr[...] = carry_scr[...] + jnp.sum(cnt, axis=1, keepdims=True)
    count_ref[...] = carry_scr[...]


def _rank_call(route):
    t = route.shape[1]
    tm = RANK_TM
    tile = pl.BlockSpec((ROUTE_ROWS, tm), lambda i: (0, i))
    return pl.pallas_call(
        _rank_kernel,
        grid=(t // tm,),
        in_specs=[tile],
        out_specs=[tile, pl.BlockSpec((N_EXPERTS, LANES), lambda i: (0, 0))],
        out_shape=[
            jax.ShapeDtypeStruct((ROUTE_ROWS, t), F32),
            jax.ShapeDtypeStruct((N_EXPERTS, LANES), F32),
        ],
        scratch_shapes=[pltpu.VMEM((N_EXPERTS, LANES), F32)],
        compiler_params=_params("arbitrary"),
        name="rank",
    )(route)


def _slot_kernel(route_ref, rank_ref, start_ref, slot_ref):
    tm = route_ref.shape[1]
    starts = start_ref[...]
    slots = [jnp.sum(jnp.where(hit, starts, 0.0), axis=0, keepdims=True) + rank_ref[k:k + 1, :]
             for k, hit in enumerate(_expert_hits(route_ref, tm))]
    slot_ref[...] = jnp.concatenate(slots + [jnp.zeros((ROUTE_ROWS - TOP_K, tm), F32)], axis=0)


def _slot_call(route, rank, pad_starts):
    t = route.shape[1]
    tm = RANK_TM
    tile = pl.BlockSpec((ROUTE_ROWS, tm), lambda i: (0, i))
    return pl.pallas_call(
        _slot_kernel,
        grid=(t // tm,),
        in_specs=[tile, tile, pl.BlockSpec((N_EXPERTS, tm), lambda i: (0, 0))],
        out_specs=tile,
        out_shape=jax.ShapeDtypeStruct((ROUTE_ROWS, t), F32),
        compiler_params=_params("parallel"),
        name="slots",
    )(route, rank, jnp.broadcast_to(pad_starts[:, None], (N_EXPERTS, tm)))


def _for_static_slot(slot, fn):
    for par in range(2):
        pl.when(slot == par)(functools.partial(fn, par))


def _dispatch_kernel(cnt_ref, pstart_ref, dest_hbm, h_hbm, xs_hbm, idx_smem, hbuf, zbuf,
                     isem, hsem, dsem, zsem):
    tm = MOE_TM
    nd = TOP_K * tm
    i = pl.program_id(0)
    n = pl.num_programs(0)
    slot = i % 2
    n_slots = xs_hbm.shape[0]

    def idx_copy(step, sl):
        return pltpu.make_async_copy(dest_hbm.at[step], idx_smem.at[sl], isem.at[sl])

    def tile_copy(step, bf):
        rows = pl.ds(pl.multiple_of(step * tm, tm), tm)
        return pltpu.make_async_copy(h_hbm.at[rows], hbuf.at[bf], hsem.at[bf])

    def wait_scatter(bf):
        for _ in range(TOP_K):
            pltpu.make_async_copy(hbuf.at[bf], xs_hbm.at[pl.ds(0, tm)], dsem.at[bf]).wait()

    @pl.when(i == 0)
    def _():
        idx_copy(0, 0).start()
        tile_copy(0, 0).start()
        zbuf[...] = jnp.zeros(zbuf.shape, F32)

        def zero_gaps(wait):
            def per_expert(e, used):
                cnt = cnt_ref[e]
                gap = (MOE_BLOCK - cnt % MOE_BLOCK) % MOE_BLOCK
                row = pstart_ref[e] + cnt
                for k in range(MOE_BLOCK.bit_length() - 1):
                    size = 1 << k
                    copy = pltpu.make_async_copy(zbuf.at[pl.ds(0, size)], xs_hbm.at[pl.ds(row, size)], zsem)
                    pl.when(((gap >> k) & 1) == 1)(copy.wait if wait else copy.start)
                    row = row + (gap & size)
                return used + cnt + gap
            return lax.fori_loop(0, N_EXPERTS, per_expert, 0)
        zero_gaps(wait=False)
        used = zero_gaps(wait=True)

        def zero_block(blk):
            rows = pl.ds(pl.multiple_of(blk * MOE_BLOCK, MOE_BLOCK), MOE_BLOCK)
            return pltpu.make_async_copy(zbuf, xs_hbm.at[rows], zsem)

        def start_blk(blk, c):
            zero_block(blk).start()
            return c

        def wait_blk(blk, c):
            zero_block(blk).wait()
            return c
        lax.fori_loop(used // MOE_BLOCK, n_slots // MOE_BLOCK, start_blk, 0)
        lax.fori_loop(used // MOE_BLOCK, n_slots // MOE_BLOCK, wait_blk, 0)

    @pl.when(i > 0)
    def _():
        wait_scatter(1 - slot)

    @pl.when(i + 1 < n)
    def _():
        idx_copy(i + 1, 1 - slot).start()
        tile_copy(i + 1, 1 - slot).start()

    idx_copy(i, slot).wait()
    tile_copy(i, slot).wait()

    def scatter_rows(par):
        for r in range(nd):
            copy = pltpu.make_async_copy(hbuf.at[par, r % tm], xs_hbm.at[idx_smem[par, r]], dsem.at[par])
            copy.start(priority=r % DMA_QUEUES)
    _for_static_slot(slot, scatter_rows)

    @pl.when(i == n - 1)
    def _():
        wait_scatter(slot)


def _dispatch_call(counts, pad_starts, dest_tiles, h2, n_slots):
    n_steps = dest_tiles.shape[0]
    grid_spec = pltpu.PrefetchScalarGridSpec(
        num_scalar_prefetch=2,
        grid=(n_steps,),
        in_specs=[pl.BlockSpec(memory_space=pl.ANY), pl.BlockSpec(memory_space=pl.ANY)],
        out_specs=pl.BlockSpec(memory_space=pl.ANY),
        scratch_shapes=[
            pltpu.SMEM((2, TOP_K * MOE_TM), I32),
            pltpu.VMEM((2, MOE_TM) + ROW_TILE, F32),
            pltpu.VMEM((MOE_BLOCK,) + ROW_TILE, F32),
            pltpu.SemaphoreType.DMA((2,)),
            pltpu.SemaphoreType.DMA((2,)),
            pltpu.SemaphoreType.DMA((2,)),
            pltpu.SemaphoreType.DMA(()),
        ],
    )
    return pl.pallas_call(
        _dispatch_kernel,
        grid_spec=grid_spec,
        out_shape=jax.ShapeDtypeStruct((n_slots,) + ROW_TILE, F32),
        compiler_params=_params("arbitrary"),
        name="dispatch",
    )(counts, pad_starts, dest_tiles, h2)


def _expert_kernel(be_ref, x_ref, wg_ref, wu_ref, wd_ref, y_ref, wg_scr, wu_scr, wd_scr):
    i = pl.program_id(0)

    @pl.when((i == 0) | (be_ref[i] != be_ref[jnp.maximum(i - 1, 0)]))
    def _():
        wg_scr[...] = wg_ref[0].astype(BF16)
        wu_scr[...] = wu_ref[0].astype(BF16)
        wd_scr[...] = wd_ref[0].astype(BF16)

    xb = _load_token_tiles(x_ref, 0, MOE_BLOCK).astype(BF16)
    gate = _dot(xb, wg_scr[...])
    up = _dot(xb, wu_scr[...])
    hid = (gate * jax.nn.sigmoid(gate) * up).astype(BF16)
    _store_token_tiles(y_ref, _dot(hid, wd_scr[...]))


def _expert_call(blk_expert, xs, wg, wu, wd):
    n_blk = blk_expert.shape[0]
    d = wg.shape[1]
    blk = (MOE_BLOCK * ROW_TILE[0], LANES)
    grid_spec = pltpu.PrefetchScalarGridSpec(
        num_scalar_prefetch=1,
        grid=(n_blk,),
        in_specs=[
            pl.BlockSpec(blk, lambda i, be: (i, 0)),
            pl.BlockSpec((1, d, EXPERT_FF), lambda i, be: (be[i], 0, 0)),
            pl.BlockSpec((1, d, EXPERT_FF), lambda i, be: (be[i], 0, 0)),
            pl.BlockSpec((1, EXPERT_FF, d), lambda i, be: (be[i], 0, 0)),
        ],
        out_specs=pl.BlockSpec(blk, lambda i, be: (i, 0)),
        scratch_shapes=[
            pltpu.VMEM((d, EXPERT_FF), BF16),
            pltpu.VMEM((d, EXPERT_FF), BF16),
            pltpu.VMEM((EXPERT_FF, d), BF16),
        ],
    )
    return pl.pallas_call(
        _expert_kernel,
        grid_spec=grid_spec,
        out_shape=jax.ShapeDtypeStruct(xs.shape, F32),
        compiler_params=_params("arbitrary"),
        name="experts",
    )(blk_expert, xs, wg, wu, wd)


def _combine_kernel(dest_hbm, y_hbm, x1_ref, route_ref, mod_ref, o_ref, idx_smem, ybuf, isem, dsem):
    tm = MOE_TM
    nd = TOP_K * tm
    i = pl.program_id(0)
    n = pl.num_programs(0)
    slot = i % 2
    nxt = 1 - slot

    def idx_copy(step, sl):
        return pltpu.make_async_copy(dest_hbm.at[step], idx_smem.at[sl], isem.at[sl])

    ns = ROW_TILE[0]

    def step_rows(sl):
        return pltpu.make_async_copy(y_hbm.at[pl.ds(0, nd * ns)], ybuf.at[sl], dsem.at[sl])

    def issue_rows(par):
        for r in range(nd):
            src = y_hbm.at[pl.ds(pl.multiple_of(idx_smem[par, r] * ns, ns), ns)]
            copy = pltpu.make_async_copy(src, ybuf.at[par, pl.ds(r * ns, ns)], dsem.at[par])
            copy.start(priority=r % DMA_QUEUES)

    @pl.when(i == 0)
    def _():
        idx_copy(0, 0).start()
        idx_copy(0, 0).wait()
        issue_rows(0)

        @pl.when(n > 1)
        def _():
            idx_copy(1, 1).start()

    @pl.when(i + 1 < n)
    def _():
        idx_copy(i + 1, nxt).wait()
        _for_static_slot(nxt, issue_rows)

    @pl.when(i + 2 < n)
    def _():
        idx_copy(i + 2, slot).start()

    step_rows(slot).wait()
    route = jnp.concatenate([route_ref[...], jnp.zeros((LANES - ROUTE_ROWS, tm), F32)], axis=0).T
    w1 = route[:, 2:3]
    w2 = route[:, 3:4]
    y = (w1 * _load_token_tiles(ybuf, 0, tm, lead=(slot,))
         + w2 * _load_token_tiles(ybuf, tm, tm, lead=(slot,)))
    o_ref[...] = x1_ref[...] + mod_ref[0, 5:6, :] * y


def _combine_call(dest_tiles, y_slots, x1, route, mod6, s):
    t, d = x1.shape
    tm = MOE_TM
    per_b = s // tm
    return pl.pallas_call(
        _combine_kernel,
        grid=(t // tm,),
        in_specs=[
            pl.BlockSpec(memory_space=pl.ANY),
            pl.BlockSpec(memory_space=pl.ANY),
            pl.BlockSpec((tm, d), lambda i: (i, 0)),
            pl.BlockSpec((ROUTE_ROWS, tm), lambda i: (0, i)),
            pl.BlockSpec((1, 6, d), lambda i: (i // per_b, 0, 0)),
        ],
        out_specs=pl.BlockSpec((tm, d), lambda i: (i, 0)),
        out_shape=jax.ShapeDtypeStruct((t, d), F32),
        scratch_shapes=[
            pltpu.SMEM((2, TOP_K * tm), I32),
            pltpu.VMEM((2, TOP_K * tm * ROW_TILE[0], LANES), F32),
            pltpu.SemaphoreType.DMA((2,)),
            pltpu.SemaphoreType.DMA((2,)),
        ],
        compiler_params=_params("arbitrary"),
        name="combine",
    )(dest_tiles, y_slots, x1, route, mod6)


def _rope_tables(pos):
    inv = ROPE_THETA ** (-jnp.arange(0, HEAD_DIM, 2, dtype=F32) / HEAD_DIM)
    ang = pos.astype(F32)[..., None] * inv
    return jnp.cos(ang), jnp.sin(ang)


def _rope_tiles(pos):
    cos, sin = _rope_tables(pos)
    return (jnp.concatenate([cos, cos, cos, cos], axis=-1),
            jnp.concatenate([-sin, sin, -sin, sin], axis=-1))


def _selection_constants(s):
    n_c = s // CMP_STRIDE
    n_s = s // SEL_BLOCK
    cs = np.arange(n_c) * CMP_STRIDE
    ss = np.arange(n_s) * SEL_BLOCK
    ov = np.clip(np.minimum(cs[:, None] + CMP_BLOCK, ss[None, :] + SEL_BLOCK)
                 - np.maximum(cs[:, None], ss[None, :]), 0, None).astype(np.float32) / CMP_BLOCK
    kq = np.arange(NSA_TQ)[:, None] - np.arange(NSA_TQ)[None, :]
    wbias = np.stack([np.where(kq >= 0, 0.0, MASKED), np.where(kq <= 0, 0.0, MASKED)]).astype(np.float32)
    return jnp.asarray(ov.T, BF16), jnp.asarray(wbias)


def _mixer_and_router(x, c, positions, w_ada, b_ada, norm1_g, norm2_g, w_in, nsa_q_norm, nsa_k_norm,
                      cmp_pe_k, cmp_w1_k, cmp_w2_k, cmp_pe_v, cmp_w1_v, cmp_w2_v, dil_q_norm,
                      dil_k_norm, w_up_a, w_up_b, w_out, w_group, b_group, w_router, b_router):
    b, s, d = x.shape
    scale = HEAD_DIM ** -0.5
    mod6 = _mod_call(c, w_ada, b_ada).reshape(b, 6, d)

    c1 = A_Q
    c2 = c1 + 6 * A_KV
    c3 = c2 + 3 * NSA_HEADS
    c4 = c3 + 3 * DIL_W
    w_perm = jnp.concatenate([
        w_in[:, :c2], w_in[:, c3:c4],
        jnp.pad(w_in[:, c2:c3], ((0, 0), (0, LANES - 3 * NSA_HEADS))), w_in[:, c4:]], axis=1).astype(BF16)
    two = lambda g: jnp.concatenate([g, g]).astype(F32)
    gains = jnp.stack([two(nsa_q_norm) * (scale * LOG2E), two(nsa_k_norm), two(dil_q_norm) * scale,
                       two(dil_k_norm)])
    cos_t, sin_t = _rope_tables(positions)
    qext, kvc, ksl, vsl, kwp, vwp, dil0, dil1, dil2, ga, gm = _proj_call(
        x, mod6, norm1_g.reshape(1, d), w_perm, gains, cos_t, sin_t)

    pe = jnp.stack([jnp.concatenate([p, p], axis=1) for p in (cmp_pe_k, cmp_pe_v)])
    zero_w = jnp.zeros((CMP_BLOCK, HEAD_DIM, CMP_HIDDEN), F32)

    def per_position(w):
        w = w.reshape(CMP_BLOCK, HEAD_DIM, CMP_HIDDEN)
        return jnp.concatenate([jnp.concatenate([w, zero_w], axis=2),
                                jnp.concatenate([zero_w, w], axis=2)], axis=1)
    w1 = jnp.stack([per_position(cmp_w1_k), per_position(cmp_w1_v)]).astype(BF16)
    zeros = jnp.zeros((CMP_HIDDEN, HEAD_DIM), F32)
    ext = lambda w: jnp.stack([jnp.concatenate([w, zeros], 1), jnp.concatenate([zeros, w], 1)])
    w2ext = jnp.stack([ext(cmp_w2_k), ext(cmp_w2_v)]).astype(BF16)
    cmp_pos = jnp.pad(positions[:, CMP_BLOCK - 1::CMP_STRIDE], ((0, 0), (0, 1)))
    ccos, csin = _rope_tiles(cmp_pos)
    kc, vc = _cmp_call(kvc, pe, w1, w2ext, two(nsa_k_norm).reshape(1, LANES), ccos, csin)

    ov_t, wbias = _selection_constants(s)
    o_a = _nsa_call(qext, kc, vc, ksl, vsl, kwp, vwp, ga, ov_t, wbias)

    ods, lses = zip(*[_dil_call(qkv.reshape(b, 3, s, LANES), dl)
                      for qkv, (_, dl) in zip((dil0, dil1, dil2), DIL_PAIRS)])

    t = b * s
    w_r = jnp.concatenate([w_group, w_router.transpose(1, 0, 2).reshape(d, N_EXPERTS)], axis=1)
    w_r = jnp.pad(w_r, ((0, 0), (0, LANES - w_r.shape[1])))
    rcat = jnp.concatenate(_split(w_r.T), axis=0)
    rb = jnp.pad(jnp.concatenate([b_group, b_router.reshape(-1)]), (0, LANES - N_GROUPS - N_EXPERTS))
    return _merge_call(
        x.reshape(t, d), o_a.reshape(t, A_Q), ods, lses, gm.reshape(t, 2 * d), mod6,
        w_up_a.astype(BF16), w_up_b.astype(BF16), w_out.astype(BF16), norm2_g.reshape(1, d),
        rcat, rb.reshape(LANES, 1).astype(F32), s), mod6


def _moe(x1, h2, route, mod6, w_e_gate, w_e_up, w_e_down, s):
    t, _ = x1.shape
    rank, counts = _rank_call(route)
    counts = counts[:, 0].astype(I32)
    padded = (counts + MOE_BLOCK - 1) // MOE_BLOCK * MOE_BLOCK
    pad_ends = jnp.cumsum(padded)
    pad_starts = pad_ends - padded
    dest = _slot_call(route, rank, pad_starts.astype(F32))[0:TOP_K].astype(I32)
    n_slots = t * TOP_K + N_EXPERTS * MOE_BLOCK
    n_blk = n_slots // MOE_BLOCK
    blk_start = jnp.arange(n_blk, dtype=I32) * MOE_BLOCK
    blk_expert = jnp.minimum(jnp.sum((pad_ends[None, :] <= blk_start[:, None]).astype(I32), axis=1),
                             N_EXPERTS - 1)
    tm = MOE_TM
    dest_tiles = dest.reshape(TOP_K, t // tm, tm).transpose(1, 0, 2).reshape(t // tm, TOP_K * tm)
    xs = _dispatch_call(counts, pad_starts.astype(I32), dest_tiles, h2.reshape((t,) + ROW_TILE), n_slots)
    y_slots = _expert_call(blk_expert, xs.reshape(n_slots * ROW_TILE[0], LANES), w_e_gate, w_e_up, w_e_down)
    return _combine_call(dest_tiles, y_slots, x1, route, mod6, s)


def kernel(x, c, positions, w_ada, b_ada, norm1_g, norm2_g, w_in, nsa_q_norm, nsa_k_norm, cmp_pe_k,
           cmp_w1_k, cmp_w2_k, cmp_pe_v, cmp_w1_v, cmp_w2_v, dil_q_norm, dil_k_norm, w_up_a, w_up_b,
           w_out, w_group, b_group, w_router, b_router, w_e_gate, w_e_up, w_e_down):
    b, s, d = x.shape
    assert w_ada.shape[0] == 1 and d == D_MODEL and s % NSA_TK == 0
    (x1, h2, route), mod6 = _mixer_and_router(
        x, c, positions, w_ada[0], b_ada[0], norm1_g[0], norm2_g[0], w_in[0], nsa_q_norm[0],
        nsa_k_norm[0], cmp_pe_k[0], cmp_w1_k[0], cmp_w2_k[0], cmp_pe_v[0], cmp_w1_v[0], cmp_w2_v[0],
        dil_q_norm[0], dil_k_norm[0], w_up_a[0], w_up_b[0], w_out[0], w_group[0], b_group[0],
        w_router[0], b_router[0])
    out = _moe(x1, h2, route, mod6, w_e_gate[0], w_e_up[0], w_e_down[0], s)
    return out.reshape(b, s, d)
```

```python
import functools

import jax
import jax.numpy as jnp
import numpy as np
from jax import lax
from jax.experimental import pallas as pl
from jax.experimental.pallas import tpu as pltpu

F32 = jnp.float32
BF16 = jnp.bfloat16
I32 = jnp.int32

D_MODEL = 1024
HEAD_DIM = 64
LANES = 128
ROW_TILE = (D_MODEL // LANES, LANES)
ROPE_THETA = 10000.0
EPS = 1e-6
LOG2E = 1.4426950408889634
NEG_INF = -1e30
FORCE_SCORE = 1e9
MASKED = -1e30
PICKED = -3e38

NSA_HEADS = 8
NSA_KV_HEADS = 2
NSA_GROUP = 4
CMP_BLOCK = 32
CMP_STRIDE = 16
CMP_HIDDEN = 256
SEL_BLOCK = 64
N_SEL = 8
N_LOCAL_SEL = 2
WINDOW = 512
DIL_PAIRS = ((128, 1), (512, 4), (2048, 16))
DIL_GROUPS = 3
A_Q = NSA_HEADS * HEAD_DIM
A_KV = NSA_KV_HEADS * HEAD_DIM
DIL_W = 2 * DIL_GROUPS * HEAD_DIM
N_GROUPS = 4
EXPERTS_PER_GROUP = 8
N_EXPERTS = 32
TOP_K = 2
ROUTE_ROWS = 8
EXPERT_FF = 512
MOE_BLOCK = 512

VMEM_LIMIT = 56 * 1024 * 1024

T_QA = 0
T_KVC = 4
T_KVA = 6
T_DIL = 10
T_GA = 19
T_GM = 20
N_TILES = 36

PROJ_TM = 512
NSA_TQ = 128
NSA_TK = 512
NSA_NQ = 4
DIL_T = 128
DIL_UNROLL = 4
MERGE_TM = 512
RANK_TM = 1024
SLOT_TM = 4096
MOE_TM = 256
DMA_QUEUES = 2


def _dot(a, b):
    return jnp.dot(a, b, preferred_element_type=F32)


def _dot_nt(a, b):
    return lax.dot_general(a, b, (((1,), (1,)), ((), ())), preferred_element_type=F32)


def _dot_tn(a, b):
    return lax.dot_general(a, b, (((0,), (0,)), ((), ())), preferred_element_type=F32)


def _split(a):
    hi = a.astype(BF16)
    lo = (a - hi.astype(F32)).astype(BF16)
    return hi, lo


def _load_token_tiles(ref, first, n, lead=()):
    ns = ROW_TILE[0]
    return jnp.concatenate(
        [ref[lead + (pl.ds(first * ns + s, n, stride=ns), slice(None))] for s in range(ns)], axis=1)


def _store_token_tiles(ref, rows, first=0):
    ns = ROW_TILE[0]
    for s in range(ns):
        ref[pl.ds(first * ns + s, rows.shape[0], stride=ns), :] = rows[:, s * LANES:(s + 1) * LANES]


def _params(*sem):
    return pltpu.CompilerParams(dimension_semantics=sem, vmem_limit_bytes=VMEM_LIMIT)


def _mod_kernel(c_ref, whi_ref, wlo_ref, b_ref, o_ref):
    c = c_ref[...]
    ca = c * jax.nn.sigmoid(c)
    hi, lo = _split(ca)
    whi = whi_ref[...]
    o_ref[...] = _dot(hi, whi) + _dot(lo, whi) + _dot(hi, wlo_ref[...]) + b_ref[...]


def _mod_call(c, w_ada, b_ada):
    b, d = c.shape
    n = w_ada.shape[1]
    whi, wlo = _split(w_ada)
    tn = 1024
    return pl.pallas_call(
        _mod_kernel,
        grid=(n // tn,),
        in_specs=[
            pl.BlockSpec((b, d), lambda j: (0, 0)),
            pl.BlockSpec((d, tn), lambda j: (0, j)),
            pl.BlockSpec((d, tn), lambda j: (0, j)),
            pl.BlockSpec((1, tn), lambda j: (0, j)),
        ],
        out_specs=pl.BlockSpec((b, tn), lambda j: (0, j)),
        out_shape=jax.ShapeDtypeStruct((b, n), F32),
        compiler_params=_params("parallel"),
        name="mod",
    )(c, whi, wlo, b_ada.reshape(1, n))


def _norm_rope(a, gain, cos, sin_signed, lo, first):
    sq = a * a
    s0 = jnp.sum(jnp.where(lo, sq, 0.0), axis=-1, keepdims=True)
    s1 = jnp.sum(jnp.where(lo, 0.0, sq), axis=-1, keepdims=True)
    r = jnp.where(lo, lax.rsqrt(s0 * (1.0 / HEAD_DIM) + EPS), lax.rsqrt(s1 * (1.0 / HEAD_DIM) + EPS))
    y = a * r * gain
    rot = jnp.where(first, pltpu.roll(y, 96, 1), pltpu.roll(y, 32, 1))
    return y * cos + rot * sin_signed


def _proj_kernel(x_ref, mod_ref, g1_ref, w_ref, gains_ref, cos_ref, sin_ref, kw0_ref, vw0_ref,
                 qext_ref, kvc_ref, ksl_ref, vsl_ref, kwp_ref, vwp_ref, dil0_ref, dil1_ref, dil2_ref,
                 ga_ref, gm_ref, perm_scr):
    del kw0_ref, vw0_ref
    tm = x_ref.shape[1]
    x = x_ref[0]
    ms = jnp.mean(x * x, axis=-1, keepdims=True)
    y = x * lax.rsqrt(ms + EPS) * g1_ref[...]
    sh1 = mod_ref[0, 0:1, :]
    sc1 = mod_ref[0, 1:2, :]
    h = (y * (1.0 + sc1) + sh1).astype(BF16)
    lane = lax.broadcasted_iota(I32, (tm, LANES), 1)
    lo = lane < HEAD_DIM
    first = (lane & (HEAD_DIM - 1)) < (HEAD_DIM // 2)
    cos = jnp.concatenate([cos_ref[0]] * 4, axis=1)
    sin = jnp.concatenate([sin_ref[0]] * 4, axis=1)
    sin = jnp.where(first, -sin, sin)
    nr = functools.partial(_norm_rope, cos=cos, sin_signed=sin, lo=lo, first=first)

    for c in range(N_TILES // 2):
        acc = _dot(h, w_ref[:, c * 2 * LANES:(c + 1) * 2 * LANES])
        for half in range(2):
            t = 2 * c + half
            a = acc[:, half * LANES:(half + 1) * LANES]
            if t < T_KVC:
                yq = nr(a, gains_ref[0:1, :])
                rq = pltpu.roll(yq, HEAD_DIM, 1)
                if t // 2 == 0:
                    e0 = jnp.where(lo, yq, 0.0)
                    e1 = jnp.where(lo, rq, 0.0)
                else:
                    e0 = jnp.where(lo, 0.0, rq)
                    e1 = jnp.where(lo, 0.0, yq)
                qext_ref[0, 2 * t] = e0.astype(BF16)
                qext_ref[0, 2 * t + 1] = e1.astype(BF16)
            elif t < T_KVA:
                kvc_ref[0, t - T_KVC] = a
            elif t < T_DIL:
                j = t - T_KVA
                k_out, v_out = (ksl_ref, vsl_ref) if j < 2 else (kwp_ref, vwp_ref)
                if j % 2 == 0:
                    k_out[0] = nr(a, gains_ref[1:2, :]).astype(BF16)
                else:
                    v_out[0, 0] = jnp.where(lo, a, 1.0).astype(BF16)
                    v_out[0, 1] = jnp.where(lo, pltpu.roll(a, HEAD_DIM, 1), 1.0).astype(BF16)
            elif t < T_GA:
                j = t - T_DIL
                if j < 3:
                    v = nr(a, gains_ref[2:3, :])
                elif j < 6:
                    v = nr(a, gains_ref[3:4, :])
                else:
                    v = a
                kind, gi = divmod(j, DIL_GROUPS)
                dl = DIL_PAIRS[gi][1]
                if dl == 1:
                    dil0_ref[0, kind] = v.astype(BF16)
                else:
                    out = (dil0_ref, dil1_ref, dil2_ref)[gi]
                    perm_scr[...] = v
                    for r in range(dl):
                        out[0, kind, r] = perm_scr[pl.ds(r, tm // dl, stride=dl), :].astype(BF16)
            elif t < T_GM:
                ga_ref[0] = jax.nn.sigmoid(a)
            else:
                j = t - T_GM
                gm_ref[0, :, j * LANES:(j + 1) * LANES] = jax.nn.sigmoid(a).astype(BF16)


def _proj_call(x, mod6, g1, w_perm, gains, cos_t, sin_t):
    b, s, d = x.shape
    tm = PROJ_TM
    n = N_TILES * LANES
    row = lambda bi, i: (bi, i, 0)
    heads = lambda bi, i: (bi, 0, i, 0)
    pad = WINDOW // tm
    in_specs = [
        pl.BlockSpec((1, tm, d), row),
        pl.BlockSpec((1, 6, d), lambda bi, i: (bi, 0, 0)),
        pl.BlockSpec((1, d), lambda bi, i: (0, 0)),
        pl.BlockSpec((d, n), lambda bi, i: (0, 0)),
        pl.BlockSpec((4, LANES), lambda bi, i: (0, 0)),
        pl.BlockSpec((1, tm, HEAD_DIM // 2), row),
        pl.BlockSpec((1, tm, HEAD_DIM // 2), row),
        pl.BlockSpec(memory_space=pl.ANY),
        pl.BlockSpec(memory_space=pl.ANY),
    ]
    out_specs = [
        pl.BlockSpec((1, NSA_HEADS, tm, LANES), heads),
        pl.BlockSpec((1, 2, tm, LANES), heads),
        pl.BlockSpec((1, tm, LANES), row),
        pl.BlockSpec((1, NSA_KV_HEADS, tm, LANES), heads),
        pl.BlockSpec((1, tm, LANES), lambda bi, i: (bi, i + pad, 0)),
        pl.BlockSpec((1, NSA_KV_HEADS, tm, LANES), lambda bi, i: (bi, 0, i + pad, 0)),
        pl.BlockSpec((1, 3, tm, LANES), heads),
        *[pl.BlockSpec((1, 3, dl, tm // dl, LANES), lambda bi, i: (bi, 0, 0, i, 0)) for _, dl in DIL_PAIRS[1:]],
        pl.BlockSpec((1, tm, LANES), row),
        pl.BlockSpec((1, tm, 2 * d), row),
    ]
    out_shape = [
        jax.ShapeDtypeStruct((b, NSA_HEADS, s, LANES), BF16),
        jax.ShapeDtypeStruct((b, 2, s, LANES), F32),
        jax.ShapeDtypeStruct((b, s, LANES), BF16),
        jax.ShapeDtypeStruct((b, NSA_KV_HEADS, s, LANES), BF16),
        jax.ShapeDtypeStruct((b, s + WINDOW, LANES), BF16),
        jax.ShapeDtypeStruct((b, NSA_KV_HEADS, s + WINDOW, LANES), BF16),
        jax.ShapeDtypeStruct((b, 3, s, LANES), BF16),
        *[jax.ShapeDtypeStruct((b, 3, dl, s // dl, LANES), BF16) for _, dl in DIL_PAIRS[1:]],
        jax.ShapeDtypeStruct((b, s, LANES), F32),
        jax.ShapeDtypeStruct((b, s, 2 * d), BF16),
    ]
    kw0 = jnp.zeros(out_shape[4].shape, BF16)
    vw0 = jnp.zeros(out_shape[5].shape, BF16)
    return pl.pallas_call(
        _proj_kernel,
        grid=(b, s // tm),
        in_specs=in_specs,
        out_specs=out_specs,
        out_shape=out_shape,
        scratch_shapes=[pltpu.VMEM((tm, LANES), F32)],
        input_output_aliases={7: 4, 8: 5},
        compiler_params=_params("parallel", "parallel"),
        name="proj",
    )(x, mod6, g1, w_perm, gains, cos_t, sin_t, kw0, vw0)


def _cmp_kernel(x_ref, pe_ref, w1_ref, w2_ref, gain_ref, cos_ref, sin_ref, kc_ref, vc_ref):
    nb = kc_ref.shape[1]
    lane = lax.broadcasted_iota(I32, (nb, LANES), 1)
    lo = lane < HEAD_DIM
    first = (lane & (HEAD_DIM - 1)) < (HEAD_DIM // 2)
    for kind in range(2):
        top = jnp.zeros((nb, 2 * CMP_HIDDEN), F32)
        bot = jnp.zeros((nb, 2 * CMP_HIDDEN), F32)
        for r in range(CMP_STRIDE):
            x = x_ref[0, kind, pl.ds(r, nb, stride=CMP_STRIDE), :]
            top = top + _dot((x + pe_ref[kind, r:r + 1, :]).astype(BF16), w1_ref[kind, r])
            bot = bot + _dot((x + pe_ref[kind, CMP_STRIDE + r:CMP_STRIDE + r + 1, :]).astype(BF16),
                             w1_ref[kind, CMP_STRIDE + r])
        hid = top + pltpu.roll(bot, nb - 1, 0)
        hid = (hid * jax.nn.sigmoid(hid)).astype(BF16)
        out = (_dot(hid[:, 0:CMP_HIDDEN], w2_ref[kind, 0])
               + _dot(hid[:, CMP_HIDDEN:2 * CMP_HIDDEN], w2_ref[kind, 1]))
        if kind == 0:
            out = _norm_rope(out, gain_ref[...], cos_ref[0], sin_ref[0], lo, first)
            kc_ref[0] = out.astype(BF16)
        else:
            vc_ref[0, 0] = jnp.where(lo, out, 0.0).astype(BF16)
            vc_ref[0, 1] = jnp.where(lo, pltpu.roll(out, HEAD_DIM, 1), 0.0).astype(BF16)


def _cmp_call(kvc, pe, w1, w2ext, gain_k, ccos, csin):
    b, _, s, _ = kvc.shape
    nb = s // CMP_STRIDE
    return pl.pallas_call(
        _cmp_kernel,
        grid=(b,),
        in_specs=[
            pl.BlockSpec((1, 2, s, LANES), lambda bi: (bi, 0, 0, 0)),
            pl.BlockSpec(pe.shape, lambda bi: (0, 0, 0)),
            pl.BlockSpec(w1.shape, lambda bi: (0, 0, 0, 0)),
            pl.BlockSpec((2, 2, CMP_HIDDEN, LANES), lambda bi: (0, 0, 0, 0)),
            pl.BlockSpec((1, LANES), lambda bi: (0, 0)),
            pl.BlockSpec((1, nb, LANES), lambda bi: (bi, 0, 0)),
            pl.BlockSpec((1, nb, LANES), lambda bi: (bi, 0, 0)),
        ],
        out_specs=[
            pl.BlockSpec((1, nb, LANES), lambda bi: (bi, 0, 0)),
            pl.BlockSpec((1, NSA_KV_HEADS, nb, LANES), lambda bi: (bi, 0, 0, 0)),
        ],
        out_shape=[
            jax.ShapeDtypeStruct((b, nb, LANES), BF16),
            jax.ShapeDtypeStruct((b, NSA_KV_HEADS, nb, LANES), BF16),
        ],
        compiler_params=_params("parallel"),
        name="cmp",
    )(kvc, pe, w1, w2ext, gain_k, ccos, csin)


def _softmax_cols(s, exp=jnp.exp):
    p = exp(s - jnp.max(s, axis=0, keepdims=True))
    return p, jnp.sum(p, axis=0, keepdims=True)


def _block_max(s, offs, bs):
    m = None
    for j, off in enumerate(offs):
        mj = jnp.max(s[j * bs:(j + 1) * bs], axis=0, keepdims=True) + off
        m = mj if m is None else jnp.maximum(m, mj)
    return m


def _block_exp2(s, offs, bs, m):
    return jnp.concatenate(
        [jnp.exp2(s[j * bs:(j + 1) * bs] - (m - off)) for j, off in enumerate(offs)], axis=0)


def _nsa_kernel(q_ref, kc_ref, vc_ref, ksl_ref, vsl_ref, kw_ref, vw_ref, ga_ref, ov_ref, wb_ref,
                o_ref, m_scr, acc_scr, sb_scr, *s_scrs):
    tq = NSA_TQ
    tk = NSA_TK
    hd = HEAD_DIM
    cols = NSA_GROUP * tq
    tiles = range(NSA_NQ)
    s_buf = lambda u, c: s_scrs[2 * u + c % 2]
    n_blk = ov_ref.shape[0]
    n_chunks = ksl_ref.shape[1] // tk
    bpc = tk // SEL_BLOCK
    kh = pl.program_id(1)
    base = pl.multiple_of(pl.program_id(2) * (NSA_NQ * tq), NSA_NQ * tq)
    t0 = [base + u * tq for u in tiles]
    q4 = [q_ref[0, :, u * tq:(u + 1) * tq, :].reshape(cols, LANES) for u in tiles]
    lane_q = lax.broadcasted_iota(I32, (1, tq), 1)
    lane_q4 = lax.broadcasted_iota(I32, (1, cols), 1) & (tq - 1)
    per_head = lambda b: jnp.concatenate([b] * NSA_GROUP, axis=1)
    older_edge = wb_ref[0]
    causal_edge = wb_ref[1]
    nwin = WINDOW + tq
    n_wb = nwin // tq
    blk = lax.broadcasted_iota(I32, (n_blk, 1), 0)
    blk_f = blk.astype(F32)
    ov = ov_ref[...]

    s_cmp = [_dot_nt(kc_ref[0], q4[u]) for u in tiles]
    s_win = [_dot_nt(kw_ref[0, pl.ds(t0[u], nwin), :], q4[u]) for u in tiles]
    s_diag = [_dot_nt(ksl_ref[0, pl.ds(base, (u + 1) * tq), :], q4[u]) for u in tiles]
    for u in tiles:
        s_buf(u, 0)[...] = _dot_nt(ksl_ref[0, 0:tk, :], q4[u])

    o_cmp, sel_bias = [], []
    for u in tiles:
        nb = s_cmp[u].shape[0]
        cmp_end = lax.broadcasted_iota(I32, (nb, 1), 0) * CMP_STRIDE + (CMP_BLOCK - 1)
        valid = cmp_end <= t0[u] + lane_q4
        e, den = _softmax_cols(jnp.where(valid, s_cmp[u], NEG_INF), jnp.exp2)
        p = jnp.where(valid, e / den, 0.0)
        o_cmp.append(_dot_tn(vc_ref[0, 0], p.astype(BF16))[0:hd])
        psum = p[:, 0:tq] + p[:, tq:2 * tq] + p[:, 2 * tq:3 * tq] + p[:, 3 * tq:4 * tq]
        p_hi, p_lo = _split(psum)
        imp = _dot(ov, p_hi) + _dot(ov, p_lo)
        rel = ((t0[u] + lane_q) >> 6) - blk
        forced = (blk == 0) | ((rel >= 0) & (rel < N_LOCAL_SEL))
        score = jnp.where(rel < 0, NEG_INF, jnp.where(forced, FORCE_SCORE, imp))
        bias = jnp.full((n_blk, tq), MASKED, F32)
        for _ in range(N_SEL):
            best = jnp.max(score, axis=0, keepdims=True)
            first = jnp.min(jnp.where(score == best, blk_f, float(n_blk)), axis=0, keepdims=True)
            pick = blk_f == first
            bias = jnp.where(pick, 0.0, bias)
            score = jnp.where(pick, PICKED, score)
        sel_bias.append(per_head(jnp.where(rel < 0, MASKED, bias)))

    o_win = []
    for u in tiles:
        sw = jnp.concatenate([s_win[u][0:tq] + per_head(older_edge), s_win[u][tq:nwin - tq],
                              s_win[u][nwin - tq:nwin] + per_head(causal_edge)], axis=0)
        w_offs = [jnp.where(t0[u] - WINDOW + j * tq >= 0, 0.0, MASKED) for j in range(n_wb)]
        pw = _block_exp2(sw, w_offs, tq, _block_max(sw, w_offs, tq))
        ow = _dot_tn(vw_ref[0, 0, pl.ds(t0[u], nwin), :], pw.astype(BF16))
        o_win.append(ow[0:hd] / ow[hd:hd + 1])

    diag = base // tk
    for u in tiles:
        sb_scr[u] = sel_bias[u]
        nk = (u + 1) * tq
        sd = s_diag[u][nk - tq:nk] + per_head(causal_edge)
        if u > 0:
            sd = jnp.concatenate([s_diag[u][0:nk - tq], sd], axis=0)
        offs = [sb_scr[u, pl.ds(diag * bpc + j, 1), :] for j in range(nk // SEL_BLOCK)]
        m_new = _block_max(sd, offs, SEL_BLOCK)
        vd = vsl_ref[0, 0, pl.ds(base, nk), :]
        acc_scr[u] = _dot_tn(vd, _block_exp2(sd, offs, SEL_BLOCK, m_new).astype(BF16))
        m_scr[u] = m_new

    def chunk(c):
        if c + 2 < n_chunks:
            for u in tiles:
                s_buf(u, c + 1)[...] = _dot_nt(ksl_ref[0, (c + 1) * tk:(c + 2) * tk, :], q4[u])
        vb = vsl_ref[0, 0, c * tk:(c + 1) * tk, :]
        for u in tiles:
            sc = s_buf(u, c)[...]
            offs = [sel_bias[u][c * bpc + j:c * bpc + j + 1] for j in range(bpc)]
            m_old = m_scr[u]
            m_new = jnp.maximum(m_old, _block_max(sc, offs, SEL_BLOCK))
            pe = _block_exp2(sc, offs, SEL_BLOCK, m_new)
            acc_scr[u] = jnp.exp2(m_old - m_new) * acc_scr[u] + _dot_tn(vb, pe.astype(BF16))
            m_scr[u] = m_new

    for c in range(n_chunks - 1):
        pl.when(c < diag)(functools.partial(chunk, c))

    is0 = kh == 0
    for u in tiles:
        rows = slice(u * tq, (u + 1) * tq)
        o_slc = acc_scr[u, 0:hd, :] / acc_scr[u, hd:hd + 1, :]
        gat = ga_ref[0, rows, :].T
        heads = []
        for g in range(NSA_GROUP):
            c = slice(g * tq, (g + 1) * tq)
            og = jnp.zeros((hd, tq), F32)
            for gi, ob in enumerate((o_cmp[u], o_slc, o_win[u])):
                c0 = gi * NSA_HEADS + g
                c1 = c0 + NSA_GROUP
                gate = jnp.where(is0, gat[c0:c0 + 1, :], gat[c1:c1 + 1, :])
                og = og + gate * ob[:, c]
            heads.append(og)
        for pair in range(2):
            tile = jnp.concatenate([heads[2 * pair], heads[2 * pair + 1]], axis=0)
            o_ref[0, rows, pair * LANES:(pair + 1) * LANES] = tile.T.astype(BF16)


def _nsa_call(qext, kc, vc, ksl, vsl, kwp, vwp, ga, ov_t, wbias):
    b, _, s, _ = qext.shape
    tq = NSA_TQ * NSA_NQ
    nb = kc.shape[1]
    cols = NSA_GROUP * NSA_TQ
    assert NSA_TK == tq
    shared = lambda rows: pl.BlockSpec((1, rows, LANES), lambda bi, k, i: (bi, 0, 0))
    per_kv = lambda rows: pl.BlockSpec((1, 1, rows, LANES), lambda bi, k, i: (bi, k, 0, 0))
    return pl.pallas_call(
        _nsa_kernel,
        grid=(b, NSA_KV_HEADS, s // tq),
        in_specs=[
            pl.BlockSpec((1, NSA_GROUP, tq, LANES), lambda bi, k, i: (bi, k, i, 0)),
            shared(nb), per_kv(nb),
            shared(s), per_kv(s),
            shared(s + WINDOW), per_kv(s + WINDOW),
            pl.BlockSpec((1, tq, LANES), lambda bi, k, i: (bi, i, 0)),
            pl.BlockSpec(ov_t.shape, lambda bi, k, i: (0, 0)),
            pl.BlockSpec(wbias.shape, lambda bi, k, i: (0, 0, 0)),
        ],
        out_specs=pl.BlockSpec((1, tq, 2 * LANES), lambda bi, k, i: (bi, i, k)),
        out_shape=jax.ShapeDtypeStruct((b, s, A_Q), BF16),
        scratch_shapes=[
            pltpu.VMEM((NSA_NQ, 1, cols), F32),
            pltpu.VMEM((NSA_NQ, LANES, cols), F32),
            pltpu.VMEM((NSA_NQ, ov_t.shape[0], cols), F32),
            *[pltpu.VMEM((NSA_TK, cols), F32)] * (2 * NSA_NQ),
        ],
        compiler_params=_params("parallel", "parallel", "parallel"),
        name="nsa",
    )(qext, kc, vc, ksl, vsl, kwp, vwp, ga, ov_t, wbias)


def _dil_kernel(q_ref, k_ref, v_ref, o_ref, lse_ref, *, seg_tiles):
    t = DIL_T
    lo = lax.broadcasted_iota(I32, (t, LANES), 1) < HEAD_DIM
    kj = lax.broadcasted_iota(I32, (t, 2 * t), 0)
    qi = lax.broadcasted_iota(I32, (t, 2 * t), 1) & (t - 1)
    top = lax.broadcasted_iota(I32, (LANES, t), 0) < HEAD_DIM

    def scores(i):
        has_prev = (i & (seg_tiles - 1)) != 0
        cs = pl.multiple_of(i * t, t)
        ps = pl.multiple_of(jnp.maximum(i - 1, 0) * t, t)
        q = q_ref[0, 0, pl.ds(cs, t), :]
        zero = jnp.zeros_like(q)
        q2 = jnp.concatenate([jnp.where(lo, q, zero), jnp.where(lo, zero, q)], axis=0)
        sp = _dot_nt(k_ref[0, 0, pl.ds(ps, t), :], q2)
        sc = _dot_nt(k_ref[0, 0, pl.ds(cs, t), :], q2)
        return has_prev, ps, cs, sp, sc

    def attend(has_prev, ps, cs, sp, sc):
        sp = jnp.where((kj >= qi) & has_prev, sp, MASKED)
        sc = jnp.where(qi >= kj, sc, MASKED)
        m = jnp.maximum(jnp.max(sp, axis=0, keepdims=True), jnp.max(sc, axis=0, keepdims=True))
        pp = jnp.exp(sp - m)
        pc = jnp.exp(sc - m)
        l = jnp.sum(pp, axis=0, keepdims=True) + jnp.sum(pc, axis=0, keepdims=True)
        o = (_dot_tn(v_ref[0, 0, pl.ds(ps, t), :], pp.astype(BF16))
             + _dot_tn(v_ref[0, 0, pl.ds(cs, t), :], pc.astype(BF16))) / l
        lse = m + jnp.log(l)
        o_ref[0, pl.ds(cs, t), :] = jnp.where(top, o[:, 0:t], o[:, t:2 * t]).T
        lse_ref[0, pl.ds(cs, t), :] = jnp.where(top, lse[:, 0:t], lse[:, t:2 * t]).T

    def tiles(j, carry):
        group = [scores(j * DIL_UNROLL + u) for u in range(DIL_UNROLL)]
        for args in group:
            attend(*args)
        return carry

    lax.fori_loop(0, q_ref.shape[2] // (t * DIL_UNROLL), tiles, 0)


def _dil_call(qkv, dl):
    b, _, s, _ = qkv.shape
    kind = lambda j: pl.BlockSpec((1, 1, s, LANES), lambda bi: (bi, j, 0, 0))
    out = pl.BlockSpec((1, s, LANES), lambda bi: (bi, 0, 0))
    return pl.pallas_call(
        functools.partial(_dil_kernel, seg_tiles=s // dl // DIL_T),
        grid=(b,),
        in_specs=[kind(0), kind(1), kind(2)],
        out_specs=[out, out],
        out_shape=[jax.ShapeDtypeStruct((b, s, LANES), F32)] * 2,
        compiler_params=_params("parallel"),
        name=f"dil{dl}",
    )(qkv, qkv, qkv)


def _merge_kernel(x_ref, oa_ref, od0_ref, od1_ref, od2_ref, ls0_ref, ls1_ref, ls2_ref, gm_ref, mod_ref,
                  wa_ref, wb_ref, wo_ref, g2_ref, rcat_ref, rb_ref, x1_ref, h2_ref, route_ref, perm_scr):
    tm = x_ref.shape[0]
    d = x_ref.shape[1]
    hm = tm // 2
    for n, ref in enumerate((od1_ref, od2_ref, ls1_ref, ls2_ref)):
        dl = ref.shape[1]
        for r in range(dl):
            perm_scr[n, pl.ds(r, tm // dl, stride=dl), :] = ref[0, r]

    def up(rows):
        l0, l1, l2 = ls0_ref[rows], perm_scr[2, rows], perm_scr[3, rows]
        mx = jnp.maximum(jnp.maximum(l0, l1), l2)
        e0, e1, e2 = jnp.exp(l0 - mx), jnp.exp(l1 - mx), jnp.exp(l2 - mx)
        den = e0 + e1 + e2
        ob = jnp.concatenate([(od0_ref[rows] * (e0 / den)).astype(BF16),
                              (perm_scr[0, rows] * (e1 / den)).astype(BF16),
                              (perm_scr[1, rows] * (e2 / den)).astype(BF16)], axis=1)
        return _dot(oa_ref[rows], wa_ref[...]), _dot(ob, wb_ref[...])

    def out_proj(rows, ya, yb):
        y = gm_ref[rows, 0:d].astype(F32) * ya + gm_ref[rows, d:2 * d].astype(F32) * yb
        return _dot(y.astype(BF16), wo_ref[...])

    def residual_and_logits(h, rows, z):
        x1 = x_ref[rows] + mod_ref[0, 2:3, :] * z
        x1_ref[rows] = x1
        ms = jnp.mean(x1 * x1, axis=-1, keepdims=True)
        h2 = x1 * lax.rsqrt(ms + EPS) * g2_ref[...]
        h2 = h2 * (1.0 + mod_ref[0, 4:5, :]) + mod_ref[0, 3:4, :]
        _store_token_tiles(h2_ref, h2, first=h * hm)
        hi, lo = _split(h2)
        both = _dot_nt(rcat_ref[...], hi)
        return both[0:LANES] + both[LANES:2 * LANES] + _dot_nt(rcat_ref[0:LANES, :], lo) + rb_ref[...]

    def route(h, logits):
        row = lax.broadcasted_iota(I32, (LANES, hm), 0).astype(F32)
        gl = jnp.where(row < N_GROUPS, logits, NEG_INF)
        gmax = jnp.max(gl, axis=0, keepdims=True)
        g_w = 1.0 / jnp.sum(jnp.exp(gl - gmax), axis=0, keepdims=True)
        g_idx = jnp.min(jnp.where(gl == gmax, row, float(LANES)), axis=0, keepdims=True)
        e_lo = N_GROUPS + EXPERTS_PER_GROUP * g_idx
        el = jnp.where((row >= e_lo) & (row < e_lo + EXPERTS_PER_GROUP), logits, NEG_INF)
        m1 = jnp.max(el, axis=0, keepdims=True)
        i1 = jnp.min(jnp.where(el == m1, row, float(LANES)), axis=0, keepdims=True)
        el2 = jnp.where(row == i1, PICKED, el)
        m2 = jnp.max(el2, axis=0, keepdims=True)
        i2 = jnp.min(jnp.where(el2 == m2, row, float(LANES)), axis=0, keepdims=True)
        ex = jnp.exp(m2 - m1)
        w1 = g_w * (1.0 / (1.0 + ex))
        w2 = g_w * (ex / (1.0 + ex))
        route_ref[:, h * hm:(h + 1) * hm] = jnp.concatenate(
            [i1 - N_GROUPS, i2 - N_GROUPS, w1, w2, jnp.zeros((ROUTE_ROWS - 4, hm), F32)], axis=0)

    halves = (slice(0, hm), slice(hm, tm))
    ups = [up(rows) for rows in halves]
    zs = [out_proj(rows, *u) for rows, u in zip(halves, ups)]
    logits = [residual_and_logits(h, rows, z) for h, (rows, z) in enumerate(zip(halves, zs))]
    for h, lg in enumerate(logits):
        route(h, lg)


def _merge_call(x2, oa, ods, lses, gm, mod6, wa, wb, wo, g2, rcat, rb, s):
    t, d = x2.shape
    b = t // s
    tm = MERGE_TM
    per_b = s // tm
    row = lambda i: (i, 0)
    const = lambda i: (0, 0)

    def by_class(dl):
        return pl.BlockSpec((1, dl, tm // dl, LANES), lambda i: (i // per_b, 0, i % per_b, 0))

    dls = [dl for _, dl in DIL_PAIRS]
    assert dls[0] == 1
    group_specs = [pl.BlockSpec((tm, LANES), row)] + [by_class(dl) for dl in dls[1:]]
    views = lambda arrs: [arrs[0].reshape(t, LANES)] + [
        a.reshape(b, dl, s // dl, LANES) for a, dl in zip(arrs[1:], dls[1:])]
    return pl.pallas_call(
        _merge_kernel,
        grid=(t // tm,),
        in_specs=[
            pl.BlockSpec((tm, d), row),
            pl.BlockSpec((tm, A_Q), row),
            *group_specs, *group_specs,
            pl.BlockSpec((tm, 2 * d), row),
            pl.BlockSpec((1, 6, d), lambda i: (i // per_b, 0, 0)),
            pl.BlockSpec(wa.shape, const),
            pl.BlockSpec(wb.shape, const),
            pl.BlockSpec(wo.shape, const),
            pl.BlockSpec((1, d), const),
            pl.BlockSpec(rcat.shape, const),
            pl.BlockSpec((LANES, 1), const),
        ],
        out_specs=[
            pl.BlockSpec((tm, d), row),
            pl.BlockSpec((tm * ROW_TILE[0], LANES), row),
            pl.BlockSpec((ROUTE_ROWS, tm), lambda i: (0, i)),
        ],
        out_shape=[
            jax.ShapeDtypeStruct((t, d), F32),
            jax.ShapeDtypeStruct((t * ROW_TILE[0], LANES), F32),
            jax.ShapeDtypeStruct((ROUTE_ROWS, t), F32),
        ],
        scratch_shapes=[pltpu.VMEM((2 * (len(dls) - 1), tm, LANES), F32)],
        compiler_params=_params("parallel"),
        name="merge",
    )(x2, oa, *views(ods), *views(lses), gm, mod6, wa, wb, wo, g2, rcat, rb)


def _expert_hits(route_ref, tm):
    row = lax.broadcasted_iota(I32, (N_EXPERTS, tm), 0).astype(F32)
    return [row == route_ref[k:k + 1, :] for k in range(TOP_K)]


def _rank_kernel(route_ref, rank_ref, count_ref, carry_scr):
    tm = route_ref.shape[1]

    @pl.when(pl.program_id(0) == 0)
    def _():
        carry_scr[...] = jnp.zeros(carry_scr.shape, F32)

    hits = _expert_hits(route_ref, tm)
    cnt = jnp.where(hits[0] | hits[1], 1.0, 0.0)
    r = lax.broadcasted_iota(I32, (tm, tm), 0)
    c = lax.broadcasted_iota(I32, (tm, tm), 1)
    earlier = jnp.where(r < c, 1.0, 0.0).astype(BF16)
    before = _dot(cnt.astype(BF16), earlier) + carry_scr[:, 0:1]
    ranks = [jnp.sum(jnp.where(hit, before, 0.0), axis=0, keepdims=True) for hit in hits]
    rank_ref[...] = jnp.concatenate(ranks + [jnp.zeros((ROUTE_ROWS - TOP_K, tm), F32)], axis=0)
    carry_scr[...] = carry_scr[...] + jnp.sum(cnt, axis=1, keepdims=True)
    count_ref[...] = carry_scr[...]


def _rank_call(route):
    t = route.shape[1]
    tm = RANK_TM
    tile = pl.BlockSpec((ROUTE_ROWS, tm), lambda i: (0, i))
    return pl.pallas_call(
        _rank_kernel,
        grid=(t // tm,),
        in_specs=[tile],
        out_specs=[tile, pl.BlockSpec((N_EXPERTS, LANES), lambda i: (0, 0))],
        out_shape=[
            jax.ShapeDtypeStruct((ROUTE_ROWS, t), F32),
            jax.ShapeDtypeStruct((N_EXPERTS, LANES), F32),
        ],
        scratch_shapes=[pltpu.VMEM((N_EXPERTS, LANES), F32)],
        compiler_params=_params("arbitrary"),
        name="rank",
    )(route)


def _slot_kernel(route_ref, rank_ref, start_ref, slot_ref):
    tm = route_ref.shape[1]
    starts = start_ref[...]
    slots = [jnp.sum(jnp.where(hit, starts, 0.0), axis=0, keepdims=True) + rank_ref[k:k + 1, :]
             for k, hit in enumerate(_expert_hits(route_ref, tm))]
    slot_ref[...] = jnp.concatenate(slots + [jnp.zeros((ROUTE_ROWS - TOP_K, tm), F32)], axis=0)


def _slot_call(route, rank, pad_starts):
    t = route.shape[1]
    tm = min(SLOT_TM, t)
    tile = pl.BlockSpec((ROUTE_ROWS, tm), lambda i: (0, i))
    return pl.pallas_call(
        _slot_kernel,
        grid=(t // tm,),
        in_specs=[tile, tile, pl.BlockSpec((N_EXPERTS, tm), lambda i: (0, 0))],
        out_specs=tile,
        out_shape=jax.ShapeDtypeStruct((ROUTE_ROWS, t), F32),
        compiler_params=_params("parallel"),
        name="slots",
    )(route, rank, jnp.broadcast_to(pad_starts[:, None], (N_EXPERTS, tm)))


def _for_static_slot(slot, fn):
    for par in range(2):
        pl.when(slot == par)(functools.partial(fn, par))


def _dispatch_kernel(cnt_ref, pstart_ref, dest_hbm, h_hbm, xs_hbm, idx_smem, hbuf, zbuf,
                     isem, hsem, dsem, zsem):
    tm = MOE_TM
    nd = TOP_K * tm
    i = pl.program_id(0)
    n = pl.num_programs(0)
    slot = i % 2
    n_slots = xs_hbm.shape[0]

    def idx_copy(step, sl):
        return pltpu.make_async_copy(dest_hbm.at[step], idx_smem.at[sl], isem.at[sl])

    def tile_copy(step, bf):
        rows = pl.ds(pl.multiple_of(step * tm, tm), tm)
        return pltpu.make_async_copy(h_hbm.at[rows], hbuf.at[bf], hsem.at[bf])

    def wait_scatter(bf):
        for _ in range(TOP_K):
            pltpu.make_async_copy(hbuf.at[bf], xs_hbm.at[pl.ds(0, tm)], dsem.at[bf]).wait()

    @pl.when(i == 0)
    def _():
        idx_copy(0, 0).start()
        tile_copy(0, 0).start()
        zbuf[...] = jnp.zeros(zbuf.shape, F32)

        def zero_gaps(wait):
            def per_expert(e, used):
                cnt = cnt_ref[e]
                gap = (MOE_BLOCK - cnt % MOE_BLOCK) % MOE_BLOCK
                row = pstart_ref[e] + cnt
                for k in range(MOE_BLOCK.bit_length() - 1):
                    size = 1 << k
                    copy = pltpu.make_async_copy(zbuf.at[pl.ds(0, size)], xs_hbm.at[pl.ds(row, size)], zsem)
                    pl.when(((gap >> k) & 1) == 1)(copy.wait if wait else copy.start)
                    row = row + (gap & size)
                return used + cnt + gap
            return lax.fori_loop(0, N_EXPERTS, per_expert, 0)
        zero_gaps(wait=False)
        used = zero_gaps(wait=True)

        def zero_block(blk):
            rows = pl.ds(pl.multiple_of(blk * MOE_BLOCK, MOE_BLOCK), MOE_BLOCK)
            return pltpu.make_async_copy(zbuf, xs_hbm.at[rows], zsem)

        def start_blk(blk, c):
            zero_block(blk).start()
            return c

        def wait_blk(blk, c):
            zero_block(blk).wait()
            return c
        lax.fori_loop(used // MOE_BLOCK, n_slots // MOE_BLOCK, start_blk, 0)
        lax.fori_loop(used // MOE_BLOCK, n_slots // MOE_BLOCK, wait_blk, 0)

    @pl.when(i > 0)
    def _():
        wait_scatter(1 - slot)

    @pl.when(i + 1 < n)
    def _():
        idx_copy(i + 1, 1 - slot).start()
        tile_copy(i + 1, 1 - slot).start()

    idx_copy(i, slot).wait()
    tile_copy(i, slot).wait()

    def scatter_rows(par):
        for r in range(nd):
            copy = pltpu.make_async_copy(hbuf.at[par, r % tm], xs_hbm.at[idx_smem[par, r]], dsem.at[par])
            copy.start(priority=r % DMA_QUEUES)
    _for_static_slot(slot, scatter_rows)

    @pl.when(i == n - 1)
    def _():
        wait_scatter(slot)


def _dispatch_call(counts, pad_starts, dest_tiles, h2, n_slots):
    n_steps = dest_tiles.shape[0]
    grid_spec = pltpu.PrefetchScalarGridSpec(
        num_scalar_prefetch=2,
        grid=(n_steps,),
        in_specs=[pl.BlockSpec(memory_space=pl.ANY), pl.BlockSpec(memory_space=pl.ANY)],
        out_specs=pl.BlockSpec(memory_space=pl.ANY),
        scratch_shapes=[
            pltpu.SMEM((2, TOP_K * MOE_TM), I32),
            pltpu.VMEM((2, MOE_TM) + ROW_TILE, F32),
            pltpu.VMEM((MOE_BLOCK,) + ROW_TILE, F32),
            pltpu.SemaphoreType.DMA((2,)),
            pltpu.SemaphoreType.DMA((2,)),
            pltpu.SemaphoreType.DMA((2,)),
            pltpu.SemaphoreType.DMA(()),
        ],
    )
    return pl.pallas_call(
        _dispatch_kernel,
        grid_spec=grid_spec,
        out_shape=jax.ShapeDtypeStruct((n_slots,) + ROW_TILE, F32),
        compiler_params=_params("arbitrary"),
        name="dispatch",
    )(counts, pad_starts, dest_tiles, h2)


def _expert_kernel(be_ref, x_ref, wg_ref, wu_ref, wd_ref, y_ref, wg_scr, wu_scr, wd_scr):
    i = pl.program_id(0)

    @pl.when((i == 0) | (be_ref[i] != be_ref[jnp.maximum(i - 1, 0)]))
    def _():
        wg_scr[...] = wg_ref[0].astype(BF16)
        wu_scr[...] = wu_ref[0].astype(BF16)
        wd_scr[...] = wd_ref[0].astype(BF16)

    xb = _load_token_tiles(x_ref, 0, MOE_BLOCK).astype(BF16)
    gate = _dot(xb, wg_scr[...])
    up = _dot(xb, wu_scr[...])
    hid = (gate * jax.nn.sigmoid(gate) * up).astype(BF16)
    _store_token_tiles(y_ref, _dot(hid, wd_scr[...]))


def _expert_call(blk_expert, xs, wg, wu, wd):
    n_blk = blk_expert.shape[0]
    d = wg.shape[1]
    blk = (MOE_BLOCK * ROW_TILE[0], LANES)
    grid_spec = pltpu.PrefetchScalarGridSpec(
        num_scalar_prefetch=1,
        grid=(n_blk,),
        in_specs=[
            pl.BlockSpec(blk, lambda i, be: (i, 0)),
            pl.BlockSpec((1, d, EXPERT_FF), lambda i, be: (be[i], 0, 0)),
            pl.BlockSpec((1, d, EXPERT_FF), lambda i, be: (be[i], 0, 0)),
            pl.BlockSpec((1, EXPERT_FF, d), lambda i, be: (be[i], 0, 0)),
        ],
        out_specs=pl.BlockSpec(blk, lambda i, be: (i, 0)),
        scratch_shapes=[
            pltpu.VMEM((d, EXPERT_FF), BF16),
            pltpu.VMEM((d, EXPERT_FF), BF16),
            pltpu.VMEM((EXPERT_FF, d), BF16),
        ],
    )
    return pl.pallas_call(
        _expert_kernel,
        grid_spec=grid_spec,
        out_shape=jax.ShapeDtypeStruct(xs.shape, F32),
        compiler_params=_params("arbitrary"),
        name="experts",
    )(blk_expert, xs, wg, wu, wd)


def _combine_kernel(dest_hbm, y_hbm, x1_ref, route_ref, mod_ref, o_ref, idx_smem, ybuf, isem, dsem):
    tm = MOE_TM
    nd = TOP_K * tm
    i = pl.program_id(0)
    n = pl.num_programs(0)
    slot = i % 2
    nxt = 1 - slot

    def idx_copy(step, sl):
        return pltpu.make_async_copy(dest_hbm.at[step], idx_smem.at[sl], isem.at[sl])

    ns = ROW_TILE[0]

    def step_rows(sl):
        return pltpu.make_async_copy(y_hbm.at[pl.ds(0, nd * ns)], ybuf.at[sl], dsem.at[sl])

    def issue_rows(par):
        for r in range(nd):
            src = y_hbm.at[pl.ds(pl.multiple_of(idx_smem[par, r] * ns, ns), ns)]
            copy = pltpu.make_async_copy(src, ybuf.at[par, pl.ds(r * ns, ns)], dsem.at[par])
            copy.start(priority=r % DMA_QUEUES)

    @pl.when(i == 0)
    def _():
        idx_copy(0, 0).start()
        idx_copy(0, 0).wait()
        issue_rows(0)

        @pl.when(n > 1)
        def _():
            idx_copy(1, 1).start()

    @pl.when(i + 1 < n)
    def _():
        idx_copy(i + 1, nxt).wait()
        _for_static_slot(nxt, issue_rows)

    @pl.when(i + 2 < n)
    def _():
        idx_copy(i + 2, slot).start()

    step_rows(slot).wait()
    route = jnp.concatenate([route_ref[...], jnp.zeros((LANES - ROUTE_ROWS, tm), F32)], axis=0).T
    w1 = route[:, 2:3]
    w2 = route[:, 3:4]
    y = (w1 * _load_token_tiles(ybuf, 0, tm, lead=(slot,))
         + w2 * _load_token_tiles(ybuf, tm, tm, lead=(slot,)))
    o_ref[...] = x1_ref[...] + mod_ref[0, 5:6, :] * y


def _combine_call(dest_tiles, y_slots, x1, route, mod6, s):
    t, d = x1.shape
    tm = MOE_TM
    per_b = s // tm
    return pl.pallas_call(
        _combine_kernel,
        grid=(t // tm,),
        in_specs=[
            pl.BlockSpec(memory_space=pl.ANY),
            pl.BlockSpec(memory_space=pl.ANY),
            pl.BlockSpec((tm, d), lambda i: (i, 0)),
            pl.BlockSpec((ROUTE_ROWS, tm), lambda i: (0, i)),
            pl.BlockSpec((1, 6, d), lambda i: (i // per_b, 0, 0)),
        ],
        out_specs=pl.BlockSpec((tm, d), lambda i: (i, 0)),
        out_shape=jax.ShapeDtypeStruct((t, d), F32),
        scratch_shapes=[
            pltpu.SMEM((2, TOP_K * tm), I32),
            pltpu.VMEM((2, TOP_K * tm * ROW_TILE[0], LANES), F32),
            pltpu.SemaphoreType.DMA((2,)),
            pltpu.SemaphoreType.DMA((2,)),
        ],
        compiler_params=_params("arbitrary"),
        name="combine",
    )(dest_tiles, y_slots, x1, route, mod6)


def _rope_tables(pos):
    inv = ROPE_THETA ** (-jnp.arange(0, HEAD_DIM, 2, dtype=F32) / HEAD_DIM)
    ang = pos.astype(F32)[..., None] * inv
    return jnp.cos(ang), jnp.sin(ang)


def _rope_tiles(pos):
    cos, sin = _rope_tables(pos)
    return (jnp.concatenate([cos, cos, cos, cos], axis=-1),
            jnp.concatenate([-sin, sin, -sin, sin], axis=-1))


def _selection_constants(s):
    n_c = s // CMP_STRIDE
    n_s = s // SEL_BLOCK
    cs = np.arange(n_c) * CMP_STRIDE
    ss = np.arange(n_s) * SEL_BLOCK
    ov = np.clip(np.minimum(cs[:, None] + CMP_BLOCK, ss[None, :] + SEL_BLOCK)
                 - np.maximum(cs[:, None], ss[None, :]), 0, None).astype(np.float32) / CMP_BLOCK
    kq = np.arange(NSA_TQ)[:, None] - np.arange(NSA_TQ)[None, :]
    wbias = np.stack([np.where(kq >= 0, 0.0, MASKED), np.where(kq <= 0, 0.0, MASKED)]).astype(np.float32)
    return jnp.asarray(ov.T, BF16), jnp.asarray(wbias)


def _mixer_and_router(x, c, positions, w_ada, b_ada, norm1_g, norm2_g, w_in, nsa_q_norm, nsa_k_norm,
                      cmp_pe_k, cmp_w1_k, cmp_w2_k, cmp_pe_v, cmp_w1_v, cmp_w2_v, dil_q_norm,
                      dil_k_norm, w_up_a, w_up_b, w_out, w_group, b_group, w_router, b_router):
    b, s, d = x.shape
    scale = HEAD_DIM ** -0.5
    mod6 = _mod_call(c, w_ada, b_ada).reshape(b, 6, d)

    c1 = A_Q
    c2 = c1 + 6 * A_KV
    c3 = c2 + 3 * NSA_HEADS
    c4 = c3 + 3 * DIL_W
    w_perm = jnp.concatenate([
        w_in[:, :c2], w_in[:, c3:c4],
        jnp.pad(w_in[:, c2:c3], ((0, 0), (0, LANES - 3 * NSA_HEADS))), w_in[:, c4:]], axis=1).astype(BF16)
    two = lambda g: jnp.concatenate([g, g]).astype(F32)
    gains = jnp.stack([two(nsa_q_norm) * (scale * LOG2E), two(nsa_k_norm), two(dil_q_norm) * scale,
                       two(dil_k_norm)])
    cos_t, sin_t = _rope_tables(positions)
    qext, kvc, ksl, vsl, kwp, vwp, dil0, dil1, dil2, ga, gm = _proj_call(
        x, mod6, norm1_g.reshape(1, d), w_perm, gains, cos_t, sin_t)

    pe = jnp.stack([jnp.concatenate([p, p], axis=1) for p in (cmp_pe_k, cmp_pe_v)])
    zero_w = jnp.zeros((CMP_BLOCK, HEAD_DIM, CMP_HIDDEN), F32)

    def per_position(w):
        w = w.reshape(CMP_BLOCK, HEAD_DIM, CMP_HIDDEN)
        return jnp.concatenate([jnp.concatenate([w, zero_w], axis=2),
                                jnp.concatenate([zero_w, w], axis=2)], axis=1)
    w1 = jnp.stack([per_position(cmp_w1_k), per_position(cmp_w1_v)]).astype(BF16)
    zeros = jnp.zeros((CMP_HIDDEN, HEAD_DIM), F32)
    ext = lambda w: jnp.stack([jnp.concatenate([w, zeros], 1), jnp.concatenate([zeros, w], 1)])
    w2ext = jnp.stack([ext(cmp_w2_k), ext(cmp_w2_v)]).astype(BF16)
    cmp_pos = jnp.pad(positions[:, CMP_BLOCK - 1::CMP_STRIDE], ((0, 0), (0, 1)))
    ccos, csin = _rope_tiles(cmp_pos)
    kc, vc = _cmp_call(kvc, pe, w1, w2ext, two(nsa_k_norm).reshape(1, LANES), ccos, csin)

    ov_t, wbias = _selection_constants(s)
    o_a = _nsa_call(qext, kc, vc, ksl, vsl, kwp, vwp, ga, ov_t, wbias)

    ods, lses = zip(*[_dil_call(qkv.reshape(b, 3, s, LANES), dl)
                      for qkv, (_, dl) in zip((dil0, dil1, dil2), DIL_PAIRS)])

    t = b * s
    w_r = jnp.concatenate([w_group, w_router.transpose(1, 0, 2).reshape(d, N_EXPERTS)], axis=1)
    w_r = jnp.pad(w_r, ((0, 0), (0, LANES - w_r.shape[1])))
    rcat = jnp.concatenate(_split(w_r.T), axis=0)
    rb = jnp.pad(jnp.concatenate([b_group, b_router.reshape(-1)]), (0, LANES - N_GROUPS - N_EXPERTS))
    return _merge_call(
        x.reshape(t, d), o_a.reshape(t, A_Q), ods, lses, gm.reshape(t, 2 * d), mod6,
        w_up_a.astype(BF16), w_up_b.astype(BF16), w_out.astype(BF16), norm2_g.reshape(1, d),
        rcat, rb.reshape(LANES, 1).astype(F32), s), mod6


def _moe(x1, h2, route, mod6, w_e_gate, w_e_up, w_e_down, s):
    t, _ = x1.shape
    rank, counts = _rank_call(route)
    counts = counts[:, 0].astype(I32)
    padded = (counts + MOE_BLOCK - 1) // MOE_BLOCK * MOE_BLOCK
    pad_ends = jnp.cumsum(padded)
    pad_starts = pad_ends - padded
    dest = _slot_call(route, rank, pad_starts.astype(F32))[0:TOP_K].astype(I32)
    n_slots = t * TOP_K + N_EXPERTS * MOE_BLOCK
    n_blk = n_slots // MOE_BLOCK
    blk_start = jnp.arange(n_blk, dtype=I32) * MOE_BLOCK
    blk_expert = jnp.minimum(jnp.sum((pad_ends[None, :] <= blk_start[:, None]).astype(I32), axis=1),
                             N_EXPERTS - 1)
    tm = MOE_TM
    dest_tiles = dest.reshape(TOP_K, t // tm, tm).transpose(1, 0, 2).reshape(t // tm, TOP_K * tm)
    xs = _dispatch_call(counts, pad_starts.astype(I32), dest_tiles, h2.reshape((t,) + ROW_TILE), n_slots)
    y_slots = _expert_call(blk_expert, xs.reshape(n_slots * ROW_TILE[0], LANES), w_e_gate, w_e_up, w_e_down)
    return _combine_call(dest_tiles, y_slots, x1, route, mod6, s)


def kernel(x, c, positions, w_ada, b_ada, norm1_g, norm2_g, w_in, nsa_q_norm, nsa_k_norm, cmp_pe_k,
           cmp_w1_k, cmp_w2_k, cmp_pe_v, cmp_w1_v, cmp_w2_v, dil_q_norm, dil_k_norm, w_up_a, w_up_b,
           w_out, w_group, b_group, w_router, b_router, w_e_gate, w_e_up, w_e_down):
    b, s, d = x.shape
    assert w_ada.shape[0] == 1 and d == D_MODEL and s % NSA_TK == 0
    (x1, h2, route), mod6 = _mixer_and_router(
        x, c, positions, w_ada[0], b_ada[0], norm1_g[0], norm2_g[0], w_in[0], nsa_q_norm[0],
        nsa_k_norm[0], cmp_pe_k[0], cmp_w1_k[0], cmp_w2_k[0], cmp_pe_v[0], cmp_w1_v[0], cmp_w2_v[0],
        dil_q_norm[0], dil_k_norm[0], w_up_a[0], w_up_b[0], w_out[0], w_group[0], b_group[0],
        w_router[0], b_router[0])
    out = _moe(x1, h2, route, mod6, w_e_gate[0], w_e_up[0], w_e_down[0], s)
    return out.reshape(b, s, d)
```

```python
import functools

import jax
import jax.numpy as jnp
import numpy as np
from jax import lax
from jax.experimental import pallas as pl
from jax.experimental.pallas import tpu as pltpu

F32 = jnp.float32
BF16 = jnp.bfloat16
I32 = jnp.int32

D_MODEL = 1024
HEAD_DIM = 64
LANES = 128
ROW_TILE = (D_MODEL // LANES, LANES)
ROPE_THETA = 10000.0
EPS = 1e-6
LOG2E = 1.4426950408889634
NEG_INF = -1e30
FORCE_SCORE = 1e9
MASKED = -1e30
PICKED = -3e38

NSA_HEADS = 8
NSA_KV_HEADS = 2
NSA_GROUP = 4
CMP_BLOCK = 32
CMP_STRIDE = 16
CMP_HIDDEN = 256
SEL_BLOCK = 64
N_SEL = 8
N_LOCAL_SEL = 2
WINDOW = 512
DIL_PAIRS = ((128, 1), (512, 4), (2048, 16))
DIL_GROUPS = 3
A_Q = NSA_HEADS * HEAD_DIM
A_KV = NSA_KV_HEADS * HEAD_DIM
DIL_W = 2 * DIL_GROUPS * HEAD_DIM
N_GROUPS = 4
EXPERTS_PER_GROUP = 8
N_EXPERTS = 32
TOP_K = 2
ROUTE_ROWS = 8
EXPERT_FF = 512
MOE_BLOCK = 512

VMEM_LIMIT = 56 * 1024 * 1024

T_QA = 0
T_KVC = 4
T_KVA = 6
T_DIL = 10
T_GA = 19
T_GM = 20
N_TILES = 36

PROJ_TM = 512
NSA_TQ = 128
NSA_TK = 512
NSA_NQ = 4
NSA_AHEAD = 1
DIL_T = 128
DIL_UNROLL = 16
MERGE_TM = 512
RANK_TM = 1024
SLOT_TM = 4096
MOE_TM = 256
DMA_QUEUES = 2


def _dot(a, b):
    return jnp.dot(a, b, preferred_element_type=F32)


def _dot_nt(a, b):
    return lax.dot_general(a, b, (((1,), (1,)), ((), ())), preferred_element_type=F32)


def _dot_tn(a, b):
    return lax.dot_general(a, b, (((0,), (0,)), ((), ())), preferred_element_type=F32)


def _split(a):
    hi = a.astype(BF16)
    lo = (a - hi.astype(F32)).astype(BF16)
    return hi, lo


def _load_token_tiles(ref, first, n, lead=()):
    ns = ROW_TILE[0]
    return jnp.concatenate(
        [ref[lead + (pl.ds(first * ns + s, n, stride=ns), slice(None))] for s in range(ns)], axis=1)


def _store_token_tiles(ref, rows, first=0):
    ns = ROW_TILE[0]
    for s in range(ns):
        ref[pl.ds(first * ns + s, rows.shape[0], stride=ns), :] = rows[:, s * LANES:(s + 1) * LANES]


def _params(*sem):
    return pltpu.CompilerParams(dimension_semantics=sem, vmem_limit_bytes=VMEM_LIMIT)


def _mod_kernel(c_ref, whi_ref, wlo_ref, b_ref, o_ref):
    c = c_ref[...]
    ca = c * jax.nn.sigmoid(c)
    hi, lo = _split(ca)
    whi = whi_ref[...]
    o_ref[...] = _dot(hi, whi) + _dot(lo, whi) + _dot(hi, wlo_ref[...]) + b_ref[...]


def _mod_call(c, w_ada, b_ada):
    b, d = c.shape
    n = w_ada.shape[1]
    whi, wlo = _split(w_ada)
    tn = 1024
    return pl.pallas_call(
        _mod_kernel,
        grid=(n // tn,),
        in_specs=[
            pl.BlockSpec((b, d), lambda j: (0, 0)),
            pl.BlockSpec((d, tn), lambda j: (0, j)),
            pl.BlockSpec((d, tn), lambda j: (0, j)),
            pl.BlockSpec((1, tn), lambda j: (0, j)),
        ],
        out_specs=pl.BlockSpec((b, tn), lambda j: (0, j)),
        out_shape=jax.ShapeDtypeStruct((b, n), F32),
        compiler_params=_params("parallel"),
        name="mod",
    )(c, whi, wlo, b_ada.reshape(1, n))


def _norm_rope(a, gain, cos, sin_signed, lo, first):
    sq = a * a
    s0 = jnp.sum(jnp.where(lo, sq, 0.0), axis=-1, keepdims=True)
    s1 = jnp.sum(jnp.where(lo, 0.0, sq), axis=-1, keepdims=True)
    r = jnp.where(lo, lax.rsqrt(s0 * (1.0 / HEAD_DIM) + EPS), lax.rsqrt(s1 * (1.0 / HEAD_DIM) + EPS))
    y = a * r * gain
    rot = jnp.where(first, pltpu.roll(y, 96, 1), pltpu.roll(y, 32, 1))
    return y * cos + rot * sin_signed


def _proj_kernel(x_ref, mod_ref, g1_ref, w_ref, gains_ref, cos_ref, sin_ref, kw0_ref, vw0_ref,
                 qext_ref, kvc_ref, ksl_ref, vsl_ref, kwp_ref, vwp_ref, dil0_ref, dil1_ref, dil2_ref,
                 ga_ref, gm_ref, perm_scr):
    del kw0_ref, vw0_ref
    tm = x_ref.shape[1]
    x = x_ref[0]
    ms = jnp.mean(x * x, axis=-1, keepdims=True)
    y = x * lax.rsqrt(ms + EPS) * g1_ref[...]
    sh1 = mod_ref[0, 0:1, :]
    sc1 = mod_ref[0, 1:2, :]
    h = (y * (1.0 + sc1) + sh1).astype(BF16)
    lane = lax.broadcasted_iota(I32, (tm, LANES), 1)
    lo = lane < HEAD_DIM
    first = (lane & (HEAD_DIM - 1)) < (HEAD_DIM // 2)
    cos = jnp.concatenate([cos_ref[0]] * 4, axis=1)
    sin = jnp.concatenate([sin_ref[0]] * 4, axis=1)
    sin = jnp.where(first, -sin, sin)
    nr = functools.partial(_norm_rope, cos=cos, sin_signed=sin, lo=lo, first=first)

    for c in range(N_TILES // 2):
        acc = _dot(h, w_ref[:, c * 2 * LANES:(c + 1) * 2 * LANES])
        for half in range(2):
            t = 2 * c + half
            a = acc[:, half * LANES:(half + 1) * LANES]
            if t < T_KVC:
                yq = nr(a, gains_ref[0:1, :])
                rq = pltpu.roll(yq, HEAD_DIM, 1)
                if t // 2 == 0:
                    e0 = jnp.where(lo, yq, 0.0)
                    e1 = jnp.where(lo, rq, 0.0)
                else:
                    e0 = jnp.where(lo, 0.0, rq)
                    e1 = jnp.where(lo, 0.0, yq)
                qext_ref[0, 2 * t] = e0.astype(BF16)
                qext_ref[0, 2 * t + 1] = e1.astype(BF16)
            elif t < T_KVA:
                kvc_ref[0, t - T_KVC] = a
            elif t < T_DIL:
                j = t - T_KVA
                k_out, v_out = (ksl_ref, vsl_ref) if j < 2 else (kwp_ref, vwp_ref)
                if j % 2 == 0:
                    k_out[0] = nr(a, gains_ref[1:2, :]).astype(BF16)
                else:
                    v_out[0, 0] = jnp.where(lo, a, 1.0).astype(BF16)
                    v_out[0, 1] = jnp.where(lo, pltpu.roll(a, HEAD_DIM, 1), 1.0).astype(BF16)
            elif t < T_GA:
                j = t - T_DIL
                if j < 3:
                    v = nr(a, gains_ref[2:3, :])
                elif j < 6:
                    v = nr(a, gains_ref[3:4, :])
                else:
                    v = a
                kind, gi = divmod(j, DIL_GROUPS)
                dl = DIL_PAIRS[gi][1]
                if dl == 1:
                    dil0_ref[0, kind] = v.astype(BF16)
                else:
                    out = (dil0_ref, dil1_ref, dil2_ref)[gi]
                    perm_scr[...] = v
                    for r in range(dl):
                        out[0, kind, r] = perm_scr[pl.ds(r, tm // dl, stride=dl), :].astype(BF16)
            elif t < T_GM:
                ga_ref[0] = jax.nn.sigmoid(a)
            else:
                j = t - T_GM
                gm_ref[0, :, j * LANES:(j + 1) * LANES] = jax.nn.sigmoid(a).astype(BF16)


def _proj_call(x, mod6, g1, w_perm, gains, cos_t, sin_t):
    b, s, d = x.shape
    tm = PROJ_TM
    n = N_TILES * LANES
    row = lambda bi, i: (bi, i, 0)
    heads = lambda bi, i: (bi, 0, i, 0)
    pad = WINDOW // tm
    in_specs = [
        pl.BlockSpec((1, tm, d), row),
        pl.BlockSpec((1, 6, d), lambda bi, i: (bi, 0, 0)),
        pl.BlockSpec((1, d), lambda bi, i: (0, 0)),
        pl.BlockSpec((d, n), lambda bi, i: (0, 0)),
        pl.BlockSpec((4, LANES), lambda bi, i: (0, 0)),
        pl.BlockSpec((1, tm, HEAD_DIM // 2), row),
        pl.BlockSpec((1, tm, HEAD_DIM // 2), row),
        pl.BlockSpec(memory_space=pl.ANY),
        pl.BlockSpec(memory_space=pl.ANY),
    ]
    out_specs = [
        pl.BlockSpec((1, NSA_HEADS, tm, LANES), heads),
        pl.BlockSpec((1, 2, tm, LANES), heads),
        pl.BlockSpec((1, tm, LANES), row),
        pl.BlockSpec((1, NSA_KV_HEADS, tm, LANES), heads),
        pl.BlockSpec((1, tm, LANES), lambda bi, i: (bi, i + pad, 0)),
        pl.BlockSpec((1, NSA_KV_HEADS, tm, LANES), lambda bi, i: (bi, 0, i + pad, 0)),
        pl.BlockSpec((1, 3, tm, LANES), heads),
        *[pl.BlockSpec((1, 3, dl, tm // dl, LANES), lambda bi, i: (bi, 0, 0, i, 0)) for _, dl in DIL_PAIRS[1:]],
        pl.BlockSpec((1, tm, LANES), row),
        pl.BlockSpec((1, tm, 2 * d), row),
    ]
    out_shape = [
        jax.ShapeDtypeStruct((b, NSA_HEADS, s, LANES), BF16),
        jax.ShapeDtypeStruct((b, 2, s, LANES), F32),
        jax.ShapeDtypeStruct((b, s, LANES), BF16),
        jax.ShapeDtypeStruct((b, NSA_KV_HEADS, s, LANES), BF16),
        jax.ShapeDtypeStruct((b, s + WINDOW, LANES), BF16),
        jax.ShapeDtypeStruct((b, NSA_KV_HEADS, s + WINDOW, LANES), BF16),
        jax.ShapeDtypeStruct((b, 3, s, LANES), BF16),
        *[jax.ShapeDtypeStruct((b, 3, dl, s // dl, LANES), BF16) for _, dl in DIL_PAIRS[1:]],
        jax.ShapeDtypeStruct((b, s, LANES), F32),
        jax.ShapeDtypeStruct((b, s, 2 * d), BF16),
    ]
    kw0 = jnp.zeros(out_shape[4].shape, BF16)
    vw0 = jnp.zeros(out_shape[5].shape, BF16)
    return pl.pallas_call(
        _proj_kernel,
        grid=(b, s // tm),
        in_specs=in_specs,
        out_specs=out_specs,
        out_shape=out_shape,
        scratch_shapes=[pltpu.VMEM((tm, LANES), F32)],
        input_output_aliases={7: 4, 8: 5},
        compiler_params=_params("parallel", "parallel"),
        name="proj",
    )(x, mod6, g1, w_perm, gains, cos_t, sin_t, kw0, vw0)


def _cmp_kernel(x_ref, pe_ref, w1_ref, w2_ref, gain_ref, cos_ref, sin_ref, kc_ref, vc_ref):
    nb = kc_ref.shape[1]
    lane = lax.broadcasted_iota(I32, (nb, LANES), 1)
    lo = lane < HEAD_DIM
    first = (lane & (HEAD_DIM - 1)) < (HEAD_DIM // 2)
    for kind in range(2):
        top = jnp.zeros((nb, 2 * CMP_HIDDEN), F32)
        bot = jnp.zeros((nb, 2 * CMP_HIDDEN), F32)
        for r in range(CMP_STRIDE):
            x = x_ref[0, kind, pl.ds(r, nb, stride=CMP_STRIDE), :]
            top = top + _dot((x + pe_ref[kind, r:r + 1, :]).astype(BF16), w1_ref[kind, r])
            bot = bot + _dot((x + pe_ref[kind, CMP_STRIDE + r:CMP_STRIDE + r + 1, :]).astype(BF16),
                             w1_ref[kind, CMP_STRIDE + r])
        hid = top + pltpu.roll(bot, nb - 1, 0)
        hid = (hid * jax.nn.sigmoid(hid)).astype(BF16)
        out = (_dot(hid[:, 0:CMP_HIDDEN], w2_ref[kind, 0])
               + _dot(hid[:, CMP_HIDDEN:2 * CMP_HIDDEN], w2_ref[kind, 1]))
        if kind == 0:
            out = _norm_rope(out, gain_ref[...], cos_ref[0], sin_ref[0], lo, first)
            kc_ref[0] = out.astype(BF16)
        else:
            vc_ref[0, 0] = jnp.where(lo, out, 0.0).astype(BF16)
            vc_ref[0, 1] = jnp.where(lo, pltpu.roll(out, HEAD_DIM, 1), 0.0).astype(BF16)


def _cmp_call(kvc, pe, w1, w2ext, gain_k, ccos, csin):
    b, _, s, _ = kvc.shape
    nb = s // CMP_STRIDE
    return pl.pallas_call(
        _cmp_kernel,
        grid=(b,),
        in_specs=[
            pl.BlockSpec((1, 2, s, LANES), lambda bi: (bi, 0, 0, 0)),
            pl.BlockSpec(pe.shape, lambda bi: (0, 0, 0)),
            pl.BlockSpec(w1.shape, lambda bi: (0, 0, 0, 0)),
            pl.BlockSpec((2, 2, CMP_HIDDEN, LANES), lambda bi: (0, 0, 0, 0)),
            pl.BlockSpec((1, LANES), lambda bi: (0, 0)),
            pl.BlockSpec((1, nb, LANES), lambda bi: (bi, 0, 0)),
            pl.BlockSpec((1, nb, LANES), lambda bi: (bi, 0, 0)),
        ],
        out_specs=[
            pl.BlockSpec((1, nb, LANES), lambda bi: (bi, 0, 0)),
            pl.BlockSpec((1, NSA_KV_HEADS, nb, LANES), lambda bi: (bi, 0, 0, 0)),
        ],
        out_shape=[
            jax.ShapeDtypeStruct((b, nb, LANES), BF16),
            jax.ShapeDtypeStruct((b, NSA_KV_HEADS, nb, LANES), BF16),
        ],
        compiler_params=_params("parallel"),
        name="cmp",
    )(kvc, pe, w1, w2ext, gain_k, ccos, csin)


def _softmax_cols(s, exp=jnp.exp):
    p = exp(s - jnp.max(s, axis=0, keepdims=True))
    return p, jnp.sum(p, axis=0, keepdims=True)


def _block_max(s, offs, bs):
    m = None
    for j, off in enumerate(offs):
        mj = jnp.max(s[j * bs:(j + 1) * bs], axis=0, keepdims=True) + off
        m = mj if m is None else jnp.maximum(m, mj)
    return m


def _block_exp2(s, offs, bs, m):
    return jnp.concatenate(
        [jnp.exp2(s[j * bs:(j + 1) * bs] - (m - off)) for j, off in enumerate(offs)], axis=0)


def _nsa_kernel(q_ref, kc_ref, vc_ref, ksl_ref, vsl_ref, kw_ref, vw_ref, ga_ref, ov_ref, wb_ref,
                o_ref, m_scr, acc_scr, sb_scr, *s_scrs):
    tq = NSA_TQ
    tk = NSA_TK
    hd = HEAD_DIM
    cols = NSA_GROUP * tq
    tiles = range(NSA_NQ)
    s_buf = lambda u, c: s_scrs[2 * u + c % 2]
    n_blk = ov_ref.shape[0]
    n_chunks = ksl_ref.shape[1] // tk
    bpc = tk // SEL_BLOCK
    kh = pl.program_id(1)
    base = pl.multiple_of(pl.program_id(2) * (NSA_NQ * tq), NSA_NQ * tq)
    t0 = [base + u * tq for u in tiles]
    q4 = [q_ref[0, :, u * tq:(u + 1) * tq, :].reshape(cols, LANES) for u in tiles]
    lane_q = lax.broadcasted_iota(I32, (1, tq), 1)
    lane_q4 = lax.broadcasted_iota(I32, (1, cols), 1) & (tq - 1)
    per_head = lambda b: jnp.concatenate([b] * NSA_GROUP, axis=1)
    older_edge = wb_ref[0]
    causal_edge = wb_ref[1]
    nwin = WINDOW + tq
    n_wb = nwin // tq
    blk = lax.broadcasted_iota(I32, (n_blk, 1), 0)
    blk_f = blk.astype(F32)
    ov = ov_ref[...]

    diag = base // tk
    s_cmp, s_win, s_diag = {}, {}, {}
    o_cmp, o_win, sel_bias = {}, {}, {}

    def qk_products(u):
        s_cmp[u] = _dot_nt(kc_ref[0], q4[u])
        s_win[u] = _dot_nt(kw_ref[0, pl.ds(t0[u], nwin), :], q4[u])
        s_diag[u] = _dot_nt(ksl_ref[0, pl.ds(base, (u + 1) * tq), :], q4[u])
        s_buf(u, 0)[...] = _dot_nt(ksl_ref[0, 0:tk, :], q4[u])

    def compressed_and_select(u):
        nb = s_cmp[u].shape[0]
        cmp_end = lax.broadcasted_iota(I32, (nb, 1), 0) * CMP_STRIDE + (CMP_BLOCK - 1)
        valid = cmp_end <= t0[u] + lane_q4
        e, den = _softmax_cols(jnp.where(valid, s_cmp[u], NEG_INF), jnp.exp2)
        p = jnp.where(valid, e / den, 0.0)
        o_cmp[u] = _dot_tn(vc_ref[0, 0], p.astype(BF16))[0:hd]
        psum = p[:, 0:tq] + p[:, tq:2 * tq] + p[:, 2 * tq:3 * tq] + p[:, 3 * tq:4 * tq]
        p_hi, p_lo = _split(psum)
        imp = _dot(ov, p_hi) + _dot(ov, p_lo)
        rel = ((t0[u] + lane_q) >> 6) - blk
        forced = (blk == 0) | ((rel >= 0) & (rel < N_LOCAL_SEL))
        score = jnp.where(rel < 0, NEG_INF, jnp.where(forced, FORCE_SCORE, imp))
        bias = jnp.full((n_blk, tq), MASKED, F32)
        for _ in range(N_SEL):
            best = jnp.max(score, axis=0, keepdims=True)
            first = jnp.min(jnp.where(score == best, blk_f, float(n_blk)), axis=0, keepdims=True)
            pick = blk_f == first
            bias = jnp.where(pick, 0.0, bias)
            score = jnp.where(pick, PICKED, score)
        sel_bias[u] = per_head(jnp.where(rel < 0, MASKED, bias))
        sb_scr[u] = sel_bias[u]

    def window(u):
        sw = jnp.concatenate([s_win[u][0:tq] + per_head(older_edge), s_win[u][tq:nwin - tq],
                              s_win[u][nwin - tq:nwin] + per_head(causal_edge)], axis=0)
        w_offs = [jnp.where(t0[u] - WINDOW + j * tq >= 0, 0.0, MASKED) for j in range(n_wb)]
        pw = _block_exp2(sw, w_offs, tq, _block_max(sw, w_offs, tq))
        ow = _dot_tn(vw_ref[0, 0, pl.ds(t0[u], nwin), :], pw.astype(BF16))
        o_win[u] = ow[0:hd] / ow[hd:hd + 1]

    def diagonal(u):
        nk = (u + 1) * tq
        sd = s_diag[u][nk - tq:nk] + per_head(causal_edge)
        if u > 0:
            sd = jnp.concatenate([s_diag[u][0:nk - tq], sd], axis=0)
        offs = [sb_scr[u, pl.ds(diag * bpc + j, 1), :] for j in range(nk // SEL_BLOCK)]
        m_new = _block_max(sd, offs, SEL_BLOCK)
        vd = vsl_ref[0, 0, pl.ds(base, nk), :]
        acc_scr[u] = _dot_tn(vd, _block_exp2(sd, offs, SEL_BLOCK, m_new).astype(BF16))
        m_scr[u] = m_new

    for stage in (qk_products, compressed_and_select, window, diagonal):
        for u in tiles:
            stage(u)

    def chunk(c):
        ahead = c + 2 < n_chunks

        def next_scores(u):
            s_buf(u, c + 1)[...] = _dot_nt(ksl_ref[0, (c + 1) * tk:(c + 2) * tk, :], q4[u])

        if ahead:
            for u in range(min(NSA_AHEAD, NSA_NQ)):
                next_scores(u)
        vb = vsl_ref[0, 0, c * tk:(c + 1) * tk, :]
        for u in tiles:
            sc = s_buf(u, c)[...]
            offs = [sel_bias[u][c * bpc + j:c * bpc + j + 1] for j in range(bpc)]
            m_old = m_scr[u]
            m_new = jnp.maximum(m_old, _block_max(sc, offs, SEL_BLOCK))
            pe = _block_exp2(sc, offs, SEL_BLOCK, m_new)
            acc_scr[u] = jnp.exp2(m_old - m_new) * acc_scr[u] + _dot_tn(vb, pe.astype(BF16))
            m_scr[u] = m_new
            if ahead and u + NSA_AHEAD < NSA_NQ:
                next_scores(u + NSA_AHEAD)

    for c in range(n_chunks - 1):
        pl.when(c < diag)(functools.partial(chunk, c))

    is0 = kh == 0
    for u in tiles:
        rows = slice(u * tq, (u + 1) * tq)
        o_slc = acc_scr[u, 0:hd, :] / acc_scr[u, hd:hd + 1, :]
        gat = ga_ref[0, rows, :].T
        heads = []
        for g in range(NSA_GROUP):
            c = slice(g * tq, (g + 1) * tq)
            og = jnp.zeros((hd, tq), F32)
            for gi, ob in enumerate((o_cmp[u], o_slc, o_win[u])):
                c0 = gi * NSA_HEADS + g
                c1 = c0 + NSA_GROUP
                gate = jnp.where(is0, gat[c0:c0 + 1, :], gat[c1:c1 + 1, :])
                og = og + gate * ob[:, c]
            heads.append(og)
        for pair in range(2):
            tile = jnp.concatenate([heads[2 * pair], heads[2 * pair + 1]], axis=0)
            o_ref[0, rows, pair * LANES:(pair + 1) * LANES] = tile.T.astype(BF16)


def _nsa_call(qext, kc, vc, ksl, vsl, kwp, vwp, ga, ov_t, wbias):
    b, _, s, _ = qext.shape
    tq = NSA_TQ * NSA_NQ
    nb = kc.shape[1]
    cols = NSA_GROUP * NSA_TQ
    assert NSA_TK == tq
    shared = lambda rows: pl.BlockSpec((1, rows, LANES), lambda bi, k, i: (bi, 0, 0))
    per_kv = lambda rows: pl.BlockSpec((1, 1, rows, LANES), lambda bi, k, i: (bi, k, 0, 0))
    return pl.pallas_call(
        _nsa_kernel,
        grid=(b, NSA_KV_HEADS, s // tq),
        in_specs=[
            pl.BlockSpec((1, NSA_GROUP, tq, LANES), lambda bi, k, i: (bi, k, i, 0)),
            shared(nb), per_kv(nb),
            shared(s), per_kv(s),
            shared(s + WINDOW), per_kv(s + WINDOW),
            pl.BlockSpec((1, tq, LANES), lambda bi, k, i: (bi, i, 0)),
            pl.BlockSpec(ov_t.shape, lambda bi, k, i: (0, 0)),
            pl.BlockSpec(wbias.shape, lambda bi, k, i: (0, 0, 0)),
        ],
        out_specs=pl.BlockSpec((1, tq, 2 * LANES), lambda bi, k, i: (bi, i, k)),
        out_shape=jax.ShapeDtypeStruct((b, s, A_Q), BF16),
        scratch_shapes=[
            pltpu.VMEM((NSA_NQ, 1, cols), F32),
            pltpu.VMEM((NSA_NQ, LANES, cols), F32),
            pltpu.VMEM((NSA_NQ, ov_t.shape[0], cols), F32),
            *[pltpu.VMEM((NSA_TK, cols), F32)] * (2 * NSA_NQ),
        ],
        compiler_params=_params("parallel", "parallel", "parallel"),
        name="nsa",
    )(qext, kc, vc, ksl, vsl, kwp, vwp, ga, ov_t, wbias)


def _dil_kernel(q_ref, k_ref, v_ref, o_ref, lse_ref, *, seg_tiles):
    t = DIL_T
    lo = lax.broadcasted_iota(I32, (t, LANES), 1) < HEAD_DIM
    kj = lax.broadcasted_iota(I32, (t, 2 * t), 0)
    qi = lax.broadcasted_iota(I32, (t, 2 * t), 1) & (t - 1)
    top = lax.broadcasted_iota(I32, (LANES, t), 0) < HEAD_DIM

    def scores(i):
        has_prev = (i & (seg_tiles - 1)) != 0
        cs = pl.multiple_of(i * t, t)
        ps = pl.multiple_of(jnp.maximum(i - 1, 0) * t, t)
        q = q_ref[0, 0, pl.ds(cs, t), :]
        zero = jnp.zeros_like(q)
        q2 = jnp.concatenate([jnp.where(lo, q, zero), jnp.where(lo, zero, q)], axis=0)
        sp = _dot_nt(k_ref[0, 0, pl.ds(ps, t), :], q2)
        sc = _dot_nt(k_ref[0, 0, pl.ds(cs, t), :], q2)
        return has_prev, ps, cs, sp, sc

    def attend(has_prev, ps, cs, sp, sc):
        sp = jnp.where((kj >= qi) & has_prev, sp, MASKED)
        sc = jnp.where(qi >= kj, sc, MASKED)
        m = jnp.maximum(jnp.max(sp, axis=0, keepdims=True), jnp.max(sc, axis=0, keepdims=True))
        pp = jnp.exp(sp - m)
        pc = jnp.exp(sc - m)
        l = jnp.sum(pp, axis=0, keepdims=True) + jnp.sum(pc, axis=0, keepdims=True)
        o = (_dot_tn(v_ref[0, 0, pl.ds(ps, t), :], pp.astype(BF16))
             + _dot_tn(v_ref[0, 0, pl.ds(cs, t), :], pc.astype(BF16))) / l
        lse = m + jnp.log(l)
        o_ref[0, pl.ds(cs, t), :] = jnp.where(top, o[:, 0:t], o[:, t:2 * t]).T
        lse_ref[0, pl.ds(cs, t), :] = jnp.where(top, lse[:, 0:t], lse[:, t:2 * t]).T

    def tiles(j, carry):
        group = [scores(j * DIL_UNROLL + u) for u in range(DIL_UNROLL)]
        for args in group:
            attend(*args)
        return carry

    lax.fori_loop(0, q_ref.shape[2] // (t * DIL_UNROLL), tiles, 0)


def _dil_call(qkv, dl):
    b, _, s, _ = qkv.shape
    kind = lambda j: pl.BlockSpec((1, 1, s, LANES), lambda bi: (bi, j, 0, 0))
    out = pl.BlockSpec((1, s, LANES), lambda bi: (bi, 0, 0))
    return pl.pallas_call(
        functools.partial(_dil_kernel, seg_tiles=s // dl // DIL_T),
        grid=(b,),
        in_specs=[kind(0), kind(1), kind(2)],
        out_specs=[out, out],
        out_shape=[jax.ShapeDtypeStruct((b, s, LANES), F32)] * 2,
        compiler_params=_params("parallel"),
        name=f"dil{dl}",
    )(qkv, qkv, qkv)


def _merge_kernel(x_ref, oa_ref, od0_ref, od1_ref, od2_ref, ls0_ref, ls1_ref, ls2_ref, gm_ref, mod_ref,
                  wa_ref, wb_ref, wo_ref, g2_ref, rcat_ref, rb_ref, x1_ref, h2_ref, route_ref, perm_scr):
    tm = x_ref.shape[0]
    d = x_ref.shape[1]
    hm = tm // 2
    for n, ref in enumerate((od1_ref, od2_ref, ls1_ref, ls2_ref)):
        dl = ref.shape[1]
        for r in range(dl):
            perm_scr[n, pl.ds(r, tm // dl, stride=dl), :] = ref[0, r]

    def up(rows):
        l0, l1, l2 = ls0_ref[rows], perm_scr[2, rows], perm_scr[3, rows]
        mx = jnp.maximum(jnp.maximum(l0, l1), l2)
        e0, e1, e2 = jnp.exp(l0 - mx), jnp.exp(l1 - mx), jnp.exp(l2 - mx)
        den = e0 + e1 + e2
        ob = jnp.concatenate([(od0_ref[rows] * (e0 / den)).astype(BF16),
                              (perm_scr[0, rows] * (e1 / den)).astype(BF16),
                              (perm_scr[1, rows] * (e2 / den)).astype(BF16)], axis=1)
        return _dot(oa_ref[rows], wa_ref[...]), _dot(ob, wb_ref[...])

    def out_proj(rows, ya, yb):
        y = gm_ref[rows, 0:d].astype(F32) * ya + gm_ref[rows, d:2 * d].astype(F32) * yb
        return _dot(y.astype(BF16), wo_ref[...])

    def residual_and_logits(h, rows, z):
        x1 = x_ref[rows] + mod_ref[0, 2:3, :] * z
        x1_ref[rows] = x1
        ms = jnp.mean(x1 * x1, axis=-1, keepdims=True)
        h2 = x1 * lax.rsqrt(ms + EPS) * g2_ref[...]
        h2 = h2 * (1.0 + mod_ref[0, 4:5, :]) + mod_ref[0, 3:4, :]
        _store_token_tiles(h2_ref, h2, first=h * hm)
        hi, lo = _split(h2)
        both = _dot_nt(rcat_ref[...], hi)
        return both[0:LANES] + both[LANES:2 * LANES] + _dot_nt(rcat_ref[0:LANES, :], lo) + rb_ref[...]

    def route(h, logits):
        row = lax.broadcasted_iota(I32, (LANES, hm), 0).astype(F32)
        gl = jnp.where(row < N_GROUPS, logits, NEG_INF)
        gmax = jnp.max(gl, axis=0, keepdims=True)
        g_w = 1.0 / jnp.sum(jnp.exp(gl - gmax), axis=0, keepdims=True)
        g_idx = jnp.min(jnp.where(gl == gmax, row, float(LANES)), axis=0, keepdims=True)
        e_lo = N_GROUPS + EXPERTS_PER_GROUP * g_idx
        el = jnp.where((row >= e_lo) & (row < e_lo + EXPERTS_PER_GROUP), logits, NEG_INF)
        m1 = jnp.max(el, axis=0, keepdims=True)
        i1 = jnp.min(jnp.where(el == m1, row, float(LANES)), axis=0, keepdims=True)
        el2 = jnp.where(row == i1, PICKED, el)
        m2 = jnp.max(el2, axis=0, keepdims=True)
        i2 = jnp.min(jnp.where(el2 == m2, row, float(LANES)), axis=0, keepdims=True)
        ex = jnp.exp(m2 - m1)
        w1 = g_w * (1.0 / (1.0 + ex))
        w2 = g_w * (ex / (1.0 + ex))
        route_ref[:, h * hm:(h + 1) * hm] = jnp.concatenate(
            [i1 - N_GROUPS, i2 - N_GROUPS, w1, w2, jnp.zeros((ROUTE_ROWS - 4, hm), F32)], axis=0)

    halves = (slice(0, hm), slice(hm, tm))
    ups = [up(rows) for rows in halves]
    zs = [out_proj(rows, *u) for rows, u in zip(halves, ups)]
    logits = [residual_and_logits(h, rows, z) for h, (rows, z) in enumerate(zip(halves, zs))]
    for h, lg in enumerate(logits):
        route(h, lg)


def _merge_call(x2, oa, ods, lses, gm, mod6, wa, wb, wo, g2, rcat, rb, s):
    t, d = x2.shape
    b = t // s
    tm = MERGE_TM
    per_b = s // tm
    row = lambda i: (i, 0)
    const = lambda i: (0, 0)

    def by_class(dl):
        return pl.BlockSpec((1, dl, tm // dl, LANES), lambda i: (i // per_b, 0, i % per_b, 0))

    dls = [dl for _, dl in DIL_PAIRS]
    assert dls[0] == 1
    group_specs = [pl.BlockSpec((tm, LANES), row)] + [by_class(dl) for dl in dls[1:]]
    views = lambda arrs: [arrs[0].reshape(t, LANES)] + [
        a.reshape(b, dl, s // dl, LANES) for a, dl in zip(arrs[1:], dls[1:])]
    return pl.pallas_call(
        _merge_kernel,
        grid=(t // tm,),
        in_specs=[
            pl.BlockSpec((tm, d), row),
            pl.BlockSpec((tm, A_Q), row),
            *group_specs, *group_specs,
            pl.BlockSpec((tm, 2 * d), row),
            pl.BlockSpec((1, 6, d), lambda i: (i // per_b, 0, 0)),
            pl.BlockSpec(wa.shape, const),
            pl.BlockSpec(wb.shape, const),
            pl.BlockSpec(wo.shape, const),
            pl.BlockSpec((1, d), const),
            pl.BlockSpec(rcat.shape, const),
            pl.BlockSpec((LANES, 1), const),
        ],
        out_specs=[
            pl.BlockSpec((tm, d), row),
            pl.BlockSpec((tm * ROW_TILE[0], LANES), row),
            pl.BlockSpec((ROUTE_ROWS, tm), lambda i: (0, i)),
        ],
        out_shape=[
            jax.ShapeDtypeStruct((t, d), F32),
            jax.ShapeDtypeStruct((t * ROW_TILE[0], LANES), F32),
            jax.ShapeDtypeStruct((ROUTE_ROWS, t), F32),
        ],
        scratch_shapes=[pltpu.VMEM((2 * (len(dls) - 1), tm, LANES), F32)],
        compiler_params=_params("parallel"),
        name="merge",
    )(x2, oa, *views(ods), *views(lses), gm, mod6, wa, wb, wo, g2, rcat, rb)


def _expert_hits(route_ref, tm):
    row = lax.broadcasted_iota(I32, (N_EXPERTS, tm), 0).astype(F32)
    return [row == route_ref[k:k + 1, :] for k in range(TOP_K)]


def _rank_kernel(route_ref, rank_ref, count_ref, carry_scr):
    tm = route_ref.shape[1]

    @pl.when(pl.program_id(0) == 0)
    def _():
        carry_scr[...] = jnp.zeros(carry_scr.shape, F32)

    hits = _expert_hits(route_ref, tm)
    cnt = jnp.where(hits[0] | hits[1], 1.0, 0.0)
    r = lax.broadcasted_iota(I32, (tm, tm), 0)
    c = lax.broadcasted_iota(I32, (tm, tm), 1)
    earlier = jnp.where(r < c, 1.0, 0.0).astype(BF16)
    before = _dot(cnt.astype(BF16), earlier) + carry_scr[:, 0:1]
    ranks = [jnp.sum(jnp.where(hit, before, 0.0), axis=0, keepdims=True) for hit in hits]
    rank_ref[...] = jnp.concatenate(ranks + [jnp.zeros((ROUTE_ROWS - TOP_K, tm), F32)], axis=0)
    carry_scr[...] = carry_scr[...] + jnp.sum(cnt, axis=1, keepdims=True)
    count_ref[...] = carry_scr[...]


def _rank_call(route):
    t = route.shape[1]
    tm = RANK_TM
    tile = pl.BlockSpec((ROUTE_ROWS, tm), lambda i: (0, i))
    return pl.pallas_call(
        _rank_kernel,
        grid=(t // tm,),
        in_specs=[tile],
        out_specs=[tile, pl.BlockSpec((N_EXPERTS, LANES), lambda i: (0, 0))],
        out_shape=[
            jax.ShapeDtypeStruct((ROUTE_ROWS, t), F32),
            jax.ShapeDtypeStruct((N_EXPERTS, LANES), F32),
        ],
        scratch_shapes=[pltpu.VMEM((N_EXPERTS, LANES), F32)],
        compiler_params=_params("arbitrary"),
        name="rank",
    )(route)


def _slot_kernel(route_ref, rank_ref, start_ref, slot_ref):
    tm = route_ref.shape[1]
    starts = start_ref[...]
    slots = [jnp.sum(jnp.where(hit, starts, 0.0), axis=0, keepdims=True) + rank_ref[k:k + 1, :]
             for k, hit in enumerate(_expert_hits(route_ref, tm))]
    slot_ref[...] = jnp.concatenate(slots + [jnp.zeros((ROUTE_ROWS - TOP_K, tm), F32)], axis=0)


def _slot_call(route, rank, pad_starts):
    t = route.shape[1]
    tm = min(SLOT_TM, t)
    tile = pl.BlockSpec((ROUTE_ROWS, tm), lambda i: (0, i))
    return pl.pallas_call(
        _slot_kernel,
        grid=(t // tm,),
        in_specs=[tile, tile, pl.BlockSpec((N_EXPERTS, tm), lambda i: (0, 0))],
        out_specs=tile,
        out_shape=jax.ShapeDtypeStruct((ROUTE_ROWS, t), F32),
        compiler_params=_params("parallel"),
        name="slots",
    )(route, rank, jnp.broadcast_to(pad_starts[:, None], (N_EXPERTS, tm)))


def _for_static_slot(slot, fn):
    for par in range(2):
        pl.when(slot == par)(functools.partial(fn, par))


def _dispatch_kernel(cnt_ref, pstart_ref, dest_hbm, h_hbm, xs_hbm, idx_smem, hbuf, zbuf,
                     isem, hsem, dsem, zsem):
    tm = MOE_TM
    nd = TOP_K * tm
    i = pl.program_id(0)
    n = pl.num_programs(0)
    slot = i % 2
    n_slots = xs_hbm.shape[0]

    def idx_copy(step, sl):
        return pltpu.make_async_copy(dest_hbm.at[step], idx_smem.at[sl], isem.at[sl])

    def tile_copy(step, bf):
        rows = pl.ds(pl.multiple_of(step * tm, tm), tm)
        return pltpu.make_async_copy(h_hbm.at[rows], hbuf.at[bf], hsem.at[bf])

    def wait_scatter(bf):
        for _ in range(TOP_K):
            pltpu.make_async_copy(hbuf.at[bf], xs_hbm.at[pl.ds(0, tm)], dsem.at[bf]).wait()

    @pl.when(i == 0)
    def _():
        idx_copy(0, 0).start()
        tile_copy(0, 0).start()
        zbuf[...] = jnp.zeros(zbuf.shape, F32)

        def zero_gaps(wait):
            def per_expert(e, used):
                cnt = cnt_ref[e]
                gap = (MOE_BLOCK - cnt % MOE_BLOCK) % MOE_BLOCK
                row = pstart_ref[e] + cnt
                for k in range(MOE_BLOCK.bit_length() - 1):
                    size = 1 << k
                    copy = pltpu.make_async_copy(zbuf.at[pl.ds(0, size)], xs_hbm.at[pl.ds(row, size)], zsem)
                    pl.when(((gap >> k) & 1) == 1)(copy.wait if wait else copy.start)
                    row = row + (gap & size)
                return used + cnt + gap
            return lax.fori_loop(0, N_EXPERTS, per_expert, 0)
        zero_gaps(wait=False)
        used = zero_gaps(wait=True)

        def zero_block(blk):
            rows = pl.ds(pl.multiple_of(blk * MOE_BLOCK, MOE_BLOCK), MOE_BLOCK)
            return pltpu.make_async_copy(zbuf, xs_hbm.at[rows], zsem)

        def start_blk(blk, c):
            zero_block(blk).start()
            return c

        def wait_blk(blk, c):
            zero_block(blk).wait()
            return c
        lax.fori_loop(used // MOE_BLOCK, n_slots // MOE_BLOCK, start_blk, 0)
        lax.fori_loop(used // MOE_BLOCK, n_slots // MOE_BLOCK, wait_blk, 0)

    @pl.when(i > 0)
    def _():
        wait_scatter(1 - slot)

    @pl.when(i + 1 < n)
    def _():
        idx_copy(i + 1, 1 - slot).start()
        tile_copy(i + 1, 1 - slot).start()

    idx_copy(i, slot).wait()
    tile_copy(i, slot).wait()

    def scatter_rows(par):
        for r in range(nd):
            copy = pltpu.make_async_copy(hbuf.at[par, r % tm], xs_hbm.at[idx_smem[par, r]], dsem.at[par])
            copy.start(priority=r % DMA_QUEUES)
    _for_static_slot(slot, scatter_rows)

    @pl.when(i == n - 1)
    def _():
        wait_scatter(slot)


def _dispatch_call(counts, pad_starts, dest_tiles, h2, n_slots):
    n_steps = dest_tiles.shape[0]
    grid_spec = pltpu.PrefetchScalarGridSpec(
        num_scalar_prefetch=2,
        grid=(n_steps,),
        in_specs=[pl.BlockSpec(memory_space=pl.ANY), pl.BlockSpec(memory_space=pl.ANY)],
        out_specs=pl.BlockSpec(memory_space=pl.ANY),
        scratch_shapes=[
            pltpu.SMEM((2, TOP_K * MOE_TM), I32),
            pltpu.VMEM((2, MOE_TM) + ROW_TILE, F32),
            pltpu.VMEM((MOE_BLOCK,) + ROW_TILE, F32),
            pltpu.SemaphoreType.DMA((2,)),
            pltpu.SemaphoreType.DMA((2,)),
            pltpu.SemaphoreType.DMA((2,)),
            pltpu.SemaphoreType.DMA(()),
        ],
    )
    return pl.pallas_call(
        _dispatch_kernel,
        grid_spec=grid_spec,
        out_shape=jax.ShapeDtypeStruct((n_slots,) + ROW_TILE, F32),
        compiler_params=_params("arbitrary"),
        name="dispatch",
    )(counts, pad_starts, dest_tiles, h2)


def _expert_kernel(be_ref, x_ref, wg_ref, wu_ref, wd_ref, y_ref, wg_scr, wu_scr, wd_scr):
    i = pl.program_id(0)

    @pl.when((i == 0) | (be_ref[i] != be_ref[jnp.maximum(i - 1, 0)]))
    def _():
        wg_scr[...] = wg_ref[0].astype(BF16)
        wu_scr[...] = wu_ref[0].astype(BF16)
        wd_scr[...] = wd_ref[0].astype(BF16)

    xb = _load_token_tiles(x_ref, 0, MOE_BLOCK).astype(BF16)
    gate = _dot(xb, wg_scr[...])
    up = _dot(xb, wu_scr[...])
    hid = (gate * jax.nn.sigmoid(gate) * up).astype(BF16)
    _store_token_tiles(y_ref, _dot(hid, wd_scr[...]))


def _expert_call(blk_expert, xs, wg, wu, wd):
    n_blk = blk_expert.shape[0]
    d = wg.shape[1]
    blk = (MOE_BLOCK * ROW_TILE[0], LANES)
    grid_spec = pltpu.PrefetchScalarGridSpec(
        num_scalar_prefetch=1,
        grid=(n_blk,),
        in_specs=[
            pl.BlockSpec(blk, lambda i, be: (i, 0)),
            pl.BlockSpec((1, d, EXPERT_FF), lambda i, be: (be[i], 0, 0)),
            pl.BlockSpec((1, d, EXPERT_FF), lambda i, be: (be[i], 0, 0)),
            pl.BlockSpec((1, EXPERT_FF, d), lambda i, be: (be[i], 0, 0)),
        ],
        out_specs=pl.BlockSpec(blk, lambda i, be: (i, 0)),
        scratch_shapes=[
            pltpu.VMEM((d, EXPERT_FF), BF16),
            pltpu.VMEM((d, EXPERT_FF), BF16),
            pltpu.VMEM((EXPERT_FF, d), BF16),
        ],
    )
    return pl.pallas_call(
        _expert_kernel,
        grid_spec=grid_spec,
        out_shape=jax.ShapeDtypeStruct(xs.shape, F32),
        compiler_params=_params("arbitrary"),
        name="experts",
    )(blk_expert, xs, wg, wu, wd)


def _combine_kernel(dest_hbm, y_hbm, x1_ref, route_ref, mod_ref, o_ref, idx_smem, ybuf, isem, dsem):
    tm = MOE_TM
    nd = TOP_K * tm
    i = pl.program_id(0)
    n = pl.num_programs(0)
    slot = i % 2
    nxt = 1 - slot

    def idx_copy(step, sl):
        return pltpu.make_async_copy(dest_hbm.at[step], idx_smem.at[sl], isem.at[sl])

    ns = ROW_TILE[0]

    def step_rows(sl):
        return pltpu.make_async_copy(y_hbm.at[pl.ds(0, nd * ns)], ybuf.at[sl], dsem.at[sl])

    def issue_rows(par):
        for r in range(nd):
            src = y_hbm.at[pl.ds(pl.multiple_of(idx_smem[par, r] * ns, ns), ns)]
            copy = pltpu.make_async_copy(src, ybuf.at[par, pl.ds(r * ns, ns)], dsem.at[par])
            copy.start(priority=r % DMA_QUEUES)

    @pl.when(i == 0)
    def _():
        idx_copy(0, 0).start()
        idx_copy(0, 0).wait()
        issue_rows(0)

        @pl.when(n > 1)
        def _():
            idx_copy(1, 1).start()

    @pl.when(i + 1 < n)
    def _():
        idx_copy(i + 1, nxt).wait()
        _for_static_slot(nxt, issue_rows)

    @pl.when(i + 2 < n)
    def _():
        idx_copy(i + 2, slot).start()

    step_rows(slot).wait()
    route = jnp.concatenate([route_ref[...], jnp.zeros((LANES - ROUTE_ROWS, tm), F32)], axis=0).T
    w1 = route[:, 2:3]
    w2 = route[:, 3:4]
    y = (w1 * _load_token_tiles(ybuf, 0, tm, lead=(slot,))
         + w2 * _load_token_tiles(ybuf, tm, tm, lead=(slot,)))
    o_ref[...] = x1_ref[...] + mod_ref[0, 5:6, :] * y


def _combine_call(dest_tiles, y_slots, x1, route, mod6, s):
    t, d = x1.shape
    tm = MOE_TM
    per_b = s // tm
    return pl.pallas_call(
        _combine_kernel,
        grid=(t // tm,),
        in_specs=[
            pl.BlockSpec(memory_space=pl.ANY),
            pl.BlockSpec(memory_space=pl.ANY),
            pl.BlockSpec((tm, d), lambda i: (i, 0)),
            pl.BlockSpec((ROUTE_ROWS, tm), lambda i: (0, i)),
            pl.BlockSpec((1, 6, d), lambda i: (i // per_b, 0, 0)),
        ],
        out_specs=pl.BlockSpec((tm, d), lambda i: (i, 0)),
        out_shape=jax.ShapeDtypeStruct((t, d), F32),
        scratch_shapes=[
            pltpu.SMEM((2, TOP_K * tm), I32),
            pltpu.VMEM((2, TOP_K * tm * ROW_TILE[0], LANES), F32),
            pltpu.SemaphoreType.DMA((2,)),
            pltpu.SemaphoreType.DMA((2,)),
        ],
        compiler_params=_params("arbitrary"),
        name="combine",
    )(dest_tiles, y_slots, x1, route, mod6)


def _rope_tables(pos):
    inv = ROPE_THETA ** (-jnp.arange(0, HEAD_DIM, 2, dtype=F32) / HEAD_DIM)
    ang = pos.astype(F32)[..., None] * inv
    return jnp.cos(ang), jnp.sin(ang)


def _rope_tiles(pos):
    cos, sin = _rope_tables(pos)
    return (jnp.concatenate([cos, cos, cos, cos], axis=-1),
            jnp.concatenate([-sin, sin, -sin, sin], axis=-1))


def _selection_constants(s):
    n_c = s // CMP_STRIDE
    n_s = s // SEL_BLOCK
    cs = np.arange(n_c) * CMP_STRIDE
    ss = np.arange(n_s) * SEL_BLOCK
    ov = np.clip(np.minimum(cs[:, None] + CMP_BLOCK, ss[None, :] + SEL_BLOCK)
                 - np.maximum(cs[:, None], ss[None, :]), 0, None).astype(np.float32) / CMP_BLOCK
    kq = np.arange(NSA_TQ)[:, None] - np.arange(NSA_TQ)[None, :]
    wbias = np.stack([np.where(kq >= 0, 0.0, MASKED), np.where(kq <= 0, 0.0, MASKED)]).astype(np.float32)
    return jnp.asarray(ov.T, BF16), jnp.asarray(wbias)


def _mixer_and_router(x, c, positions, w_ada, b_ada, norm1_g, norm2_g, w_in, nsa_q_norm, nsa_k_norm,
                      cmp_pe_k, cmp_w1_k, cmp_w2_k, cmp_pe_v, cmp_w1_v, cmp_w2_v, dil_q_norm,
                      dil_k_norm, w_up_a, w_up_b, w_out, w_group, b_group, w_router, b_router):
    b, s, d = x.shape
    scale = HEAD_DIM ** -0.5
    mod6 = _mod_call(c, w_ada, b_ada).reshape(b, 6, d)

    c1 = A_Q
    c2 = c1 + 6 * A_KV
    c3 = c2 + 3 * NSA_HEADS
    c4 = c3 + 3 * DIL_W
    w_perm = jnp.concatenate([
        w_in[:, :c2], w_in[:, c3:c4],
        jnp.pad(w_in[:, c2:c3], ((0, 0), (0, LANES - 3 * NSA_HEADS))), w_in[:, c4:]], axis=1).astype(BF16)
    two = lambda g: jnp.concatenate([g, g]).astype(F32)
    gains = jnp.stack([two(nsa_q_norm) * (scale * LOG2E), two(nsa_k_norm), two(dil_q_norm) * scale,
                       two(dil_k_norm)])
    cos_t, sin_t = _rope_tables(positions)
    qext, kvc, ksl, vsl, kwp, vwp, dil0, dil1, dil2, ga, gm = _proj_call(
        x, mod6, norm1_g.reshape(1, d), w_perm, gains, cos_t, sin_t)

    pe = jnp.stack([jnp.concatenate([p, p], axis=1) for p in (cmp_pe_k, cmp_pe_v)])
    zero_w = jnp.zeros((CMP_BLOCK, HEAD_DIM, CMP_HIDDEN), F32)

    def per_position(w):
        w = w.reshape(CMP_BLOCK, HEAD_DIM, CMP_HIDDEN)
        return jnp.concatenate([jnp.concatenate([w, zero_w], axis=2),
                                jnp.concatenate([zero_w, w], axis=2)], axis=1)
    w1 = jnp.stack([per_position(cmp_w1_k), per_position(cmp_w1_v)]).astype(BF16)
    zeros = jnp.zeros((CMP_HIDDEN, HEAD_DIM), F32)
    ext = lambda w: jnp.stack([jnp.concatenate([w, zeros], 1), jnp.concatenate([zeros, w], 1)])
    w2ext = jnp.stack([ext(cmp_w2_k), ext(cmp_w2_v)]).astype(BF16)
    cmp_pos = jnp.pad(positions[:, CMP_BLOCK - 1::CMP_STRIDE], ((0, 0), (0, 1)))
    ccos, csin = _rope_tiles(cmp_pos)
    kc, vc = _cmp_call(kvc, pe, w1, w2ext, two(nsa_k_norm).reshape(1, LANES), ccos, csin)

    ov_t, wbias = _selection_constants(s)
    o_a = _nsa_call(qext, kc, vc, ksl, vsl, kwp, vwp, ga, ov_t, wbias)

    ods, lses = zip(*[_dil_call(qkv.reshape(b, 3, s, LANES), dl)
                      for qkv, (_, dl) in zip((dil0, dil1, dil2), DIL_PAIRS)])

    t = b * s
    w_r = jnp.concatenate([w_group, w_router.transpose(1, 0, 2).reshape(d, N_EXPERTS)], axis=1)
    w_r = jnp.pad(w_r, ((0, 0), (0, LANES - w_r.shape[1])))
    rcat = jnp.concatenate(_split(w_r.T), axis=0)
    rb = jnp.pad(jnp.concatenate([b_group, b_router.reshape(-1)]), (0, LANES - N_GROUPS - N_EXPERTS))
    return _merge_call(
        x.reshape(t, d), o_a.reshape(t, A_Q), ods, lses, gm.reshape(t, 2 * d), mod6,
        w_up_a.astype(BF16), w_up_b.astype(BF16), w_out.astype(BF16), norm2_g.reshape(1, d),
        rcat, rb.reshape(LANES, 1).astype(F32), s), mod6


def _moe(x1, h2, route, mod6, w_e_gate, w_e_up, w_e_down, s):
    t, _ = x1.shape
    rank, counts = _rank_call(route)
    counts = counts[:, 0].astype(I32)
    padded = (counts + MOE_BLOCK - 1) // MOE_BLOCK * MOE_BLOCK
    pad_ends = jnp.cumsum(padded)
    pad_starts = pad_ends - padded
    dest = _slot_call(route, rank, pad_starts.astype(F32))[0:TOP_K].astype(I32)
    n_slots = t * TOP_K + N_EXPERTS * MOE_BLOCK
    n_blk = n_slots // MOE_BLOCK
    blk_start = jnp.arange(n_blk, dtype=I32) * MOE_BLOCK
    blk_expert = jnp.minimum(jnp.sum((pad_ends[None, :] <= blk_start[:, None]).astype(I32), axis=1),
                             N_EXPERTS - 1)
    tm = MOE_TM
    dest_tiles = dest.reshape(TOP_K, t // tm, tm).transpose(1, 0, 2).reshape(t // tm, TOP_K * tm)
    xs = _dispatch_call(counts, pad_starts.astype(I32), dest_tiles, h2.reshape((t,) + ROW_TILE), n_slots)
    y_slots = _expert_call(blk_expert, xs.reshape(n_slots * ROW_TILE[0], LANES), w_e_gate, w_e_up, w_e_down)
    return _combine_call(dest_tiles, y_slots, x1, route, mod6, s)


def kernel(x, c, positions, w_ada, b_ada, norm1_g, norm2_g, w_in, nsa_q_norm, nsa_k_norm, cmp_pe_k,
           cmp_w1_k, cmp_w2_k, cmp_pe_v, cmp_w1_v, cmp_w2_v, dil_q_norm, dil_k_norm, w_up_a, w_up_b,
           w_out, w_group, b_group, w_router, b_router, w_e_gate, w_e_up, w_e_down):
    b, s, d = x.shape
    assert w_ada.shape[0] == 1 and d == D_MODEL and s % NSA_TK == 0
    (x1, h2, route), mod6 = _mixer_and_router(
        x, c, positions, w_ada[0], b_ada[0], norm1_g[0], norm2_g[0], w_in[0], nsa_q_norm[0],
        nsa_k_norm[0], cmp_pe_k[0], cmp_w1_k[0], cmp_w2_k[0], cmp_pe_v[0], cmp_w1_v[0], cmp_w2_v[0],
        dil_q_norm[0], dil_k_norm[0], w_up_a[0], w_up_b[0], w_out[0], w_group[0], b_group[0],
        w_router[0], b_router[0])
    out = _moe(x1, h2, route, mod6, w_e_gate[0], w_e_up[0], w_e_down[0], s)
    return out.reshape(b, s, d)
```

```python
import functools

import jax
import jax.numpy as jnp
import numpy as np
from jax import lax
from jax.experimental import pallas as pl
from jax.experimental.pallas import tpu as pltpu

F32 = jnp.float32
BF16 = jnp.bfloat16
I32 = jnp.int32

D_MODEL = 1024
HEAD_DIM = 64
LANES = 128
ROW_TILE = (D_MODEL // LANES, LANES)
ROPE_THETA = 10000.0
EPS = 1e-6
LOG2E = 1.4426950408889634
NEG_INF = -1e30
FORCE_SCORE = 1e9
MASKED = -1e30
PICKED = -3e38

NSA_HEADS = 8
NSA_KV_HEADS = 2
NSA_GROUP = 4
CMP_BLOCK = 32
CMP_STRIDE = 16
CMP_HIDDEN = 256
SEL_BLOCK = 64
N_SEL = 8
N_LOCAL_SEL = 2
WINDOW = 512
DIL_PAIRS = ((128, 1), (512, 4), (2048, 16))
DIL_GROUPS = 3
A_Q = NSA_HEADS * HEAD_DIM
A_KV = NSA_KV_HEADS * HEAD_DIM
DIL_W = 2 * DIL_GROUPS * HEAD_DIM
N_GROUPS = 4
EXPERTS_PER_GROUP = 8
N_EXPERTS = 32
TOP_K = 2
ROUTE_ROWS = 8
EXPERT_FF = 512
MOE_BLOCK = 512

VMEM_LIMIT = 56 * 1024 * 1024

T_KVC = 4
T_KVA = 6
T_DIL = 10
T_GA = 19
T_GM = 20
N_TILES = 36

PROJ_TM = 512
NSA_TQ = 128
NSA_TK = 512
NSA_NQ = 4
NSA_AHEAD = 1
DIL_T = 128
DIL_UNROLL = 16
MERGE_TM = 512
RANK_TM = 1024
SLOT_TM = 4096
MOE_TM = 256
DMA_QUEUES = 2


def _dot(a, b):
    return jnp.dot(a, b, preferred_element_type=F32)


def _dot_nt(a, b):
    return lax.dot_general(a, b, (((1,), (1,)), ((), ())), preferred_element_type=F32)


def _dot_tn(a, b):
    return lax.dot_general(a, b, (((0,), (0,)), ((), ())), preferred_element_type=F32)


def _split(a):
    hi = a.astype(BF16)
    lo = (a - hi.astype(F32)).astype(BF16)
    return hi, lo


def _load_token_tiles(ref, first, n, lead=()):
    ns = ROW_TILE[0]
    return jnp.concatenate(
        [ref[lead + (pl.ds(first * ns + s, n, stride=ns), slice(None))] for s in range(ns)], axis=1)


def _store_token_tiles(ref, rows, first=0):
    ns = ROW_TILE[0]
    for s in range(ns):
        ref[pl.ds(first * ns + s, rows.shape[0], stride=ns), :] = rows[:, s * LANES:(s + 1) * LANES]


def _params(*sem):
    return pltpu.CompilerParams(dimension_semantics=sem, vmem_limit_bytes=VMEM_LIMIT)


def _mod_kernel(c_ref, whi_ref, wlo_ref, b_ref, o_ref):
    c = c_ref[...]
    ca = c * jax.nn.sigmoid(c)
    hi, lo = _split(ca)
    whi = whi_ref[...]
    o_ref[...] = _dot(hi, whi) + _dot(lo, whi) + _dot(hi, wlo_ref[...]) + b_ref[...]


def _mod_call(c, w_ada, b_ada):
    b, d = c.shape
    n = w_ada.shape[1]
    whi, wlo = _split(w_ada)
    tn = 1024
    return pl.pallas_call(
        _mod_kernel,
        grid=(n // tn,),
        in_specs=[
            pl.BlockSpec((b, d), lambda j: (0, 0)),
            pl.BlockSpec((d, tn), lambda j: (0, j)),
            pl.BlockSpec((d, tn), lambda j: (0, j)),
            pl.BlockSpec((1, tn), lambda j: (0, j)),
        ],
        out_specs=pl.BlockSpec((b, tn), lambda j: (0, j)),
        out_shape=jax.ShapeDtypeStruct((b, n), F32),
        compiler_params=_params("parallel"),
        name="mod",
    )(c, whi, wlo, b_ada.reshape(1, n))


def _norm_rope(a, gain, cos, sin_signed, lo, first):
    sq = a * a
    s0 = jnp.sum(jnp.where(lo, sq, 0.0), axis=-1, keepdims=True)
    s1 = jnp.sum(jnp.where(lo, 0.0, sq), axis=-1, keepdims=True)
    r = jnp.where(lo, lax.rsqrt(s0 * (1.0 / HEAD_DIM) + EPS), lax.rsqrt(s1 * (1.0 / HEAD_DIM) + EPS))
    y = a * r * gain
    rot = jnp.where(first, pltpu.roll(y, 96, 1), pltpu.roll(y, 32, 1))
    return y * cos + rot * sin_signed


def _proj_kernel(x_ref, mod_ref, g1_ref, w_ref, gains_ref, cos_ref, sin_ref, kw0_ref, vw0_ref,
                 qext_ref, kvc_ref, ksl_ref, vsl_ref, kwp_ref, vwp_ref, dil0_ref, dil1_ref, dil2_ref,
                 ga_ref, gm_ref, perm_scr):
    del kw0_ref, vw0_ref
    tm = x_ref.shape[1]
    x = x_ref[0]
    ms = jnp.mean(x * x, axis=-1, keepdims=True)
    y = x * lax.rsqrt(ms + EPS) * g1_ref[...]
    sh1 = mod_ref[0, 0:1, :]
    sc1 = mod_ref[0, 1:2, :]
    h = (y * (1.0 + sc1) + sh1).astype(BF16)
    lane = lax.broadcasted_iota(I32, (tm, LANES), 1)
    lo = lane < HEAD_DIM
    first = (lane & (HEAD_DIM - 1)) < (HEAD_DIM // 2)
    cos = jnp.concatenate([cos_ref[0]] * 4, axis=1)
    sin = jnp.concatenate([sin_ref[0]] * 4, axis=1)
    sin = jnp.where(first, -sin, sin)
    nr = functools.partial(_norm_rope, cos=cos, sin_signed=sin, lo=lo, first=first)

    for c in range(N_TILES // 2):
        acc = _dot(h, w_ref[:, c * 2 * LANES:(c + 1) * 2 * LANES])
        for half in range(2):
            t = 2 * c + half
            a = acc[:, half * LANES:(half + 1) * LANES]
            if t < T_KVC:
                yq = nr(a, gains_ref[0:1, :])
                rq = pltpu.roll(yq, HEAD_DIM, 1)
                if t // 2 == 0:
                    e0 = jnp.where(lo, yq, 0.0)
                    e1 = jnp.where(lo, rq, 0.0)
                else:
                    e0 = jnp.where(lo, 0.0, rq)
                    e1 = jnp.where(lo, 0.0, yq)
                qext_ref[0, 2 * t] = e0.astype(BF16)
                qext_ref[0, 2 * t + 1] = e1.astype(BF16)
            elif t < T_KVA:
                kvc_ref[0, t - T_KVC] = a
            elif t < T_DIL:
                j = t - T_KVA
                k_out, v_out = (ksl_ref, vsl_ref) if j < 2 else (kwp_ref, vwp_ref)
                if j % 2 == 0:
                    k_out[0] = nr(a, gains_ref[1:2, :]).astype(BF16)
                else:
                    v_out[0, 0] = jnp.where(lo, a, 1.0).astype(BF16)
                    v_out[0, 1] = jnp.where(lo, pltpu.roll(a, HEAD_DIM, 1), 1.0).astype(BF16)
            elif t < T_GA:
                j = t - T_DIL
                if j < 3:
                    v = nr(a, gains_ref[2:3, :])
                elif j < 6:
                    v = nr(a, gains_ref[3:4, :])
                else:
                    v = a
                kind, gi = divmod(j, DIL_GROUPS)
                dl = DIL_PAIRS[gi][1]
                if dl == 1:
                    dil0_ref[0, kind] = v.astype(BF16)
                else:
                    out = (dil0_ref, dil1_ref, dil2_ref)[gi]
                    perm_scr[...] = v
                    for r in range(dl):
                        out[0, kind, r] = perm_scr[pl.ds(r, tm // dl, stride=dl), :].astype(BF16)
            elif t < T_GM:
                ga_ref[0] = jax.nn.sigmoid(a)
            else:
                j = t - T_GM
                gm_ref[0, :, j * LANES:(j + 1) * LANES] = jax.nn.sigmoid(a).astype(BF16)


def _proj_call(x, mod6, g1, w_perm, gains, cos_t, sin_t):
    b, s, d = x.shape
    tm = PROJ_TM
    n = N_TILES * LANES
    row = lambda bi, i: (bi, i, 0)
    heads = lambda bi, i: (bi, 0, i, 0)
    pad = WINDOW // tm
    in_specs = [
        pl.BlockSpec((1, tm, d), row),
        pl.BlockSpec((1, 6, d), lambda bi, i: (bi, 0, 0)),
        pl.BlockSpec((1, d), lambda bi, i: (0, 0)),
        pl.BlockSpec((d, n), lambda bi, i: (0, 0)),
        pl.BlockSpec((4, LANES), lambda bi, i: (0, 0)),
        pl.BlockSpec((1, tm, HEAD_DIM // 2), row),
        pl.BlockSpec((1, tm, HEAD_DIM // 2), row),
        pl.BlockSpec(memory_space=pl.ANY),
        pl.BlockSpec(memory_space=pl.ANY),
    ]
    out_specs = [
        pl.BlockSpec((1, NSA_HEADS, tm, LANES), heads),
        pl.BlockSpec((1, 2, tm, LANES), heads),
        pl.BlockSpec((1, tm, LANES), row),
        pl.BlockSpec((1, NSA_KV_HEADS, tm, LANES), heads),
        pl.BlockSpec((1, tm, LANES), lambda bi, i: (bi, i + pad, 0)),
        pl.BlockSpec((1, NSA_KV_HEADS, tm, LANES), lambda bi, i: (bi, 0, i + pad, 0)),
        pl.BlockSpec((1, 3, tm, LANES), heads),
        *[pl.BlockSpec((1, 3, dl, tm // dl, LANES), lambda bi, i: (bi, 0, 0, i, 0)) for _, dl in DIL_PAIRS[1:]],
        pl.BlockSpec((1, tm, LANES), row),
        pl.BlockSpec((1, tm, 2 * d), row),
    ]
    out_shape = [
        jax.ShapeDtypeStruct((b, NSA_HEADS, s, LANES), BF16),
        jax.ShapeDtypeStruct((b, 2, s, LANES), F32),
        jax.ShapeDtypeStruct((b, s, LANES), BF16),
        jax.ShapeDtypeStruct((b, NSA_KV_HEADS, s, LANES), BF16),
        jax.ShapeDtypeStruct((b, s + WINDOW, LANES), BF16),
        jax.ShapeDtypeStruct((b, NSA_KV_HEADS, s + WINDOW, LANES), BF16),
        jax.ShapeDtypeStruct((b, 3, s, LANES), BF16),
        *[jax.ShapeDtypeStruct((b, 3, dl, s // dl, LANES), BF16) for _, dl in DIL_PAIRS[1:]],
        jax.ShapeDtypeStruct((b, s, LANES), F32),
        jax.ShapeDtypeStruct((b, s, 2 * d), BF16),
    ]
    kw0 = jnp.zeros(out_shape[4].shape, BF16)
    vw0 = jnp.zeros(out_shape[5].shape, BF16)
    return pl.pallas_call(
        _proj_kernel,
        grid=(b, s // tm),
        in_specs=in_specs,
        out_specs=out_specs,
        out_shape=out_shape,
        scratch_shapes=[pltpu.VMEM((tm, LANES), F32)],
        input_output_aliases={7: 4, 8: 5},
        compiler_params=_params("parallel", "parallel"),
        name="proj",
    )(x, mod6, g1, w_perm, gains, cos_t, sin_t, kw0, vw0)


def _cmp_kernel(x_ref, pe_ref, w1_ref, w2_ref, gain_ref, cos_ref, sin_ref, kc_ref, vc_ref):
    nb = kc_ref.shape[1]
    lane = lax.broadcasted_iota(I32, (nb, LANES), 1)
    lo = lane < HEAD_DIM
    first = (lane & (HEAD_DIM - 1)) < (HEAD_DIM // 2)
    for kind in range(2):
        top = jnp.zeros((nb, 2 * CMP_HIDDEN), F32)
        bot = jnp.zeros((nb, 2 * CMP_HIDDEN), F32)
        for r in range(CMP_STRIDE):
            x = x_ref[0, kind, pl.ds(r, nb, stride=CMP_STRIDE), :]
            top = top + _dot((x + pe_ref[kind, r:r + 1, :]).astype(BF16), w1_ref[kind, r])
            bot = bot + _dot((x + pe_ref[kind, CMP_STRIDE + r:CMP_STRIDE + r + 1, :]).astype(BF16),
                             w1_ref[kind, CMP_STRIDE + r])
        hid = top + pltpu.roll(bot, nb - 1, 0)
        hid = (hid * jax.nn.sigmoid(hid)).astype(BF16)
        out = (_dot(hid[:, 0:CMP_HIDDEN], w2_ref[kind, 0])
               + _dot(hid[:, CMP_HIDDEN:2 * CMP_HIDDEN], w2_ref[kind, 1]))
        if kind == 0:
            out = _norm_rope(out, gain_ref[...], cos_ref[0], sin_ref[0], lo, first)
            kc_ref[0] = out.astype(BF16)
        else:
            vc_ref[0, 0] = jnp.where(lo, out, 0.0).astype(BF16)
            vc_ref[0, 1] = jnp.where(lo, pltpu.roll(out, HEAD_DIM, 1), 0.0).astype(BF16)


def _cmp_call(kvc, pe, w1, w2ext, gain_k, ccos, csin):
    b, _, s, _ = kvc.shape
    nb = s // CMP_STRIDE
    return pl.pallas_call(
        _cmp_kernel,
        grid=(b,),
        in_specs=[
            pl.BlockSpec((1, 2, s, LANES), lambda bi: (bi, 0, 0, 0)),
            pl.BlockSpec(pe.shape, lambda bi: (0, 0, 0)),
            pl.BlockSpec(w1.shape, lambda bi: (0, 0, 0, 0)),
            pl.BlockSpec((2, 2, CMP_HIDDEN, LANES), lambda bi: (0, 0, 0, 0)),
            pl.BlockSpec((1, LANES), lambda bi: (0, 0)),
            pl.BlockSpec((1, nb, LANES), lambda bi: (bi, 0, 0)),
            pl.BlockSpec((1, nb, LANES), lambda bi: (bi, 0, 0)),
        ],
        out_specs=[
            pl.BlockSpec((1, nb, LANES), lambda bi: (bi, 0, 0)),
            pl.BlockSpec((1, NSA_KV_HEADS, nb, LANES), lambda bi: (bi, 0, 0, 0)),
        ],
        out_shape=[
            jax.ShapeDtypeStruct((b, nb, LANES), BF16),
            jax.ShapeDtypeStruct((b, NSA_KV_HEADS, nb, LANES), BF16),
        ],
        compiler_params=_params("parallel"),
        name="cmp",
    )(kvc, pe, w1, w2ext, gain_k, ccos, csin)


def _softmax_cols(s, exp=jnp.exp):
    p = exp(s - jnp.max(s, axis=0, keepdims=True))
    return p, jnp.sum(p, axis=0, keepdims=True)


def _block_max(s, offs, bs):
    m = None
    for j, off in enumerate(offs):
        mj = jnp.max(s[j * bs:(j + 1) * bs], axis=0, keepdims=True) + off
        m = mj if m is None else jnp.maximum(m, mj)
    return m


def _block_exp2(s, offs, bs, m):
    return jnp.concatenate(
        [jnp.exp2(s[j * bs:(j + 1) * bs] - (m - off)) for j, off in enumerate(offs)], axis=0)


def _nsa_kernel(q_ref, kc_ref, vc_ref, ksl_ref, vsl_ref, kw_ref, vw_ref, ga_ref, ov_ref, wb_ref,
                o_ref, m_scr, acc_scr, sb_scr, *s_scrs):
    tq = NSA_TQ
    tk = NSA_TK
    hd = HEAD_DIM
    cols = NSA_GROUP * tq
    tiles = range(NSA_NQ)
    s_buf = lambda u, c: s_scrs[2 * u + c % 2]
    n_blk = ov_ref.shape[0]
    n_chunks = ksl_ref.shape[1] // tk
    bpc = tk // SEL_BLOCK
    kh = pl.program_id(1)
    base = pl.multiple_of(pl.program_id(2) * (NSA_NQ * tq), NSA_NQ * tq)
    t0 = [base + u * tq for u in tiles]
    q4 = [q_ref[0, :, u * tq:(u + 1) * tq, :].reshape(cols, LANES) for u in tiles]
    lane_q = lax.broadcasted_iota(I32, (1, tq), 1)
    lane_q4 = lax.broadcasted_iota(I32, (1, cols), 1) & (tq - 1)
    per_head = lambda b: jnp.concatenate([b] * NSA_GROUP, axis=1)
    older_edge = wb_ref[0]
    causal_edge = wb_ref[1]
    nwin = WINDOW + tq
    n_wb = nwin // tq
    blk = lax.broadcasted_iota(I32, (n_blk, 1), 0)
    blk_f = blk.astype(F32)
    ov = ov_ref[...]

    diag = base // tk
    s_cmp, s_win, s_diag = {}, {}, {}
    o_cmp, o_win, sel_bias = {}, {}, {}

    def qk_products(u):
        s_cmp[u] = _dot_nt(kc_ref[0], q4[u])
        s_win[u] = _dot_nt(kw_ref[0, pl.ds(t0[u], nwin), :], q4[u])

    def qk_selected(u):
        s_diag[u] = _dot_nt(ksl_ref[0, pl.ds(base, (u + 1) * tq), :], q4[u])
        s_buf(u, 0)[...] = _dot_nt(ksl_ref[0, 0:tk, :], q4[u])

    def compressed_and_select(u):
        nb = s_cmp[u].shape[0]
        cmp_end = lax.broadcasted_iota(I32, (nb, 1), 0) * CMP_STRIDE + (CMP_BLOCK - 1)
        valid = cmp_end <= t0[u] + lane_q4
        e, den = _softmax_cols(jnp.where(valid, s_cmp[u], NEG_INF), jnp.exp2)
        p = jnp.where(valid, e / den, 0.0)
        o_cmp[u] = _dot_tn(vc_ref[0, 0], p.astype(BF16))[0:hd]
        psum = p[:, 0:tq] + p[:, tq:2 * tq] + p[:, 2 * tq:3 * tq] + p[:, 3 * tq:4 * tq]
        p_hi, p_lo = _split(psum)
        imp = _dot(ov, p_hi) + _dot(ov, p_lo)
        rel = ((t0[u] + lane_q) >> 6) - blk
        forced = (blk == 0) | ((rel >= 0) & (rel < N_LOCAL_SEL))
        score = jnp.where(rel < 0, NEG_INF, jnp.where(forced, FORCE_SCORE, imp))
        bias = jnp.full((n_blk, tq), MASKED, F32)
        for _ in range(N_SEL):
            best = jnp.max(score, axis=0, keepdims=True)
            first = jnp.min(jnp.where(score == best, blk_f, float(n_blk)), axis=0, keepdims=True)
            pick = blk_f == first
            bias = jnp.where(pick, 0.0, bias)
            score = jnp.where(pick, PICKED, score)
        sel_bias[u] = per_head(jnp.where(rel < 0, MASKED, bias))
        sb_scr[u] = sel_bias[u]

    def window(u):
        sw = jnp.concatenate([s_win[u][0:tq] + per_head(older_edge), s_win[u][tq:nwin - tq],
                              s_win[u][nwin - tq:nwin] + per_head(causal_edge)], axis=0)
        w_offs = [jnp.where(t0[u] - WINDOW + j * tq >= 0, 0.0, MASKED) for j in range(n_wb)]
        pw = _block_exp2(sw, w_offs, tq, _block_max(sw, w_offs, tq))
        ow = _dot_tn(vw_ref[0, 0, pl.ds(t0[u], nwin), :], pw.astype(BF16))
        o_win[u] = ow[0:hd] / ow[hd:hd + 1]

    def diagonal(u):
        nk = (u + 1) * tq
        sd = s_diag[u][nk - tq:nk] + per_head(causal_edge)
        if u > 0:
            sd = jnp.concatenate([s_diag[u][0:nk - tq], sd], axis=0)
        offs = [sb_scr[u, pl.ds(diag * bpc + j, 1), :] for j in range(nk // SEL_BLOCK)]
        m_new = _block_max(sd, offs, SEL_BLOCK)
        vd = vsl_ref[0, 0, pl.ds(base, nk), :]
        acc_scr[u] = _dot_tn(vd, _block_exp2(sd, offs, SEL_BLOCK, m_new).astype(BF16))
        m_scr[u] = m_new

    for u in tiles:
        qk_products(u)
    for u in tiles:
        compressed_and_select(u)
        qk_selected(u)
    for u in tiles:
        window(u)
        diagonal(u)

    def chunk(c):
        ahead = c + 2 < n_chunks

        def next_scores(u):
            s_buf(u, c + 1)[...] = _dot_nt(ksl_ref[0, (c + 1) * tk:(c + 2) * tk, :], q4[u])

        if ahead:
            for u in range(min(NSA_AHEAD, NSA_NQ)):
                next_scores(u)
        vb = vsl_ref[0, 0, c * tk:(c + 1) * tk, :]
        for u in tiles:
            sc = s_buf(u, c)[...]
            offs = [sel_bias[u][c * bpc + j:c * bpc + j + 1] for j in range(bpc)]
            m_old = m_scr[u]
            m_new = jnp.maximum(m_old, _block_max(sc, offs, SEL_BLOCK))
            pe = _block_exp2(sc, offs, SEL_BLOCK, m_new)
            acc_scr[u] = jnp.exp2(m_old - m_new) * acc_scr[u] + _dot_tn(vb, pe.astype(BF16))
            m_scr[u] = m_new
            if ahead and u + NSA_AHEAD < NSA_NQ:
                next_scores(u + NSA_AHEAD)

    for c in range(n_chunks - 1):
        pl.when(c < diag)(functools.partial(chunk, c))

    is0 = kh == 0
    for u in tiles:
        rows = slice(u * tq, (u + 1) * tq)
        o_slc = acc_scr[u, 0:hd, :] / acc_scr[u, hd:hd + 1, :]
        gat = ga_ref[0, rows, :].T
        heads = []
        for g in range(NSA_GROUP):
            c = slice(g * tq, (g + 1) * tq)
            og = jnp.zeros((hd, tq), F32)
            for gi, ob in enumerate((o_cmp[u], o_slc, o_win[u])):
                c0 = gi * NSA_HEADS + g
                c1 = c0 + NSA_GROUP
                gate = jnp.where(is0, gat[c0:c0 + 1, :], gat[c1:c1 + 1, :])
                og = og + gate * ob[:, c]
            heads.append(og)
        for pair in range(2):
            tile = jnp.concatenate([heads[2 * pair], heads[2 * pair + 1]], axis=0)
            o_ref[0, rows, pair * LANES:(pair + 1) * LANES] = tile.T.astype(BF16)


def _nsa_call(qext, kc, vc, ksl, vsl, kwp, vwp, ga, ov_t, wbias):
    b, _, s, _ = qext.shape
    tq = NSA_TQ * NSA_NQ
    nb = kc.shape[1]
    cols = NSA_GROUP * NSA_TQ
    assert NSA_TK == tq
    shared = lambda rows: pl.BlockSpec((1, rows, LANES), lambda bi, k, i: (bi, 0, 0))
    per_kv = lambda rows: pl.BlockSpec((1, 1, rows, LANES), lambda bi, k, i: (bi, k, 0, 0))
    return pl.pallas_call(
        _nsa_kernel,
        grid=(b, NSA_KV_HEADS, s // tq),
        in_specs=[
            pl.BlockSpec((1, NSA_GROUP, tq, LANES), lambda bi, k, i: (bi, k, i, 0)),
            shared(nb), per_kv(nb),
            shared(s), per_kv(s),
            shared(s + WINDOW), per_kv(s + WINDOW),
            pl.BlockSpec((1, tq, LANES), lambda bi, k, i: (bi, i, 0)),
            pl.BlockSpec(ov_t.shape, lambda bi, k, i: (0, 0)),
            pl.BlockSpec(wbias.shape, lambda bi, k, i: (0, 0, 0)),
        ],
        out_specs=pl.BlockSpec((1, tq, 2 * LANES), lambda bi, k, i: (bi, i, k)),
        out_shape=jax.ShapeDtypeStruct((b, s, A_Q), BF16),
        scratch_shapes=[
            pltpu.VMEM((NSA_NQ, 1, cols), F32),
            pltpu.VMEM((NSA_NQ, LANES, cols), F32),
            pltpu.VMEM((NSA_NQ, ov_t.shape[0], cols), F32),
            *[pltpu.VMEM((NSA_TK, cols), F32)] * (2 * NSA_NQ),
        ],
        compiler_params=_params("parallel", "parallel", "parallel"),
        name="nsa",
    )(qext, kc, vc, ksl, vsl, kwp, vwp, ga, ov_t, wbias)


def _dil_kernel(q_ref, k_ref, v_ref, o_ref, lse_ref, *, seg_tiles):
    t = DIL_T
    lo = lax.broadcasted_iota(I32, (t, LANES), 1) < HEAD_DIM
    kj = lax.broadcasted_iota(I32, (t, 2 * t), 0)
    qi = lax.broadcasted_iota(I32, (t, 2 * t), 1) & (t - 1)
    top = lax.broadcasted_iota(I32, (LANES, t), 0) < HEAD_DIM

    def scores(i):
        has_prev = (i & (seg_tiles - 1)) != 0
        cs = pl.multiple_of(i * t, t)
        ps = pl.multiple_of(jnp.maximum(i - 1, 0) * t, t)
        q = q_ref[0, 0, pl.ds(cs, t), :]
        zero = jnp.zeros_like(q)
        q2 = jnp.concatenate([jnp.where(lo, q, zero), jnp.where(lo, zero, q)], axis=0)
        sp = _dot_nt(k_ref[0, 0, pl.ds(ps, t), :], q2)
        sc = _dot_nt(k_ref[0, 0, pl.ds(cs, t), :], q2)
        return has_prev, ps, cs, sp, sc

    def attend(has_prev, ps, cs, sp, sc):
        sp = jnp.where((kj >= qi) & has_prev, sp, MASKED)
        sc = jnp.where(qi >= kj, sc, MASKED)
        m = jnp.maximum(jnp.max(sp, axis=0, keepdims=True), jnp.max(sc, axis=0, keepdims=True))
        pp = jnp.exp(sp - m)
        pc = jnp.exp(sc - m)
        l = jnp.sum(pp, axis=0, keepdims=True) + jnp.sum(pc, axis=0, keepdims=True)
        o = (_dot_tn(v_ref[0, 0, pl.ds(ps, t), :], pp.astype(BF16))
             + _dot_tn(v_ref[0, 0, pl.ds(cs, t), :], pc.astype(BF16))) / l
        lse = m + jnp.log(l)
        o_ref[0, pl.ds(cs, t), :] = jnp.where(top, o[:, 0:t], o[:, t:2 * t]).T
        lse_ref[0, pl.ds(cs, t), :] = jnp.where(top, lse[:, 0:t], lse[:, t:2 * t]).T

    def tiles(j, carry):
        group = [scores(j * DIL_UNROLL + u) for u in range(DIL_UNROLL)]
        for args in group:
            attend(*args)
        return carry

    lax.fori_loop(0, q_ref.shape[2] // (t * DIL_UNROLL), tiles, 0)


def _dil_call(qkv, dl):
    b, _, s, _ = qkv.shape
    kind = lambda j: pl.BlockSpec((1, 1, s, LANES), lambda bi: (bi, j, 0, 0))
    out = pl.BlockSpec((1, s, LANES), lambda bi: (bi, 0, 0))
    return pl.pallas_call(
        functools.partial(_dil_kernel, seg_tiles=s // dl // DIL_T),
        grid=(b,),
        in_specs=[kind(0), kind(1), kind(2)],
        out_specs=[out, out],
        out_shape=[jax.ShapeDtypeStruct((b, s, LANES), F32)] * 2,
        compiler_params=_params("parallel"),
        name=f"dil{dl}",
    )(qkv, qkv, qkv)


def _merge_kernel(x_ref, oa_ref, od0_ref, od1_ref, od2_ref, ls0_ref, ls1_ref, ls2_ref, gm_ref, mod_ref,
                  wa_ref, wb_ref, wo_ref, g2_ref, rcat_ref, rb_ref, x1_ref, h2_ref, route_ref, perm_scr):
    tm = x_ref.shape[0]
    d = x_ref.shape[1]
    hm = tm // 2
    for n, ref in enumerate((od1_ref, od2_ref, ls1_ref, ls2_ref)):
        dl = ref.shape[1]
        for r in range(dl):
            perm_scr[n, pl.ds(r, tm // dl, stride=dl), :] = ref[0, r]

    def up(rows):
        l0, l1, l2 = ls0_ref[rows], perm_scr[2, rows], perm_scr[3, rows]
        mx = jnp.maximum(jnp.maximum(l0, l1), l2)
        e0, e1, e2 = jnp.exp(l0 - mx), jnp.exp(l1 - mx), jnp.exp(l2 - mx)
        den = e0 + e1 + e2
        ob = jnp.concatenate([(od0_ref[rows] * (e0 / den)).astype(BF16),
                              (perm_scr[0, rows] * (e1 / den)).astype(BF16),
                              (perm_scr[1, rows] * (e2 / den)).astype(BF16)], axis=1)
        return _dot(oa_ref[rows], wa_ref[...]), _dot(ob, wb_ref[...])

    def out_proj(rows, ya, yb):
        y = gm_ref[rows, 0:d].astype(F32) * ya + gm_ref[rows, d:2 * d].astype(F32) * yb
        return _dot(y.astype(BF16), wo_ref[...])

    def residual_and_logits(h, rows, z):
        x1 = x_ref[rows] + mod_ref[0, 2:3, :] * z
        x1_ref[rows] = x1
        ms = jnp.mean(x1 * x1, axis=-1, keepdims=True)
        h2 = x1 * lax.rsqrt(ms + EPS) * g2_ref[...]
        h2 = h2 * (1.0 + mod_ref[0, 4:5, :]) + mod_ref[0, 3:4, :]
        _store_token_tiles(h2_ref, h2, first=h * hm)
        hi, lo = _split(h2)
        both = _dot_nt(rcat_ref[...], hi)
        return both[0:LANES] + both[LANES:2 * LANES] + _dot_nt(rcat_ref[0:LANES, :], lo) + rb_ref[...]

    def route(h, logits):
        row = lax.broadcasted_iota(I32, (LANES, hm), 0).astype(F32)
        gl = jnp.where(row < N_GROUPS, logits, NEG_INF)
        gmax = jnp.max(gl, axis=0, keepdims=True)
        g_w = 1.0 / jnp.sum(jnp.exp(gl - gmax), axis=0, keepdims=True)
        g_idx = jnp.min(jnp.where(gl == gmax, row, float(LANES)), axis=0, keepdims=True)
        e_lo = N_GROUPS + EXPERTS_PER_GROUP * g_idx
        el = jnp.where((row >= e_lo) & (row < e_lo + EXPERTS_PER_GROUP), logits, NEG_INF)
        m1 = jnp.max(el, axis=0, keepdims=True)
        i1 = jnp.min(jnp.where(el == m1, row, float(LANES)), axis=0, keepdims=True)
        el2 = jnp.where(row == i1, PICKED, el)
        m2 = jnp.max(el2, axis=0, keepdims=True)
        i2 = jnp.min(jnp.where(el2 == m2, row, float(LANES)), axis=0, keepdims=True)
        ex = jnp.exp(m2 - m1)
        w1 = g_w * (1.0 / (1.0 + ex))
        w2 = g_w * (ex / (1.0 + ex))
        route_ref[:, h * hm:(h + 1) * hm] = jnp.concatenate(
            [i1 - N_GROUPS, i2 - N_GROUPS, w1, w2, jnp.zeros((ROUTE_ROWS - 4, hm), F32)], axis=0)

    halves = (slice(0, hm), slice(hm, tm))
    ups = [up(rows) for rows in halves]
    zs = [out_proj(rows, *u) for rows, u in zip(halves, ups)]
    logits = [residual_and_logits(h, rows, z) for h, (rows, z) in enumerate(zip(halves, zs))]
    for h, lg in enumerate(logits):
        route(h, lg)


def _merge_call(x2, oa, ods, lses, gm, mod6, wa, wb, wo, g2, rcat, rb, s):
    t, d = x2.shape
    b = t // s
    tm = MERGE_TM
    per_b = s // tm
    row = lambda i: (i, 0)
    const = lambda i: (0, 0)

    def by_class(dl):
        return pl.BlockSpec((1, dl, tm // dl, LANES), lambda i: (i // per_b, 0, i % per_b, 0))

    dls = [dl for _, dl in DIL_PAIRS]
    assert dls[0] == 1
    group_specs = [pl.BlockSpec((tm, LANES), row)] + [by_class(dl) for dl in dls[1:]]
    views = lambda arrs: [arrs[0].reshape(t, LANES)] + [
        a.reshape(b, dl, s // dl, LANES) for a, dl in zip(arrs[1:], dls[1:])]
    return pl.pallas_call(
        _merge_kernel,
        grid=(t // tm,),
        in_specs=[
            pl.BlockSpec((tm, d), row),
            pl.BlockSpec((tm, A_Q), row),
            *group_specs, *group_specs,
            pl.BlockSpec((tm, 2 * d), row),
            pl.BlockSpec((1, 6, d), lambda i: (i // per_b, 0, 0)),
            pl.BlockSpec(wa.shape, const),
            pl.BlockSpec(wb.shape, const),
            pl.BlockSpec(wo.shape, const),
            pl.BlockSpec((1, d), const),
            pl.BlockSpec(rcat.shape, const),
            pl.BlockSpec((LANES, 1), const),
        ],
        out_specs=[
            pl.BlockSpec((tm, d), row),
            pl.BlockSpec((tm * ROW_TILE[0], LANES), row),
            pl.BlockSpec((ROUTE_ROWS, tm), lambda i: (0, i)),
        ],
        out_shape=[
            jax.ShapeDtypeStruct((t, d), F32),
            jax.ShapeDtypeStruct((t * ROW_TILE[0], LANES), F32),
            jax.ShapeDtypeStruct((ROUTE_ROWS, t), F32),
        ],
        scratch_shapes=[pltpu.VMEM((2 * (len(dls) - 1), tm, LANES), F32)],
        compiler_params=_params("parallel"),
        name="merge",
    )(x2, oa, *views(ods), *views(lses), gm, mod6, wa, wb, wo, g2, rcat, rb)


def _expert_hits(route_ref, tm):
    row = lax.broadcasted_iota(I32, (N_EXPERTS, tm), 0).astype(F32)
    return [row == route_ref[k:k + 1, :] for k in range(TOP_K)]


def _rank_kernel(route_ref, rank_ref, count_ref, carry_scr):
    tm = route_ref.shape[1]

    @pl.when(pl.program_id(0) == 0)
    def _():
        carry_scr[...] = jnp.zeros(carry_scr.shape, F32)

    hits = _expert_hits(route_ref, tm)
    cnt = jnp.where(hits[0] | hits[1], 1.0, 0.0)
    r = lax.broadcasted_iota(I32, (tm, tm), 0)
    c = lax.broadcasted_iota(I32, (tm, tm), 1)
    earlier = jnp.where(r < c, 1.0, 0.0).astype(BF16)
    before = _dot(cnt.astype(BF16), earlier) + carry_scr[:, 0:1]
    ranks = [jnp.sum(jnp.where(hit, before, 0.0), axis=0, keepdims=True) for hit in hits]
    rank_ref[...] = jnp.concatenate(ranks + [jnp.zeros((ROUTE_ROWS - TOP_K, tm), F32)], axis=0)
    carry_scr[...] = carry_scr[...] + jnp.sum(cnt, axis=1, keepdims=True)
    count_ref[...] = carry_scr[...]


def _rank_call(route):
    t = route.shape[1]
    tm = RANK_TM
    tile = pl.BlockSpec((ROUTE_ROWS, tm), lambda i: (0, i))
    return pl.pallas_call(
        _rank_kernel,
        grid=(t // tm,),
        in_specs=[tile],
        out_specs=[tile, pl.BlockSpec((N_EXPERTS, LANES), lambda i: (0, 0))],
        out_shape=[
            jax.ShapeDtypeStruct((ROUTE_ROWS, t), F32),
            jax.ShapeDtypeStruct((N_EXPERTS, LANES), F32),
        ],
        scratch_shapes=[pltpu.VMEM((N_EXPERTS, LANES), F32)],
        compiler_params=_params("arbitrary"),
        name="rank",
    )(route)


def _slot_kernel(route_ref, rank_ref, start_ref, slot_ref):
    tm = route_ref.shape[1]
    starts = start_ref[...]
    slots = [jnp.sum(jnp.where(hit, starts, 0.0), axis=0, keepdims=True) + rank_ref[k:k + 1, :]
             for k, hit in enumerate(_expert_hits(route_ref, tm))]
    slot_ref[...] = jnp.concatenate(slots + [jnp.zeros((ROUTE_ROWS - TOP_K, tm), F32)], axis=0)


def _slot_call(route, rank, pad_starts):
    t = route.shape[1]
    tm = min(SLOT_TM, t)
    tile = pl.BlockSpec((ROUTE_ROWS, tm), lambda i: (0, i))
    return pl.pallas_call(
        _slot_kernel,
        grid=(t // tm,),
        in_specs=[tile, tile, pl.BlockSpec((N_EXPERTS, tm), lambda i: (0, 0))],
        out_specs=tile,
        out_shape=jax.ShapeDtypeStruct((ROUTE_ROWS, t), F32),
        compiler_params=_params("parallel"),
        name="slots",
    )(route, rank, jnp.broadcast_to(pad_starts[:, None], (N_EXPERTS, tm)))


def _for_static_slot(slot, fn):
    for par in range(2):
        pl.when(slot == par)(functools.partial(fn, par))


def _dispatch_kernel(cnt_ref, pstart_ref, dest_hbm, h_hbm, xs_hbm, idx_smem, hbuf, zbuf,
                     isem, hsem, dsem, zsem):
    tm = MOE_TM
    nd = TOP_K * tm
    i = pl.program_id(0)
    n = pl.num_programs(0)
    slot = i % 2
    n_slots = xs_hbm.shape[0]

    def idx_copy(step, sl):
        return pltpu.make_async_copy(dest_hbm.at[step], idx_smem.at[sl], isem.at[sl])

    def tile_copy(step, bf):
        rows = pl.ds(pl.multiple_of(step * tm, tm), tm)
        return pltpu.make_async_copy(h_hbm.at[rows], hbuf.at[bf], hsem.at[bf])

    def wait_scatter(bf):
        for _ in range(TOP_K):
            pltpu.make_async_copy(hbuf.at[bf], xs_hbm.at[pl.ds(0, tm)], dsem.at[bf]).wait()

    @pl.when(i == 0)
    def _():
        idx_copy(0, 0).start()
        tile_copy(0, 0).start()
        zbuf[...] = jnp.zeros(zbuf.shape, F32)

        def zero_gaps(wait):
            def per_expert(e, used):
                cnt = cnt_ref[e]
                gap = (MOE_BLOCK - cnt % MOE_BLOCK) % MOE_BLOCK
                row = pstart_ref[e] + cnt
                for k in range(MOE_BLOCK.bit_length() - 1):
                    size = 1 << k
                    copy = pltpu.make_async_copy(zbuf.at[pl.ds(0, size)], xs_hbm.at[pl.ds(row, size)], zsem)
                    pl.when(((gap >> k) & 1) == 1)(copy.wait if wait else copy.start)
                    row = row + (gap & size)
                return used + cnt + gap
            return lax.fori_loop(0, N_EXPERTS, per_expert, 0)
        zero_gaps(wait=False)
        used = zero_gaps(wait=True)

        def zero_block(blk):
            rows = pl.ds(pl.multiple_of(blk * MOE_BLOCK, MOE_BLOCK), MOE_BLOCK)
            return pltpu.make_async_copy(zbuf, xs_hbm.at[rows], zsem)

        def start_blk(blk, c):
            zero_block(blk).start()
            return c

        def wait_blk(blk, c):
            zero_block(blk).wait()
            return c
        lax.fori_loop(used // MOE_BLOCK, n_slots // MOE_BLOCK, start_blk, 0)
        lax.fori_loop(used // MOE_BLOCK, n_slots // MOE_BLOCK, wait_blk, 0)

    @pl.when(i > 0)
    def _():
        wait_scatter(1 - slot)

    @pl.when(i + 1 < n)
    def _():
        idx_copy(i + 1, 1 - slot).start()
        tile_copy(i + 1, 1 - slot).start()

    idx_copy(i, slot).wait()
    tile_copy(i, slot).wait()

    def scatter_rows(par):
        for r in range(nd):
            copy = pltpu.make_async_copy(hbuf.at[par, r % tm], xs_hbm.at[idx_smem[par, r]], dsem.at[par])
            copy.start(priority=r % DMA_QUEUES)
    _for_static_slot(slot, scatter_rows)

    @pl.when(i == n - 1)
    def _():
        wait_scatter(slot)


def _dispatch_call(counts, pad_starts, dest_tiles, h2, n_slots):
    n_steps = dest_tiles.shape[0]
    grid_spec = pltpu.PrefetchScalarGridSpec(
        num_scalar_prefetch=2,
        grid=(n_steps,),
        in_specs=[pl.BlockSpec(memory_space=pl.ANY), pl.BlockSpec(memory_space=pl.ANY)],
        out_specs=pl.BlockSpec(memory_space=pl.ANY),
        scratch_shapes=[
            pltpu.SMEM((2, TOP_K * MOE_TM), I32),
            pltpu.VMEM((2, MOE_TM) + ROW_TILE, F32),
            pltpu.VMEM((MOE_BLOCK,) + ROW_TILE, F32),
            pltpu.SemaphoreType.DMA((2,)),
            pltpu.SemaphoreType.DMA((2,)),
            pltpu.SemaphoreType.DMA((2,)),
            pltpu.SemaphoreType.DMA(()),
        ],
    )
    return pl.pallas_call(
        _dispatch_kernel,
        grid_spec=grid_spec,
        out_shape=jax.ShapeDtypeStruct((n_slots,) + ROW_TILE, F32),
        compiler_params=_params("arbitrary"),
        name="dispatch",
    )(counts, pad_starts, dest_tiles, h2)


def _expert_kernel(be_ref, x_ref, wg_ref, wu_ref, wd_ref, y_ref, wg_scr, wu_scr, wd_scr):
    i = pl.program_id(0)

    @pl.when((i == 0) | (be_ref[i] != be_ref[jnp.maximum(i - 1, 0)]))
    def _():
        wg_scr[...] = wg_ref[0].astype(BF16)
        wu_scr[...] = wu_ref[0].astype(BF16)
        wd_scr[...] = wd_ref[0].astype(BF16)

    xb = _load_token_tiles(x_ref, 0, MOE_BLOCK).astype(BF16)
    gate = _dot(xb, wg_scr[...])
    up = _dot(xb, wu_scr[...])
    hid = (gate * jax.nn.sigmoid(gate) * up).astype(BF16)
    _store_token_tiles(y_ref, _dot(hid, wd_scr[...]))


def _expert_call(blk_expert, xs, wg, wu, wd):
    n_blk = blk_expert.shape[0]
    d = wg.shape[1]
    blk = (MOE_BLOCK * ROW_TILE[0], LANES)
    grid_spec = pltpu.PrefetchScalarGridSpec(
        num_scalar_prefetch=1,
        grid=(n_blk,),
        in_specs=[
            pl.BlockSpec(blk, lambda i, be: (i, 0)),
            pl.BlockSpec((1, d, EXPERT_FF), lambda i, be: (be[i], 0, 0)),
            pl.BlockSpec((1, d, EXPERT_FF), lambda i, be: (be[i], 0, 0)),
            pl.BlockSpec((1, EXPERT_FF, d), lambda i, be: (be[i], 0, 0)),
        ],
        out_specs=pl.BlockSpec(blk, lambda i, be: (i, 0)),
        scratch_shapes=[
            pltpu.VMEM((d, EXPERT_FF), BF16),
            pltpu.VMEM((d, EXPERT_FF), BF16),
            pltpu.VMEM((EXPERT_FF, d), BF16),
        ],
    )
    return pl.pallas_call(
        _expert_kernel,
        grid_spec=grid_spec,
        out_shape=jax.ShapeDtypeStruct(xs.shape, F32),
        compiler_params=_params("arbitrary"),
        name="experts",
    )(blk_expert, xs, wg, wu, wd)


def _combine_kernel(dest_hbm, y_hbm, x1_ref, route_ref, mod_ref, o_ref, idx_smem, ybuf, isem, dsem):
    tm = MOE_TM
    nd = TOP_K * tm
    i = pl.program_id(0)
    n = pl.num_programs(0)
    slot = i % 2
    nxt = 1 - slot

    def idx_copy(step, sl):
        return pltpu.make_async_copy(dest_hbm.at[step], idx_smem.at[sl], isem.at[sl])

    ns = ROW_TILE[0]

    def step_rows(sl):
        return pltpu.make_async_copy(y_hbm.at[pl.ds(0, nd * ns)], ybuf.at[sl], dsem.at[sl])

    def issue_rows(par):
        for r in range(nd):
            src = y_hbm.at[pl.ds(pl.multiple_of(idx_smem[par, r] * ns, ns), ns)]
            copy = pltpu.make_async_copy(src, ybuf.at[par, pl.ds(r * ns, ns)], dsem.at[par])
            copy.start(priority=r % DMA_QUEUES)

    @pl.when(i == 0)
    def _():
        idx_copy(0, 0).start()
        idx_copy(0, 0).wait()
        issue_rows(0)

        @pl.when(n > 1)
        def _():
            idx_copy(1, 1).start()

    @pl.when(i + 1 < n)
    def _():
        idx_copy(i + 1, nxt).wait()
        _for_static_slot(nxt, issue_rows)

    @pl.when(i + 2 < n)
    def _():
        idx_copy(i + 2, slot).start()

    step_rows(slot).wait()
    route = jnp.concatenate([route_ref[...], jnp.zeros((LANES - ROUTE_ROWS, tm), F32)], axis=0).T
    w1 = route[:, 2:3]
    w2 = route[:, 3:4]
    y = (w1 * _load_token_tiles(ybuf, 0, tm, lead=(slot,))
         + w2 * _load_token_tiles(ybuf, tm, tm, lead=(slot,)))
    o_ref[...] = x1_ref[...] + mod_ref[0, 5:6, :] * y


def _combine_call(dest_tiles, y_slots, x1, route, mod6, s):
    t, d = x1.shape
    tm = MOE_TM
    per_b = s // tm
    return pl.pallas_call(
        _combine_kernel,
        grid=(t // tm,),
        in_specs=[
            pl.BlockSpec(memory_space=pl.ANY),
            pl.BlockSpec(memory_space=pl.ANY),
            pl.BlockSpec((tm, d), lambda i: (i, 0)),
            pl.BlockSpec((ROUTE_ROWS, tm), lambda i: (0, i)),
            pl.BlockSpec((1, 6, d), lambda i: (i // per_b, 0, 0)),
        ],
        out_specs=pl.BlockSpec((tm, d), lambda i: (i, 0)),
        out_shape=jax.ShapeDtypeStruct((t, d), F32),
        scratch_shapes=[
            pltpu.SMEM((2, TOP_K * tm), I32),
            pltpu.VMEM((2, TOP_K * tm * ROW_TILE[0], LANES), F32),
            pltpu.SemaphoreType.DMA((2,)),
            pltpu.SemaphoreType.DMA((2,)),
        ],
        compiler_params=_params("arbitrary"),
        name="combine",
    )(dest_tiles, y_slots, x1, route, mod6)


def _rope_tables(pos):
    inv = ROPE_THETA ** (-jnp.arange(0, HEAD_DIM, 2, dtype=F32) / HEAD_DIM)
    ang = pos.astype(F32)[..., None] * inv
    return jnp.cos(ang), jnp.sin(ang)


def _rope_tiles(pos):
    cos, sin = _rope_tables(pos)
    return (jnp.concatenate([cos, cos, cos, cos], axis=-1),
            jnp.concatenate([-sin, sin, -sin, sin], axis=-1))


def _selection_constants(s):
    n_c = s // CMP_STRIDE
    n_s = s // SEL_BLOCK
    cs = np.arange(n_c) * CMP_STRIDE
    ss = np.arange(n_s) * SEL_BLOCK
    ov = np.clip(np.minimum(cs[:, None] + CMP_BLOCK, ss[None, :] + SEL_BLOCK)
                 - np.maximum(cs[:, None], ss[None, :]), 0, None).astype(np.float32) / CMP_BLOCK
    kq = np.arange(NSA_TQ)[:, None] - np.arange(NSA_TQ)[None, :]
    wbias = np.stack([np.where(kq >= 0, 0.0, MASKED), np.where(kq <= 0, 0.0, MASKED)]).astype(np.float32)
    return jnp.asarray(ov.T, BF16), jnp.asarray(wbias)


def _mixer_and_router(x, c, positions, w_ada, b_ada, norm1_g, norm2_g, w_in, nsa_q_norm, nsa_k_norm,
                      cmp_pe_k, cmp_w1_k, cmp_w2_k, cmp_pe_v, cmp_w1_v, cmp_w2_v, dil_q_norm,
                      dil_k_norm, w_up_a, w_up_b, w_out, w_group, b_group, w_router, b_router):
    b, s, d = x.shape
    scale = HEAD_DIM ** -0.5
    mod6 = _mod_call(c, w_ada, b_ada).reshape(b, 6, d)

    c1 = A_Q
    c2 = c1 + 6 * A_KV
    c3 = c2 + 3 * NSA_HEADS
    c4 = c3 + 3 * DIL_W
    w_perm = jnp.concatenate([
        w_in[:, :c2], w_in[:, c3:c4],
        jnp.pad(w_in[:, c2:c3], ((0, 0), (0, LANES - 3 * NSA_HEADS))), w_in[:, c4:]], axis=1).astype(BF16)
    two = lambda g: jnp.concatenate([g, g]).astype(F32)
    gains = jnp.stack([two(nsa_q_norm) * (scale * LOG2E), two(nsa_k_norm), two(dil_q_norm) * scale,
                       two(dil_k_norm)])
    cos_t, sin_t = _rope_tables(positions)
    qext, kvc, ksl, vsl, kwp, vwp, dil0, dil1, dil2, ga, gm = _proj_call(
        x, mod6, norm1_g.reshape(1, d), w_perm, gains, cos_t, sin_t)

    pe = jnp.stack([jnp.concatenate([p, p], axis=1) for p in (cmp_pe_k, cmp_pe_v)])
    zero_w = jnp.zeros((CMP_BLOCK, HEAD_DIM, CMP_HIDDEN), F32)

    def per_position(w):
        w = w.reshape(CMP_BLOCK, HEAD_DIM, CMP_HIDDEN)
        return jnp.concatenate([jnp.concatenate([w, zero_w], axis=2),
                                jnp.concatenate([zero_w, w], axis=2)], axis=1)
    w1 = jnp.stack([per_position(cmp_w1_k), per_position(cmp_w1_v)]).astype(BF16)
    zeros = jnp.zeros((CMP_HIDDEN, HEAD_DIM), F32)
    ext = lambda w: jnp.stack([jnp.concatenate([w, zeros], 1), jnp.concatenate([zeros, w], 1)])
    w2ext = jnp.stack([ext(cmp_w2_k), ext(cmp_w2_v)]).astype(BF16)
    cmp_pos = jnp.pad(positions[:, CMP_BLOCK - 1::CMP_STRIDE], ((0, 0), (0, 1)))
    ccos, csin = _rope_tiles(cmp_pos)
    kc, vc = _cmp_call(kvc, pe, w1, w2ext, two(nsa_k_norm).reshape(1, LANES), ccos, csin)

    ov_t, wbias = _selection_constants(s)
    o_a = _nsa_call(qext, kc, vc, ksl, vsl, kwp, vwp, ga, ov_t, wbias)

    ods, lses = zip(*[_dil_call(qkv.reshape(b, 3, s, LANES), dl)
                      for qkv, (_, dl) in zip((dil0, dil1, dil2), DIL_PAIRS)])

    t = b * s
    w_r = jnp.concatenate([w_group, w_router.transpose(1, 0, 2).reshape(d, N_EXPERTS)], axis=1)
    w_r = jnp.pad(w_r, ((0, 0), (0, LANES - w_r.shape[1])))
    rcat = jnp.concatenate(_split(w_r.T), axis=0)
    rb = jnp.pad(jnp.concatenate([b_group, b_router.reshape(-1)]), (0, LANES - N_GROUPS - N_EXPERTS))
    return _merge_call(
        x.reshape(t, d), o_a.reshape(t, A_Q), ods, lses, gm.reshape(t, 2 * d), mod6,
        w_up_a.astype(BF16), w_up_b.astype(BF16), w_out.astype(BF16), norm2_g.reshape(1, d),
        rcat, rb.reshape(LANES, 1).astype(F32), s), mod6


def _moe(x1, h2, route, mod6, w_e_gate, w_e_up, w_e_down, s):
    t, _ = x1.shape
    rank, counts = _rank_call(route)
    counts = counts[:, 0].astype(I32)
    padded = (counts + MOE_BLOCK - 1) // MOE_BLOCK * MOE_BLOCK
    pad_ends = jnp.cumsum(padded)
    pad_starts = pad_ends - padded
    dest = _slot_call(route, rank, pad_starts.astype(F32))[0:TOP_K].astype(I32)
    n_slots = t * TOP_K + N_EXPERTS * MOE_BLOCK
    n_blk = n_slots // MOE_BLOCK
    blk_start = jnp.arange(n_blk, dtype=I32) * MOE_BLOCK
    blk_expert = jnp.minimum(jnp.sum((pad_ends[None, :] <= blk_start[:, None]).astype(I32), axis=1),
                             N_EXPERTS - 1)
    tm = MOE_TM
    dest_tiles = dest.reshape(TOP_K, t // tm, tm).transpose(1, 0, 2).reshape(t // tm, TOP_K * tm)
    xs = _dispatch_call(counts, pad_starts.astype(I32), dest_tiles, h2.reshape((t,) + ROW_TILE), n_slots)
    y_slots = _expert_call(blk_expert, xs.reshape(n_slots * ROW_TILE[0], LANES), w_e_gate, w_e_up, w_e_down)
    return _combine_call(dest_tiles, y_slots, x1, route, mod6, s)


def kernel(x, c, positions, w_ada, b_ada, norm1_g, norm2_g, w_in, nsa_q_norm, nsa_k_norm, cmp_pe_k,
           cmp_w1_k, cmp_w2_k, cmp_pe_v, cmp_w1_v, cmp_w2_v, dil_q_norm, dil_k_norm, w_up_a, w_up_b,
           w_out, w_group, b_group, w_router, b_router, w_e_gate, w_e_up, w_e_down):
    b, s, d = x.shape
    assert w_ada.shape[0] == 1 and d == D_MODEL and s % NSA_TK == 0
    (x1, h2, route), mod6 = _mixer_and_router(
        x, c, positions, w_ada[0], b_ada[0], norm1_g[0], norm2_g[0], w_in[0], nsa_q_norm[0],
        nsa_k_norm[0], cmp_pe_k[0], cmp_w1_k[0], cmp_w2_k[0], cmp_pe_v[0], cmp_w1_v[0], cmp_w2_v[0],
        dil_q_norm[0], dil_k_norm[0], w_up_a[0], w_up_b[0], w_out[0], w_group[0], b_group[0],
        w_router[0], b_router[0])
    out = _moe(x1, h2, route, mod6, w_e_gate[0], w_e_up[0], w_e_down[0], s)
    return out.reshape(b, s, d)
```

```python
import functools

import jax
import jax.numpy as jnp
import numpy as np
from jax import lax
from jax.experimental import pallas as pl
from jax.experimental.pallas import tpu as pltpu

F32 = jnp.float32
BF16 = jnp.bfloat16
I32 = jnp.int32

D_MODEL = 1024
HEAD_DIM = 64
LANES = 128
ROW_TILE = (D_MODEL // LANES, LANES)
ROPE_THETA = 10000.0
EPS = 1e-6
LOG2E = 1.4426950408889634
NEG_INF = -1e30
FORCE_SCORE = 1e9
MASKED = -1e30
PICKED = -3e38

NSA_HEADS = 8
NSA_KV_HEADS = 2
NSA_GROUP = 4
CMP_BLOCK = 32
CMP_STRIDE = 16
CMP_HIDDEN = 256
SEL_BLOCK = 64
N_SEL = 8
N_LOCAL_SEL = 2
WINDOW = 512
DIL_PAIRS = ((128, 1), (512, 4), (2048, 16))
DIL_GROUPS = 3
A_Q = NSA_HEADS * HEAD_DIM
A_KV = NSA_KV_HEADS * HEAD_DIM
DIL_W = 2 * DIL_GROUPS * HEAD_DIM
N_GROUPS = 4
EXPERTS_PER_GROUP = 8
N_EXPERTS = 32
TOP_K = 2
ROUTE_ROWS = 8
EXPERT_FF = 512
MOE_BLOCK = 512

VMEM_LIMIT = 56 * 1024 * 1024

T_KVC = 4
T_KVA = 6
T_DIL = 10
T_GA = 19
T_GM = 20
N_TILES = 36

PROJ_TM = 512
NSA_TQ = 128
NSA_TK = 512
NSA_NQ = 4
NSA_AHEAD = 1
DIL_T = 128
DIL_UNROLL = 16
MERGE_TM = 1024
RANK_TM = 1024
SLOT_TM = 4096
MOE_TM = 256
DMA_QUEUES = 2


def _dot(a, b):
    return jnp.dot(a, b, preferred_element_type=F32)


def _dot_nt(a, b):
    return lax.dot_general(a, b, (((1,), (1,)), ((), ())), preferred_element_type=F32)


def _dot_tn(a, b):
    return lax.dot_general(a, b, (((0,), (0,)), ((), ())), preferred_element_type=F32)


def _split(a):
    hi = a.astype(BF16)
    lo = (a - hi.astype(F32)).astype(BF16)
    return hi, lo


def _load_token_tiles(ref, first, n, lead=()):
    ns = ROW_TILE[0]
    return jnp.concatenate(
        [ref[lead + (pl.ds(first * ns + s, n, stride=ns), slice(None))] for s in range(ns)], axis=1)


def _store_token_tiles(ref, rows, first=0):
    ns = ROW_TILE[0]
    for s in range(ns):
        ref[pl.ds(first * ns + s, rows.shape[0], stride=ns), :] = rows[:, s * LANES:(s + 1) * LANES]


def _params(*sem):
    return pltpu.CompilerParams(dimension_semantics=sem, vmem_limit_bytes=VMEM_LIMIT)


def _mod_kernel(c_ref, whi_ref, wlo_ref, b_ref, o_ref):
    c = c_ref[...]
    ca = c * jax.nn.sigmoid(c)
    hi, lo = _split(ca)
    whi = whi_ref[...]
    o_ref[...] = _dot(hi, whi) + _dot(lo, whi) + _dot(hi, wlo_ref[...]) + b_ref[...]


def _mod_call(c, w_ada, b_ada):
    b, d = c.shape
    n = w_ada.shape[1]
    whi, wlo = _split(w_ada)
    tn = 1024
    return pl.pallas_call(
        _mod_kernel,
        grid=(n // tn,),
        in_specs=[
            pl.BlockSpec((b, d), lambda j: (0, 0)),
            pl.BlockSpec((d, tn), lambda j: (0, j)),
            pl.BlockSpec((d, tn), lambda j: (0, j)),
            pl.BlockSpec((1, tn), lambda j: (0, j)),
        ],
        out_specs=pl.BlockSpec((b, tn), lambda j: (0, j)),
        out_shape=jax.ShapeDtypeStruct((b, n), F32),
        compiler_params=_params("parallel"),
        name="mod",
    )(c, whi, wlo, b_ada.reshape(1, n))


def _norm_rope(a, gain, cos, sin_signed, lo, first):
    sq = a * a
    s0 = jnp.sum(jnp.where(lo, sq, 0.0), axis=-1, keepdims=True)
    s1 = jnp.sum(jnp.where(lo, 0.0, sq), axis=-1, keepdims=True)
    r = jnp.where(lo, lax.rsqrt(s0 * (1.0 / HEAD_DIM) + EPS), lax.rsqrt(s1 * (1.0 / HEAD_DIM) + EPS))
    y = a * r * gain
    rot = jnp.where(first, pltpu.roll(y, 96, 1), pltpu.roll(y, 32, 1))
    return y * cos + rot * sin_signed


def _proj_kernel(x_ref, mod_ref, g1_ref, w_ref, gains_ref, cos_ref, sin_ref, kw0_ref, vw0_ref,
                 qext_ref, kvc_ref, ksl_ref, vsl_ref, kwp_ref, vwp_ref, dil0_ref, dil1_ref, dil2_ref,
                 ga_ref, gm_ref, perm_scr):
    del kw0_ref, vw0_ref
    tm = x_ref.shape[1]
    x = x_ref[0]
    ms = jnp.mean(x * x, axis=-1, keepdims=True)
    y = x * lax.rsqrt(ms + EPS) * g1_ref[...]
    sh1 = mod_ref[0, 0:1, :]
    sc1 = mod_ref[0, 1:2, :]
    h = (y * (1.0 + sc1) + sh1).astype(BF16)
    lane = lax.broadcasted_iota(I32, (tm, LANES), 1)
    lo = lane < HEAD_DIM
    first = (lane & (HEAD_DIM - 1)) < (HEAD_DIM // 2)
    cos = jnp.concatenate([cos_ref[0]] * 4, axis=1)
    sin = jnp.concatenate([sin_ref[0]] * 4, axis=1)
    sin = jnp.where(first, -sin, sin)
    nr = functools.partial(_norm_rope, cos=cos, sin_signed=sin, lo=lo, first=first)

    for c in range(N_TILES // 2):
        acc = _dot(h, w_ref[:, c * 2 * LANES:(c + 1) * 2 * LANES])
        for half in range(2):
            t = 2 * c + half
            a = acc[:, half * LANES:(half + 1) * LANES]
            if t < T_KVC:
                yq = nr(a, gains_ref[0:1, :])
                rq = pltpu.roll(yq, HEAD_DIM, 1)
                if t // 2 == 0:
                    e0 = jnp.where(lo, yq, 0.0)
                    e1 = jnp.where(lo, rq, 0.0)
                else:
                    e0 = jnp.where(lo, 0.0, rq)
                    e1 = jnp.where(lo, 0.0, yq)
                qext_ref[0, 2 * t] = e0.astype(BF16)
                qext_ref[0, 2 * t + 1] = e1.astype(BF16)
            elif t < T_KVA:
                kvc_ref[0, t - T_KVC] = a
            elif t < T_DIL:
                j = t - T_KVA
                k_out, v_out = (ksl_ref, vsl_ref) if j < 2 else (kwp_ref, vwp_ref)
                if j % 2 == 0:
                    k_out[0] = nr(a, gains_ref[1:2, :]).astype(BF16)
                else:
                    v_out[0, 0] = jnp.where(lo, a, 1.0).astype(BF16)
                    v_out[0, 1] = jnp.where(lo, pltpu.roll(a, HEAD_DIM, 1), 1.0).astype(BF16)
            elif t < T_GA:
                j = t - T_DIL
                if j < 3:
                    v = nr(a, gains_ref[2:3, :])
                elif j < 6:
                    v = nr(a, gains_ref[3:4, :])
                else:
                    v = a
                kind, gi = divmod(j, DIL_GROUPS)
                dl = DIL_PAIRS[gi][1]
                if dl == 1:
                    dil0_ref[0, kind] = v.astype(BF16)
                else:
                    out = (dil0_ref, dil1_ref, dil2_ref)[gi]
                    perm_scr[...] = v
                    for r in range(dl):
                        out[0, kind, r] = perm_scr[pl.ds(r, tm // dl, stride=dl), :].astype(BF16)
            elif t < T_GM:
                ga_ref[0] = jax.nn.sigmoid(a)
            else:
                j = t - T_GM
                gm_ref[0, :, j * LANES:(j + 1) * LANES] = jax.nn.sigmoid(a).astype(BF16)


def _proj_call(x, mod6, g1, w_perm, gains, cos_t, sin_t):
    b, s, d = x.shape
    tm = PROJ_TM
    n = N_TILES * LANES
    row = lambda bi, i: (bi, i, 0)
    heads = lambda bi, i: (bi, 0, i, 0)
    pad = WINDOW // tm
    in_specs = [
        pl.BlockSpec((1, tm, d), row),
        pl.BlockSpec((1, 6, d), lambda bi, i: (bi, 0, 0)),
        pl.BlockSpec((1, d), lambda bi, i: (0, 0)),
        pl.BlockSpec((d, n), lambda bi, i: (0, 0)),
        pl.BlockSpec((4, LANES), lambda bi, i: (0, 0)),
        pl.BlockSpec((1, tm, HEAD_DIM // 2), row),
        pl.BlockSpec((1, tm, HEAD_DIM // 2), row),
        pl.BlockSpec(memory_space=pl.ANY),
        pl.BlockSpec(memory_space=pl.ANY),
    ]
    out_specs = [
        pl.BlockSpec((1, NSA_HEADS, tm, LANES), heads),
        pl.BlockSpec((1, 2, tm, LANES), heads),
        pl.BlockSpec((1, tm, LANES), row),
        pl.BlockSpec((1, NSA_KV_HEADS, tm, LANES), heads),
        pl.BlockSpec((1, tm, LANES), lambda bi, i: (bi, i + pad, 0)),
        pl.BlockSpec((1, NSA_KV_HEADS, tm, LANES), lambda bi, i: (bi, 0, i + pad, 0)),
        pl.BlockSpec((1, 3, tm, LANES), heads),
        *[pl.BlockSpec((1, 3, dl, tm // dl, LANES), lambda bi, i: (bi, 0, 0, i, 0)) for _, dl in DIL_PAIRS[1:]],
        pl.BlockSpec((1, tm, LANES), row),
        pl.BlockSpec((1, tm, 2 * d), row),
    ]
    out_shape = [
        jax.ShapeDtypeStruct((b, NSA_HEADS, s, LANES), BF16),
        jax.ShapeDtypeStruct((b, 2, s, LANES), F32),
        jax.ShapeDtypeStruct((b, s, LANES), BF16),
        jax.ShapeDtypeStruct((b, NSA_KV_HEADS, s, LANES), BF16),
        jax.ShapeDtypeStruct((b, s + WINDOW, LANES), BF16),
        jax.ShapeDtypeStruct((b, NSA_KV_HEADS, s + WINDOW, LANES), BF16),
        jax.ShapeDtypeStruct((b, 3, s, LANES), BF16),
        *[jax.ShapeDtypeStruct((b, 3, dl, s // dl, LANES), BF16) for _, dl in DIL_PAIRS[1:]],
        jax.ShapeDtypeStruct((b, s, LANES), F32),
        jax.ShapeDtypeStruct((b, s, 2 * d), BF16),
    ]
    kw0 = jnp.zeros(out_shape[4].shape, BF16)
    vw0 = jnp.zeros(out_shape[5].shape, BF16)
    return pl.pallas_call(
        _proj_kernel,
        grid=(b, s // tm),
        in_specs=in_specs,
        out_specs=out_specs,
        out_shape=out_shape,
        scratch_shapes=[pltpu.VMEM((tm, LANES), F32)],
        input_output_aliases={7: 4, 8: 5},
        compiler_params=_params("parallel", "parallel"),
        name="proj",
    )(x, mod6, g1, w_perm, gains, cos_t, sin_t, kw0, vw0)


def _cmp_kernel(x_ref, pe_ref, w1_ref, w2_ref, gain_ref, cos_ref, sin_ref, kc_ref, vc_ref):
    nb = kc_ref.shape[1]
    lane = lax.broadcasted_iota(I32, (nb, LANES), 1)
    lo = lane < HEAD_DIM
    first = (lane & (HEAD_DIM - 1)) < (HEAD_DIM // 2)
    for kind in range(2):
        top = jnp.zeros((nb, 2 * CMP_HIDDEN), F32)
        bot = jnp.zeros((nb, 2 * CMP_HIDDEN), F32)
        for r in range(CMP_STRIDE):
            x = x_ref[0, kind, pl.ds(r, nb, stride=CMP_STRIDE), :]
            top = top + _dot((x + pe_ref[kind, r:r + 1, :]).astype(BF16), w1_ref[kind, r])
            bot = bot + _dot((x + pe_ref[kind, CMP_STRIDE + r:CMP_STRIDE + r + 1, :]).astype(BF16),
                             w1_ref[kind, CMP_STRIDE + r])
        hid = top + pltpu.roll(bot, nb - 1, 0)
        hid = (hid * jax.nn.sigmoid(hid)).astype(BF16)
        out = (_dot(hid[:, 0:CMP_HIDDEN], w2_ref[kind, 0])
               + _dot(hid[:, CMP_HIDDEN:2 * CMP_HIDDEN], w2_ref[kind, 1]))
        if kind == 0:
            out = _norm_rope(out, gain_ref[...], cos_ref[0], sin_ref[0], lo, first)
            kc_ref[0] = out.astype(BF16)
        else:
            vc_ref[0, 0] = jnp.where(lo, out, 0.0).astype(BF16)
            vc_ref[0, 1] = jnp.where(lo, pltpu.roll(out, HEAD_DIM, 1), 0.0).astype(BF16)


def _cmp_call(kvc, pe, w1, w2ext, gain_k, ccos, csin):
    b, _, s, _ = kvc.shape
    nb = s // CMP_STRIDE
    return pl.pallas_call(
        _cmp_kernel,
        grid=(b,),
        in_specs=[
            pl.BlockSpec((1, 2, s, LANES), lambda bi: (bi, 0, 0, 0)),
            pl.BlockSpec(pe.shape, lambda bi: (0, 0, 0)),
            pl.BlockSpec(w1.shape, lambda bi: (0, 0, 0, 0)),
            pl.BlockSpec((2, 2, CMP_HIDDEN, LANES), lambda bi: (0, 0, 0, 0)),
            pl.BlockSpec((1, LANES), lambda bi: (0, 0)),
            pl.BlockSpec((1, nb, LANES), lambda bi: (bi, 0, 0)),
            pl.BlockSpec((1, nb, LANES), lambda bi: (bi, 0, 0)),
        ],
        out_specs=[
            pl.BlockSpec((1, nb, LANES), lambda bi: (bi, 0, 0)),
            pl.BlockSpec((1, NSA_KV_HEADS, nb, LANES), lambda bi: (bi, 0, 0, 0)),
        ],
        out_shape=[
            jax.ShapeDtypeStruct((b, nb, LANES), BF16),
            jax.ShapeDtypeStruct((b, NSA_KV_HEADS, nb, LANES), BF16),
        ],
        compiler_params=_params("parallel"),
        name="cmp",
    )(kvc, pe, w1, w2ext, gain_k, ccos, csin)


def _softmax_cols(s, exp=jnp.exp):
    p = exp(s - jnp.max(s, axis=0, keepdims=True))
    return p, jnp.sum(p, axis=0, keepdims=True)


def _block_max(s, offs, bs):
    m = None
    for j, off in enumerate(offs):
        mj = jnp.max(s[j * bs:(j + 1) * bs], axis=0, keepdims=True) + off
        m = mj if m is None else jnp.maximum(m, mj)
    return m


def _block_exp2(s, offs, bs, m):
    return jnp.concatenate(
        [jnp.exp2(s[j * bs:(j + 1) * bs] - (m - off)) for j, off in enumerate(offs)], axis=0)


def _nsa_kernel(q_ref, kc_ref, vc_ref, ksl_ref, vsl_ref, kw_ref, vw_ref, ga_ref, ov_ref, wb_ref,
                o_ref, m_scr, acc_scr, sb_scr, *s_scrs):
    tq = NSA_TQ
    tk = NSA_TK
    hd = HEAD_DIM
    cols = NSA_GROUP * tq
    tiles = range(NSA_NQ)
    s_buf = lambda u, c: s_scrs[2 * u + c % 2]
    n_blk = ov_ref.shape[0]
    n_chunks = ksl_ref.shape[1] // tk
    bpc = tk // SEL_BLOCK
    kh = pl.program_id(1)
    base = pl.multiple_of(pl.program_id(2) * (NSA_NQ * tq), NSA_NQ * tq)
    t0 = [base + u * tq for u in tiles]
    q4 = [q_ref[0, :, u * tq:(u + 1) * tq, :].reshape(cols, LANES) for u in tiles]
    lane_q = lax.broadcasted_iota(I32, (1, tq), 1)
    lane_q4 = lax.broadcasted_iota(I32, (1, cols), 1) & (tq - 1)
    per_head = lambda b: jnp.concatenate([b] * NSA_GROUP, axis=1)
    older_edge = wb_ref[0]
    causal_edge = wb_ref[1]
    nwin = WINDOW + tq
    n_wb = nwin // tq
    blk = lax.broadcasted_iota(I32, (n_blk, 1), 0)
    blk_f = blk.astype(F32)
    ov = ov_ref[...]

    diag = base // tk
    s_cmp, s_win, s_diag = {}, {}, {}
    o_cmp, o_win, sel_bias = {}, {}, {}

    def qk_products(u):
        s_cmp[u] = _dot_nt(kc_ref[0], q4[u])
        s_win[u] = _dot_nt(kw_ref[0, pl.ds(t0[u], nwin), :], q4[u])

    def qk_selected(u):
        s_diag[u] = _dot_nt(ksl_ref[0, pl.ds(base, (u + 1) * tq), :], q4[u])
        s_buf(u, 0)[...] = _dot_nt(ksl_ref[0, 0:tk, :], q4[u])

    def compressed_and_select(u):
        nb = s_cmp[u].shape[0]
        cmp_end = lax.broadcasted_iota(I32, (nb, 1), 0) * CMP_STRIDE + (CMP_BLOCK - 1)
        valid = cmp_end <= t0[u] + lane_q4
        e, den = _softmax_cols(jnp.where(valid, s_cmp[u], NEG_INF), jnp.exp2)
        p = jnp.where(valid, e / den, 0.0)
        o_cmp[u] = _dot_tn(vc_ref[0, 0], p.astype(BF16))[0:hd]
        psum = p[:, 0:tq] + p[:, tq:2 * tq] + p[:, 2 * tq:3 * tq] + p[:, 3 * tq:4 * tq]
        p_hi, p_lo = _split(psum)
        imp = _dot(ov, p_hi) + _dot(ov, p_lo)
        rel = ((t0[u] + lane_q) >> 6) - blk
        forced = (blk == 0) | ((rel >= 0) & (rel < N_LOCAL_SEL))
        score = jnp.where(rel < 0, NEG_INF, jnp.where(forced, FORCE_SCORE, imp))
        bias = jnp.full((n_blk, tq), MASKED, F32)
        for _ in range(N_SEL):
            best = jnp.max(score, axis=0, keepdims=True)
            first = jnp.min(jnp.where(score == best, blk_f, float(n_blk)), axis=0, keepdims=True)
            pick = blk_f == first
            bias = jnp.where(pick, 0.0, bias)
            score = jnp.where(pick, PICKED, score)
        sel_bias[u] = per_head(jnp.where(rel < 0, MASKED, bias))
        sb_scr[u] = sel_bias[u]

    def window(u):
        sw = jnp.concatenate([s_win[u][0:tq] + per_head(older_edge), s_win[u][tq:nwin - tq],
                              s_win[u][nwin - tq:nwin] + per_head(causal_edge)], axis=0)
        w_offs = [jnp.where(t0[u] - WINDOW + j * tq >= 0, 0.0, MASKED) for j in range(n_wb)]
        pw = _block_exp2(sw, w_offs, tq, _block_max(sw, w_offs, tq))
        ow = _dot_tn(vw_ref[0, 0, pl.ds(t0[u], nwin), :], pw.astype(BF16))
        o_win[u] = ow[0:hd] / ow[hd:hd + 1]

    def diagonal(u):
        nk = (u + 1) * tq
        sd = s_diag[u][nk - tq:nk] + per_head(causal_edge)
        if u > 0:
            sd = jnp.concatenate([s_diag[u][0:nk - tq], sd], axis=0)
        offs = [sb_scr[u, pl.ds(diag * bpc + j, 1), :] for j in range(nk // SEL_BLOCK)]
        m_new = _block_max(sd, offs, SEL_BLOCK)
        vd = vsl_ref[0, 0, pl.ds(base, nk), :]
        acc_scr[u] = _dot_tn(vd, _block_exp2(sd, offs, SEL_BLOCK, m_new).astype(BF16))
        m_scr[u] = m_new

    for u in tiles:
        qk_products(u)
    for u in tiles:
        compressed_and_select(u)
        qk_selected(u)
    for u in tiles:
        window(u)
        diagonal(u)

    def chunk(c):
        ahead = c + 2 < n_chunks

        def next_scores(u):
            s_buf(u, c + 1)[...] = _dot_nt(ksl_ref[0, (c + 1) * tk:(c + 2) * tk, :], q4[u])

        if ahead:
            for u in range(min(NSA_AHEAD, NSA_NQ)):
                next_scores(u)
        vb = vsl_ref[0, 0, c * tk:(c + 1) * tk, :]
        for u in tiles:
            sc = s_buf(u, c)[...]
            offs = [sel_bias[u][c * bpc + j:c * bpc + j + 1] for j in range(bpc)]
            m_old = m_scr[u]
            m_new = jnp.maximum(m_old, _block_max(sc, offs, SEL_BLOCK))
            pe = _block_exp2(sc, offs, SEL_BLOCK, m_new)
            acc_scr[u] = jnp.exp2(m_old - m_new) * acc_scr[u] + _dot_tn(vb, pe.astype(BF16))
            m_scr[u] = m_new
            if ahead and u + NSA_AHEAD < NSA_NQ:
                next_scores(u + NSA_AHEAD)

    for c in range(n_chunks - 1):
        pl.when(c < diag)(functools.partial(chunk, c))

    is0 = kh == 0
    for u in tiles:
        rows = slice(u * tq, (u + 1) * tq)
        o_slc = acc_scr[u, 0:hd, :] / acc_scr[u, hd:hd + 1, :]
        gat = ga_ref[0, rows, :].T
        heads = []
        for g in range(NSA_GROUP):
            c = slice(g * tq, (g + 1) * tq)
            og = jnp.zeros((hd, tq), F32)
            for gi, ob in enumerate((o_cmp[u], o_slc, o_win[u])):
                c0 = gi * NSA_HEADS + g
                c1 = c0 + NSA_GROUP
                gate = jnp.where(is0, gat[c0:c0 + 1, :], gat[c1:c1 + 1, :])
                og = og + gate * ob[:, c]
            heads.append(og)
        for pair in range(2):
            tile = jnp.concatenate([heads[2 * pair], heads[2 * pair + 1]], axis=0)
            o_ref[0, rows, pair * LANES:(pair + 1) * LANES] = tile.T.astype(BF16)


def _nsa_call(qext, kc, vc, ksl, vsl, kwp, vwp, ga, ov_t, wbias):
    b, _, s, _ = qext.shape
    tq = NSA_TQ * NSA_NQ
    nb = kc.shape[1]
    cols = NSA_GROUP * NSA_TQ
    assert NSA_TK == tq
    shared = lambda rows: pl.BlockSpec((1, rows, LANES), lambda bi, k, i: (bi, 0, 0))
    per_kv = lambda rows: pl.BlockSpec((1, 1, rows, LANES), lambda bi, k, i: (bi, k, 0, 0))
    return pl.pallas_call(
        _nsa_kernel,
        grid=(b, NSA_KV_HEADS, s // tq),
        in_specs=[
            pl.BlockSpec((1, NSA_GROUP, tq, LANES), lambda bi, k, i: (bi, k, i, 0)),
            shared(nb), per_kv(nb),
            shared(s), per_kv(s),
            shared(s + WINDOW), per_kv(s + WINDOW),
            pl.BlockSpec((1, tq, LANES), lambda bi, k, i: (bi, i, 0)),
            pl.BlockSpec(ov_t.shape, lambda bi, k, i: (0, 0)),
            pl.BlockSpec(wbias.shape, lambda bi, k, i: (0, 0, 0)),
        ],
        out_specs=pl.BlockSpec((1, tq, 2 * LANES), lambda bi, k, i: (bi, i, k)),
        out_shape=jax.ShapeDtypeStruct((b, s, A_Q), BF16),
        scratch_shapes=[
            pltpu.VMEM((NSA_NQ, 1, cols), F32),
            pltpu.VMEM((NSA_NQ, LANES, cols), F32),
            pltpu.VMEM((NSA_NQ, ov_t.shape[0], cols), F32),
            *[pltpu.VMEM((NSA_TK, cols), F32)] * (2 * NSA_NQ),
        ],
        compiler_params=_params("parallel", "parallel", "parallel"),
        name="nsa",
    )(qext, kc, vc, ksl, vsl, kwp, vwp, ga, ov_t, wbias)


def _dil_kernel(q_ref, k_ref, v_ref, o_ref, lse_ref, *, seg_tiles):
    t = DIL_T
    lo = lax.broadcasted_iota(I32, (t, LANES), 1) < HEAD_DIM
    kj = lax.broadcasted_iota(I32, (t, 2 * t), 0)
    qi = lax.broadcasted_iota(I32, (t, 2 * t), 1) & (t - 1)
    top = lax.broadcasted_iota(I32, (LANES, t), 0) < HEAD_DIM

    def scores(i):
        has_prev = (i & (seg_tiles - 1)) != 0
        cs = pl.multiple_of(i * t, t)
        ps = pl.multiple_of(jnp.maximum(i - 1, 0) * t, t)
        q = q_ref[0, 0, pl.ds(cs, t), :]
        zero = jnp.zeros_like(q)
        q2 = jnp.concatenate([jnp.where(lo, q, zero), jnp.where(lo, zero, q)], axis=0)
        sp = _dot_nt(k_ref[0, 0, pl.ds(ps, t), :], q2)
        sc = _dot_nt(k_ref[0, 0, pl.ds(cs, t), :], q2)
        return has_prev, ps, cs, sp, sc

    def attend(has_prev, ps, cs, sp, sc):
        sp = jnp.where((kj >= qi) & has_prev, sp, MASKED)
        sc = jnp.where(qi >= kj, sc, MASKED)
        m = jnp.maximum(jnp.max(sp, axis=0, keepdims=True), jnp.max(sc, axis=0, keepdims=True))
        pp = jnp.exp(sp - m)
        pc = jnp.exp(sc - m)
        l = jnp.sum(pp, axis=0, keepdims=True) + jnp.sum(pc, axis=0, keepdims=True)
        o = (_dot_tn(v_ref[0, 0, pl.ds(ps, t), :], pp.astype(BF16))
             + _dot_tn(v_ref[0, 0, pl.ds(cs, t), :], pc.astype(BF16))) / l
        lse = m + jnp.log(l)
        o_ref[0, pl.ds(cs, t), :] = jnp.where(top, o[:, 0:t], o[:, t:2 * t]).T
        lse_ref[0, pl.ds(cs, t), :] = jnp.where(top, lse[:, 0:t], lse[:, t:2 * t]).T

    def tiles(j, carry):
        group = [scores(j * DIL_UNROLL + u) for u in range(DIL_UNROLL)]
        for args in group:
            attend(*args)
        return carry

    lax.fori_loop(0, q_ref.shape[2] // (t * DIL_UNROLL), tiles, 0)


def _dil_call(qkv, dl):
    b, _, s, _ = qkv.shape
    kind = lambda j: pl.BlockSpec((1, 1, s, LANES), lambda bi: (bi, j, 0, 0))
    out = pl.BlockSpec((1, s, LANES), lambda bi: (bi, 0, 0))
    return pl.pallas_call(
        functools.partial(_dil_kernel, seg_tiles=s // dl // DIL_T),
        grid=(b,),
        in_specs=[kind(0), kind(1), kind(2)],
        out_specs=[out, out],
        out_shape=[jax.ShapeDtypeStruct((b, s, LANES), F32)] * 2,
        compiler_params=_params("parallel"),
        name=f"dil{dl}",
    )(qkv, qkv, qkv)


def _merge_kernel(x_ref, oa_ref, od0_ref, od1_ref, od2_ref, ls0_ref, ls1_ref, ls2_ref, gm_ref, mod_ref,
                  wa_ref, wb_ref, wo_ref, g2_ref, rcat_ref, rb_ref, x1_ref, h2_ref, route_ref, perm_scr):
    tm = x_ref.shape[0]
    d = x_ref.shape[1]
    hm = tm // 2
    for n, ref in enumerate((od1_ref, od2_ref, ls1_ref, ls2_ref)):
        dl = ref.shape[1]
        for r in range(dl):
            perm_scr[n, pl.ds(r, tm // dl, stride=dl), :] = ref[0, r]

    def up(rows):
        l0, l1, l2 = ls0_ref[rows], perm_scr[2, rows], perm_scr[3, rows]
        mx = jnp.maximum(jnp.maximum(l0, l1), l2)
        e0, e1, e2 = jnp.exp(l0 - mx), jnp.exp(l1 - mx), jnp.exp(l2 - mx)
        den = e0 + e1 + e2
        ob = jnp.concatenate([(od0_ref[rows] * (e0 / den)).astype(BF16),
                              (perm_scr[0, rows] * (e1 / den)).astype(BF16),
                              (perm_scr[1, rows] * (e2 / den)).astype(BF16)], axis=1)
        return _dot(oa_ref[rows], wa_ref[...]), _dot(ob, wb_ref[...])

    def out_proj(rows, ya, yb):
        y = gm_ref[rows, 0:d].astype(F32) * ya + gm_ref[rows, d:2 * d].astype(F32) * yb
        return _dot(y.astype(BF16), wo_ref[...])

    def residual_and_logits(h, rows, z):
        x1 = x_ref[rows] + mod_ref[0, 2:3, :] * z
        x1_ref[rows] = x1
        ms = jnp.mean(x1 * x1, axis=-1, keepdims=True)
        h2 = x1 * lax.rsqrt(ms + EPS) * g2_ref[...]
        h2 = h2 * (1.0 + mod_ref[0, 4:5, :]) + mod_ref[0, 3:4, :]
        _store_token_tiles(h2_ref, h2, first=h * hm)
        hi, lo = _split(h2)
        both = _dot_nt(rcat_ref[...], hi)
        return both[0:LANES] + both[LANES:2 * LANES] + _dot_nt(rcat_ref[0:LANES, :], lo) + rb_ref[...]

    def route(h, logits):
        row = lax.broadcasted_iota(I32, (LANES, hm), 0).astype(F32)
        gl = jnp.where(row < N_GROUPS, logits, NEG_INF)
        gmax = jnp.max(gl, axis=0, keepdims=True)
        g_w = 1.0 / jnp.sum(jnp.exp(gl - gmax), axis=0, keepdims=True)
        g_idx = jnp.min(jnp.where(gl == gmax, row, float(LANES)), axis=0, keepdims=True)
        e_lo = N_GROUPS + EXPERTS_PER_GROUP * g_idx
        el = jnp.where((row >= e_lo) & (row < e_lo + EXPERTS_PER_GROUP), logits, NEG_INF)
        m1 = jnp.max(el, axis=0, keepdims=True)
        i1 = jnp.min(jnp.where(el == m1, row, float(LANES)), axis=0, keepdims=True)
        el2 = jnp.where(row == i1, PICKED, el)
        m2 = jnp.max(el2, axis=0, keepdims=True)
        i2 = jnp.min(jnp.where(el2 == m2, row, float(LANES)), axis=0, keepdims=True)
        ex = jnp.exp(m2 - m1)
        w1 = g_w * (1.0 / (1.0 + ex))
        w2 = g_w * (ex / (1.0 + ex))
        route_ref[:, h * hm:(h + 1) * hm] = jnp.concatenate(
            [i1 - N_GROUPS, i2 - N_GROUPS, w1, w2, jnp.zeros((ROUTE_ROWS - 4, hm), F32)], axis=0)

    halves = (slice(0, hm), slice(hm, tm))
    ups = [up(rows) for rows in halves]
    zs = [out_proj(rows, *u) for rows, u in zip(halves, ups)]
    logits = [residual_and_logits(h, rows, z) for h, (rows, z) in enumerate(zip(halves, zs))]
    for h, lg in enumerate(logits):
        route(h, lg)


def _merge_call(x2, oa, ods, lses, gm, mod6, wa, wb, wo, g2, rcat, rb, s):
    t, d = x2.shape
    b = t // s
    tm = MERGE_TM
    per_b = s // tm
    row = lambda i: (i, 0)
    const = lambda i: (0, 0)

    def by_class(dl):
        return pl.BlockSpec((1, dl, tm // dl, LANES), lambda i: (i // per_b, 0, i % per_b, 0))

    dls = [dl for _, dl in DIL_PAIRS]
    assert dls[0] == 1
    group_specs = [pl.BlockSpec((tm, LANES), row)] + [by_class(dl) for dl in dls[1:]]
    views = lambda arrs: [arrs[0].reshape(t, LANES)] + [
        a.reshape(b, dl, s // dl, LANES) for a, dl in zip(arrs[1:], dls[1:])]
    return pl.pallas_call(
        _merge_kernel,
        grid=(t // tm,),
        in_specs=[
            pl.BlockSpec((tm, d), row),
            pl.BlockSpec((tm, A_Q), row),
            *group_specs, *group_specs,
            pl.BlockSpec((tm, 2 * d), row),
            pl.BlockSpec((1, 6, d), lambda i: (i // per_b, 0, 0)),
            pl.BlockSpec(wa.shape, const),
            pl.BlockSpec(wb.shape, const),
            pl.BlockSpec(wo.shape, const),
            pl.BlockSpec((1, d), const),
            pl.BlockSpec(rcat.shape, const),
            pl.BlockSpec((LANES, 1), const),
        ],
        out_specs=[
            pl.BlockSpec((tm, d), row),
            pl.BlockSpec((tm * ROW_TILE[0], LANES), row),
            pl.BlockSpec((ROUTE_ROWS, tm), lambda i: (0, i)),
        ],
        out_shape=[
            jax.ShapeDtypeStruct((t, d), F32),
            jax.ShapeDtypeStruct((t * ROW_TILE[0], LANES), F32),
            jax.ShapeDtypeStruct((ROUTE_ROWS, t), F32),
        ],
        scratch_shapes=[pltpu.VMEM((2 * (len(dls) - 1), tm, LANES), F32)],
        compiler_params=_params("parallel"),
        name="merge",
    )(x2, oa, *views(ods), *views(lses), gm, mod6, wa, wb, wo, g2, rcat, rb)


def _expert_hits(route_ref, tm):
    row = lax.broadcasted_iota(I32, (N_EXPERTS, tm), 0).astype(F32)
    return [row == route_ref[k:k + 1, :] for k in range(TOP_K)]


def _rank_kernel(route_ref, rank_ref, count_ref, carry_scr):
    tm = route_ref.shape[1]

    @pl.when(pl.program_id(0) == 0)
    def _():
        carry_scr[...] = jnp.zeros(carry_scr.shape, F32)

    hits = _expert_hits(route_ref, tm)
    cnt = jnp.where(hits[0] | hits[1], 1.0, 0.0)
    r = lax.broadcasted_iota(I32, (tm, tm), 0)
    c = lax.broadcasted_iota(I32, (tm, tm), 1)
    earlier = jnp.where(r < c, 1.0, 0.0).astype(BF16)
    before = _dot(cnt.astype(BF16), earlier) + carry_scr[:, 0:1]
    ranks = [jnp.sum(jnp.where(hit, before, 0.0), axis=0, keepdims=True) for hit in hits]
    rank_ref[...] = jnp.concatenate(ranks + [jnp.zeros((ROUTE_ROWS - TOP_K, tm), F32)], axis=0)
    carry_scr[...] = carry_scr[...] + jnp.sum(cnt, axis=1, keepdims=True)
    count_ref[...] = carry_scr[...]


def _rank_call(route):
    t = route.shape[1]
    tm = RANK_TM
    tile = pl.BlockSpec((ROUTE_ROWS, tm), lambda i: (0, i))
    return pl.pallas_call(
        _rank_kernel,
        grid=(t // tm,),
        in_specs=[tile],
        out_specs=[tile, pl.BlockSpec((N_EXPERTS, LANES), lambda i: (0, 0))],
        out_shape=[
            jax.ShapeDtypeStruct((ROUTE_ROWS, t), F32),
            jax.ShapeDtypeStruct((N_EXPERTS, LANES), F32),
        ],
        scratch_shapes=[pltpu.VMEM((N_EXPERTS, LANES), F32)],
        compiler_params=_params("arbitrary"),
        name="rank",
    )(route)


def _slot_kernel(route_ref, rank_ref, start_ref, slot_ref):
    tm = route_ref.shape[1]
    starts = start_ref[...]
    slots = [jnp.sum(jnp.where(hit, starts, 0.0), axis=0, keepdims=True) + rank_ref[k:k + 1, :]
             for k, hit in enumerate(_expert_hits(route_ref, tm))]
    slot_ref[...] = jnp.concatenate(slots + [jnp.zeros((ROUTE_ROWS - TOP_K, tm), F32)], axis=0)


def _slot_call(route, rank, pad_starts):
    t = route.shape[1]
    tm = min(SLOT_TM, t)
    tile = pl.BlockSpec((ROUTE_ROWS, tm), lambda i: (0, i))
    return pl.pallas_call(
        _slot_kernel,
        grid=(t // tm,),
        in_specs=[tile, tile, pl.BlockSpec((N_EXPERTS, tm), lambda i: (0, 0))],
        out_specs=tile,
        out_shape=jax.ShapeDtypeStruct((ROUTE_ROWS, t), F32),
        compiler_params=_params("parallel"),
        name="slots",
    )(route, rank, jnp.broadcast_to(pad_starts[:, None], (N_EXPERTS, tm)))


def _for_static_slot(slot, fn):
    for par in range(2):
        pl.when(slot == par)(functools.partial(fn, par))


def _dispatch_kernel(cnt_ref, pstart_ref, dest_hbm, h_hbm, xs_hbm, idx_smem, hbuf, zbuf,
                     isem, hsem, dsem, zsem):
    tm = MOE_TM
    nd = TOP_K * tm
    i = pl.program_id(0)
    n = pl.num_programs(0)
    slot = i % 2
    n_slots = xs_hbm.shape[0]

    def idx_copy(step, sl):
        return pltpu.make_async_copy(dest_hbm.at[step], idx_smem.at[sl], isem.at[sl])

    def tile_copy(step, bf):
        rows = pl.ds(pl.multiple_of(step * tm, tm), tm)
        return pltpu.make_async_copy(h_hbm.at[rows], hbuf.at[bf], hsem.at[bf])

    def wait_scatter(bf):
        for _ in range(TOP_K):
            pltpu.make_async_copy(hbuf.at[bf], xs_hbm.at[pl.ds(0, tm)], dsem.at[bf]).wait()

    @pl.when(i == 0)
    def _():
        idx_copy(0, 0).start()
        tile_copy(0, 0).start()
        zbuf[...] = jnp.zeros(zbuf.shape, F32)

        def zero_gaps(wait):
            def per_expert(e, used):
                cnt = cnt_ref[e]
                gap = (MOE_BLOCK - cnt % MOE_BLOCK) % MOE_BLOCK
                row = pstart_ref[e] + cnt
                for k in range(MOE_BLOCK.bit_length() - 1):
                    size = 1 << k
                    copy = pltpu.make_async_copy(zbuf.at[pl.ds(0, size)], xs_hbm.at[pl.ds(row, size)], zsem)
                    pl.when(((gap >> k) & 1) == 1)(copy.wait if wait else copy.start)
                    row = row + (gap & size)
                return used + cnt + gap
            return lax.fori_loop(0, N_EXPERTS, per_expert, 0)
        zero_gaps(wait=False)
        used = zero_gaps(wait=True)

        def zero_block(blk):
            rows = pl.ds(pl.multiple_of(blk * MOE_BLOCK, MOE_BLOCK), MOE_BLOCK)
            return pltpu.make_async_copy(zbuf, xs_hbm.at[rows], zsem)

        def start_blk(blk, c):
            zero_block(blk).start()
            return c

        def wait_blk(blk, c):
            zero_block(blk).wait()
            return c
        lax.fori_loop(used // MOE_BLOCK, n_slots // MOE_BLOCK, start_blk, 0)
        lax.fori_loop(used // MOE_BLOCK, n_slots // MOE_BLOCK, wait_blk, 0)

    @pl.when(i > 0)
    def _():
        wait_scatter(1 - slot)

    @pl.when(i + 1 < n)
    def _():
        idx_copy(i + 1, 1 - slot).start()
        tile_copy(i + 1, 1 - slot).start()

    idx_copy(i, slot).wait()
    tile_copy(i, slot).wait()

    def scatter_rows(par):
        for r in range(nd):
            copy = pltpu.make_async_copy(hbuf.at[par, r % tm], xs_hbm.at[idx_smem[par, r]], dsem.at[par])
            copy.start(priority=r % DMA_QUEUES)
    _for_static_slot(slot, scatter_rows)

    @pl.when(i == n - 1)
    def _():
        wait_scatter(slot)


def _dispatch_call(counts, pad_starts, dest_tiles, h2, n_slots):
    n_steps = dest_tiles.shape[0]
    grid_spec = pltpu.PrefetchScalarGridSpec(
        num_scalar_prefetch=2,
        grid=(n_steps,),
        in_specs=[pl.BlockSpec(memory_space=pl.ANY), pl.BlockSpec(memory_space=pl.ANY)],
        out_specs=pl.BlockSpec(memory_space=pl.ANY),
        scratch_shapes=[
            pltpu.SMEM((2, TOP_K * MOE_TM), I32),
            pltpu.VMEM((2, MOE_TM) + ROW_TILE, F32),
            pltpu.VMEM((MOE_BLOCK,) + ROW_TILE, F32),
            pltpu.SemaphoreType.DMA((2,)),
            pltpu.SemaphoreType.DMA((2,)),
            pltpu.SemaphoreType.DMA((2,)),
            pltpu.SemaphoreType.DMA(()),
        ],
    )
    return pl.pallas_call(
        _dispatch_kernel,
        grid_spec=grid_spec,
        out_shape=jax.ShapeDtypeStruct((n_slots,) + ROW_TILE, F32),
        compiler_params=_params("arbitrary"),
        name="dispatch",
    )(counts, pad_starts, dest_tiles, h2)


def _expert_kernel(be_ref, x_ref, wg_ref, wu_ref, wd_ref, y_ref, wg_scr, wu_scr, wd_scr):
    i = pl.program_id(0)

    @pl.when((i == 0) | (be_ref[i] != be_ref[jnp.maximum(i - 1, 0)]))
    def _():
        wg_scr[...] = wg_ref[0].astype(BF16)
        wu_scr[...] = wu_ref[0].astype(BF16)
        wd_scr[...] = wd_ref[0].astype(BF16)

    xb = _load_token_tiles(x_ref, 0, MOE_BLOCK).astype(BF16)
    gate = _dot(xb, wg_scr[...])
    up = _dot(xb, wu_scr[...])
    hid = (gate * jax.nn.sigmoid(gate) * up).astype(BF16)
    _store_token_tiles(y_ref, _dot(hid, wd_scr[...]))


def _expert_call(blk_expert, xs, wg, wu, wd):
    n_blk = blk_expert.shape[0]
    d = wg.shape[1]
    blk = (MOE_BLOCK * ROW_TILE[0], LANES)
    grid_spec = pltpu.PrefetchScalarGridSpec(
        num_scalar_prefetch=1,
        grid=(n_blk,),
        in_specs=[
            pl.BlockSpec(blk, lambda i, be: (i, 0)),
            pl.BlockSpec((1, d, EXPERT_FF), lambda i, be: (be[i], 0, 0)),
            pl.BlockSpec((1, d, EXPERT_FF), lambda i, be: (be[i], 0, 0)),
            pl.BlockSpec((1, EXPERT_FF, d), lambda i, be: (be[i], 0, 0)),
        ],
        out_specs=pl.BlockSpec(blk, lambda i, be: (i, 0)),
        scratch_shapes=[
            pltpu.VMEM((d, EXPERT_FF), BF16),
            pltpu.VMEM((d, EXPERT_FF), BF16),
            pltpu.VMEM((EXPERT_FF, d), BF16),
        ],
    )
    return pl.pallas_call(
        _expert_kernel,
        grid_spec=grid_spec,
        out_shape=jax.ShapeDtypeStruct(xs.shape, F32),
        compiler_params=_params("arbitrary"),
        name="experts",
    )(blk_expert, xs, wg, wu, wd)


def _combine_kernel(dest_hbm, y_hbm, x1_ref, route_ref, mod_ref, o_ref, idx_smem, ybuf, isem, dsem):
    tm = MOE_TM
    nd = TOP_K * tm
    i = pl.program_id(0)
    n = pl.num_programs(0)
    slot = i % 2
    nxt = 1 - slot

    def idx_copy(step, sl):
        return pltpu.make_async_copy(dest_hbm.at[step], idx_smem.at[sl], isem.at[sl])

    ns = ROW_TILE[0]

    def step_rows(sl):
        return pltpu.make_async_copy(y_hbm.at[pl.ds(0, nd * ns)], ybuf.at[sl], dsem.at[sl])

    def issue_rows(par):
        for r in range(nd):
            src = y_hbm.at[pl.ds(pl.multiple_of(idx_smem[par, r] * ns, ns), ns)]
            copy = pltpu.make_async_copy(src, ybuf.at[par, pl.ds(r * ns, ns)], dsem.at[par])
            copy.start(priority=r % DMA_QUEUES)

    @pl.when(i == 0)
    def _():
        idx_copy(0, 0).start()
        idx_copy(0, 0).wait()
        issue_rows(0)

        @pl.when(n > 1)
        def _():
            idx_copy(1, 1).start()

    @pl.when(i + 1 < n)
    def _():
        idx_copy(i + 1, nxt).wait()
        _for_static_slot(nxt, issue_rows)

    @pl.when(i + 2 < n)
    def _():
        idx_copy(i + 2, slot).start()

    step_rows(slot).wait()
    route = jnp.concatenate([route_ref[...], jnp.zeros((LANES - ROUTE_ROWS, tm), F32)], axis=0).T
    w1 = route[:, 2:3]
    w2 = route[:, 3:4]
    y = (w1 * _load_token_tiles(ybuf, 0, tm, lead=(slot,))
         + w2 * _load_token_tiles(ybuf, tm, tm, lead=(slot,)))
    o_ref[...] = x1_ref[...] + mod_ref[0, 5:6, :] * y


def _combine_call(dest_tiles, y_slots, x1, route, mod6, s):
    t, d = x1.shape
    tm = MOE_TM
    per_b = s // tm
    return pl.pallas_call(
        _combine_kernel,
        grid=(t // tm,),
        in_specs=[
            pl.BlockSpec(memory_space=pl.ANY),
            pl.BlockSpec(memory_space=pl.ANY),
            pl.BlockSpec((tm, d), lambda i: (i, 0)),
            pl.BlockSpec((ROUTE_ROWS, tm), lambda i: (0, i)),
            pl.BlockSpec((1, 6, d), lambda i: (i // per_b, 0, 0)),
        ],
        out_specs=pl.BlockSpec((tm, d), lambda i: (i, 0)),
        out_shape=jax.ShapeDtypeStruct((t, d), F32),
        scratch_shapes=[
            pltpu.SMEM((2, TOP_K * tm), I32),
            pltpu.VMEM((2, TOP_K * tm * ROW_TILE[0], LANES), F32),
            pltpu.SemaphoreType.DMA((2,)),
            pltpu.SemaphoreType.DMA((2,)),
        ],
        compiler_params=_params("arbitrary"),
        name="combine",
    )(dest_tiles, y_slots, x1, route, mod6)


def _rope_tables(pos):
    inv = ROPE_THETA ** (-jnp.arange(0, HEAD_DIM, 2, dtype=F32) / HEAD_DIM)
    ang = pos.astype(F32)[..., None] * inv
    return jnp.cos(ang), jnp.sin(ang)


def _rope_tiles(pos):
    cos, sin = _rope_tables(pos)
    return (jnp.concatenate([cos, cos, cos, cos], axis=-1),
            jnp.concatenate([-sin, sin, -sin, sin], axis=-1))


def _selection_constants(s):
    n_c = s // CMP_STRIDE
    n_s = s // SEL_BLOCK
    cs = np.arange(n_c) * CMP_STRIDE
    ss = np.arange(n_s) * SEL_BLOCK
    ov = np.clip(np.minimum(cs[:, None] + CMP_BLOCK, ss[None, :] + SEL_BLOCK)
                 - np.maximum(cs[:, None], ss[None, :]), 0, None).astype(np.float32) / CMP_BLOCK
    kq = np.arange(NSA_TQ)[:, None] - np.arange(NSA_TQ)[None, :]
    wbias = np.stack([np.where(kq >= 0, 0.0, MASKED), np.where(kq <= 0, 0.0, MASKED)]).astype(np.float32)
    return jnp.asarray(ov.T, BF16), jnp.asarray(wbias)


def _mixer_and_router(x, c, positions, w_ada, b_ada, norm1_g, norm2_g, w_in, nsa_q_norm, nsa_k_norm,
                      cmp_pe_k, cmp_w1_k, cmp_w2_k, cmp_pe_v, cmp_w1_v, cmp_w2_v, dil_q_norm,
                      dil_k_norm, w_up_a, w_up_b, w_out, w_group, b_group, w_router, b_router):
    b, s, d = x.shape
    scale = HEAD_DIM ** -0.5
    mod6 = _mod_call(c, w_ada, b_ada).reshape(b, 6, d)

    c1 = A_Q
    c2 = c1 + 6 * A_KV
    c3 = c2 + 3 * NSA_HEADS
    c4 = c3 + 3 * DIL_W
    w_perm = jnp.concatenate([
        w_in[:, :c2], w_in[:, c3:c4],
        jnp.pad(w_in[:, c2:c3], ((0, 0), (0, LANES - 3 * NSA_HEADS))), w_in[:, c4:]], axis=1).astype(BF16)
    two = lambda g: jnp.concatenate([g, g]).astype(F32)
    gains = jnp.stack([two(nsa_q_norm) * (scale * LOG2E), two(nsa_k_norm), two(dil_q_norm) * scale,
                       two(dil_k_norm)])
    cos_t, sin_t = _rope_tables(positions)
    qext, kvc, ksl, vsl, kwp, vwp, dil0, dil1, dil2, ga, gm = _proj_call(
        x, mod6, norm1_g.reshape(1, d), w_perm, gains, cos_t, sin_t)

    pe = jnp.stack([jnp.concatenate([p, p], axis=1) for p in (cmp_pe_k, cmp_pe_v)])
    zero_w = jnp.zeros((CMP_BLOCK, HEAD_DIM, CMP_HIDDEN), F32)

    def per_position(w):
        w = w.reshape(CMP_BLOCK, HEAD_DIM, CMP_HIDDEN)
        return jnp.concatenate([jnp.concatenate([w, zero_w], axis=2),
                                jnp.concatenate([zero_w, w], axis=2)], axis=1)
    w1 = jnp.stack([per_position(cmp_w1_k), per_position(cmp_w1_v)]).astype(BF16)
    zeros = jnp.zeros((CMP_HIDDEN, HEAD_DIM), F32)
    ext = lambda w: jnp.stack([jnp.concatenate([w, zeros], 1), jnp.concatenate([zeros, w], 1)])
    w2ext = jnp.stack([ext(cmp_w2_k), ext(cmp_w2_v)]).astype(BF16)
    cmp_pos = jnp.pad(positions[:, CMP_BLOCK - 1::CMP_STRIDE], ((0, 0), (0, 1)))
    ccos, csin = _rope_tiles(cmp_pos)
    kc, vc = _cmp_call(kvc, pe, w1, w2ext, two(nsa_k_norm).reshape(1, LANES), ccos, csin)

    ov_t, wbias = _selection_constants(s)
    o_a = _nsa_call(qext, kc, vc, ksl, vsl, kwp, vwp, ga, ov_t, wbias)

    ods, lses = zip(*[_dil_call(qkv.reshape(b, 3, s, LANES), dl)
                      for qkv, (_, dl) in zip((dil0, dil1, dil2), DIL_PAIRS)])

    t = b * s
    w_r = jnp.concatenate([w_group, w_router.transpose(1, 0, 2).reshape(d, N_EXPERTS)], axis=1)
    w_r = jnp.pad(w_r, ((0, 0), (0, LANES - w_r.shape[1])))
    rcat = jnp.concatenate(_split(w_r.T), axis=0)
    rb = jnp.pad(jnp.concatenate([b_group, b_router.reshape(-1)]), (0, LANES - N_GROUPS - N_EXPERTS))
    return _merge_call(
        x.reshape(t, d), o_a.reshape(t, A_Q), ods, lses, gm.reshape(t, 2 * d), mod6,
        w_up_a.astype(BF16), w_up_b.astype(BF16), w_out.astype(BF16), norm2_g.reshape(1, d),
        rcat, rb.reshape(LANES, 1).astype(F32), s), mod6


def _moe(x1, h2, route, mod6, w_e_gate, w_e_up, w_e_down, s):
    t, _ = x1.shape
    rank, counts = _rank_call(route)
    counts = counts[:, 0].astype(I32)
    padded = (counts + MOE_BLOCK - 1) // MOE_BLOCK * MOE_BLOCK
    pad_ends = jnp.cumsum(padded)
    pad_starts = pad_ends - padded
    dest = _slot_call(route, rank, pad_starts.astype(F32))[0:TOP_K].astype(I32)
    n_slots = t * TOP_K + N_EXPERTS * MOE_BLOCK
    n_blk = n_slots // MOE_BLOCK
    blk_start = jnp.arange(n_blk, dtype=I32) * MOE_BLOCK
    blk_expert = jnp.minimum(jnp.sum((pad_ends[None, :] <= blk_start[:, None]).astype(I32), axis=1),
                             N_EXPERTS - 1)
    tm = MOE_TM
    dest_tiles = dest.reshape(TOP_K, t // tm, tm).transpose(1, 0, 2).reshape(t // tm, TOP_K * tm)
    xs = _dispatch_call(counts, pad_starts.astype(I32), dest_tiles, h2.reshape((t,) + ROW_TILE), n_slots)
    y_slots = _expert_call(blk_expert, xs.reshape(n_slots * ROW_TILE[0], LANES), w_e_gate, w_e_up, w_e_down)
    return _combine_call(dest_tiles, y_slots, x1, route, mod6, s)


def kernel(x, c, positions, w_ada, b_ada, norm1_g, norm2_g, w_in, nsa_q_norm, nsa_k_norm, cmp_pe_k,
           cmp_w1_k, cmp_w2_k, cmp_pe_v, cmp_w1_v, cmp_w2_v, dil_q_norm, dil_k_norm, w_up_a, w_up_b,
           w_out, w_group, b_group, w_router, b_router, w_e_gate, w_e_up, w_e_down):
    b, s, d = x.shape
    assert w_ada.shape[0] == 1 and d == D_MODEL and s % NSA_TK == 0
    (x1, h2, route), mod6 = _mixer_and_router(
        x, c, positions, w_ada[0], b_ada[0], norm1_g[0], norm2_g[0], w_in[0], nsa_q_norm[0],
        nsa_k_norm[0], cmp_pe_k[0], cmp_w1_k[0], cmp_w2_k[0], cmp_pe_v[0], cmp_w1_v[0], cmp_w2_v[0],
        dil_q_norm[0], dil_k_norm[0], w_up_a[0], w_up_b[0], w_out[0], w_group[0], b_group[0],
        w_router[0], b_router[0])
    out = _moe(x1, h2, route, mod6, w_e_gate[0], w_e_up[0], w_e_down[0], s)
    return out.reshape(b, s, d)
```

```python
import functools

import jax
import jax.numpy as jnp
import numpy as np
from jax import lax
from jax.experimental import pallas as pl
from jax.experimental.pallas import tpu as pltpu

F32 = jnp.float32
BF16 = jnp.bfloat16
I32 = jnp.int32

D_MODEL = 1024
HEAD_DIM = 64
LANES = 128
ROW_TILE = (D_MODEL // LANES, LANES)
ROPE_THETA = 10000.0
EPS = 1e-6
LOG2E = 1.4426950408889634
NEG_INF = -1e30
FORCE_SCORE = 1e9
MASKED = -1e30
PICKED = -3e38

NSA_HEADS = 8
NSA_KV_HEADS = 2
NSA_GROUP = 4
CMP_BLOCK = 32
CMP_STRIDE = 16
CMP_HIDDEN = 256
SEL_BLOCK = 64
N_SEL = 8
N_LOCAL_SEL = 2
WINDOW = 512
DIL_PAIRS = ((128, 1), (512, 4), (2048, 16))
DIL_GROUPS = 3
A_Q = NSA_HEADS * HEAD_DIM
A_KV = NSA_KV_HEADS * HEAD_DIM
DIL_W = 2 * DIL_GROUPS * HEAD_DIM
N_GROUPS = 4
EXPERTS_PER_GROUP = 8
N_EXPERTS = 32
TOP_K = 2
ROUTE_ROWS = 8
EXPERT_FF = 512
MOE_BLOCK = 512

VMEM_LIMIT = 56 * 1024 * 1024

T_KVC = 4
T_KVA = 6
T_DIL = 10
T_GA = 19
T_GM = 20
N_TILES = 36

PROJ_TM = 512
NSA_TQ = 128
NSA_TK = 512
NSA_NQ = 4
NSA_AHEAD = 1
DIL_T = 128
DIL_UNROLL = 16
MERGE_TM = 1024
RANK_TM = 1024
SLOT_TM = 4096
MOE_TM = 256
DMA_QUEUES = 2


def _dot(a, b):
    return jnp.dot(a, b, preferred_element_type=F32)


def _dot_nt(a, b):
    return lax.dot_general(a, b, (((1,), (1,)), ((), ())), preferred_element_type=F32)


def _dot_tn(a, b):
    return lax.dot_general(a, b, (((0,), (0,)), ((), ())), preferred_element_type=F32)


def _split(a):
    hi = a.astype(BF16)
    lo = (a - hi.astype(F32)).astype(BF16)
    return hi, lo


def _load_token_tiles(ref, first, n, lead=()):
    ns = ROW_TILE[0]
    return jnp.concatenate(
        [ref[lead + (pl.ds(first * ns + s, n, stride=ns), slice(None))] for s in range(ns)], axis=1)


def _store_token_tiles(ref, rows, first=0):
    ns = ROW_TILE[0]
    for s in range(ns):
        ref[pl.ds(first * ns + s, rows.shape[0], stride=ns), :] = rows[:, s * LANES:(s + 1) * LANES]


def _params(*sem):
    return pltpu.CompilerParams(dimension_semantics=sem, vmem_limit_bytes=VMEM_LIMIT)


def _mod_kernel(c_ref, whi_ref, wlo_ref, b_ref, o_ref):
    c = c_ref[...]
    ca = c * jax.nn.sigmoid(c)
    hi, lo = _split(ca)
    whi = whi_ref[...]
    o_ref[...] = _dot(hi, whi) + _dot(lo, whi) + _dot(hi, wlo_ref[...]) + b_ref[...]


def _mod_call(c, w_ada, b_ada):
    b, d = c.shape
    n = w_ada.shape[1]
    whi, wlo = _split(w_ada)
    tn = 1024
    return pl.pallas_call(
        _mod_kernel,
        grid=(n // tn,),
        in_specs=[
            pl.BlockSpec((b, d), lambda j: (0, 0)),
            pl.BlockSpec((d, tn), lambda j: (0, j)),
            pl.BlockSpec((d, tn), lambda j: (0, j)),
            pl.BlockSpec((1, tn), lambda j: (0, j)),
        ],
        out_specs=pl.BlockSpec((b, tn), lambda j: (0, j)),
        out_shape=jax.ShapeDtypeStruct((b, n), F32),
        compiler_params=_params("parallel"),
        name="mod",
    )(c, whi, wlo, b_ada.reshape(1, n))


def _norm_rope(a, gain, cos, sin_signed, lo, first):
    sq = a * a
    s0 = jnp.sum(jnp.where(lo, sq, 0.0), axis=-1, keepdims=True)
    s1 = jnp.sum(jnp.where(lo, 0.0, sq), axis=-1, keepdims=True)
    r = jnp.where(lo, lax.rsqrt(s0 * (1.0 / HEAD_DIM) + EPS), lax.rsqrt(s1 * (1.0 / HEAD_DIM) + EPS))
    y = a * r * gain
    rot = jnp.where(first, pltpu.roll(y, 96, 1), pltpu.roll(y, 32, 1))
    return y * cos + rot * sin_signed


def _proj_kernel(x_ref, mod_ref, g1_ref, w_ref, gains_ref, cos_ref, sin_ref, kw0_ref, vw0_ref,
                 qext_ref, kvc_ref, ksl_ref, vsl_ref, kwp_ref, vwp_ref, dil0_ref, dil1_ref, dil2_ref,
                 ga_ref, gm_ref, perm_scr):
    del kw0_ref, vw0_ref
    tm = x_ref.shape[1]
    x = x_ref[0]
    ms = jnp.mean(x * x, axis=-1, keepdims=True)
    y = x * lax.rsqrt(ms + EPS) * g1_ref[...]
    sh1 = mod_ref[0, 0:1, :]
    sc1 = mod_ref[0, 1:2, :]
    h = (y * (1.0 + sc1) + sh1).astype(BF16)
    lane = lax.broadcasted_iota(I32, (tm, LANES), 1)
    lo = lane < HEAD_DIM
    first = (lane & (HEAD_DIM - 1)) < (HEAD_DIM // 2)
    cos = jnp.concatenate([cos_ref[0]] * 4, axis=1)
    sin = jnp.concatenate([sin_ref[0]] * 4, axis=1)
    sin = jnp.where(first, -sin, sin)
    nr = functools.partial(_norm_rope, cos=cos, sin_signed=sin, lo=lo, first=first)

    for c in range(N_TILES // 2):
        acc = _dot(h, w_ref[:, c * 2 * LANES:(c + 1) * 2 * LANES])
        for half in range(2):
            t = 2 * c + half
            a = acc[:, half * LANES:(half + 1) * LANES]
            if t < T_KVC:
                yq = nr(a, gains_ref[0:1, :])
                rq = pltpu.roll(yq, HEAD_DIM, 1)
                if t // 2 == 0:
                    e0 = jnp.where(lo, yq, 0.0)
                    e1 = jnp.where(lo, rq, 0.0)
                else:
                    e0 = jnp.where(lo, 0.0, rq)
                    e1 = jnp.where(lo, 0.0, yq)
                qext_ref[0, 2 * t] = e0.astype(BF16)
                qext_ref[0, 2 * t + 1] = e1.astype(BF16)
            elif t < T_KVA:
                kvc_ref[0, t - T_KVC] = a
            elif t < T_DIL:
                j = t - T_KVA
                k_out, v_out = (ksl_ref, vsl_ref) if j < 2 else (kwp_ref, vwp_ref)
                if j % 2 == 0:
                    k_out[0] = nr(a, gains_ref[1:2, :]).astype(BF16)
                else:
                    v_out[0, 0] = jnp.where(lo, a, 1.0).astype(BF16)
                    v_out[0, 1] = jnp.where(lo, pltpu.roll(a, HEAD_DIM, 1), 1.0).astype(BF16)
            elif t < T_GA:
                j = t - T_DIL
                if j < 3:
                    v = nr(a, gains_ref[2:3, :])
                elif j < 6:
                    v = nr(a, gains_ref[3:4, :])
                else:
                    v = a
                kind, gi = divmod(j, DIL_GROUPS)
                dl = DIL_PAIRS[gi][1]
                if dl == 1:
                    dil0_ref[0, kind] = v.astype(BF16)
                else:
                    out = (dil0_ref, dil1_ref, dil2_ref)[gi]
                    perm_scr[...] = v
                    for r in range(dl):
                        out[0, kind, r] = perm_scr[pl.ds(r, tm // dl, stride=dl), :].astype(BF16)
            elif t < T_GM:
                ga_ref[0] = jax.nn.sigmoid(a)
            else:
                j = t - T_GM
                gm_ref[0, :, j * LANES:(j + 1) * LANES] = jax.nn.sigmoid(a).astype(BF16)


def _proj_call(x, mod6, g1, w_perm, gains, cos_t, sin_t):
    b, s, d = x.shape
    tm = PROJ_TM
    n = N_TILES * LANES
    row = lambda bi, i: (bi, i, 0)
    heads = lambda bi, i: (bi, 0, i, 0)
    pad = WINDOW // tm
    in_specs = [
        pl.BlockSpec((1, tm, d), row),
        pl.BlockSpec((1, 6, d), lambda bi, i: (bi, 0, 0)),
        pl.BlockSpec((1, d), lambda bi, i: (0, 0)),
        pl.BlockSpec((d, n), lambda bi, i: (0, 0)),
        pl.BlockSpec((4, LANES), lambda bi, i: (0, 0)),
        pl.BlockSpec((1, tm, HEAD_DIM // 2), row),
        pl.BlockSpec((1, tm, HEAD_DIM // 2), row),
        pl.BlockSpec(memory_space=pl.ANY),
        pl.BlockSpec(memory_space=pl.ANY),
    ]
    out_specs = [
        pl.BlockSpec((1, NSA_HEADS, tm, LANES), heads),
        pl.BlockSpec((1, 2, tm, LANES), heads),
        pl.BlockSpec((1, tm, LANES), row),
        pl.BlockSpec((1, NSA_KV_HEADS, tm, LANES), heads),
        pl.BlockSpec((1, tm, LANES), lambda bi, i: (bi, i + pad, 0)),
        pl.BlockSpec((1, NSA_KV_HEADS, tm, LANES), lambda bi, i: (bi, 0, i + pad, 0)),
        pl.BlockSpec((1, 3, tm, LANES), heads),
        *[pl.BlockSpec((1, 3, dl, tm // dl, LANES), lambda bi, i: (bi, 0, 0, i, 0)) for _, dl in DIL_PAIRS[1:]],
        pl.BlockSpec((1, tm, LANES), row),
        pl.BlockSpec((1, tm, 2 * d), row),
    ]
    out_shape = [
        jax.ShapeDtypeStruct((b, NSA_HEADS, s, LANES), BF16),
        jax.ShapeDtypeStruct((b, 2, s, LANES), F32),
        jax.ShapeDtypeStruct((b, s, LANES), BF16),
        jax.ShapeDtypeStruct((b, NSA_KV_HEADS, s, LANES), BF16),
        jax.ShapeDtypeStruct((b, s + WINDOW, LANES), BF16),
        jax.ShapeDtypeStruct((b, NSA_KV_HEADS, s + WINDOW, LANES), BF16),
        jax.ShapeDtypeStruct((b, 3, s, LANES), BF16),
        *[jax.ShapeDtypeStruct((b, 3, dl, s // dl, LANES), BF16) for _, dl in DIL_PAIRS[1:]],
        jax.ShapeDtypeStruct((b, s, LANES), F32),
        jax.ShapeDtypeStruct((b, s, 2 * d), BF16),
    ]
    kw0 = jnp.zeros(out_shape[4].shape, BF16)
    vw0 = jnp.zeros(out_shape[5].shape, BF16)
    return pl.pallas_call(
        _proj_kernel,
        grid=(b, s // tm),
        in_specs=in_specs,
        out_specs=out_specs,
        out_shape=out_shape,
        scratch_shapes=[pltpu.VMEM((tm, LANES), F32)],
        input_output_aliases={7: 4, 8: 5},
        compiler_params=_params("parallel", "parallel"),
        name="proj",
    )(x, mod6, g1, w_perm, gains, cos_t, sin_t, kw0, vw0)


def _cmp_kernel(x_ref, pe_ref, w1_ref, w2_ref, gain_ref, cos_ref, sin_ref, kc_ref, vc_ref):
    nb = kc_ref.shape[1]
    lane = lax.broadcasted_iota(I32, (nb, LANES), 1)
    lo = lane < HEAD_DIM
    first = (lane & (HEAD_DIM - 1)) < (HEAD_DIM // 2)
    for kind in range(2):
        top = jnp.zeros((nb, 2 * CMP_HIDDEN), F32)
        bot = jnp.zeros((nb, 2 * CMP_HIDDEN), F32)
        for r in range(CMP_STRIDE):
            x = x_ref[0, kind, pl.ds(r, nb, stride=CMP_STRIDE), :]
            top = top + _dot((x + pe_ref[kind, r:r + 1, :]).astype(BF16), w1_ref[kind, r])
            bot = bot + _dot((x + pe_ref[kind, CMP_STRIDE + r:CMP_STRIDE + r + 1, :]).astype(BF16),
                             w1_ref[kind, CMP_STRIDE + r])
        hid = top + pltpu.roll(bot, nb - 1, 0)
        hid = (hid * jax.nn.sigmoid(hid)).astype(BF16)
        out = (_dot(hid[:, 0:CMP_HIDDEN], w2_ref[kind, 0])
               + _dot(hid[:, CMP_HIDDEN:2 * CMP_HIDDEN], w2_ref[kind, 1]))
        if kind == 0:
            out = _norm_rope(out, gain_ref[...], cos_ref[0], sin_ref[0], lo, first)
            kc_ref[0] = out.astype(BF16)
        else:
            vc_ref[0, 0] = jnp.where(lo, out, 0.0).astype(BF16)
            vc_ref[0, 1] = jnp.where(lo, pltpu.roll(out, HEAD_DIM, 1), 0.0).astype(BF16)


def _cmp_call(kvc, pe, w1, w2ext, gain_k, ccos, csin):
    b, _, s, _ = kvc.shape
    nb = s // CMP_STRIDE
    return pl.pallas_call(
        _cmp_kernel,
        grid=(b,),
        in_specs=[
            pl.BlockSpec((1, 2, s, LANES), lambda bi: (bi, 0, 0, 0)),
            pl.BlockSpec(pe.shape, lambda bi: (0, 0, 0)),
            pl.BlockSpec(w1.shape, lambda bi: (0, 0, 0, 0)),
            pl.BlockSpec((2, 2, CMP_HIDDEN, LANES), lambda bi: (0, 0, 0, 0)),
            pl.BlockSpec((1, LANES), lambda bi: (0, 0)),
            pl.BlockSpec((1, nb, LANES), lambda bi: (bi, 0, 0)),
            pl.BlockSpec((1, nb, LANES), lambda bi: (bi, 0, 0)),
        ],
        out_specs=[
            pl.BlockSpec((1, nb, LANES), lambda bi: (bi, 0, 0)),
            pl.BlockSpec((1, NSA_KV_HEADS, nb, LANES), lambda bi: (bi, 0, 0, 0)),
        ],
        out_shape=[
            jax.ShapeDtypeStruct((b, nb, LANES), BF16),
            jax.ShapeDtypeStruct((b, NSA_KV_HEADS, nb, LANES), BF16),
        ],
        compiler_params=_params("parallel"),
        name="cmp",
    )(kvc, pe, w1, w2ext, gain_k, ccos, csin)


def _softmax_cols(s, exp=jnp.exp):
    p = exp(s - jnp.max(s, axis=0, keepdims=True))
    return p, jnp.sum(p, axis=0, keepdims=True)


def _block_max(s, offs, bs):
    m = None
    for j, off in enumerate(offs):
        mj = jnp.max(s[j * bs:(j + 1) * bs], axis=0, keepdims=True) + off
        m = mj if m is None else jnp.maximum(m, mj)
    return m


def _block_exp2(s, offs, bs, m):
    return jnp.concatenate(
        [jnp.exp2(s[j * bs:(j + 1) * bs] - (m - off)) for j, off in enumerate(offs)], axis=0)


def _nsa_kernel(q_ref, kc_ref, vc_ref, ksl_ref, vsl_ref, kw_ref, vw_ref, ga_ref, ov_ref, wb_ref,
                o_ref, m_scr, acc_scr, sb_scr, *s_scrs):
    tq = NSA_TQ
    tk = NSA_TK
    hd = HEAD_DIM
    cols = NSA_GROUP * tq
    tiles = range(NSA_NQ)
    s_buf = lambda u, c: s_scrs[2 * u + c % 2]
    n_blk = ov_ref.shape[0]
    n_chunks = ksl_ref.shape[1] // tk
    bpc = tk // SEL_BLOCK
    kh = pl.program_id(1)
    base = pl.multiple_of(pl.program_id(2) * (NSA_NQ * tq), NSA_NQ * tq)
    t0 = [base + u * tq for u in tiles]
    q4 = [q_ref[0, :, u * tq:(u + 1) * tq, :].reshape(cols, LANES) for u in tiles]
    lane_q = lax.broadcasted_iota(I32, (1, tq), 1)
    lane_q4 = lax.broadcasted_iota(I32, (1, cols), 1) & (tq - 1)
    per_head = lambda b: jnp.concatenate([b] * NSA_GROUP, axis=1)
    older_edge = wb_ref[0]
    causal_edge = wb_ref[1]
    nwin = WINDOW + tq
    n_wb = nwin // tq
    blk = lax.broadcasted_iota(I32, (n_blk, 1), 0)
    blk_f = blk.astype(F32)
    ov = ov_ref[...]

    diag = base // tk
    s_cmp, s_win, s_diag = {}, {}, {}
    o_cmp, o_win, sel_bias = {}, {}, {}

    def qk_products(u):
        s_cmp[u] = _dot_nt(kc_ref[0], q4[u])
        s_win[u] = _dot_nt(kw_ref[0, pl.ds(t0[u], nwin), :], q4[u])

    def qk_selected(u):
        s_diag[u] = _dot_nt(ksl_ref[0, pl.ds(base, (u + 1) * tq), :], q4[u])
        s_buf(u, 0)[...] = _dot_nt(ksl_ref[0, 0:tk, :], q4[u])

    def compressed_and_select(u):
        nb = s_cmp[u].shape[0]
        cmp_end = lax.broadcasted_iota(I32, (nb, 1), 0) * CMP_STRIDE + (CMP_BLOCK - 1)
        valid = cmp_end <= t0[u] + lane_q4
        e, den = _softmax_cols(jnp.where(valid, s_cmp[u], NEG_INF), jnp.exp2)
        p = jnp.where(valid, e / den, 0.0)
        o_cmp[u] = _dot_tn(vc_ref[0, 0], p.astype(BF16))[0:hd]
        psum = p[:, 0:tq] + p[:, tq:2 * tq] + p[:, 2 * tq:3 * tq] + p[:, 3 * tq:4 * tq]
        p_hi, p_lo = _split(psum)
        imp = _dot(ov, p_hi) + _dot(ov, p_lo)
        rel = ((t0[u] + lane_q) >> 6) - blk
        forced = (blk == 0) | ((rel >= 0) & (rel < N_LOCAL_SEL))
        score = jnp.where(rel < 0, NEG_INF, jnp.where(forced, FORCE_SCORE, imp))
        bias = jnp.full((n_blk, tq), MASKED, F32)
        for _ in range(N_SEL):
            best = jnp.max(score, axis=0, keepdims=True)
            first = jnp.min(jnp.where(score == best, blk_f, float(n_blk)), axis=0, keepdims=True)
            pick = blk_f == first
            bias = jnp.where(pick, 0.0, bias)
            score = jnp.where(pick, PICKED, score)
        sel_bias[u] = per_head(jnp.where(rel < 0, MASKED, bias))
        sb_scr[u] = sel_bias[u]

    def window(u):
        sw = jnp.concatenate([s_win[u][0:tq] + per_head(older_edge), s_win[u][tq:nwin - tq],
                              s_win[u][nwin - tq:nwin] + per_head(causal_edge)], axis=0)
        w_offs = [jnp.where(t0[u] - WINDOW + j * tq >= 0, 0.0, MASKED) for j in range(n_wb)]
        pw = _block_exp2(sw, w_offs, tq, _block_max(sw, w_offs, tq))
        ow = _dot_tn(vw_ref[0, 0, pl.ds(t0[u], nwin), :], pw.astype(BF16))
        o_win[u] = ow[0:hd] / ow[hd:hd + 1]

    def diagonal(u):
        nk = (u + 1) * tq
        sd = s_diag[u][nk - tq:nk] + per_head(causal_edge)
        if u > 0:
            sd = jnp.concatenate([s_diag[u][0:nk - tq], sd], axis=0)
        offs = [sb_scr[u, pl.ds(diag * bpc + j, 1), :] for j in range(nk // SEL_BLOCK)]
        m_new = _block_max(sd, offs, SEL_BLOCK)
        vd = vsl_ref[0, 0, pl.ds(base, nk), :]
        acc_scr[u] = _dot_tn(vd, _block_exp2(sd, offs, SEL_BLOCK, m_new).astype(BF16))
        m_scr[u] = m_new

    for u in tiles:
        qk_products(u)
    for u in tiles:
        compressed_and_select(u)
        qk_selected(u)
    for u in tiles:
        window(u)
        diagonal(u)

    def chunk(c):
        ahead = c + 2 < n_chunks

        def next_scores(u):
            s_buf(u, c + 1)[...] = _dot_nt(ksl_ref[0, (c + 1) * tk:(c + 2) * tk, :], q4[u])

        if ahead:
            for u in range(min(NSA_AHEAD, NSA_NQ)):
                next_scores(u)
        vb = vsl_ref[0, 0, c * tk:(c + 1) * tk, :]
        for u in tiles:
            sc = s_buf(u, c)[...]
            offs = [sel_bias[u][c * bpc + j:c * bpc + j + 1] for j in range(bpc)]
            m_old = m_scr[u]
            m_new = jnp.maximum(m_old, _block_max(sc, offs, SEL_BLOCK))
            pe = _block_exp2(sc, offs, SEL_BLOCK, m_new)
            acc_scr[u] = jnp.exp2(m_old - m_new) * acc_scr[u] + _dot_tn(vb, pe.astype(BF16))
            m_scr[u] = m_new
            if ahead and u + NSA_AHEAD < NSA_NQ:
                next_scores(u + NSA_AHEAD)

    for c in range(n_chunks - 1):
        pl.when(c < diag)(functools.partial(chunk, c))

    is0 = kh == 0
    for u in tiles:
        rows = slice(u * tq, (u + 1) * tq)
        o_slc = acc_scr[u, 0:hd, :] / acc_scr[u, hd:hd + 1, :]
        gat = ga_ref[0, rows, :].T
        heads = []
        for g in range(NSA_GROUP):
            c = slice(g * tq, (g + 1) * tq)
            og = jnp.zeros((hd, tq), F32)
            for gi, ob in enumerate((o_cmp[u], o_slc, o_win[u])):
                c0 = gi * NSA_HEADS + g
                c1 = c0 + NSA_GROUP
                gate = jnp.where(is0, gat[c0:c0 + 1, :], gat[c1:c1 + 1, :])
                og = og + gate * ob[:, c]
            heads.append(og)
        for pair in range(2):
            tile = jnp.concatenate([heads[2 * pair], heads[2 * pair + 1]], axis=0)
            o_ref[0, rows, pair * LANES:(pair + 1) * LANES] = tile.T.astype(BF16)


def _nsa_call(qext, kc, vc, ksl, vsl, kwp, vwp, ga, ov_t, wbias):
    b, _, s, _ = qext.shape
    tq = NSA_TQ * NSA_NQ
    nb = kc.shape[1]
    cols = NSA_GROUP * NSA_TQ
    assert NSA_TK == tq
    shared = lambda rows: pl.BlockSpec((1, rows, LANES), lambda bi, k, i: (bi, 0, 0))
    per_kv = lambda rows: pl.BlockSpec((1, 1, rows, LANES), lambda bi, k, i: (bi, k, 0, 0))
    return pl.pallas_call(
        _nsa_kernel,
        grid=(b, NSA_KV_HEADS, s // tq),
        in_specs=[
            pl.BlockSpec((1, NSA_GROUP, tq, LANES), lambda bi, k, i: (bi, k, i, 0)),
            shared(nb), per_kv(nb),
            shared(s), per_kv(s),
            shared(s + WINDOW), per_kv(s + WINDOW),
            pl.BlockSpec((1, tq, LANES), lambda bi, k, i: (bi, i, 0)),
            pl.BlockSpec(ov_t.shape, lambda bi, k, i: (0, 0)),
            pl.BlockSpec(wbias.shape, lambda bi, k, i: (0, 0, 0)),
        ],
        out_specs=pl.BlockSpec((1, tq, 2 * LANES), lambda bi, k, i: (bi, i, k)),
        out_shape=jax.ShapeDtypeStruct((b, s, A_Q), BF16),
        scratch_shapes=[
            pltpu.VMEM((NSA_NQ, 1, cols), F32),
            pltpu.VMEM((NSA_NQ, LANES, cols), F32),
            pltpu.VMEM((NSA_NQ, ov_t.shape[0], cols), F32),
            *[pltpu.VMEM((NSA_TK, cols), F32)] * (2 * NSA_NQ),
        ],
        compiler_params=_params("parallel", "parallel", "parallel"),
        name="nsa",
    )(qext, kc, vc, ksl, vsl, kwp, vwp, ga, ov_t, wbias)


def _dil_kernel(q_ref, k_ref, v_ref, o_ref, lse_ref, *, seg_tiles):
    t = DIL_T
    lo = lax.broadcasted_iota(I32, (t, LANES), 1) < HEAD_DIM
    kj = lax.broadcasted_iota(I32, (t, 2 * t), 0)
    qi = lax.broadcasted_iota(I32, (t, 2 * t), 1) & (t - 1)
    top = lax.broadcasted_iota(I32, (LANES, t), 0) < HEAD_DIM

    def scores(i):
        has_prev = (i & (seg_tiles - 1)) != 0
        cs = pl.multiple_of(i * t, t)
        ps = pl.multiple_of(jnp.maximum(i - 1, 0) * t, t)
        q = q_ref[0, 0, pl.ds(cs, t), :]
        zero = jnp.zeros_like(q)
        q2 = jnp.concatenate([jnp.where(lo, q, zero), jnp.where(lo, zero, q)], axis=0)
        sp = _dot_nt(k_ref[0, 0, pl.ds(ps, t), :], q2)
        sc = _dot_nt(k_ref[0, 0, pl.ds(cs, t), :], q2)
        return has_prev, ps, cs, sp, sc

    def attend(has_prev, ps, cs, sp, sc):
        sp = jnp.where((kj >= qi) & has_prev, sp, MASKED)
        sc = jnp.where(qi >= kj, sc, MASKED)
        m = jnp.maximum(jnp.max(sp, axis=0, keepdims=True), jnp.max(sc, axis=0, keepdims=True))
        pp = jnp.exp(sp - m)
        pc = jnp.exp(sc - m)
        l = jnp.sum(pp, axis=0, keepdims=True) + jnp.sum(pc, axis=0, keepdims=True)
        o = (_dot_tn(v_ref[0, 0, pl.ds(ps, t), :], pp.astype(BF16))
             + _dot_tn(v_ref[0, 0, pl.ds(cs, t), :], pc.astype(BF16))) / l
        lse = m + jnp.log(l)
        o_ref[0, pl.ds(cs, t), :] = jnp.where(top, o[:, 0:t], o[:, t:2 * t]).T
        lse_ref[0, pl.ds(cs, t), :] = jnp.where(top, lse[:, 0:t], lse[:, t:2 * t]).T

    def tiles(j, carry):
        group = [scores(j * DIL_UNROLL + u) for u in range(DIL_UNROLL)]
        for args in group:
            attend(*args)
        return carry

    lax.fori_loop(0, q_ref.shape[2] // (t * DIL_UNROLL), tiles, 0)


def _dil_call(qkv, dl):
    b, _, s, _ = qkv.shape
    kind = lambda j: pl.BlockSpec((1, 1, s, LANES), lambda bi: (bi, j, 0, 0))
    out = pl.BlockSpec((1, s, LANES), lambda bi: (bi, 0, 0))
    return pl.pallas_call(
        functools.partial(_dil_kernel, seg_tiles=s // dl // DIL_T),
        grid=(b,),
        in_specs=[kind(0), kind(1), kind(2)],
        out_specs=[out, out],
        out_shape=[jax.ShapeDtypeStruct((b, s, LANES), F32)] * 2,
        compiler_params=_params("parallel"),
        name=f"dil{dl}",
    )(qkv, qkv, qkv)


def _merge_kernel(x_ref, oa_ref, od0_ref, od1_ref, od2_ref, ls0_ref, ls1_ref, ls2_ref, gm_ref, mod_ref,
                  wa_ref, wb_ref, wo_ref, g2_ref, rcat_ref, rb_ref, x1_ref, h2_ref, route_ref, perm_scr):
    tm = x_ref.shape[0]
    d = x_ref.shape[1]
    hm = tm // 2
    for n, ref in enumerate((od1_ref, od2_ref, ls1_ref, ls2_ref)):
        dl = ref.shape[1]
        for r in range(dl):
            perm_scr[n, pl.ds(r, tm // dl, stride=dl), :] = ref[0, r]

    def up(rows):
        l0, l1, l2 = ls0_ref[rows], perm_scr[2, rows], perm_scr[3, rows]
        mx = jnp.maximum(jnp.maximum(l0, l1), l2)
        e0, e1, e2 = jnp.exp(l0 - mx), jnp.exp(l1 - mx), jnp.exp(l2 - mx)
        den = e0 + e1 + e2
        ob = jnp.concatenate([(od0_ref[rows] * (e0 / den)).astype(BF16),
                              (perm_scr[0, rows] * (e1 / den)).astype(BF16),
                              (perm_scr[1, rows] * (e2 / den)).astype(BF16)], axis=1)
        return _dot(oa_ref[rows], wa_ref[...]), _dot(ob, wb_ref[...])

    def out_proj(rows, ya, yb):
        y = gm_ref[rows, 0:d].astype(F32) * ya + gm_ref[rows, d:2 * d].astype(F32) * yb
        return _dot(y.astype(BF16), wo_ref[...])

    def residual_and_logits(h, rows, z):
        x1 = x_ref[rows] + mod_ref[0, 2:3, :] * z
        x1_ref[rows] = x1
        ms = jnp.mean(x1 * x1, axis=-1, keepdims=True)
        h2 = x1 * lax.rsqrt(ms + EPS) * g2_ref[...]
        h2 = h2 * (1.0 + mod_ref[0, 4:5, :]) + mod_ref[0, 3:4, :]
        _store_token_tiles(h2_ref, h2, first=h * hm)
        hi, lo = _split(h2)
        both = _dot_nt(rcat_ref[...], hi)
        return both[0:LANES] + both[LANES:2 * LANES] + _dot_nt(rcat_ref[0:LANES, :], lo) + rb_ref[...]

    def route(h, logits):
        row = lax.broadcasted_iota(I32, (LANES, hm), 0).astype(F32)
        gl = jnp.where(row < N_GROUPS, logits, NEG_INF)
        gmax = jnp.max(gl, axis=0, keepdims=True)
        g_w = 1.0 / jnp.sum(jnp.exp(gl - gmax), axis=0, keepdims=True)
        g_idx = jnp.min(jnp.where(gl == gmax, row, float(LANES)), axis=0, keepdims=True)
        e_lo = N_GROUPS + EXPERTS_PER_GROUP * g_idx
        el = jnp.where((row >= e_lo) & (row < e_lo + EXPERTS_PER_GROUP), logits, NEG_INF)
        m1 = jnp.max(el, axis=0, keepdims=True)
        i1 = jnp.min(jnp.where(el == m1, row, float(LANES)), axis=0, keepdims=True)
        el2 = jnp.where(row == i1, PICKED, el)
        m2 = jnp.max(el2, axis=0, keepdims=True)
        i2 = jnp.min(jnp.where(el2 == m2, row, float(LANES)), axis=0, keepdims=True)
        ex = jnp.exp(m2 - m1)
        w1 = g_w * (1.0 / (1.0 + ex))
        w2 = g_w * (ex / (1.0 + ex))
        route_ref[:, h * hm:(h + 1) * hm] = jnp.concatenate(
            [i1 - N_GROUPS, i2 - N_GROUPS, w1, w2, jnp.zeros((ROUTE_ROWS - 4, hm), F32)], axis=0)

    halves = (slice(0, hm), slice(hm, tm))
    ups = [up(rows) for rows in halves]
    zs = [out_proj(rows, *u) for rows, u in zip(halves, ups)]
    logits = [residual_and_logits(h, rows, z) for h, (rows, z) in enumerate(zip(halves, zs))]
    for h, lg in enumerate(logits):
        route(h, lg)


def _merge_call(x2, oa, ods, lses, gm, mod6, wa, wb, wo, g2, rcat, rb, s):
    t, d = x2.shape
    b = t // s
    tm = MERGE_TM
    per_b = s // tm
    row = lambda i: (i, 0)
    const = lambda i: (0, 0)

    def by_class(dl):
        return pl.BlockSpec((1, dl, tm // dl, LANES), lambda i: (i // per_b, 0, i % per_b, 0))

    dls = [dl for _, dl in DIL_PAIRS]
    assert dls[0] == 1
    group_specs = [pl.BlockSpec((tm, LANES), row)] + [by_class(dl) for dl in dls[1:]]
    views = lambda arrs: [arrs[0].reshape(t, LANES)] + [
        a.reshape(b, dl, s // dl, LANES) for a, dl in zip(arrs[1:], dls[1:])]
    return pl.pallas_call(
        _merge_kernel,
        grid=(t // tm,),
        in_specs=[
            pl.BlockSpec((tm, d), row),
            pl.BlockSpec((tm, A_Q), row),
            *group_specs, *group_specs,
            pl.BlockSpec((tm, 2 * d), row),
            pl.BlockSpec((1, 6, d), lambda i: (i // per_b, 0, 0)),
            pl.BlockSpec(wa.shape, const),
            pl.BlockSpec(wb.shape, const),
            pl.BlockSpec(wo.shape, const),
            pl.BlockSpec((1, d), const),
            pl.BlockSpec(rcat.shape, const),
            pl.BlockSpec((LANES, 1), const),
        ],
        out_specs=[
            pl.BlockSpec((tm, d), row),
            pl.BlockSpec((tm * ROW_TILE[0], LANES), row),
            pl.BlockSpec((ROUTE_ROWS, tm), lambda i: (0, i)),
        ],
        out_shape=[
            jax.ShapeDtypeStruct((t, d), F32),
            jax.ShapeDtypeStruct((t * ROW_TILE[0], LANES), F32),
            jax.ShapeDtypeStruct((ROUTE_ROWS, t), F32),
        ],
        scratch_shapes=[pltpu.VMEM((2 * (len(dls) - 1), tm, LANES), F32)],
        compiler_params=_params("parallel"),
        name="merge",
    )(x2, oa, *views(ods), *views(lses), gm, mod6, wa, wb, wo, g2, rcat, rb)


def _expert_hits(route_ref, tm):
    row = lax.broadcasted_iota(I32, (N_EXPERTS, tm), 0).astype(F32)
    return [row == route_ref[k:k + 1, :] for k in range(TOP_K)]


def _rank_kernel(route_ref, rank_ref, count_ref, carry_scr):
    tm = route_ref.shape[1]

    @pl.when(pl.program_id(0) == 0)
    def _():
        carry_scr[...] = jnp.zeros(carry_scr.shape, F32)

    hits = _expert_hits(route_ref, tm)
    cnt = jnp.where(hits[0] | hits[1], 1.0, 0.0)
    r = lax.broadcasted_iota(I32, (tm, tm), 0)
    c = lax.broadcasted_iota(I32, (tm, tm), 1)
    earlier = jnp.where(r < c, 1.0, 0.0).astype(BF16)
    before = _dot(cnt.astype(BF16), earlier) + carry_scr[:, 0:1]
    ranks = [jnp.sum(jnp.where(hit, before, 0.0), axis=0, keepdims=True) for hit in hits]
    rank_ref[...] = jnp.concatenate(ranks + [jnp.zeros((ROUTE_ROWS - TOP_K, tm), F32)], axis=0)
    carry_scr[...] = carry_scr[...] + jnp.sum(cnt, axis=1, keepdims=True)
    count_ref[...] = carry_scr[...]


def _rank_call(route):
    t = route.shape[1]
    tm = RANK_TM
    tile = pl.BlockSpec((ROUTE_ROWS, tm), lambda i: (0, i))
    return pl.pallas_call(
        _rank_kernel,
        grid=(t // tm,),
        in_specs=[tile],
        out_specs=[tile, pl.BlockSpec((N_EXPERTS, LANES), lambda i: (0, 0))],
        out_shape=[
            jax.ShapeDtypeStruct((ROUTE_ROWS, t), F32),
            jax.ShapeDtypeStruct((N_EXPERTS, LANES), F32),
        ],
        scratch_shapes=[pltpu.VMEM((N_EXPERTS, LANES), F32)],
        compiler_params=_params("arbitrary"),
        name="rank",
    )(route)


def _slot_kernel(route_ref, rank_ref, start_ref, slot_ref):
    tm = route_ref.shape[1]
    starts = start_ref[...]
    slots = [jnp.sum(jnp.where(hit, starts, 0.0), axis=0, keepdims=True) + rank_ref[k:k + 1, :]
             for k, hit in enumerate(_expert_hits(route_ref, tm))]
    slot_ref[...] = jnp.concatenate(slots + [jnp.zeros((ROUTE_ROWS - TOP_K, tm), F32)], axis=0)


def _slot_call(route, rank, pad_starts):
    t = route.shape[1]
    tm = min(SLOT_TM, t)
    tile = pl.BlockSpec((ROUTE_ROWS, tm), lambda i: (0, i))
    return pl.pallas_call(
        _slot_kernel,
        grid=(t // tm,),
        in_specs=[tile, tile, pl.BlockSpec((N_EXPERTS, tm), lambda i: (0, 0))],
        out_specs=tile,
        out_shape=jax.ShapeDtypeStruct((ROUTE_ROWS, t), F32),
        compiler_params=_params("parallel"),
        name="slots",
    )(route, rank, jnp.broadcast_to(pad_starts[:, None], (N_EXPERTS, tm)))


def _for_static_slot(slot, fn):
    for par in range(2):
        pl.when(slot == par)(functools.partial(fn, par))


def _dispatch_kernel(cnt_ref, pstart_ref, dest_hbm, h_hbm, xs_hbm, idx_smem, hbuf, zbuf,
                     isem, hsem, dsem, zsem):
    tm = MOE_TM
    nd = TOP_K * tm
    i = pl.program_id(0)
    n = pl.num_programs(0)
    slot = i % 2
    n_slots = xs_hbm.shape[0]

    def idx_copy(step, sl):
        return pltpu.make_async_copy(dest_hbm.at[step], idx_smem.at[sl], isem.at[sl])

    def tile_copy(step, bf):
        rows = pl.ds(pl.multiple_of(step * tm, tm), tm)
        return pltpu.make_async_copy(h_hbm.at[rows], hbuf.at[bf], hsem.at[bf])

    def wait_scatter(bf):
        for _ in range(TOP_K):
            pltpu.make_async_copy(hbuf.at[bf], xs_hbm.at[pl.ds(0, tm)], dsem.at[bf]).wait()

    @pl.when(i == 0)
    def _():
        idx_copy(0, 0).start()
        tile_copy(0, 0).start()
        zbuf[...] = jnp.zeros(zbuf.shape, F32)

        def zero_gaps(wait):
            def per_expert(e, used):
                cnt = cnt_ref[e]
                gap = (MOE_BLOCK - cnt % MOE_BLOCK) % MOE_BLOCK
                row = pstart_ref[e] + cnt
                for k in range(MOE_BLOCK.bit_length() - 1):
                    size = 1 << k
                    copy = pltpu.make_async_copy(zbuf.at[pl.ds(0, size)], xs_hbm.at[pl.ds(row, size)], zsem)
                    pl.when(((gap >> k) & 1) == 1)(copy.wait if wait else copy.start)
                    row = row + (gap & size)
                return used + cnt + gap
            return lax.fori_loop(0, N_EXPERTS, per_expert, 0)
        zero_gaps(wait=False)
        used = zero_gaps(wait=True)

        def zero_block(blk):
            rows = pl.ds(pl.multiple_of(blk * MOE_BLOCK, MOE_BLOCK), MOE_BLOCK)
            return pltpu.make_async_copy(zbuf, xs_hbm.at[rows], zsem)

        def start_blk(blk, c):
            zero_block(blk).start()
            return c

        def wait_blk(blk, c):
            zero_block(blk).wait()
            return c
        lax.fori_loop(used // MOE_BLOCK, n_slots // MOE_BLOCK, start_blk, 0)
        lax.fori_loop(used // MOE_BLOCK, n_slots // MOE_BLOCK, wait_blk, 0)

    @pl.when(i > 0)
    def _():
        wait_scatter(1 - slot)

    @pl.when(i + 1 < n)
    def _():
        idx_copy(i + 1, 1 - slot).start()
        tile_copy(i + 1, 1 - slot).start()

    idx_copy(i, slot).wait()
    tile_copy(i, slot).wait()

    def scatter_rows(par):
        for r in range(nd):
            copy = pltpu.make_async_copy(hbuf.at[par, r % tm], xs_hbm.at[idx_smem[par, r]], dsem.at[par])
            copy.start(priority=r % DMA_QUEUES)
    _for_static_slot(slot, scatter_rows)

    @pl.when(i == n - 1)
    def _():
        wait_scatter(slot)


def _dispatch_call(counts, pad_starts, dest_tiles, h2, n_slots):
    n_steps = dest_tiles.shape[0]
    grid_spec = pltpu.PrefetchScalarGridSpec(
        num_scalar_prefetch=2,
        grid=(n_steps,),
        in_specs=[pl.BlockSpec(memory_space=pl.ANY), pl.BlockSpec(memory_space=pl.ANY)],
        out_specs=pl.BlockSpec(memory_space=pl.ANY),
        scratch_shapes=[
            pltpu.SMEM((2, TOP_K * MOE_TM), I32),
            pltpu.VMEM((2, MOE_TM) + ROW_TILE, F32),
            pltpu.VMEM((MOE_BLOCK,) + ROW_TILE, F32),
            pltpu.SemaphoreType.DMA((2,)),
            pltpu.SemaphoreType.DMA((2,)),
            pltpu.SemaphoreType.DMA((2,)),
            pltpu.SemaphoreType.DMA(()),
        ],
    )
    return pl.pallas_call(
        _dispatch_kernel,
        grid_spec=grid_spec,
        out_shape=jax.ShapeDtypeStruct((n_slots,) + ROW_TILE, F32),
        compiler_params=_params("arbitrary"),
        name="dispatch",
    )(counts, pad_starts, dest_tiles, h2)


def _expert_kernel(be_ref, first_ref, par_ref, nxt_ref, x_ref, wg_hbm, wu_hbm, wd_hbm, y_ref,
                   wg_scr, wu_scr, wd_scr, wg_f32, wu_f32, wd_f32, wsem):
    i = pl.program_id(0)

    def weight_copies(e, p):
        return [pltpu.make_async_copy(src.at[e], dst.at[p], wsem.at[p, j])
                for j, (src, dst) in enumerate(((wg_hbm, wg_f32), (wu_hbm, wu_f32), (wd_hbm, wd_f32)))]

    @pl.when(i == 0)
    def _():
        for copy in weight_copies(be_ref[0], 0):
            copy.start()

    @pl.when(first_ref[i] == 1)
    def _():
        p = par_ref[i]
        for copy in weight_copies(be_ref[i], p):
            copy.wait()
        wg_scr[...] = wg_f32[p].astype(BF16)
        wu_scr[...] = wu_f32[p].astype(BF16)
        wd_scr[...] = wd_f32[p].astype(BF16)

        @pl.when(nxt_ref[i] >= 0)
        def _():
            for copy in weight_copies(nxt_ref[i], 1 - p):
                copy.start()

    xb = _load_token_tiles(x_ref, 0, MOE_BLOCK).astype(BF16)
    gate = _dot(xb, wg_scr[...])
    up = _dot(xb, wu_scr[...])
    hid = (gate * jax.nn.sigmoid(gate) * up).astype(BF16)
    _store_token_tiles(y_ref, _dot(hid, wd_scr[...]))


def _expert_call(blk_expert, xs, wg, wu, wd):
    n_blk = blk_expert.shape[0]
    d = wg.shape[1]
    blk = (MOE_BLOCK * ROW_TILE[0], LANES)
    prev = jnp.concatenate([jnp.full((1,), -1, I32), blk_expert[:-1]])
    first = (blk_expert != prev).astype(I32)
    parity = (jnp.cumsum(first) - 1) % 2
    ids = jnp.arange(N_EXPERTS, dtype=I32)
    present = jnp.any(blk_expert[None, :] == ids[:, None], axis=1)
    later = jnp.where(present[None, :] & (ids[None, :] > ids[:, None]), ids[None, :], N_EXPERTS)
    next_of = jnp.min(later, axis=1)
    nxt = jnp.where(next_of[blk_expert] < N_EXPERTS, next_of[blk_expert], -1).astype(I32)
    block = lambda i, *_: (i, 0)
    grid_spec = pltpu.PrefetchScalarGridSpec(
        num_scalar_prefetch=4,
        grid=(n_blk,),
        in_specs=[
            pl.BlockSpec(blk, block),
            pl.BlockSpec(memory_space=pl.ANY),
            pl.BlockSpec(memory_space=pl.ANY),
            pl.BlockSpec(memory_space=pl.ANY),
        ],
        out_specs=pl.BlockSpec(blk, block),
        scratch_shapes=[
            pltpu.VMEM((d, EXPERT_FF), BF16),
            pltpu.VMEM((d, EXPERT_FF), BF16),
            pltpu.VMEM((EXPERT_FF, d), BF16),
            pltpu.VMEM((2, d, EXPERT_FF), F32),
            pltpu.VMEM((2, d, EXPERT_FF), F32),
            pltpu.VMEM((2, EXPERT_FF, d), F32),
            pltpu.SemaphoreType.DMA((2, 3)),
        ],
    )
    return pl.pallas_call(
        _expert_kernel,
        grid_spec=grid_spec,
        out_shape=jax.ShapeDtypeStruct(xs.shape, F32),
        compiler_params=_params("arbitrary"),
        name="experts",
    )(blk_expert, first, parity.astype(I32), nxt, xs, wg, wu, wd)


def _combine_kernel(dest_hbm, y_hbm, x1_ref, route_ref, mod_ref, o_ref, idx_smem, ybuf, isem, dsem):
    tm = MOE_TM
    nd = TOP_K * tm
    i = pl.program_id(0)
    n = pl.num_programs(0)
    slot = i % 2
    nxt = 1 - slot

    def idx_copy(step, sl):
        return pltpu.make_async_copy(dest_hbm.at[step], idx_smem.at[sl], isem.at[sl])

    ns = ROW_TILE[0]

    def step_rows(sl):
        return pltpu.make_async_copy(y_hbm.at[pl.ds(0, nd * ns)], ybuf.at[sl], dsem.at[sl])

    def issue_rows(par):
        for r in range(nd):
            src = y_hbm.at[pl.ds(pl.multiple_of(idx_smem[par, r] * ns, ns), ns)]
            copy = pltpu.make_async_copy(src, ybuf.at[par, pl.ds(r * ns, ns)], dsem.at[par])
            copy.start(priority=r % DMA_QUEUES)

    @pl.when(i == 0)
    def _():
        idx_copy(0, 0).start()
        idx_copy(0, 0).wait()
        issue_rows(0)

        @pl.when(n > 1)
        def _():
            idx_copy(1, 1).start()

    @pl.when(i + 1 < n)
    def _():
        idx_copy(i + 1, nxt).wait()
        _for_static_slot(nxt, issue_rows)

    @pl.when(i + 2 < n)
    def _():
        idx_copy(i + 2, slot).start()

    step_rows(slot).wait()
    route = jnp.concatenate([route_ref[...], jnp.zeros((LANES - ROUTE_ROWS, tm), F32)], axis=0).T
    w1 = route[:, 2:3]
    w2 = route[:, 3:4]
    y = (w1 * _load_token_tiles(ybuf, 0, tm, lead=(slot,))
         + w2 * _load_token_tiles(ybuf, tm, tm, lead=(slot,)))
    o_ref[...] = x1_ref[...] + mod_ref[0, 5:6, :] * y


def _combine_call(dest_tiles, y_slots, x1, route, mod6, s):
    t, d = x1.shape
    tm = MOE_TM
    per_b = s // tm
    return pl.pallas_call(
        _combine_kernel,
        grid=(t // tm,),
        in_specs=[
            pl.BlockSpec(memory_space=pl.ANY),
            pl.BlockSpec(memory_space=pl.ANY),
            pl.BlockSpec((tm, d), lambda i: (i, 0)),
            pl.BlockSpec((ROUTE_ROWS, tm), lambda i: (0, i)),
            pl.BlockSpec((1, 6, d), lambda i: (i // per_b, 0, 0)),
        ],
        out_specs=pl.BlockSpec((tm, d), lambda i: (i, 0)),
        out_shape=jax.ShapeDtypeStruct((t, d), F32),
        scratch_shapes=[
            pltpu.SMEM((2, TOP_K * tm), I32),
            pltpu.VMEM((2, TOP_K * tm * ROW_TILE[0], LANES), F32),
            pltpu.SemaphoreType.DMA((2,)),
            pltpu.SemaphoreType.DMA((2,)),
        ],
        compiler_params=_params("arbitrary"),
        name="combine",
    )(dest_tiles, y_slots, x1, route, mod6)


def _rope_tables(pos):
    inv = ROPE_THETA ** (-jnp.arange(0, HEAD_DIM, 2, dtype=F32) / HEAD_DIM)
    ang = pos.astype(F32)[..., None] * inv
    return jnp.cos(ang), jnp.sin(ang)


def _rope_tiles(pos):
    cos, sin = _rope_tables(pos)
    return (jnp.concatenate([cos, cos, cos, cos], axis=-1),
            jnp.concatenate([-sin, sin, -sin, sin], axis=-1))


def _selection_constants(s):
    n_c = s // CMP_STRIDE
    n_s = s // SEL_BLOCK
    cs = np.arange(n_c) * CMP_STRIDE
    ss = np.arange(n_s) * SEL_BLOCK
    ov = np.clip(np.minimum(cs[:, None] + CMP_BLOCK, ss[None, :] + SEL_BLOCK)
                 - np.maximum(cs[:, None], ss[None, :]), 0, None).astype(np.float32) / CMP_BLOCK
    kq = np.arange(NSA_TQ)[:, None] - np.arange(NSA_TQ)[None, :]
    wbias = np.stack([np.where(kq >= 0, 0.0, MASKED), np.where(kq <= 0, 0.0, MASKED)]).astype(np.float32)
    return jnp.asarray(ov.T, BF16), jnp.asarray(wbias)


def _mixer_and_router(x, c, positions, w_ada, b_ada, norm1_g, norm2_g, w_in, nsa_q_norm, nsa_k_norm,
                      cmp_pe_k, cmp_w1_k, cmp_w2_k, cmp_pe_v, cmp_w1_v, cmp_w2_v, dil_q_norm,
                      dil_k_norm, w_up_a, w_up_b, w_out, w_group, b_group, w_router, b_router):
    b, s, d = x.shape
    scale = HEAD_DIM ** -0.5
    mod6 = _mod_call(c, w_ada, b_ada).reshape(b, 6, d)

    c1 = A_Q
    c2 = c1 + 6 * A_KV
    c3 = c2 + 3 * NSA_HEADS
    c4 = c3 + 3 * DIL_W
    w_perm = jnp.concatenate([
        w_in[:, :c2], w_in[:, c3:c4],
        jnp.pad(w_in[:, c2:c3], ((0, 0), (0, LANES - 3 * NSA_HEADS))), w_in[:, c4:]], axis=1).astype(BF16)
    two = lambda g: jnp.concatenate([g, g]).astype(F32)
    gains = jnp.stack([two(nsa_q_norm) * (scale * LOG2E), two(nsa_k_norm), two(dil_q_norm) * scale,
                       two(dil_k_norm)])
    cos_t, sin_t = _rope_tables(positions)
    qext, kvc, ksl, vsl, kwp, vwp, dil0, dil1, dil2, ga, gm = _proj_call(
        x, mod6, norm1_g.reshape(1, d), w_perm, gains, cos_t, sin_t)

    pe = jnp.stack([jnp.concatenate([p, p], axis=1) for p in (cmp_pe_k, cmp_pe_v)])
    zero_w = jnp.zeros((CMP_BLOCK, HEAD_DIM, CMP_HIDDEN), F32)

    def per_position(w):
        w = w.reshape(CMP_BLOCK, HEAD_DIM, CMP_HIDDEN)
        return jnp.concatenate([jnp.concatenate([w, zero_w], axis=2),
                                jnp.concatenate([zero_w, w], axis=2)], axis=1)
    w1 = jnp.stack([per_position(cmp_w1_k), per_position(cmp_w1_v)]).astype(BF16)
    zeros = jnp.zeros((CMP_HIDDEN, HEAD_DIM), F32)
    ext = lambda w: jnp.stack([jnp.concatenate([w, zeros], 1), jnp.concatenate([zeros, w], 1)])
    w2ext = jnp.stack([ext(cmp_w2_k), ext(cmp_w2_v)]).astype(BF16)
    cmp_pos = jnp.pad(positions[:, CMP_BLOCK - 1::CMP_STRIDE], ((0, 0), (0, 1)))
    ccos, csin = _rope_tiles(cmp_pos)
    kc, vc = _cmp_call(kvc, pe, w1, w2ext, two(nsa_k_norm).reshape(1, LANES), ccos, csin)

    ov_t, wbias = _selection_constants(s)
    o_a = _nsa_call(qext, kc, vc, ksl, vsl, kwp, vwp, ga, ov_t, wbias)

    ods, lses = zip(*[_dil_call(qkv.reshape(b, 3, s, LANES), dl)
                      for qkv, (_, dl) in zip((dil0, dil1, dil2), DIL_PAIRS)])

    t = b * s
    w_r = jnp.concatenate([w_group, w_router.transpose(1, 0, 2).reshape(d, N_EXPERTS)], axis=1)
    w_r = jnp.pad(w_r, ((0, 0), (0, LANES - w_r.shape[1])))
    rcat = jnp.concatenate(_split(w_r.T), axis=0)
    rb = jnp.pad(jnp.concatenate([b_group, b_router.reshape(-1)]), (0, LANES - N_GROUPS - N_EXPERTS))
    return _merge_call(
        x.reshape(t, d), o_a.reshape(t, A_Q), ods, lses, gm.reshape(t, 2 * d), mod6,
        w_up_a.astype(BF16), w_up_b.astype(BF16), w_out.astype(BF16), norm2_g.reshape(1, d),
        rcat, rb.reshape(LANES, 1).astype(F32), s), mod6


def _moe(x1, h2, route, mod6, w_e_gate, w_e_up, w_e_down, s):
    t, _ = x1.shape
    rank, counts = _rank_call(route)
    counts = counts[:, 0].astype(I32)
    padded = (counts + MOE_BLOCK - 1) // MOE_BLOCK * MOE_BLOCK
    pad_ends = jnp.cumsum(padded)
    pad_starts = pad_ends - padded
    dest = _slot_call(route, rank, pad_starts.astype(F32))[0:TOP_K].astype(I32)
    n_slots = t * TOP_K + N_EXPERTS * MOE_BLOCK
    n_blk = n_slots // MOE_BLOCK
    blk_start = jnp.arange(n_blk, dtype=I32) * MOE_BLOCK
    blk_expert = jnp.minimum(jnp.sum((pad_ends[None, :] <= blk_start[:, None]).astype(I32), axis=1),
                             N_EXPERTS - 1)
    tm = MOE_TM
    dest_tiles = dest.reshape(TOP_K, t // tm, tm).transpose(1, 0, 2).reshape(t // tm, TOP_K * tm)
    xs = _dispatch_call(counts, pad_starts.astype(I32), dest_tiles, h2.reshape((t,) + ROW_TILE), n_slots)
    y_slots = _expert_call(blk_expert, xs.reshape(n_slots * ROW_TILE[0], LANES), w_e_gate, w_e_up, w_e_down)
    return _combine_call(dest_tiles, y_slots, x1, route, mod6, s)


def kernel(x, c, positions, w_ada, b_ada, norm1_g, norm2_g, w_in, nsa_q_norm, nsa_k_norm, cmp_pe_k,
           cmp_w1_k, cmp_w2_k, cmp_pe_v, cmp_w1_v, cmp_w2_v, dil_q_norm, dil_k_norm, w_up_a, w_up_b,
           w_out, w_group, b_group, w_router, b_router, w_e_gate, w_e_up, w_e_down):
    b, s, d = x.shape
    assert w_ada.shape[0] == 1 and d == D_MODEL and s % NSA_TK == 0
    (x1, h2, route), mod6 = _mixer_and_router(
        x, c, positions, w_ada[0], b_ada[0], norm1_g[0], norm2_g[0], w_in[0], nsa_q_norm[0],
        nsa_k_norm[0], cmp_pe_k[0], cmp_w1_k[0], cmp_w2_k[0], cmp_pe_v[0], cmp_w1_v[0], cmp_w2_v[0],
        dil_q_norm[0], dil_k_norm[0], w_up_a[0], w_up_b[0], w_out[0], w_group[0], b_group[0],
        w_router[0], b_router[0])
    out = _moe(x1, h2, route, mod6, w_e_gate[0], w_e_up[0], w_e_down[0], s)
    return out.reshape(b, s, d)
```
